```python
import math
import jax, jax.numpy as jnp
from jax import lax
import numpy as np

D_MODEL = 1024
BATCH = 4
SEQ = 4096
DEPTH = 1

D_MIX = D_MODEL
D_A = D_MIX // 2
A_GROUPS = 4
A_DH = D_A // A_GROUPS
A_CHUNK = 128
D_B = D_MIX - D_A
B_HEADS = 8
B_DH = D_B // B_HEADS
B_BLOCK = 256
B_TOPK = 3
B_QCHUNK = 64
D_IN_PROJ = 2 * D_A + 3 * D_B
D_PLE = 256
N_GROUPS = 4
EXP_PER_GROUP = 8
N_EXPERTS = N_GROUPS * EXP_PER_GROUP
D_FF_EXP = 256
TOP_K_EXP = 2
MOE_TCHUNK = 2048
EPS = 1e-6
NEG = -1e30

kernel_name = "hymba_gmlp_moba_hiermoe_block"


def rms_norm(x, g):
    xf = x.astype(jnp.float32)
    y = xf * lax.rsqrt(jnp.mean(xf * xf, axis=-1, keepdims=True) + EPS)
    return (y * g.astype(jnp.float32)).astype(x.dtype)


def mixer_a(u, v, g_v, w_s, b_s):
    b, s, _ = u.shape
    nc = s // A_CHUNK
    u = jax.nn.gelu(u).reshape(b, nc, A_CHUNK, A_GROUPS, A_DH)
    v = jax.nn.gelu(v).reshape(b, s, A_GROUPS, A_DH)
    v = rms_norm(v, g_v.reshape(A_GROUPS, A_DH)).reshape(b, nc, A_CHUNK, A_GROUPS, A_DH)
    ws = jnp.tril(w_s)
    mixed = jnp.einsum('gts,bnsgd->bntgd', ws, v) + b_s.T[None, None, :, :, None]
    return (u * mixed).reshape(b, s, D_A)


def mixer_b(q, k, v, g_q, g_k):
    b, s, _ = q.shape
    nb = -(-s // B_BLOCK)
    sp = nb * B_BLOCK
    pad = ((0, 0), (0, sp - s), (0, 0))
    q, k, v = (jnp.pad(t, pad).reshape(b, sp, B_HEADS, B_DH).transpose(0, 2, 1, 3) for t in (q, k, v))
    q = rms_norm(q, g_q)
    k = rms_norm(k, g_k)
    scale = 1.0 / math.sqrt(B_DH)
    kb = k.reshape(b, B_HEADS, nb, B_BLOCK, B_DH)
    vb = v.reshape(b, B_HEADS, nb, B_BLOCK, B_DH)
    kmean = jnp.mean(kb.astype(jnp.float32), axis=3)
    gate = jnp.einsum('bhsd,bhnd->bhsn', q.astype(jnp.float32), kmean)
    qblk = jnp.arange(sp) // B_BLOCK
    past = jnp.arange(nb)[None, :] < qblk[:, None]
    gate = jnp.where(past[None, None], gate, NEG)
    k_sel = min(B_TOPK, nb)
    sel_score, sel_idx = lax.top_k(gate, k_sel)
    sel_valid = sel_score > (NEG * 0.5)

    nqc = sp // B_QCHUNK
    q_ch = jnp.moveaxis(q.reshape(b, B_HEADS, nqc, B_QCHUNK, B_DH), 2, 0)
    idx_ch = jnp.moveaxis(sel_idx.reshape(b, B_HEADS, nqc, B_QCHUNK, k_sel), 2, 0)
    val_ch = jnp.moveaxis(sel_valid.reshape(b, B_HEADS, nqc, B_QCHUNK, k_sel), 2, 0)
    gather = jax.vmap(jax.vmap(lambda blocks, ix: blocks[ix]))

    def step(args):
        q_c, idx_c, val_c, c = args
        blk = (c * B_QCHUNK) // B_BLOCK
        k_own = lax.dynamic_index_in_dim(kb, blk, axis=2, keepdims=False)
        v_own = lax.dynamic_index_in_dim(vb, blk, axis=2, keepdims=False)
        k_g = gather(kb, idx_c)
        v_g = gather(vb, idx_c)
        s_g = jnp.einsum('bhqd,bhqnkd->bhqnk', q_c, k_g).astype(jnp.float32) * scale
        s_g = jnp.where(val_c[..., None], s_g, NEG).reshape(b, B_HEADS, B_QCHUNK, k_sel * B_BLOCK)
        s_o = jnp.einsum('bhqd,bhkd->bhqk', q_c, k_own).astype(jnp.float32) * scale
        qpos = c * B_QCHUNK + jnp.arange(B_QCHUNK)
        kpos = blk * B_BLOCK + jnp.arange(B_BLOCK)
        s_o = jnp.where((kpos[None, :] <= qpos[:, None])[None, None], s_o, NEG)
        probs = jax.nn.softmax(jnp.concatenate([s_g, s_o], axis=-1), axis=-1).astype(v.dtype)
        p_g = probs[..., :k_sel * B_BLOCK].reshape(b, B_HEADS, B_QCHUNK, k_sel, B_BLOCK)
        p_o = probs[..., k_sel * B_BLOCK:]
        return (jnp.einsum('bhqnk,bhqnkd->bhqd', p_g, v_g)
                + jnp.einsum('bhqk,bhkd->bhqd', p_o, v_own))

    out = lax.map(step, (q_ch, idx_ch, val_ch, jnp.arange(nqc)))
    out = jnp.moveaxis(out, 0, 2).reshape(b, B_HEADS, sp, B_DH)[:, :, :s]
    return out.transpose(0, 2, 1, 3).reshape(b, s, D_B)


def hier_moe(xn, w_group, w_expert, w_gate_e, w_up_e, w_down_e):
    t = xn.shape[0]
    g_prob = jax.nn.softmax((xn @ w_group).astype(jnp.float32), axis=-1)
    pg, g_sel = lax.top_k(g_prob, 1)
    e_logits = (xn @ w_expert).astype(jnp.float32).reshape(t, N_GROUPS, EXP_PER_GROUP)
    e_in = jnp.take_along_axis(e_logits, g_sel[:, :, None], axis=1)[:, 0]
    e_prob = jax.nn.softmax(e_in, axis=-1)
    e_w, e_idx = lax.top_k(e_prob, TOP_K_EXP)
    e_w = e_w / jnp.sum(e_w, axis=-1, keepdims=True)
    w_in = jnp.sum(jax.nn.one_hot(e_idx, EXP_PER_GROUP, dtype=jnp.float32) * e_w[..., None], axis=1)
    combine = (jax.nn.one_hot(g_sel[:, 0], N_GROUPS, dtype=jnp.float32)[:, :, None]
               * (pg[:, :, None] * w_in[:, None, :])).reshape(t, N_EXPERTS).astype(xn.dtype)

    tc = math.gcd(t, MOE_TCHUNK)
    xs = xn.reshape(t // tc, tc, -1)
    cs = combine.reshape(t // tc, tc, N_EXPERTS)

    def step(args):
        xc, cc = args
        h = jnp.einsum('td,edf->tef', xc, w_gate_e)
        u = jnp.einsum('td,edf->tef', xc, w_up_e)
        a = jax.nn.silu(h) * u * cc[:, :, None]
        return jnp.einsum('tef,efd->td', a, w_down_e)

    return lax.map(step, (xs, cs)).reshape(t, -1)


def setup_inputs(seed: int = 0) -> dict:
    key = jax.random.key(seed)
    ks = jax.random.split(key, 24)
    f = jnp.float32
    nrm = lambda k, shape, s: jax.random.normal(k, shape, f) * s
    gain = lambda k, shape: 1.0 + 0.05 * jax.random.normal(k, shape, f)
    L = DEPTH
    return {
        "x": jax.random.normal(ks[0], (BATCH, SEQ, D_MODEL), f),
        "p": jax.random.normal(ks[1], (DEPTH, BATCH, SEQ, D_PLE), f),
        "g_mix": gain(ks[2], (L, D_MODEL)),
        "w_in": nrm(ks[3], (L, D_MODEL, D_IN_PROJ), D_MODEL ** -0.5),
        "g_v_a": gain(ks[4], (L, D_A)),
        "w_s": nrm(ks[5], (L, A_GROUPS, A_CHUNK, A_CHUNK), A_CHUNK ** -0.5),
        "b_s": 1.0 + 0.1 * jax.random.normal(ks[6], (L, A_GROUPS, A_CHUNK), f),
        "g_q": gain(ks[7], (L, B_DH)),
        "g_k": gain(ks[8], (L, B_DH)),
        "g_out_a": gain(ks[9], (L, D_A)),
        "g_out_b": gain(ks[10], (L, D_B)),
        "w_out": nrm(ks[11], (L, D_MIX, D_MODEL), D_MIX ** -0.5),
        "g_ffn": gain(ks[12], (L, D_MODEL)),
        "w_group": nrm(ks[13], (L, D_MODEL, N_GROUPS), D_MODEL ** -0.5),
        "w_expert": nrm(ks[14], (L, D_MODEL, N_EXPERTS), D_MODEL ** -0.5),
        "w_gate_e": nrm(ks[15], (L, N_EXPERTS, D_MODEL, D_FF_EXP), D_MODEL ** -0.5),
        "w_up_e": nrm(ks[16], (L, N_EXPERTS, D_MODEL, D_FF_EXP), D_MODEL ** -0.5),
        "w_down_e": nrm(ks[17], (L, N_EXPERTS, D_FF_EXP, D_MODEL), D_FF_EXP ** -0.5),
        "g_ple": gain(ks[18], (L, D_MODEL)),
        "w_ple": nrm(ks[19], (L, D_PLE, D_MODEL), D_PLE ** -0.5),
        "w_ple_gate": nrm(ks[20], (L, D_MODEL, D_MODEL), D_MODEL ** -0.5),
    }


def reference(x, p, g_mix, w_in, g_v_a, w_s, b_s, g_q, g_k, g_out_a, g_out_b, w_out,
              g_ffn, w_group, w_expert, w_gate_e, w_up_e, w_down_e, g_ple, w_ple, w_ple_gate):
    b, s, d = x.shape
    h = x
    for i in range(DEPTH):
        xn = rms_norm(h, g_mix[i])
        proj = xn @ w_in[i]
        u_a, v_a, q_b, k_b, v_b = jnp.split(
            proj, [D_A, 2 * D_A, 2 * D_A + D_B, 2 * D_A + 2 * D_B], axis=-1)
        y_a = rms_norm(mixer_a(u_a, v_a, g_v_a[i], w_s[i], b_s[i]), g_out_a[i])
        y_b = rms_norm(mixer_b(q_b, k_b, v_b, g_q[i], g_k[i]), g_out_b[i])
        h = h + jnp.concatenate([y_a, y_b], axis=-1) @ w_out[i]
        hn = rms_norm(h, g_ffn[i]).reshape(b * s, d)
        h = h + hier_moe(hn, w_group[i], w_expert[i], w_gate_e[i], w_up_e[i], w_down_e[i]).reshape(b, s, d)
        gate = jax.nn.sigmoid(rms_norm(h, g_ple[i]) @ w_ple_gate[i])
        h = h + (p[i] @ w_ple[i]) * gate
    return h
```

```python
import functools

import jax
import jax.numpy as jnp
from jax import lax
from jax.experimental import pallas as pl
from jax.experimental.pallas import tpu as pltpu

F32 = jnp.float32
BF16 = jnp.bfloat16

D_MODEL = 1024
D_A = 512
A_GROUPS = 4
A_DH = D_A // A_GROUPS
A_CHUNK = 128
D_B = 512
B_HEADS = 8
B_DH = D_B // B_HEADS
B_BLOCK = 256
B_TOPK = 3
D_PLE = 256
N_GROUPS = 4
EXP_PER_GROUP = 8
N_EXPERTS = N_GROUPS * EXP_PER_GROUP
D_FF_EXP = 256
EPS = 1e-6
NEG = -1e30

LANES = 128
HEAD_PAIR = 2 * B_DH
N_PAIRS = B_HEADS // 2
ROUTER_LANES = LANES
VMEM_LIMIT = 56 * 1024 * 1024

PROJ_TM = 512
OUT_TM = 512
MOE_TM = 1024


def _rms(x, g):
    return x * lax.rsqrt(jnp.mean(x * x, axis=-1, keepdims=True) + EPS) * g


def _dot(a, b):
    return jnp.dot(a, b, preferred_element_type=F32)


def _dot_nt(a, b):
    return lax.dot_general(a, b, (((1,), (1,)), ((), ())), preferred_element_type=F32)


def _split_bf16(x):
    hi = x.astype(BF16)
    lo = (x - hi.astype(F32)).astype(BF16)
    return hi, lo


def _proj_kernel(x_ref, gmix_ref, wuv_ref, wk_ref, wqt_ref, wvt_ref, gva_ref, ws_ref, bs_ref,
                 goa_ref, gk_ref, gq_ref, bd_ref,
                 ya_ref, k_ref, qt_ref, vt_ref, prod_ref):
    tm = x_ref.shape[1]
    n_chunks = tm // A_CHUNK
    xn = _rms(x_ref[0], gmix_ref[...]).astype(BF16)

    uv = _dot(xn, wuv_ref[...])
    u = jax.nn.gelu(uv[:, :D_A])
    v = jax.nn.gelu(uv[:, D_A:])
    row = lax.broadcasted_iota(jnp.int32, (A_CHUNK, A_CHUNK), 0)
    col = lax.broadcasted_iota(jnp.int32, (A_CHUNK, A_CHUNK), 1)
    causal = col <= row
    for g in range(A_GROUPS):
        gs = slice(g * A_DH, (g + 1) * A_DH)
        vn = _rms(v[:, gs], gva_ref[:, gs]).astype(BF16)
        rhs = jnp.concatenate([vn[c * A_CHUNK:(c + 1) * A_CHUNK] for c in range(n_chunks)], axis=1)
        ws = jnp.where(causal, ws_ref[g], 0.0).astype(BF16)
        mixed = _dot(ws, rhs)
        for c in range(n_chunks):
            cs = slice(c * A_CHUNK, (c + 1) * A_CHUNK)
            prod_ref[cs, gs] = u[cs, gs] * (mixed[:, cs] + bs_ref[g])
    ya_ref[0] = _rms(prod_ref[...], goa_ref[...]).astype(ya_ref.dtype)

    k = _dot(xn, wk_ref[...])
    k2_hi, k2_lo = _split_bf16(k * k)
    k_ms = _dot(k2_hi, bd_ref[...]) + _dot(k2_lo, bd_ref[...])
    k_ref[0] = (k * lax.rsqrt(k_ms + EPS) * gk_ref[...]).astype(k_ref.dtype)

    qt = _dot_nt(wqt_ref[...], xn)
    vt = _dot_nt(wvt_ref[...], xn).astype(vt_ref.dtype)
    qn = []
    for h in range(B_HEADS):
        qh = qt[h * B_DH:(h + 1) * B_DH]
        qn.append(qh * lax.rsqrt(jnp.mean(qh * qh, axis=0, keepdims=True) + EPS))
    qn = (jnp.concatenate(qn, axis=0) * gq_ref[...]).astype(qt_ref.dtype)
    for p in range(N_PAIRS):
        ps = slice(p * HEAD_PAIR, (p + 1) * HEAD_PAIR)
        for j in range(tm // B_BLOCK):
            js = slice(j * B_BLOCK, (j + 1) * B_BLOCK)
            qt_ref[0, p, j] = qn[ps, js]
            vt_ref[0, p, j] = vt[ps, js]


def _proj_call(x, gmix, wuv, wk, wqt, wvt, gva, ws, bs, goa, gk, gq, bd):
    b, s, d = x.shape
    tm = PROJ_TM
    nb = s // B_BLOCK
    jb = tm // B_BLOCK
    const = lambda *shape: pl.BlockSpec(shape, lambda bi, ti: (0,) * len(shape))
    return pl.pallas_call(
        _proj_kernel,
        grid=(b, s // tm),
        in_specs=[
            pl.BlockSpec((1, tm, d), lambda bi, ti: (bi, ti, 0)),
            const(1, d), const(d, 2 * D_A), const(d, D_B), const(D_B, d), const(D_B, d),
            const(1, D_A), const(A_GROUPS, A_CHUNK, A_CHUNK), const(A_GROUPS, A_CHUNK, A_DH),
            const(1, D_A), const(1, D_B), const(D_B, 1), const(D_B, D_B),
        ],
        out_specs=[
            pl.BlockSpec((1, tm, D_A), lambda bi, ti: (bi, ti, 0)),
            pl.BlockSpec((1, tm, D_B), lambda bi, ti: (bi, ti, 0)),
            pl.BlockSpec((1, N_PAIRS, jb, HEAD_PAIR, B_BLOCK), lambda bi, ti: (bi, 0, ti, 0, 0)),
            pl.BlockSpec((1, N_PAIRS, jb, HEAD_PAIR, B_BLOCK), lambda bi, ti: (bi, 0, ti, 0, 0)),
        ],
        out_shape=[
            jax.ShapeDtypeStruct((b, s, D_A), BF16),
            jax.ShapeDtypeStruct((b, s, D_B), BF16),
            jax.ShapeDtypeStruct((b, N_PAIRS, nb, HEAD_PAIR, B_BLOCK), BF16),
            jax.ShapeDtypeStruct((b, N_PAIRS, nb, HEAD_PAIR, B_BLOCK), BF16),
        ],
        scratch_shapes=[pltpu.VMEM((tm, D_A), F32)],
        compiler_params=pltpu.CompilerParams(
            dimension_semantics=("parallel", "parallel"), vmem_limit_bytes=VMEM_LIMIT),
        name="proj",
    )(x, gmix, wuv, wk, wqt, wvt, gva, ws, bs, goa, gk, gq, bd)


def _moba_kernel(qt_ref, k_ref, vt_ref, y_ref, kmean_ref, sel_ref):
    qb = pl.program_id(2)
    nb = k_ref.shape[1] // B_BLOCK
    nbp = kmean_ref.shape[0] // 3

    @pl.when(qb == 0)
    def _():
        rows = [jnp.mean(k_ref[0, j * B_BLOCK:(j + 1) * B_BLOCK, :].astype(F32), axis=0, keepdims=True)
                for j in range(nb)]
        rows += [jnp.zeros((1, HEAD_PAIR), F32)] * (nbp - nb)
        km = jnp.concatenate(rows, axis=0)
        hi = km.astype(BF16)
        rest = km - hi.astype(F32)
        mid = rest.astype(BF16)
        lo = (rest - mid.astype(F32)).astype(BF16)
        kmean_ref[...] = jnp.concatenate([hi, mid, lo], axis=0)

    q_pair = qt_ref[0, 0, 0]
    head_row = lax.broadcasted_iota(jnp.int32, (HEAD_PAIR, B_BLOCK), 0) // B_DH
    blk = lax.broadcasted_iota(jnp.int32, (nbp, B_BLOCK), 0)
    key_pos = lax.broadcasted_iota(jnp.int32, (B_BLOCK, B_BLOCK), 0)
    q_pos = lax.broadcasted_iota(jnp.int32, (B_BLOCK, B_BLOCK), 1)
    past = blk < qb

    outs = []
    for hh in range(2):
        qz = jnp.where(head_row == hh, q_pair, jnp.zeros_like(q_pair))
        g3 = _dot(kmean_ref[...], qz)
        gate = (g3[:nbp] + g3[nbp:2 * nbp] + g3[2 * nbp:]) * float(B_DH) ** 0.5
        gate = jnp.where(past, gate, NEG)
        rank = jnp.zeros((nbp, B_BLOCK), F32)
        for j in range(nb):
            gj = gate[j:j + 1]
            ahead = (gj > gate) | ((gj == gate) & (blk > j))
            rank = rank + jnp.where(ahead, 1.0, 0.0)
        chosen = (rank < float(B_TOPK)) & (gate > NEG * 0.5)
        sel_ref[hh] = jnp.where(chosen, 1.0, 0.0)

        s = _dot(k_ref[0, pl.ds(pl.multiple_of(qb * B_BLOCK, B_BLOCK), B_BLOCK), :], qz)
        s = jnp.where(key_pos <= q_pos, s, NEG)
        m = jnp.max(s, axis=0, keepdims=True)
        p = jnp.exp(s - m)
        l = jnp.sum(p, axis=0, keepdims=True)
        acc = _dot(vt_ref[0, 0, qb, hh * B_DH:(hh + 1) * B_DH, :], p.astype(BF16))

        def body(j, carry, hh=hh, qz=qz):
            m, l, acc = carry
            kj = k_ref[0, pl.ds(pl.multiple_of(j * B_BLOCK, B_BLOCK), B_BLOCK), :]
            s = _dot(kj, qz)
            s = jnp.where(sel_ref[hh, pl.ds(j, 1), :] > 0.5, s, NEG)
            m_new = jnp.maximum(m, jnp.max(s, axis=0, keepdims=True))
            alpha = jnp.exp(m - m_new)
            p = jnp.exp(s - m_new)
            l = alpha * l + jnp.sum(p, axis=0, keepdims=True)
            acc = alpha * acc + _dot(vt_ref[0, 0, j, hh * B_DH:(hh + 1) * B_DH, :], p.astype(BF16))
            return m_new, l, acc

        m, l, acc = lax.fori_loop(0, qb, body, (m, l, acc))
        outs.append(acc / l)
    y_ref[0] = jnp.concatenate(outs, axis=0).T.astype(y_ref.dtype)


def _moba_call(qt, k, vt):
    b, s, _ = k.shape
    nb = s // B_BLOCK
    nbp = -(-nb // 8) * 8
    return pl.pallas_call(
        _moba_kernel,
        grid=(b, N_PAIRS, nb),
        in_specs=[
            pl.BlockSpec((1, 1, 1, HEAD_PAIR, B_BLOCK), lambda bi, pi, qi: (bi, pi, qi, 0, 0)),
            pl.BlockSpec((1, s, HEAD_PAIR), lambda bi, pi, qi: (bi, 0, pi)),
            pl.BlockSpec((1, 1, nb, HEAD_PAIR, B_BLOCK), lambda bi, pi, qi: (bi, pi, 0, 0, 0)),
        ],
        out_specs=pl.BlockSpec((1, B_BLOCK, HEAD_PAIR), lambda bi, pi, qi: (bi, qi, pi)),
        out_shape=jax.ShapeDtypeStruct((b, s, D_B), BF16),
        scratch_shapes=[pltpu.VMEM((3 * nbp, HEAD_PAIR), BF16), pltpu.VMEM((2, nbp, B_BLOCK), F32)],
        compiler_params=pltpu.CompilerParams(
            dimension_semantics=("parallel", "parallel", "arbitrary"), vmem_limit_bytes=VMEM_LIMIT),
        name="moba",
    )(qt, k, vt)


def _outmix_kernel(x_ref, ya_ref, yb_ref, gob_ref, wout_ref, gffn_ref, wr_hi_ref, wr_lo_ref,
                   h_ref, hn_ref, comb_ref):
    ya = ya_ref[...]
    yb = _rms(yb_ref[...].astype(F32), gob_ref[...]).astype(BF16)
    h = x_ref[...] + _dot(ya, wout_ref[:D_A]) + _dot(yb, wout_ref[D_A:])
    h_ref[...] = h
    hn = _rms(h, gffn_ref[...])
    hn_ref[...] = hn.astype(hn_ref.dtype)

    hn_hi, hn_lo = _split_bf16(hn)
    lg = _dot(hn_hi, wr_hi_ref[...]) + (_dot(hn_lo, wr_hi_ref[...]) + _dot(hn_hi, wr_lo_ref[...]))
    lane = lax.broadcasted_iota(jnp.int32, lg.shape, 1)
    big = ROUTER_LANES

    def first_lane(mask):
        return jnp.min(jnp.where(mask, lane, big), axis=-1, keepdims=True)

    is_g = lane < N_GROUPS
    g_max = jnp.max(jnp.where(is_g, lg, -jnp.inf), axis=-1, keepdims=True)
    p_g = 1.0 / jnp.sum(jnp.where(is_g, jnp.exp(lg - g_max), 0.0), axis=-1, keepdims=True)
    g_sel = first_lane(is_g & (lg == g_max))
    e_lane = lane - N_GROUPS
    in_grp = (e_lane >= 0) & (e_lane < N_EXPERTS) & (lax.shift_right_arithmetic(e_lane, 3) == g_sel)
    l1 = jnp.max(jnp.where(in_grp, lg, -jnp.inf), axis=-1, keepdims=True)
    i1 = first_lane(in_grp & (lg == l1))
    rest = in_grp & (lane != i1)
    l2 = jnp.max(jnp.where(rest, lg, -jnp.inf), axis=-1, keepdims=True)
    i2 = first_lane(rest & (lg == l2))
    e2 = jnp.exp(l2 - l1)
    w1 = 1.0 / (1.0 + e2)
    w2 = e2 / (1.0 + e2)
    comb_ref[...] = p_g * jnp.where(lane == i1, w1, jnp.where(lane == i2, w2, 0.0))


def _outmix_call(x2, ya2, yb2, gob, wout, gffn, wr_hi, wr_lo):
    t, d = x2.shape
    tm = OUT_TM
    const = lambda *shape: pl.BlockSpec(shape, lambda ti: (0,) * len(shape))
    tok = lambda w: pl.BlockSpec((tm, w), lambda ti: (ti, 0))
    return pl.pallas_call(
        _outmix_kernel,
        grid=(t // tm,),
        in_specs=[tok(d), tok(D_A), tok(D_B), const(1, D_B), const(d, d), const(1, d),
                  const(d, ROUTER_LANES), const(d, ROUTER_LANES)],
        out_specs=[tok(d), tok(d), tok(ROUTER_LANES)],
        out_shape=[jax.ShapeDtypeStruct((t, d), F32), jax.ShapeDtypeStruct((t, d), BF16),
                   jax.ShapeDtypeStruct((t, ROUTER_LANES), F32)],
        compiler_params=pltpu.CompilerParams(
            dimension_semantics=("parallel",), vmem_limit_bytes=VMEM_LIMIT),
        name="outmix",
    )(x2, ya2, yb2, gob, wout, gffn, wr_hi, wr_lo)


def _moe_kernel(hn_ref, comb_ref, h_ref, wg_ref, wu_ref, wd_ref, p_ref, wple_ref, gple_ref, wpg_ref,
                o_ref, acc_ref):
    e = pl.program_id(1)

    @pl.when(e == 0)
    def _():
        acc_ref[...] = h_ref[...]

    hn = hn_ref[...]
    lane = lax.broadcasted_iota(jnp.int32, comb_ref.shape, 1)
    c = jnp.sum(jnp.where(lane == e + N_GROUPS, comb_ref[...], 0.0), axis=-1, keepdims=True)
    a = jax.nn.silu(_dot(hn, wg_ref[0])) * _dot(hn, wu_ref[0]) * c
    acc_ref[...] += _dot(a.astype(BF16), wd_ref[0])

    @pl.when(e == pl.num_programs(1) - 1)
    def _():
        h = acc_ref[...]
        gate = jax.nn.sigmoid(_dot(_rms(h, gple_ref[...]).astype(BF16), wpg_ref[...]))
        o_ref[...] = h + _dot(p_ref[...].astype(BF16), wple_ref[...]) * gate


def _moe_call(hn, comb, h, wg, wu, wd, p2, wple, gple, wpg):
    t, d = h.shape
    tm = MOE_TM
    const = lambda *shape: pl.BlockSpec(shape, lambda ti, ei: (0,) * len(shape))
    tok = lambda w: pl.BlockSpec((tm, w), lambda ti, ei: (ti, 0))
    return pl.pallas_call(
        _moe_kernel,
        grid=(t // tm, N_EXPERTS),
        in_specs=[tok(d), tok(ROUTER_LANES), tok(d),
                  pl.BlockSpec((1, d, D_FF_EXP), lambda ti, ei: (ei, 0, 0)),
                  pl.BlockSpec((1, d, D_FF_EXP), lambda ti, ei: (ei, 0, 0)),
                  pl.BlockSpec((1, D_FF_EXP, d), lambda ti, ei: (ei, 0, 0)),
                  tok(D_PLE), const(D_PLE, d), const(1, d), const(d, d)],
        out_specs=tok(d),
        out_shape=jax.ShapeDtypeStruct((t, d), F32),
        scratch_shapes=[pltpu.VMEM((tm, d), F32)],
        compiler_params=pltpu.CompilerParams(
            dimension_semantics=("parallel", "arbitrary"), vmem_limit_bytes=VMEM_LIMIT),
        name="moe",
    )(hn, comb, h, wg, wu, wd, p2, wple, gple, wpg)


def _layer(h, p_i, g_mix, w_in, g_v_a, w_s, b_s, g_q, g_k, g_out_a, g_out_b, w_out, g_ffn, w_group,
           w_expert, w_gate_e, w_up_e, w_down_e, g_ple, w_ple, w_ple_gate):
    b, s, d = h.shape
    t = b * s
    row = lambda g: g.reshape(1, -1).astype(F32)
    wuv = w_in[:, :2 * D_A].astype(BF16)
    wqt = w_in[:, 2 * D_A:2 * D_A + D_B].T.astype(BF16)
    wk = w_in[:, 2 * D_A + D_B:2 * D_A + 2 * D_B].astype(BF16)
    wvt = w_in[:, 2 * D_A + 2 * D_B:].T.astype(BF16)
    bs = jnp.broadcast_to(b_s[:, :, None], (A_GROUPS, A_CHUNK, A_DH)).astype(F32)
    gk = row(jnp.tile(g_k, B_HEADS))
    gq = (jnp.tile(g_q, B_HEADS) * (1.0 / float(B_DH) ** 0.5)).reshape(D_B, 1).astype(F32)
    head = jnp.arange(D_B) // B_DH
    bd = jnp.where(head[:, None] == head[None, :], 1.0 / B_DH, 0.0).astype(BF16)
    w_r = jnp.zeros((d, ROUTER_LANES), F32)
    w_r = w_r.at[:, :N_GROUPS].set(w_group).at[:, N_GROUPS:N_GROUPS + N_EXPERTS].set(w_expert)
    wr_hi, wr_lo = _split_bf16(w_r)

    ya, k, qt, vt = _proj_call(h, row(g_mix), wuv, wk, wqt, wvt, row(g_v_a), w_s.astype(F32), bs,
                               row(g_out_a), gk, gq, bd)
    yb = _moba_call(qt, k, vt)
    h1, hn, comb = _outmix_call(h.reshape(t, d), ya.reshape(t, D_A), yb.reshape(t, D_B), row(g_out_b),
                                w_out.astype(BF16), row(g_ffn), wr_hi, wr_lo)
    out = _moe_call(hn, comb, h1, w_gate_e.astype(BF16), w_up_e.astype(BF16), w_down_e.astype(BF16),
                    p_i.reshape(t, D_PLE), w_ple.astype(BF16), row(g_ple), w_ple_gate.astype(BF16))
    return out.reshape(b, s, d)


def kernel(x, p, g_mix, w_in, g_v_a, w_s, b_s, g_q, g_k, g_out_a, g_out_b, w_out, g_ffn, w_group, w_expert,
           w_gate_e, w_up_e, w_down_e, g_ple, w_ple, w_ple_gate):
    params = (g_mix, w_in, g_v_a, w_s, b_s, g_q, g_k, g_out_a, g_out_b, w_out, g_ffn, w_group, w_expert,
              w_gate_e, w_up_e, w_down_e, g_ple, w_ple, w_ple_gate)
    h = x
    for i in range(p.shape[0]):
        h = _layer(h, p[i], *(w[i] for w in params))
    return h
```

```python
import functools

import jax
import jax.numpy as jnp
from jax import lax
from jax.experimental import pallas as pl
from jax.experimental.pallas import tpu as pltpu

F32 = jnp.float32
BF16 = jnp.bfloat16

D_MODEL = 1024
D_A = 512
A_GROUPS = 4
A_DH = D_A // A_GROUPS
A_CHUNK = 128
D_B = 512
B_HEADS = 8
B_DH = D_B // B_HEADS
B_BLOCK = 256
B_TOPK = 3
D_PLE = 256
N_GROUPS = 4
EXP_PER_GROUP = 8
N_EXPERTS = N_GROUPS * EXP_PER_GROUP
D_FF_EXP = 256
EPS = 1e-6
NEG = -1e30

LANES = 128
HEAD_PAIR = 2 * B_DH
N_PAIRS = B_HEADS // 2
ROUTER_LANES = LANES
VMEM_LIMIT = 56 * 1024 * 1024

PROJ_TM = 512
OUT_TM = 512
MOE_TM = 1024


def _rms(x, g):
    return x * lax.rsqrt(jnp.mean(x * x, axis=-1, keepdims=True) + EPS) * g


def _dot(a, b):
    return jnp.dot(a, b, preferred_element_type=F32)


def _dot_nt(a, b):
    return lax.dot_general(a, b, (((1,), (1,)), ((), ())), preferred_element_type=F32)


def _split_bf16(x):
    hi = x.astype(BF16)
    lo = (x - hi.astype(F32)).astype(BF16)
    return hi, lo


def _proj_kernel(x_ref, gmix_ref, wuv_ref, wk_ref, wqt_ref, wvt_ref, gva_ref, ws_ref, bs_ref,
                 goa_ref, gk_ref, gq_ref, bd_ref,
                 ya_ref, k_ref, qt_ref, vt_ref, prod_ref):
    tm = x_ref.shape[1]
    n_chunks = tm // A_CHUNK
    xn = _rms(x_ref[0], gmix_ref[...]).astype(BF16)

    uv = _dot(xn, wuv_ref[...])
    u = jax.nn.gelu(uv[:, :D_A])
    v = jax.nn.gelu(uv[:, D_A:])
    row = lax.broadcasted_iota(jnp.int32, (A_CHUNK, A_CHUNK), 0)
    col = lax.broadcasted_iota(jnp.int32, (A_CHUNK, A_CHUNK), 1)
    causal = col <= row
    for g in range(A_GROUPS):
        gs = slice(g * A_DH, (g + 1) * A_DH)
        vn = _rms(v[:, gs], gva_ref[:, gs]).astype(BF16)
        rhs = jnp.concatenate([vn[c * A_CHUNK:(c + 1) * A_CHUNK] for c in range(n_chunks)], axis=1)
        ws = jnp.where(causal, ws_ref[g], 0.0).astype(BF16)
        mixed = _dot(ws, rhs)
        for c in range(n_chunks):
            cs = slice(c * A_CHUNK, (c + 1) * A_CHUNK)
            prod_ref[cs, gs] = u[cs, gs] * (mixed[:, cs] + bs_ref[g])
    ya_ref[0] = _rms(prod_ref[...], goa_ref[...]).astype(ya_ref.dtype)

    k = _dot(xn, wk_ref[...])
    k2_hi, k2_lo = _split_bf16(k * k)
    k_ms = _dot(k2_hi, bd_ref[...]) + _dot(k2_lo, bd_ref[...])
    k_ref[0] = (k * lax.rsqrt(k_ms + EPS) * gk_ref[...]).astype(k_ref.dtype)

    qt = _dot_nt(wqt_ref[...], xn)
    vt = _dot_nt(wvt_ref[...], xn).astype(vt_ref.dtype)
    qn = []
    for h in range(B_HEADS):
        qh = qt[h * B_DH:(h + 1) * B_DH]
        qn.append(qh * lax.rsqrt(jnp.mean(qh * qh, axis=0, keepdims=True) + EPS))
    qn = (jnp.concatenate(qn, axis=0) * gq_ref[...]).astype(qt_ref.dtype)
    for p in range(N_PAIRS):
        ps = slice(p * HEAD_PAIR, (p + 1) * HEAD_PAIR)
        for j in range(tm // B_BLOCK):
            js = slice(j * B_BLOCK, (j + 1) * B_BLOCK)
            qt_ref[0, p, j] = qn[ps, js]
            vt_ref[0, p, j] = vt[ps, js]


def _proj_call(x, gmix, wuv, wk, wqt, wvt, gva, ws, bs, goa, gk, gq, bd):
    b, s, d = x.shape
    tm = PROJ_TM
    nb = s // B_BLOCK
    jb = tm // B_BLOCK
    const = lambda *shape: pl.BlockSpec(shape, lambda bi, ti: (0,) * len(shape))
    return pl.pallas_call(
        _proj_kernel,
        grid=(b, s // tm),
        in_specs=[
            pl.BlockSpec((1, tm, d), lambda bi, ti: (bi, ti, 0)),
            const(1, d), const(d, 2 * D_A), const(d, D_B), const(D_B, d), const(D_B, d),
            const(1, D_A), const(A_GROUPS, A_CHUNK, A_CHUNK), const(A_GROUPS, A_CHUNK, A_DH),
            const(1, D_A), const(1, D_B), const(D_B, 1), const(D_B, D_B),
        ],
        out_specs=[
            pl.BlockSpec((1, tm, D_A), lambda bi, ti: (bi, ti, 0)),
            pl.BlockSpec((1, tm, D_B), lambda bi, ti: (bi, ti, 0)),
            pl.BlockSpec((1, N_PAIRS, jb, HEAD_PAIR, B_BLOCK), lambda bi, ti: (bi, 0, ti, 0, 0)),
            pl.BlockSpec((1, N_PAIRS, jb, HEAD_PAIR, B_BLOCK), lambda bi, ti: (bi, 0, ti, 0, 0)),
        ],
        out_shape=[
            jax.ShapeDtypeStruct((b, s, D_A), BF16),
            jax.ShapeDtypeStruct((b, s, D_B), BF16),
            jax.ShapeDtypeStruct((b, N_PAIRS, nb, HEAD_PAIR, B_BLOCK), BF16),
            jax.ShapeDtypeStruct((b, N_PAIRS, nb, HEAD_PAIR, B_BLOCK), BF16),
        ],
        scratch_shapes=[pltpu.VMEM((tm, D_A), F32)],
        compiler_params=pltpu.CompilerParams(
            dimension_semantics=("parallel", "parallel"), vmem_limit_bytes=VMEM_LIMIT),
        name="proj",
    )(x, gmix, wuv, wk, wqt, wvt, gva, ws, bs, goa, gk, gq, bd)


def _moba_kernel(qt_ref, k_ref, vt_ref, gob_ref, y_ref, kmean_ref, qz_ref, sel_ref, m_ref, l_ref, acc_ref,
                 s_ref):
    qb = pl.program_id(1)
    nb = k_ref.shape[1] // B_BLOCK
    nbp = sel_ref.shape[1]

    @pl.when(qb == 0)
    def _():
        rows = [jnp.mean(k_ref[0, j * B_BLOCK:(j + 1) * B_BLOCK, :].astype(F32), axis=0, keepdims=True)
                for j in range(nb)]
        rows += [jnp.zeros((1, D_B), F32)] * (nbp - nb)
        km = jnp.concatenate(rows, axis=0)
        hi = km.astype(BF16)
        rest = km - hi.astype(F32)
        mid = rest.astype(BF16)
        lo = (rest - mid.astype(F32)).astype(BF16)
        for p in range(N_PAIRS):
            ps = slice(p * HEAD_PAIR, (p + 1) * HEAD_PAIR)
            kmean_ref[p] = jnp.concatenate([hi[:, ps], mid[:, ps], lo[:, ps]], axis=0)

    head_row = lax.broadcasted_iota(jnp.int32, (HEAD_PAIR, B_BLOCK), 0) // B_DH
    blk = lax.broadcasted_iota(jnp.int32, (nbp, B_BLOCK), 0)
    past = blk < qb

    for h in range(B_HEADS):
        p, hh = divmod(h, 2)
        q_pair = qt_ref[0, p, 0]
        qz = jnp.where(head_row == hh, q_pair, jnp.zeros_like(q_pair))
        qz_ref[h] = qz
        g3 = _dot(kmean_ref[p], qz)
        gate = (g3[:nbp] + g3[nbp:2 * nbp] + g3[2 * nbp:]) * float(B_DH) ** 0.5
        gate = jnp.where(past, gate, NEG)
        rank = jnp.zeros((nbp, B_BLOCK), F32)
        for j in range(nb):
            gj = gate[j:j + 1]
            ahead = (gj > gate) | ((gj == gate) & (blk > j))
            rank = rank + jnp.where(ahead, 1.0, 0.0)
        chosen = (rank < float(B_TOPK)) & (gate > NEG * 0.5)
        sel_ref[h] = jnp.where(chosen, 1.0, 0.0)

    def scores(h, j):
        p = h // 2
        kj = k_ref[0, pl.ds(pl.multiple_of(j * B_BLOCK, B_BLOCK), B_BLOCK), p * HEAD_PAIR:(p + 1) * HEAD_PAIR]
        return _dot(kj, qz_ref[h])

    def values(h, j):
        p, hh = divmod(h, 2)
        return vt_ref[0, p, j, hh * B_DH:(hh + 1) * B_DH, :]

    key_pos = lax.broadcasted_iota(jnp.int32, (B_BLOCK, B_BLOCK), 0)
    q_pos = lax.broadcasted_iota(jnp.int32, (B_BLOCK, B_BLOCK), 1)
    causal = key_pos <= q_pos
    for h in range(B_HEADS):
        s = jnp.where(causal, scores(h, qb), NEG)
        s_ref[h, qb] = s
        m_ref[h] = jnp.max(s, axis=0, keepdims=True)
        l_ref[h] = jnp.zeros((1, B_BLOCK), F32)
        acc_ref[h] = jnp.zeros((B_DH, B_BLOCK), F32)

    def score_pass(j, carry):
        for h in range(B_HEADS):
            s = jnp.where(sel_ref[h, pl.ds(j, 1), :] > 0.5, scores(h, j), NEG)
            s_ref[h, j] = s
            m_ref[h] = jnp.maximum(m_ref[h], jnp.max(s, axis=0, keepdims=True))
        return carry

    lax.fori_loop(0, qb, score_pass, 0)

    def value_pass(j, carry):
        for h in range(B_HEADS):
            pr = jnp.exp(s_ref[h, j] - m_ref[h])
            l_ref[h] += jnp.sum(pr, axis=0, keepdims=True)
            acc_ref[h] += _dot(values(h, j), pr.astype(BF16))
        return carry

    lax.fori_loop(0, qb + 1, value_pass, 0)

    yt = jnp.concatenate([acc_ref[h] / l_ref[h] for h in range(B_HEADS)], axis=0)
    yt = yt * lax.rsqrt(jnp.mean(yt * yt, axis=0, keepdims=True) + EPS)
    y_ref[0] = (yt.T * gob_ref[...]).astype(y_ref.dtype)


def _moba_call(qt, k, vt, gob):
    b, s, _ = k.shape
    nb = s // B_BLOCK
    nbp = -(-nb // 8) * 8
    return pl.pallas_call(
        _moba_kernel,
        grid=(b, nb),
        in_specs=[
            pl.BlockSpec((1, N_PAIRS, 1, HEAD_PAIR, B_BLOCK), lambda bi, qi: (bi, 0, qi, 0, 0)),
            pl.BlockSpec((1, s, D_B), lambda bi, qi: (bi, 0, 0)),
            pl.BlockSpec((1, N_PAIRS, nb, HEAD_PAIR, B_BLOCK), lambda bi, qi: (bi, 0, 0, 0, 0)),
            pl.BlockSpec((1, D_B), lambda bi, qi: (0, 0)),
        ],
        out_specs=pl.BlockSpec((1, B_BLOCK, D_B), lambda bi, qi: (bi, qi, 0)),
        out_shape=jax.ShapeDtypeStruct((b, s, D_B), BF16),
        scratch_shapes=[
            pltpu.VMEM((N_PAIRS, 3 * nbp, HEAD_PAIR), BF16),
            pltpu.VMEM((B_HEADS, HEAD_PAIR, B_BLOCK), BF16),
            pltpu.VMEM((B_HEADS, nbp, B_BLOCK), F32),
            pltpu.VMEM((B_HEADS, 1, B_BLOCK), F32),
            pltpu.VMEM((B_HEADS, 1, B_BLOCK), F32),
            pltpu.VMEM((B_HEADS, B_DH, B_BLOCK), F32),
            pltpu.VMEM((B_HEADS, nb, B_BLOCK, B_BLOCK), F32),
        ],
        compiler_params=pltpu.CompilerParams(
            dimension_semantics=("parallel", "arbitrary"), vmem_limit_bytes=VMEM_LIMIT),
        name="moba",
    )(qt, k, vt, gob)


def _outmix_kernel(x_ref, ya_ref, yb_ref, wout_ref, gffn_ref, wr_hi_ref, wr_lo_ref,
                   h_ref, hn_ref, comb_ref):
    h = x_ref[...] + _dot(ya_ref[...], wout_ref[:D_A]) + _dot(yb_ref[...], wout_ref[D_A:])
    h_ref[...] = h
    hn = _rms(h, gffn_ref[...])
    hn_ref[...] = hn.astype(hn_ref.dtype)

    hn_hi, hn_lo = _split_bf16(hn)
    lg = _dot(hn_hi, wr_hi_ref[...]) + (_dot(hn_lo, wr_hi_ref[...]) + _dot(hn_hi, wr_lo_ref[...]))
    lane = lax.broadcasted_iota(jnp.int32, lg.shape, 1)
    big = ROUTER_LANES

    def first_lane(mask):
        return jnp.min(jnp.where(mask, lane, big), axis=-1, keepdims=True)

    is_g = lane < N_GROUPS
    g_max = jnp.max(jnp.where(is_g, lg, -jnp.inf), axis=-1, keepdims=True)
    p_g = 1.0 / jnp.sum(jnp.where(is_g, jnp.exp(lg - g_max), 0.0), axis=-1, keepdims=True)
    g_sel = first_lane(is_g & (lg == g_max))
    e_lane = lane - N_GROUPS
    in_grp = (e_lane >= 0) & (e_lane < N_EXPERTS) & (lax.shift_right_arithmetic(e_lane, 3) == g_sel)
    l1 = jnp.max(jnp.where(in_grp, lg, -jnp.inf), axis=-1, keepdims=True)
    i1 = first_lane(in_grp & (lg == l1))
    rest = in_grp & (lane != i1)
    l2 = jnp.max(jnp.where(rest, lg, -jnp.inf), axis=-1, keepdims=True)
    i2 = first_lane(rest & (lg == l2))
    e2 = jnp.exp(l2 - l1)
    w1 = 1.0 / (1.0 + e2)
    w2 = e2 / (1.0 + e2)
    comb_ref[...] = p_g * jnp.where(lane == i1, w1, jnp.where(lane == i2, w2, 0.0))


def _outmix_call(x2, ya2, yb2, wout, gffn, wr_hi, wr_lo):
    t, d = x2.shape
    tm = OUT_TM
    const = lambda *shape: pl.BlockSpec(shape, lambda ti: (0,) * len(shape))
    tok = lambda w: pl.BlockSpec((tm, w), lambda ti: (ti, 0))
    return pl.pallas_call(
        _outmix_kernel,
        grid=(t // tm,),
        in_specs=[tok(d), tok(D_A), tok(D_B), const(d, d), const(1, d),
                  const(d, ROUTER_LANES), const(d, ROUTER_LANES)],
        out_specs=[tok(d), tok(d), tok(ROUTER_LANES)],
        out_shape=[jax.ShapeDtypeStruct((t, d), F32), jax.ShapeDtypeStruct((t, d), BF16),
                   jax.ShapeDtypeStruct((t, ROUTER_LANES), F32)],
        compiler_params=pltpu.CompilerParams(
            dimension_semantics=("parallel",), vmem_limit_bytes=VMEM_LIMIT),
        name="outmix",
    )(x2, ya2, yb2, wout, gffn, wr_hi, wr_lo)


def _moe_kernel(hn_ref, comb_ref, h_ref, wg_ref, wu_ref, wd_ref, p_ref, wple_ref, gple_ref, wpg_ref,
                o_ref, acc_ref):
    e = pl.program_id(1)

    @pl.when(e == 0)
    def _():
        acc_ref[...] = h_ref[...]

    hn = hn_ref[...]
    lane = lax.broadcasted_iota(jnp.int32, comb_ref.shape, 1)
    c = jnp.sum(jnp.where(lane == e + N_GROUPS, comb_ref[...], 0.0), axis=-1, keepdims=True)
    a = jax.nn.silu(_dot(hn, wg_ref[0])) * _dot(hn, wu_ref[0]) * c
    acc_ref[...] += _dot(a.astype(BF16), wd_ref[0])

    @pl.when(e == pl.num_programs(1) - 1)
    def _():
        h = acc_ref[...]
        gate = jax.nn.sigmoid(_dot(_rms(h, gple_ref[...]).astype(BF16), wpg_ref[...]))
        o_ref[...] = h + _dot(p_ref[...].astype(BF16), wple_ref[...]) * gate


def _moe_call(hn, comb, h, wg, wu, wd, p2, wple, gple, wpg):
    t, d = h.shape
    tm = MOE_TM
    const = lambda *shape: pl.BlockSpec(shape, lambda ti, ei: (0,) * len(shape))
    tok = lambda w: pl.BlockSpec((tm, w), lambda ti, ei: (ti, 0))
    return pl.pallas_call(
        _moe_kernel,
        grid=(t // tm, N_EXPERTS),
        in_specs=[tok(d), tok(ROUTER_LANES), tok(d),
                  pl.BlockSpec((1, d, D_FF_EXP), lambda ti, ei: (ei, 0, 0)),
                  pl.BlockSpec((1, d, D_FF_EXP), lambda ti, ei: (ei, 0, 0)),
                  pl.BlockSpec((1, D_FF_EXP, d), lambda ti, ei: (ei, 0, 0)),
                  tok(D_PLE), const(D_PLE, d), const(1, d), const(d, d)],
        out_specs=tok(d),
        out_shape=jax.ShapeDtypeStruct((t, d), F32),
        scratch_shapes=[pltpu.VMEM((tm, d), F32)],
        compiler_params=pltpu.CompilerParams(
            dimension_semantics=("parallel", "arbitrary"), vmem_limit_bytes=VMEM_LIMIT),
        name="moe",
    )(hn, comb, h, wg, wu, wd, p2, wple, gple, wpg)


def _layer(h, p_i, g_mix, w_in, g_v_a, w_s, b_s, g_q, g_k, g_out_a, g_out_b, w_out, g_ffn, w_group,
           w_expert, w_gate_e, w_up_e, w_down_e, g_ple, w_ple, w_ple_gate):
    b, s, d = h.shape
    t = b * s
    row = lambda g: g.reshape(1, -1).astype(F32)
    wuv = w_in[:, :2 * D_A].astype(BF16)
    wqt = w_in[:, 2 * D_A:2 * D_A + D_B].T.astype(BF16)
    wk = w_in[:, 2 * D_A + D_B:2 * D_A + 2 * D_B].astype(BF16)
    wvt = w_in[:, 2 * D_A + 2 * D_B:].T.astype(BF16)
    bs = jnp.broadcast_to(b_s[:, :, None], (A_GROUPS, A_CHUNK, A_DH)).astype(F32)
    gk = row(jnp.tile(g_k, B_HEADS))
    gq = (jnp.tile(g_q, B_HEADS) * (1.0 / float(B_DH) ** 0.5)).reshape(D_B, 1).astype(F32)
    head = jnp.arange(D_B) // B_DH
    bd = jnp.where(head[:, None] == head[None, :], 1.0 / B_DH, 0.0).astype(BF16)
    w_r = jnp.zeros((d, ROUTER_LANES), F32)
    w_r = w_r.at[:, :N_GROUPS].set(w_group).at[:, N_GROUPS:N_GROUPS + N_EXPERTS].set(w_expert)
    wr_hi, wr_lo = _split_bf16(w_r)

    ya, k, qt, vt = _proj_call(h, row(g_mix), wuv, wk, wqt, wvt, row(g_v_a), w_s.astype(F32), bs,
                               row(g_out_a), gk, gq, bd)
    yb = _moba_call(qt, k, vt, row(g_out_b))
    h1, hn, comb = _outmix_call(h.reshape(t, d), ya.reshape(t, D_A), yb.reshape(t, D_B),
                                w_out.astype(BF16), row(g_ffn), wr_hi, wr_lo)
    out = _moe_call(hn, comb, h1, w_gate_e.astype(BF16), w_up_e.astype(BF16), w_down_e.astype(BF16),
                    p_i.reshape(t, D_PLE), w_ple.astype(BF16), row(g_ple), w_ple_gate.astype(BF16))
    return out.reshape(b, s, d)


def kernel(x, p, g_mix, w_in, g_v_a, w_s, b_s, g_q, g_k, g_out_a, g_out_b, w_out, g_ffn, w_group, w_expert,
           w_gate_e, w_up_e, w_down_e, g_ple, w_ple, w_ple_gate):
    params = (g_mix, w_in, g_v_a, w_s, b_s, g_q, g_k, g_out_a, g_out_b, w_out, g_ffn, w_group, w_expert,
              w_gate_e, w_up_e, w_down_e, g_ple, w_ple, w_ple_gate)
    h = x
    for i in range(p.shape[0]):
        h = _layer(h, p[i], *(w[i] for w in params))
    return h
```

```python
import functools

import jax
import jax.numpy as jnp
from jax import lax
from jax.experimental import pallas as pl
from jax.experimental.pallas import tpu as pltpu

F32 = jnp.float32
BF16 = jnp.bfloat16

D_MODEL = 1024
D_A = 512
A_GROUPS = 4
A_DH = D_A // A_GROUPS
A_CHUNK = 128
D_B = 512
B_HEADS = 8
B_DH = D_B // B_HEADS
B_BLOCK = 256
B_TOPK = 3
D_PLE = 256
N_GROUPS = 4
EXP_PER_GROUP = 8
N_EXPERTS = N_GROUPS * EXP_PER_GROUP
D_FF_EXP = 256
EPS = 1e-6
NEG = -1e30

LANES = 128
HEAD_PAIR = 2 * B_DH
N_PAIRS = B_HEADS // 2
ROUTER_LANES = LANES
VMEM_LIMIT = 56 * 1024 * 1024

PROJ_TM = 512
OUT_TM = 512
EXP_TM = 256
PLE_TM = 256


def _rms(x, g):
    return x * lax.rsqrt(jnp.mean(x * x, axis=-1, keepdims=True) + EPS) * g


def _dot(a, b):
    return jnp.dot(a, b, preferred_element_type=F32)


def _dot_nt(a, b):
    return lax.dot_general(a, b, (((1,), (1,)), ((), ())), preferred_element_type=F32)


def _split_bf16(x):
    hi = x.astype(BF16)
    lo = (x - hi.astype(F32)).astype(BF16)
    return hi, lo


def _proj_kernel(x_ref, gmix_ref, wuv_ref, wk_ref, wqt_ref, wvt_ref, gva_ref, ws_ref, bs_ref,
                 goa_ref, gk_ref, gq_ref, bd_ref,
                 ya_ref, k_ref, qt_ref, vt_ref, prod_ref):
    tm = x_ref.shape[1]
    n_chunks = tm // A_CHUNK
    xn = _rms(x_ref[0], gmix_ref[...]).astype(BF16)

    uv = _dot(xn, wuv_ref[...])
    u = jax.nn.gelu(uv[:, :D_A])
    v = jax.nn.gelu(uv[:, D_A:])
    row = lax.broadcasted_iota(jnp.int32, (A_CHUNK, A_CHUNK), 0)
    col = lax.broadcasted_iota(jnp.int32, (A_CHUNK, A_CHUNK), 1)
    causal = col <= row
    for g in range(A_GROUPS):
        gs = slice(g * A_DH, (g + 1) * A_DH)
        vn = _rms(v[:, gs], gva_ref[:, gs]).astype(BF16)
        rhs = jnp.concatenate([vn[c * A_CHUNK:(c + 1) * A_CHUNK] for c in range(n_chunks)], axis=1)
        ws = jnp.where(causal, ws_ref[g], 0.0).astype(BF16)
        mixed = _dot(ws, rhs)
        for c in range(n_chunks):
            cs = slice(c * A_CHUNK, (c + 1) * A_CHUNK)
            prod_ref[cs, gs] = u[cs, gs] * (mixed[:, cs] + bs_ref[g])
    ya_ref[0] = _rms(prod_ref[...], goa_ref[...]).astype(ya_ref.dtype)

    k = _dot(xn, wk_ref[...])
    k2_hi, k2_lo = _split_bf16(k * k)
    k_ms = _dot(k2_hi, bd_ref[...]) + _dot(k2_lo, bd_ref[...])
    k_ref[0] = (k * lax.rsqrt(k_ms + EPS) * gk_ref[...]).astype(k_ref.dtype)

    qt = _dot_nt(wqt_ref[...], xn)
    vt = _dot_nt(wvt_ref[...], xn).astype(vt_ref.dtype)
    qn = []
    for h in range(B_HEADS):
        qh = qt[h * B_DH:(h + 1) * B_DH]
        qn.append(qh * lax.rsqrt(jnp.mean(qh * qh, axis=0, keepdims=True) + EPS))
    qn = (jnp.concatenate(qn, axis=0) * gq_ref[...]).astype(qt_ref.dtype)
    for p in range(N_PAIRS):
        ps = slice(p * HEAD_PAIR, (p + 1) * HEAD_PAIR)
        for j in range(tm // B_BLOCK):
            js = slice(j * B_BLOCK, (j + 1) * B_BLOCK)
            qt_ref[0, p, j] = qn[ps, js]
            vt_ref[0, p, j] = vt[ps, js]


def _proj_call(x, gmix, wuv, wk, wqt, wvt, gva, ws, bs, goa, gk, gq, bd):
    b, s, d = x.shape
    tm = PROJ_TM
    nb = s // B_BLOCK
    jb = tm // B_BLOCK
    const = lambda *shape: pl.BlockSpec(shape, lambda bi, ti: (0,) * len(shape))
    return pl.pallas_call(
        _proj_kernel,
        grid=(b, s // tm),
        in_specs=[
            pl.BlockSpec((1, tm, d), lambda bi, ti: (bi, ti, 0)),
            const(1, d), const(d, 2 * D_A), const(d, D_B), const(D_B, d), const(D_B, d),
            const(1, D_A), const(A_GROUPS, A_CHUNK, A_CHUNK), const(A_GROUPS, A_CHUNK, A_DH),
            const(1, D_A), const(1, D_B), const(D_B, 1), const(D_B, D_B),
        ],
        out_specs=[
            pl.BlockSpec((1, tm, D_A), lambda bi, ti: (bi, ti, 0)),
            pl.BlockSpec((1, tm, D_B), lambda bi, ti: (bi, ti, 0)),
            pl.BlockSpec((1, N_PAIRS, jb, HEAD_PAIR, B_BLOCK), lambda bi, ti: (bi, 0, ti, 0, 0)),
            pl.BlockSpec((1, N_PAIRS, jb, HEAD_PAIR, B_BLOCK), lambda bi, ti: (bi, 0, ti, 0, 0)),
        ],
        out_shape=[
            jax.ShapeDtypeStruct((b, s, D_A), BF16),
            jax.ShapeDtypeStruct((b, s, D_B), BF16),
            jax.ShapeDtypeStruct((b, N_PAIRS, nb, HEAD_PAIR, B_BLOCK), BF16),
            jax.ShapeDtypeStruct((b, N_PAIRS, nb, HEAD_PAIR, B_BLOCK), BF16),
        ],
        scratch_shapes=[pltpu.VMEM((tm, D_A), F32)],
        compiler_params=pltpu.CompilerParams(
            dimension_semantics=("parallel", "parallel"), vmem_limit_bytes=VMEM_LIMIT),
        name="proj",
    )(x, gmix, wuv, wk, wqt, wvt, gva, ws, bs, goa, gk, gq, bd)


def _moba_kernel(qt_ref, k_ref, vt_ref, gob_ref, y_ref, kmean_ref, qz_ref, sel_ref, m_ref, l_ref, acc_ref,
                 s_ref):
    qb = pl.program_id(1)
    nb = k_ref.shape[1] // B_BLOCK
    nbp = sel_ref.shape[1]

    @pl.when(qb == 0)
    def _():
        rows = [jnp.mean(k_ref[0, j * B_BLOCK:(j + 1) * B_BLOCK, :].astype(F32), axis=0, keepdims=True)
                for j in range(nb)]
        rows += [jnp.zeros((1, D_B), F32)] * (nbp - nb)
        km = jnp.concatenate(rows, axis=0)
        hi = km.astype(BF16)
        rest = km - hi.astype(F32)
        mid = rest.astype(BF16)
        lo = (rest - mid.astype(F32)).astype(BF16)
        for p in range(N_PAIRS):
            ps = slice(p * HEAD_PAIR, (p + 1) * HEAD_PAIR)
            kmean_ref[p] = jnp.concatenate([hi[:, ps], mid[:, ps], lo[:, ps]], axis=0)

    head_row = lax.broadcasted_iota(jnp.int32, (HEAD_PAIR, B_BLOCK), 0) // B_DH
    blk = lax.broadcasted_iota(jnp.int32, (nbp, B_BLOCK), 0)
    past = blk < qb

    for h in range(B_HEADS):
        p, hh = divmod(h, 2)
        q_pair = qt_ref[0, p, 0]
        qz = jnp.where(head_row == hh, q_pair, jnp.zeros_like(q_pair))
        qz_ref[h] = qz
        g3 = _dot(kmean_ref[p], qz)
        gate = (g3[:nbp] + g3[nbp:2 * nbp] + g3[2 * nbp:]) * float(B_DH) ** 0.5
        gate = jnp.where(past, gate, NEG)
        rank = jnp.zeros((nbp, B_BLOCK), F32)
        for j in range(nb):
            gj = gate[j:j + 1]
            ahead = (gj > gate) | ((gj == gate) & (blk > j))
            rank = rank + jnp.where(ahead, 1.0, 0.0)
        chosen = (rank < float(B_TOPK)) & (gate > NEG * 0.5)
        sel_ref[h] = jnp.where(chosen, 1.0, 0.0)

    def scores(h, j):
        p = h // 2
        kj = k_ref[0, pl.ds(pl.multiple_of(j * B_BLOCK, B_BLOCK), B_BLOCK), p * HEAD_PAIR:(p + 1) * HEAD_PAIR]
        return _dot(kj, qz_ref[h])

    def values(h, j):
        p, hh = divmod(h, 2)
        return vt_ref[0, p, j, hh * B_DH:(hh + 1) * B_DH, :]

    key_pos = lax.broadcasted_iota(jnp.int32, (B_BLOCK, B_BLOCK), 0)
    q_pos = lax.broadcasted_iota(jnp.int32, (B_BLOCK, B_BLOCK), 1)
    causal = key_pos <= q_pos
    for h in range(B_HEADS):
        s = jnp.where(causal, scores(h, qb), NEG)
        s_ref[h, qb] = s
        m_ref[h] = jnp.max(s, axis=0, keepdims=True)
        l_ref[h] = jnp.zeros((1, B_BLOCK), F32)
        acc_ref[h] = jnp.zeros((B_DH, B_BLOCK), F32)

    def score_pass(j, carry):
        for h in range(B_HEADS):
            s = jnp.where(sel_ref[h, pl.ds(j, 1), :] > 0.5, scores(h, j), NEG)
            s_ref[h, j] = s
            m_ref[h] = jnp.maximum(m_ref[h], jnp.max(s, axis=0, keepdims=True))
        return carry

    lax.fori_loop(0, qb, score_pass, 0)

    def value_pass(j, carry):
        for h in range(B_HEADS):
            pr = jnp.exp(s_ref[h, j] - m_ref[h])
            l_ref[h] += jnp.sum(pr, axis=0, keepdims=True)
            acc_ref[h] += _dot(values(h, j), pr.astype(BF16))
        return carry

    lax.fori_loop(0, qb + 1, value_pass, 0)

    yt = jnp.concatenate([acc_ref[h] / l_ref[h] for h in range(B_HEADS)], axis=0)
    yt = yt * lax.rsqrt(jnp.mean(yt * yt, axis=0, keepdims=True) + EPS)
    y_ref[0] = (yt.T * gob_ref[...]).astype(y_ref.dtype)


def _moba_call(qt, k, vt, gob):
    b, s, _ = k.shape
    nb = s // B_BLOCK
    nbp = -(-nb // 8) * 8
    return pl.pallas_call(
        _moba_kernel,
        grid=(b, nb),
        in_specs=[
            pl.BlockSpec((1, N_PAIRS, 1, HEAD_PAIR, B_BLOCK), lambda bi, qi: (bi, 0, qi, 0, 0)),
            pl.BlockSpec((1, s, D_B), lambda bi, qi: (bi, 0, 0)),
            pl.BlockSpec((1, N_PAIRS, nb, HEAD_PAIR, B_BLOCK), lambda bi, qi: (bi, 0, 0, 0, 0)),
            pl.BlockSpec((1, D_B), lambda bi, qi: (0, 0)),
        ],
        out_specs=pl.BlockSpec((1, B_BLOCK, D_B), lambda bi, qi: (bi, qi, 0)),
        out_shape=jax.ShapeDtypeStruct((b, s, D_B), BF16),
        scratch_shapes=[
            pltpu.VMEM((N_PAIRS, 3 * nbp, HEAD_PAIR), BF16),
            pltpu.VMEM((B_HEADS, HEAD_PAIR, B_BLOCK), BF16),
            pltpu.VMEM((B_HEADS, nbp, B_BLOCK), F32),
            pltpu.VMEM((B_HEADS, 1, B_BLOCK), F32),
            pltpu.VMEM((B_HEADS, 1, B_BLOCK), F32),
            pltpu.VMEM((B_HEADS, B_DH, B_BLOCK), F32),
            pltpu.VMEM((B_HEADS, nb, B_BLOCK, B_BLOCK), F32),
        ],
        compiler_params=pltpu.CompilerParams(
            dimension_semantics=("parallel", "arbitrary"), vmem_limit_bytes=VMEM_LIMIT),
        name="moba",
    )(qt, k, vt, gob)


def _outmix_kernel(x_ref, ya_ref, yb_ref, wout_ref, gffn_ref, wr_hi_ref, wr_lo_ref,
                   h_ref, hn_ref, route_ref):
    h = x_ref[...] + _dot(ya_ref[...], wout_ref[:D_A]) + _dot(yb_ref[...], wout_ref[D_A:])
    h_ref[...] = h
    hn = _rms(h, gffn_ref[...])
    hn_ref[...] = hn.astype(hn_ref.dtype)

    hn_hi, hn_lo = _split_bf16(hn)
    lg = _dot(hn_hi, wr_hi_ref[...]) + (_dot(hn_lo, wr_hi_ref[...]) + _dot(hn_hi, wr_lo_ref[...]))
    lane = lax.broadcasted_iota(jnp.int32, lg.shape, 1)
    big = ROUTER_LANES

    def first_lane(mask):
        return jnp.min(jnp.where(mask, lane, big), axis=-1, keepdims=True)

    is_g = lane < N_GROUPS
    g_max = jnp.max(jnp.where(is_g, lg, -jnp.inf), axis=-1, keepdims=True)
    p_g = 1.0 / jnp.sum(jnp.where(is_g, jnp.exp(lg - g_max), 0.0), axis=-1, keepdims=True)
    g_sel = first_lane(is_g & (lg == g_max))
    e_lane = lane - N_GROUPS
    in_grp = (e_lane >= 0) & (e_lane < N_EXPERTS) & (lax.shift_right_arithmetic(e_lane, 3) == g_sel)
    l1 = jnp.max(jnp.where(in_grp, lg, -jnp.inf), axis=-1, keepdims=True)
    i1 = first_lane(in_grp & (lg == l1))
    rest = in_grp & (lane != i1)
    l2 = jnp.max(jnp.where(rest, lg, -jnp.inf), axis=-1, keepdims=True)
    i2 = first_lane(rest & (lg == l2))
    e2 = jnp.exp(l2 - l1)
    w1 = 1.0 / (1.0 + e2)
    w2 = e2 / (1.0 + e2)
    rec = jnp.where(lane == 0, (i1 - N_GROUPS).astype(F32),
                    jnp.where(lane == 1, (i2 - N_GROUPS).astype(F32),
                              jnp.where(lane == 2, p_g * w1, jnp.where(lane == 3, p_g * w2, 0.0))))
    route_ref[...] = rec


def _outmix_call(x2, ya2, yb2, wout, gffn, wr_hi, wr_lo):
    t, d = x2.shape
    tm = OUT_TM
    const = lambda *shape: pl.BlockSpec(shape, lambda ti: (0,) * len(shape))
    tok = lambda w: pl.BlockSpec((tm, w), lambda ti: (ti, 0))
    return pl.pallas_call(
        _outmix_kernel,
        grid=(t // tm,),
        in_specs=[tok(d), tok(D_A), tok(D_B), const(d, d), const(1, d),
                  const(d, ROUTER_LANES), const(d, ROUTER_LANES)],
        out_specs=[tok(d), tok(d), tok(ROUTER_LANES)],
        out_shape=[jax.ShapeDtypeStruct((t, d), F32), jax.ShapeDtypeStruct((t, d), F32),
                   jax.ShapeDtypeStruct((t, ROUTER_LANES), F32)],
        compiler_params=pltpu.CompilerParams(
            dimension_semantics=("parallel",), vmem_limit_bytes=VMEM_LIMIT),
        name="outmix",
    )(x2, ya2, yb2, wout, gffn, wr_hi, wr_lo)


def _start_row_gather(idx_ref, src_hbm, dst_ref, sem):
    for r in range(dst_ref.shape[0]):
        row = idx_ref[0, 0, r]
        pltpu.make_async_copy(src_hbm.at[pl.ds(row, 1), :], dst_ref.at[pl.ds(r, 1), :], sem).start(priority=r % 2)


def _wait_row_gather(src_hbm, dst_ref, sem):
    pltpu.make_async_copy(src_hbm.at[pl.ds(0, dst_ref.shape[0]), :], dst_ref, sem).wait()


def _experts_kernel(tile_e_ref, n_tiles_ref, cur_ref, nxt_ref, hn_hbm, wg_ref, wu_ref, wd_ref,
                    y_ref, xbuf, sem):
    del tile_e_ref
    i = pl.program_id(0)
    n_used = n_tiles_ref[0]
    slot = lax.rem(i, 2)

    @pl.when(i == 0)
    def _():
        _start_row_gather(cur_ref, hn_hbm, xbuf.at[0], sem.at[0])

    @pl.when(i + 1 < n_used)
    def _():
        _start_row_gather(nxt_ref, hn_hbm, xbuf.at[1 - slot], sem.at[1 - slot])

    @pl.when(i < n_used)
    def _():
        _wait_row_gather(hn_hbm, xbuf.at[slot], sem.at[slot])
        x = xbuf[slot].astype(BF16)
        a = jax.nn.silu(_dot(x, wg_ref[0])) * _dot(x, wu_ref[0])
        y_ref[...] = _dot(a.astype(BF16), wd_ref[0])

    @pl.when(i >= n_used)
    def _():
        y_ref[...] = jnp.zeros_like(y_ref)


def _experts_call(tile_e, n_tiles, src_rows, hn, wg, wu, wd):
    t, d = hn.shape
    tm = EXP_TM
    nt = src_rows.shape[0]
    idx_spec = lambda f: pl.BlockSpec((1, 1, tm), lambda i, te, n: (f(i), 0, 0), memory_space=pltpu.SMEM)
    w_spec = lambda *shape: pl.BlockSpec((1,) + shape, lambda i, te, n: (te[i], 0, 0))
    return pl.pallas_call(
        _experts_kernel,
        grid_spec=pltpu.PrefetchScalarGridSpec(
            num_scalar_prefetch=2,
            grid=(nt,),
            in_specs=[idx_spec(lambda i: i), idx_spec(lambda i: jnp.minimum(i + 1, nt - 1)),
                      pl.BlockSpec(memory_space=pl.ANY),
                      w_spec(d, D_FF_EXP), w_spec(d, D_FF_EXP), w_spec(D_FF_EXP, d)],
            out_specs=pl.BlockSpec((tm, d), lambda i, te, n: (i, 0)),
            scratch_shapes=[pltpu.VMEM((2, tm, d), F32), pltpu.SemaphoreType.DMA((2,))],
        ),
        out_shape=jax.ShapeDtypeStruct((nt * tm, d), F32),
        compiler_params=pltpu.CompilerParams(
            dimension_semantics=("arbitrary",), vmem_limit_bytes=VMEM_LIMIT),
        name="experts",
    )(tile_e, n_tiles, src_rows, src_rows, hn, wg, wu, wd)


def _ple_kernel(cur_ref, nxt_ref, ys_hbm, h_ref, route_ref, p_ref, wple_ref, gple_ref, wpg_ref,
                o_ref, ybuf, sem):
    tm = h_ref.shape[0]
    i = pl.program_id(0)
    slot = lax.rem(i, 2)

    @pl.when(i == 0)
    def _():
        _start_row_gather(cur_ref, ys_hbm, ybuf.at[0], sem.at[0])

    @pl.when(i + 1 < pl.num_programs(0))
    def _():
        _start_row_gather(nxt_ref, ys_hbm, ybuf.at[1 - slot], sem.at[1 - slot])

    _wait_row_gather(ys_hbm, ybuf.at[slot], sem.at[slot])
    rec = route_ref[...]
    h = h_ref[...] + rec[:, 2:3] * ybuf[slot, :tm] + rec[:, 3:4] * ybuf[slot, tm:]
    gate = jax.nn.sigmoid(_dot(_rms(h, gple_ref[...]).astype(BF16), wpg_ref[...]))
    o_ref[...] = h + _dot(p_ref[...].astype(BF16), wple_ref[...]) * gate


def _ple_call(pos_rows, ys, h, route, p2, wple, gple, wpg):
    t, d = h.shape
    tm = PLE_TM
    n = t // tm
    idx_spec = lambda f: pl.BlockSpec((1, 1, 2 * tm), lambda i: (f(i), 0, 0), memory_space=pltpu.SMEM)
    const = lambda *shape: pl.BlockSpec(shape, lambda i: (0,) * len(shape))
    tok = lambda w: pl.BlockSpec((tm, w), lambda i: (i, 0))
    return pl.pallas_call(
        _ple_kernel,
        grid=(n,),
        in_specs=[idx_spec(lambda i: i), idx_spec(lambda i: jnp.minimum(i + 1, n - 1)),
                  pl.BlockSpec(memory_space=pl.ANY),
                  tok(d), tok(ROUTER_LANES), tok(D_PLE), const(D_PLE, d), const(1, d), const(d, d)],
        out_specs=tok(d),
        out_shape=jax.ShapeDtypeStruct((t, d), F32),
        scratch_shapes=[pltpu.VMEM((2, 2 * tm, d), F32), pltpu.SemaphoreType.DMA((2,))],
        compiler_params=pltpu.CompilerParams(
            dimension_semantics=("arbitrary",), vmem_limit_bytes=VMEM_LIMIT),
        name="ple",
    )(pos_rows, pos_rows, ys, h, route, p2, wple, gple, wpg)


def _routing_tables(route, t):
    tm = EXP_TM
    n_pairs = 2 * t
    nt = n_pairs // tm + N_EXPERTS
    e_of_pair = route[:, :2].astype(jnp.int32).reshape(n_pairs)
    onehot = (e_of_pair[:, None] == jnp.arange(N_EXPERTS, dtype=jnp.int32)[None, :]).astype(jnp.int32)
    running = jnp.cumsum(onehot, axis=0)
    rank = jnp.take_along_axis(running, e_of_pair[:, None], axis=1)[:, 0] - 1
    counts = running[-1]
    padded = ((counts + tm - 1) // tm) * tm
    seg_end = jnp.cumsum(padded)
    pos = (seg_end - padded)[e_of_pair] + rank
    src_tok = jnp.zeros((nt * tm,), jnp.int32).at[pos].set(jnp.arange(n_pairs, dtype=jnp.int32) // 2)
    tile_start = jnp.arange(nt, dtype=jnp.int32) * tm
    tile_e = jnp.minimum(jnp.sum(tile_start[:, None] >= seg_end[None, :], axis=1), N_EXPERTS - 1)
    n_tiles = (seg_end[-1] // tm).reshape(1)
    pos_rows = pos.reshape(t // PLE_TM, PLE_TM, 2).transpose(0, 2, 1).reshape(t // PLE_TM, 1, 2 * PLE_TM)
    return tile_e.astype(jnp.int32), n_tiles.astype(jnp.int32), src_tok.reshape(nt, 1, tm), pos_rows


def _layer(h, p_i, g_mix, w_in, g_v_a, w_s, b_s, g_q, g_k, g_out_a, g_out_b, w_out, g_ffn, w_group,
           w_expert, w_gate_e, w_up_e, w_down_e, g_ple, w_ple, w_ple_gate):
    b, s, d = h.shape
    t = b * s
    row = lambda g: g.reshape(1, -1).astype(F32)
    wuv = w_in[:, :2 * D_A].astype(BF16)
    wqt = w_in[:, 2 * D_A:2 * D_A + D_B].T.astype(BF16)
    wk = w_in[:, 2 * D_A + D_B:2 * D_A + 2 * D_B].astype(BF16)
    wvt = w_in[:, 2 * D_A + 2 * D_B:].T.astype(BF16)
    bs = jnp.broadcast_to(b_s[:, :, None], (A_GROUPS, A_CHUNK, A_DH)).astype(F32)
    gk = row(jnp.tile(g_k, B_HEADS))
    gq = (jnp.tile(g_q, B_HEADS) * (1.0 / float(B_DH) ** 0.5)).reshape(D_B, 1).astype(F32)
    head = jnp.arange(D_B) // B_DH
    bd = jnp.where(head[:, None] == head[None, :], 1.0 / B_DH, 0.0).astype(BF16)
    w_r = jnp.zeros((d, ROUTER_LANES), F32)
    w_r = w_r.at[:, :N_GROUPS].set(w_group).at[:, N_GROUPS:N_GROUPS + N_EXPERTS].set(w_expert)
    wr_hi, wr_lo = _split_bf16(w_r)

    ya, k, qt, vt = _proj_call(h, row(g_mix), wuv, wk, wqt, wvt, row(g_v_a), w_s.astype(F32), bs,
                               row(g_out_a), gk, gq, bd)
    yb = _moba_call(qt, k, vt, row(g_out_b))
    h1, hn, route = _outmix_call(h.reshape(t, d), ya.reshape(t, D_A), yb.reshape(t, D_B),
                                 w_out.astype(BF16), row(g_ffn), wr_hi, wr_lo)
    tile_e, n_tiles, src_rows, pos_rows = _routing_tables(route, t)
    ys = _experts_call(tile_e, n_tiles, src_rows, hn,
                       w_gate_e.astype(BF16), w_up_e.astype(BF16), w_down_e.astype(BF16))
    out = _ple_call(pos_rows, ys, h1, route, p_i.reshape(t, D_PLE), w_ple.astype(BF16), row(g_ple),
                    w_ple_gate.astype(BF16))
    return out.reshape(b, s, d)


def kernel(x, p, g_mix, w_in, g_v_a, w_s, b_s, g_q, g_k, g_out_a, g_out_b, w_out, g_ffn, w_group, w_expert,
           w_gate_e, w_up_e, w_down_e, g_ple, w_ple, w_ple_gate):
    params = (g_mix, w_in, g_v_a, w_s, b_s, g_q, g_k, g_out_a, g_out_b, w_out, g_ffn, w_group, w_expert,
              w_gate_e, w_up_e, w_down_e, g_ple, w_ple, w_ple_gate)
    h = x
    for i in range(p.shape[0]):
        h = _layer(h, p[i], *(w[i] for w in params))
    return h
```

```python
import functools

import jax
import jax.numpy as jnp
from jax import lax
from jax.experimental import pallas as pl
from jax.experimental.pallas import tpu as pltpu

F32 = jnp.float32
BF16 = jnp.bfloat16

D_MODEL = 1024
D_A = 512
A_GROUPS = 4
A_DH = D_A // A_GROUPS
A_CHUNK = 128
D_B = 512
B_HEADS = 8
B_DH = D_B // B_HEADS
B_BLOCK = 256
B_TOPK = 3
D_PLE = 256
N_GROUPS = 4
EXP_PER_GROUP = 8
N_EXPERTS = N_GROUPS * EXP_PER_GROUP
D_FF_EXP = 256
EPS = 1e-6
NEG = -1e30

LANES = 128
HEAD_PAIR = 2 * B_DH
N_PAIRS = B_HEADS // 2
ROUTER_LANES = LANES
VMEM_LIMIT = 56 * 1024 * 1024

PROJ_TM = 512
OUT_TM = 512
EXP_TM = 256
TOK_TM = 256
ROW_SUB = D_MODEL // LANES


def _rms(x, g):
    return x * lax.rsqrt(jnp.mean(x * x, axis=-1, keepdims=True) + EPS) * g


def _dot(a, b):
    return jnp.dot(a, b, preferred_element_type=F32)


def _dot_nt(a, b):
    return lax.dot_general(a, b, (((1,), (1,)), ((), ())), preferred_element_type=F32)


def _split_bf16(x):
    hi = x.astype(BF16)
    lo = (x - hi.astype(F32)).astype(BF16)
    return hi, lo


def _proj_kernel(x_ref, gmix_ref, wuv_ref, wk_ref, wqt_ref, wvt_ref, gva_ref, ws_ref, bs_ref,
                 goa_ref, gk_ref, gq_ref, bd_ref,
                 ya_ref, k_ref, qt_ref, vt_ref, prod_ref):
    tm = x_ref.shape[1]
    n_chunks = tm // A_CHUNK
    xn = _rms(x_ref[0], gmix_ref[...]).astype(BF16)

    uv = _dot(xn, wuv_ref[...])
    u = jax.nn.gelu(uv[:, :D_A])
    v = jax.nn.gelu(uv[:, D_A:])
    row = lax.broadcasted_iota(jnp.int32, (A_CHUNK, A_CHUNK), 0)
    col = lax.broadcasted_iota(jnp.int32, (A_CHUNK, A_CHUNK), 1)
    causal = col <= row
    for g in range(A_GROUPS):
        gs = slice(g * A_DH, (g + 1) * A_DH)
        vn = _rms(v[:, gs], gva_ref[:, gs]).astype(BF16)
        rhs = jnp.concatenate([vn[c * A_CHUNK:(c + 1) * A_CHUNK] for c in range(n_chunks)], axis=1)
        ws = jnp.where(causal, ws_ref[g], 0.0).astype(BF16)
        mixed = _dot(ws, rhs)
        for c in range(n_chunks):
            cs = slice(c * A_CHUNK, (c + 1) * A_CHUNK)
            prod_ref[cs, gs] = u[cs, gs] * (mixed[:, cs] + bs_ref[g])
    ya_ref[0] = _rms(prod_ref[...], goa_ref[...]).astype(ya_ref.dtype)

    k = _dot(xn, wk_ref[...])
    k2_hi, k2_lo = _split_bf16(k * k)
    k_ms = _dot(k2_hi, bd_ref[...]) + _dot(k2_lo, bd_ref[...])
    k_ref[0] = (k * lax.rsqrt(k_ms + EPS) * gk_ref[...]).astype(k_ref.dtype)

    qt = _dot_nt(wqt_ref[...], xn)
    vt = _dot_nt(wvt_ref[...], xn).astype(vt_ref.dtype)
    qn = []
    for h in range(B_HEADS):
        qh = qt[h * B_DH:(h + 1) * B_DH]
        qn.append(qh * lax.rsqrt(jnp.mean(qh * qh, axis=0, keepdims=True) + EPS))
    qn = (jnp.concatenate(qn, axis=0) * gq_ref[...]).astype(qt_ref.dtype)
    for p in range(N_PAIRS):
        ps = slice(p * HEAD_PAIR, (p + 1) * HEAD_PAIR)
        for j in range(tm // B_BLOCK):
            js = slice(j * B_BLOCK, (j + 1) * B_BLOCK)
            qt_ref[0, p, j] = qn[ps, js]
            vt_ref[0, p, j] = vt[ps, js]


def _proj_call(x, gmix, wuv, wk, wqt, wvt, gva, ws, bs, goa, gk, gq, bd):
    b, s, d = x.shape
    tm = PROJ_TM
    nb = s // B_BLOCK
    jb = tm // B_BLOCK
    const = lambda *shape: pl.BlockSpec(shape, lambda bi, ti: (0,) * len(shape))
    return pl.pallas_call(
        _proj_kernel,
        grid=(b, s // tm),
        in_specs=[
            pl.BlockSpec((1, tm, d), lambda bi, ti: (bi, ti, 0)),
            const(1, d), const(d, 2 * D_A), const(d, D_B), const(D_B, d), const(D_B, d),
            const(1, D_A), const(A_GROUPS, A_CHUNK, A_CHUNK), const(A_GROUPS, A_CHUNK, A_DH),
            const(1, D_A), const(1, D_B), const(D_B, 1), const(D_B, D_B),
        ],
        out_specs=[
            pl.BlockSpec((1, tm, D_A), lambda bi, ti: (bi, ti, 0)),
            pl.BlockSpec((1, tm, D_B), lambda bi, ti: (bi, ti, 0)),
            pl.BlockSpec((1, N_PAIRS, jb, HEAD_PAIR, B_BLOCK), lambda bi, ti: (bi, 0, ti, 0, 0)),
            pl.BlockSpec((1, N_PAIRS, jb, HEAD_PAIR, B_BLOCK), lambda bi, ti: (bi, 0, ti, 0, 0)),
        ],
        out_shape=[
            jax.ShapeDtypeStruct((b, s, D_A), BF16),
            jax.ShapeDtypeStruct((b, s, D_B), BF16),
            jax.ShapeDtypeStruct((b, N_PAIRS, nb, HEAD_PAIR, B_BLOCK), BF16),
            jax.ShapeDtypeStruct((b, N_PAIRS, nb, HEAD_PAIR, B_BLOCK), BF16),
        ],
        scratch_shapes=[pltpu.VMEM((tm, D_A), F32)],
        compiler_params=pltpu.CompilerParams(
            dimension_semantics=("parallel", "parallel"), vmem_limit_bytes=VMEM_LIMIT),
        name="proj",
    )(x, gmix, wuv, wk, wqt, wvt, gva, ws, bs, goa, gk, gq, bd)


def _moba_kernel(qt_ref, k_ref, vt_ref, gob_ref, y_ref, kmean_ref, qz_ref, sel_ref, m_ref, l_ref, acc_ref,
                 s_ref):
    qb = pl.program_id(1)
    nb = k_ref.shape[1] // B_BLOCK
    nbp = sel_ref.shape[1]

    @pl.when(qb == 0)
    def _():
        rows = [jnp.mean(k_ref[0, j * B_BLOCK:(j + 1) * B_BLOCK, :].astype(F32), axis=0, keepdims=True)
                for j in range(nb)]
        rows += [jnp.zeros((1, D_B), F32)] * (nbp - nb)
        km = jnp.concatenate(rows, axis=0)
        hi = km.astype(BF16)
        rest = km - hi.astype(F32)
        mid = rest.astype(BF16)
        lo = (rest - mid.astype(F32)).astype(BF16)
        for p in range(N_PAIRS):
            ps = slice(p * HEAD_PAIR, (p + 1) * HEAD_PAIR)
            kmean_ref[p] = jnp.concatenate([hi[:, ps], mid[:, ps], lo[:, ps]], axis=0)

    head_row = lax.broadcasted_iota(jnp.int32, (HEAD_PAIR, B_BLOCK), 0) // B_DH
    blk = lax.broadcasted_iota(jnp.int32, (nbp, B_BLOCK), 0)
    past = blk < qb

    for h in range(B_HEADS):
        p, hh = divmod(h, 2)
        q_pair = qt_ref[0, p, 0]
        qz = jnp.where(head_row == hh, q_pair, jnp.zeros_like(q_pair))
        qz_ref[h] = qz
        g3 = _dot(kmean_ref[p], qz)
        gate = (g3[:nbp] + g3[nbp:2 * nbp] + g3[2 * nbp:]) * float(B_DH) ** 0.5
        gate = jnp.where(past, gate, NEG)
        rank = jnp.zeros((nbp, B_BLOCK), F32)
        for j in range(nb):
            gj = gate[j:j + 1]
            ahead = (gj > gate) | ((gj == gate) & (blk > j))
            rank = rank + jnp.where(ahead, 1.0, 0.0)
        chosen = (rank < float(B_TOPK)) & (gate > NEG * 0.5)
        sel_ref[h] = jnp.where(chosen, 1.0, 0.0)

    def scores(h, j):
        p = h // 2
        kj = k_ref[0, pl.ds(pl.multiple_of(j * B_BLOCK, B_BLOCK), B_BLOCK), p * HEAD_PAIR:(p + 1) * HEAD_PAIR]
        return _dot(kj, qz_ref[h])

    def values(h, j):
        p, hh = divmod(h, 2)
        return vt_ref[0, p, j, hh * B_DH:(hh + 1) * B_DH, :]

    key_pos = lax.broadcasted_iota(jnp.int32, (B_BLOCK, B_BLOCK), 0)
    q_pos = lax.broadcasted_iota(jnp.int32, (B_BLOCK, B_BLOCK), 1)
    causal = key_pos <= q_pos
    for h in range(B_HEADS):
        s = jnp.where(causal, scores(h, qb), NEG)
        s_ref[h, qb] = s
        m_ref[h] = jnp.max(s, axis=0, keepdims=True)
        l_ref[h] = jnp.zeros((1, B_BLOCK), F32)
        acc_ref[h] = jnp.zeros((B_DH, B_BLOCK), F32)

    def score_pass(j, carry):
        for h in range(B_HEADS):
            s = jnp.where(sel_ref[h, pl.ds(j, 1), :] > 0.5, scores(h, j), NEG)
            s_ref[h, j] = s
            m_ref[h] = jnp.maximum(m_ref[h], jnp.max(s, axis=0, keepdims=True))
        return carry

    lax.fori_loop(0, qb, score_pass, 0)

    def value_pass(j, carry):
        for h in range(B_HEADS):
            pr = jnp.exp(s_ref[h, j] - m_ref[h])
            l_ref[h] += jnp.sum(pr, axis=0, keepdims=True)
            acc_ref[h] += _dot(values(h, j), pr.astype(BF16))
        return carry

    lax.fori_loop(0, qb + 1, value_pass, 0)

    yt = jnp.concatenate([acc_ref[h] / l_ref[h] for h in range(B_HEADS)], axis=0)
    yt = yt * lax.rsqrt(jnp.mean(yt * yt, axis=0, keepdims=True) + EPS)
    y_ref[0] = (yt.T * gob_ref[...]).astype(y_ref.dtype)


def _moba_call(qt, k, vt, gob):
    b, s, _ = k.shape
    nb = s // B_BLOCK
    nbp = -(-nb // 8) * 8
    return pl.pallas_call(
        _moba_kernel,
        grid=(b, nb),
        in_specs=[
            pl.BlockSpec((1, N_PAIRS, 1, HEAD_PAIR, B_BLOCK), lambda bi, qi: (bi, 0, qi, 0, 0)),
            pl.BlockSpec((1, s, D_B), lambda bi, qi: (bi, 0, 0)),
            pl.BlockSpec((1, N_PAIRS, nb, HEAD_PAIR, B_BLOCK), lambda bi, qi: (bi, 0, 0, 0, 0)),
            pl.BlockSpec((1, D_B), lambda bi, qi: (0, 0)),
        ],
        out_specs=pl.BlockSpec((1, B_BLOCK, D_B), lambda bi, qi: (bi, qi, 0)),
        out_shape=jax.ShapeDtypeStruct((b, s, D_B), BF16),
        scratch_shapes=[
            pltpu.VMEM((N_PAIRS, 3 * nbp, HEAD_PAIR), BF16),
            pltpu.VMEM((B_HEADS, HEAD_PAIR, B_BLOCK), BF16),
            pltpu.VMEM((B_HEADS, nbp, B_BLOCK), F32),
            pltpu.VMEM((B_HEADS, 1, B_BLOCK), F32),
            pltpu.VMEM((B_HEADS, 1, B_BLOCK), F32),
            pltpu.VMEM((B_HEADS, B_DH, B_BLOCK), F32),
            pltpu.VMEM((B_HEADS, nb, B_BLOCK, B_BLOCK), F32),
        ],
        compiler_params=pltpu.CompilerParams(
            dimension_semantics=("parallel", "arbitrary"), vmem_limit_bytes=VMEM_LIMIT),
        name="moba",
    )(qt, k, vt, gob)


def _outmix_kernel(x_ref, ya_ref, yb_ref, wout_ref, gffn_ref, wr_hi_ref, wr_lo_ref,
                   h_ref, hn_ref, route_ref):
    h = x_ref[...] + _dot(ya_ref[...], wout_ref[:D_A]) + _dot(yb_ref[...], wout_ref[D_A:])
    h_ref[...] = h
    hn = _rms(h, gffn_ref[...])
    _to_row_tiles(hn_ref, hn)

    hn_hi, hn_lo = _split_bf16(hn)
    lg = _dot(hn_hi, wr_hi_ref[...]) + (_dot(hn_lo, wr_hi_ref[...]) + _dot(hn_hi, wr_lo_ref[...]))
    lane = lax.broadcasted_iota(jnp.int32, lg.shape, 1)
    big = ROUTER_LANES

    def first_lane(mask):
        return jnp.min(jnp.where(mask, lane, big), axis=-1, keepdims=True)

    is_g = lane < N_GROUPS
    g_max = jnp.max(jnp.where(is_g, lg, -jnp.inf), axis=-1, keepdims=True)
    p_g = 1.0 / jnp.sum(jnp.where(is_g, jnp.exp(lg - g_max), 0.0), axis=-1, keepdims=True)
    g_sel = first_lane(is_g & (lg == g_max))
    e_lane = lane - N_GROUPS
    in_grp = (e_lane >= 0) & (e_lane < N_EXPERTS) & (lax.shift_right_arithmetic(e_lane, 3) == g_sel)
    l1 = jnp.max(jnp.where(in_grp, lg, -jnp.inf), axis=-1, keepdims=True)
    i1 = first_lane(in_grp & (lg == l1))
    rest = in_grp & (lane != i1)
    l2 = jnp.max(jnp.where(rest, lg, -jnp.inf), axis=-1, keepdims=True)
    i2 = first_lane(rest & (lg == l2))
    e2 = jnp.exp(l2 - l1)
    w1 = 1.0 / (1.0 + e2)
    w2 = e2 / (1.0 + e2)
    rec = jnp.where(lane == 0, (i1 - N_GROUPS).astype(F32),
                    jnp.where(lane == 1, (i2 - N_GROUPS).astype(F32),
                              jnp.where(lane == 2, p_g * w1, jnp.where(lane == 3, p_g * w2, 0.0))))
    route_ref[...] = rec


def _outmix_call(x2, ya2, yb2, wout, gffn, wr_hi, wr_lo):
    t, d = x2.shape
    tm = OUT_TM
    const = lambda *shape: pl.BlockSpec(shape, lambda ti: (0,) * len(shape))
    tok = lambda w: pl.BlockSpec((tm, w), lambda ti: (ti, 0))
    return pl.pallas_call(
        _outmix_kernel,
        grid=(t // tm,),
        in_specs=[tok(d), tok(D_A), tok(D_B), const(d, d), const(1, d),
                  const(d, ROUTER_LANES), const(d, ROUTER_LANES)],
        out_specs=[tok(d), pl.BlockSpec((tm * ROW_SUB, LANES), lambda ti: (ti, 0)), tok(ROUTER_LANES)],
        out_shape=[jax.ShapeDtypeStruct((t, d), F32), jax.ShapeDtypeStruct((t * ROW_SUB, LANES), F32),
                   jax.ShapeDtypeStruct((t, ROUTER_LANES), F32)],
        compiler_params=pltpu.CompilerParams(
            dimension_semantics=("parallel",), vmem_limit_bytes=VMEM_LIMIT),
        name="outmix",
    )(x2, ya2, yb2, wout, gffn, wr_hi, wr_lo)


def _to_row_tiles(ref, x):
    n = x.shape[0]
    for s in range(ROW_SUB):
        ref[pl.ds(s, n, stride=ROW_SUB), :] = x[:, s * LANES:(s + 1) * LANES]


def _from_row_tiles(ref, first, n):
    return jnp.concatenate(
        [ref[pl.ds(first * ROW_SUB + s, n, stride=ROW_SUB), :] for s in range(ROW_SUB)], axis=1)


def _tile_at(ref, row):
    return ref.at[pl.ds(pl.multiple_of(row * ROW_SUB, ROW_SUB), ROW_SUB), :]


def _wait_tiles(hbm, vmem_ref, n_tiles, sem):
    cap = vmem_ref.shape[0] // ROW_SUB
    while n_tiles > 0:
        n = min(n_tiles, cap)
        pltpu.make_async_copy(hbm.at[pl.ds(0, n * ROW_SUB), :], vmem_ref.at[pl.ds(0, n * ROW_SUB), :], sem).wait()
        n_tiles -= n


def _dispatch_kernel(pos_ref, pad_ref, hn_ref, xs_hbm, zero_ref, sem):
    n_rows = pos_ref.shape[2]
    tm = n_rows // 2
    n_pad = pad_ref.shape[2]
    zero_ref[...] = jnp.zeros_like(zero_ref)
    for r in range(n_rows):
        pltpu.make_async_copy(_tile_at(hn_ref, r % tm), _tile_at(xs_hbm, pos_ref[0, 0, r]),
                              sem).start(priority=r % 2)
    for r in range(n_pad):
        pltpu.make_async_copy(zero_ref, _tile_at(xs_hbm, pad_ref[0, 0, r]), sem).start(priority=r % 2)
    _wait_tiles(xs_hbm, hn_ref, n_rows + n_pad, sem)


def _dispatch_call(pos_rows, pad_rows, hn_tiles, n_sorted_rows):
    n_steps, _, n_rows = pos_rows.shape
    tm = n_rows // 2
    smem = lambda w: pl.BlockSpec((1, 1, w), lambda i: (i, 0, 0), memory_space=pltpu.SMEM)
    return pl.pallas_call(
        _dispatch_kernel,
        grid=(n_steps,),
        in_specs=[smem(n_rows), smem(pad_rows.shape[2]),
                  pl.BlockSpec((tm * ROW_SUB, LANES), lambda i: (i, 0))],
        out_specs=pl.BlockSpec(memory_space=pl.ANY),
        out_shape=jax.ShapeDtypeStruct((n_sorted_rows * ROW_SUB, LANES), F32),
        scratch_shapes=[pltpu.VMEM((ROW_SUB, LANES), F32), pltpu.SemaphoreType.DMA(())],
        compiler_params=pltpu.CompilerParams(
            dimension_semantics=("arbitrary",), vmem_limit_bytes=VMEM_LIMIT),
        name="dispatch",
    )(pos_rows, pad_rows, hn_tiles)


def _experts_kernel(tile_e_ref, n_tiles_ref, xs_ref, wg_ref, wu_ref, wd_ref, y_ref):
    del tile_e_ref
    i = pl.program_id(0)
    tm = xs_ref.shape[0] // ROW_SUB

    @pl.when(i < n_tiles_ref[0])
    def _():
        x = _from_row_tiles(xs_ref, 0, tm).astype(BF16)
        a = jax.nn.silu(_dot(x, wg_ref[0])) * _dot(x, wu_ref[0])
        _to_row_tiles(y_ref, _dot(a.astype(BF16), wd_ref[0]))

    @pl.when(i >= n_tiles_ref[0])
    def _():
        y_ref[...] = jnp.zeros_like(y_ref)


def _experts_call(tile_e, n_tiles, xs, wg, wu, wd):
    d = wg.shape[1]
    tm = EXP_TM
    nt = xs.shape[0] // (tm * ROW_SUB)
    used = lambda i, n: jnp.minimum(i, n[0] - 1)
    w_spec = lambda *shape: pl.BlockSpec((1,) + shape, lambda i, te, n: (te[used(i, n)], 0, 0))
    return pl.pallas_call(
        _experts_kernel,
        grid_spec=pltpu.PrefetchScalarGridSpec(
            num_scalar_prefetch=2,
            grid=(nt,),
            in_specs=[pl.BlockSpec((tm * ROW_SUB, LANES), lambda i, te, n: (used(i, n), 0)),
                      w_spec(d, D_FF_EXP), w_spec(d, D_FF_EXP), w_spec(D_FF_EXP, d)],
            out_specs=pl.BlockSpec((tm * ROW_SUB, LANES), lambda i, te, n: (i, 0)),
        ),
        out_shape=jax.ShapeDtypeStruct(xs.shape, F32),
        compiler_params=pltpu.CompilerParams(
            dimension_semantics=("arbitrary",), vmem_limit_bytes=VMEM_LIMIT),
        name="experts",
    )(tile_e, n_tiles, xs, wg, wu, wd)


def _ple_kernel(cur_ref, nxt_ref, ys_hbm, h_ref, route_ref, p_ref, wple_ref, gple_ref, wpg_ref,
                o_ref, ybuf0, ybuf1, sem):
    tm = h_ref.shape[0]
    i = pl.program_id(0)
    n_steps = pl.num_programs(0)
    bufs = (ybuf0, ybuf1)

    def start_gather(idx_ref, slot):
        for r in range(2 * tm):
            pltpu.make_async_copy(_tile_at(ys_hbm, idx_ref[0, 0, r]), _tile_at(bufs[slot], r),
                                  sem.at[slot]).start(priority=r % 2)

    @pl.when(i == 0)
    def _():
        start_gather(cur_ref, 0)

    def step(slot):
        _wait_tiles(ys_hbm, bufs[slot], 2 * tm, sem.at[slot])
        start_gather(nxt_ref, 1 - slot)
        rec = route_ref[...]
        h = (h_ref[...] + rec[:, 2:3] * _from_row_tiles(bufs[slot], 0, tm)
             + rec[:, 3:4] * _from_row_tiles(bufs[slot], tm, tm))
        gate = jax.nn.sigmoid(_dot(_rms(h, gple_ref[...]).astype(BF16), wpg_ref[...]))
        o_ref[...] = h + _dot(p_ref[...].astype(BF16), wple_ref[...]) * gate

    for slot in range(2):
        pl.when(lax.rem(i, 2) == slot)(functools.partial(step, slot))

    for slot in range(2):
        pl.when((i == n_steps - 1) & (lax.rem(n_steps, 2) == slot))(
            functools.partial(_wait_tiles, ys_hbm, bufs[slot], 2 * tm, sem.at[slot]))


def _ple_call(pos_rows, ys, h, route, p2, wple, gple, wpg):
    t, d = h.shape
    n_steps, _, n_rows = pos_rows.shape
    tm = n_rows // 2
    idx_spec = lambda f: pl.BlockSpec((1, 1, n_rows), lambda i: (f(i), 0, 0), memory_space=pltpu.SMEM)
    const = lambda *shape: pl.BlockSpec(shape, lambda i: (0,) * len(shape))
    tok = lambda w: pl.BlockSpec((tm, w), lambda i: (i, 0))
    return pl.pallas_call(
        _ple_kernel,
        grid=(n_steps,),
        in_specs=[idx_spec(lambda i: i), idx_spec(lambda i: lax.rem(i + 1, n_steps)),
                  pl.BlockSpec(memory_space=pl.ANY),
                  tok(d), tok(ROUTER_LANES), tok(D_PLE), const(D_PLE, d), const(1, d), const(d, d)],
        out_specs=tok(d),
        out_shape=jax.ShapeDtypeStruct((t, d), F32),
        scratch_shapes=[pltpu.VMEM((n_rows * ROW_SUB, LANES), F32), pltpu.VMEM((n_rows * ROW_SUB, LANES), F32),
                        pltpu.SemaphoreType.DMA((2,))],
        compiler_params=pltpu.CompilerParams(
            dimension_semantics=("arbitrary",), vmem_limit_bytes=VMEM_LIMIT),
        name="ple",
    )(pos_rows, pos_rows, ys, h, route, p2, wple, gple, wpg)


def _routing_tables(route, t):
    tm = EXP_TM
    n_pairs = 2 * t
    n_steps = t // TOK_TM
    nt = n_pairs // tm + N_EXPERTS
    n_fill = nt * tm - n_pairs
    e_of_pair = route[:, :2].astype(jnp.int32).reshape(n_pairs)
    onehot = (e_of_pair[:, None] == jnp.arange(N_EXPERTS, dtype=jnp.int32)[None, :]).astype(jnp.int32)
    running = jnp.cumsum(onehot, axis=0)
    rank = jnp.take_along_axis(running, e_of_pair[:, None], axis=1)[:, 0] - 1
    counts = running[-1]
    padded = ((counts + tm - 1) // tm) * tm
    seg_end = jnp.cumsum(padded)
    seg_start = seg_end - padded
    pos = seg_start[e_of_pair] + rank
    pos_rows = pos.reshape(n_steps, TOK_TM, 2).transpose(0, 2, 1).reshape(n_steps, 1, 2 * TOK_TM)
    gap_len = jnp.concatenate([padded - counts, nt * tm - seg_end[-1:]])
    gap_start = jnp.concatenate([seg_start + counts, seg_end[-1:]])
    gap_end = jnp.cumsum(gap_len)
    m = jnp.arange(n_fill, dtype=jnp.int32)
    gap = jnp.sum(m[:, None] >= gap_end[None, :], axis=1)
    pad_rows = (gap_start[gap] + m - (gap_end - gap_len)[gap]).reshape(n_steps, 1, n_fill // n_steps)
    tile_start = jnp.arange(nt, dtype=jnp.int32) * tm
    tile_e = jnp.minimum(jnp.sum(tile_start[:, None] >= seg_end[None, :], axis=1), N_EXPERTS - 1)
    n_tiles = (seg_end[-1] // tm).reshape(1)
    return tile_e.astype(jnp.int32), n_tiles.astype(jnp.int32), pos_rows, pad_rows.astype(jnp.int32), nt * tm


def _layer(h, p_i, g_mix, w_in, g_v_a, w_s, b_s, g_q, g_k, g_out_a, g_out_b, w_out, g_ffn, w_group,
           w_expert, w_gate_e, w_up_e, w_down_e, g_ple, w_ple, w_ple_gate):
    b, s, d = h.shape
    t = b * s
    row = lambda g: g.reshape(1, -1).astype(F32)
    wuv = w_in[:, :2 * D_A].astype(BF16)
    wqt = w_in[:, 2 * D_A:2 * D_A + D_B].T.astype(BF16)
    wk = w_in[:, 2 * D_A + D_B:2 * D_A + 2 * D_B].astype(BF16)
    wvt = w_in[:, 2 * D_A + 2 * D_B:].T.astype(BF16)
    bs = jnp.broadcast_to(b_s[:, :, None], (A_GROUPS, A_CHUNK, A_DH)).astype(F32)
    gk = row(jnp.tile(g_k, B_HEADS))
    gq = (jnp.tile(g_q, B_HEADS) * (1.0 / float(B_DH) ** 0.5)).reshape(D_B, 1).astype(F32)
    head = jnp.arange(D_B) // B_DH
    bd = jnp.where(head[:, None] == head[None, :], 1.0 / B_DH, 0.0).astype(BF16)
    w_r = jnp.zeros((d, ROUTER_LANES), F32)
    w_r = w_r.at[:, :N_GROUPS].set(w_group).at[:, N_GROUPS:N_GROUPS + N_EXPERTS].set(w_expert)
    wr_hi, wr_lo = _split_bf16(w_r)

    ya, k, qt, vt = _proj_call(h, row(g_mix), wuv, wk, wqt, wvt, row(g_v_a), w_s.astype(F32), bs,
                               row(g_out_a), gk, gq, bd)
    yb = _moba_call(qt, k, vt, row(g_out_b))
    h1, hn_tiles, route = _outmix_call(h.reshape(t, d), ya.reshape(t, D_A), yb.reshape(t, D_B),
                                       w_out.astype(BF16), row(g_ffn), wr_hi, wr_lo)
    tile_e, n_tiles, pos_rows, pad_rows, n_sorted_rows = _routing_tables(route, t)
    xs = _dispatch_call(pos_rows, pad_rows, hn_tiles, n_sorted_rows)
    ys = _experts_call(tile_e, n_tiles, xs,
                       w_gate_e.astype(BF16), w_up_e.astype(BF16), w_down_e.astype(BF16))
    out = _ple_call(pos_rows, ys, h1, route, p_i.reshape(t, D_PLE), w_ple.astype(BF16), row(g_ple),
                    w_ple_gate.astype(BF16))
    return out.reshape(b, s, d)


def kernel(x, p, g_mix, w_in, g_v_a, w_s, b_s, g_q, g_k, g_out_a, g_out_b, w_out, g_ffn, w_group, w_expert,
           w_gate_e, w_up_e, w_down_e, g_ple, w_ple, w_ple_gate):
    params = (g_mix, w_in, g_v_a, w_s, b_s, g_q, g_k, g_out_a, g_out_b, w_out, g_ffn, w_group, w_expert,
              w_gate_e, w_up_e, w_down_e, g_ple, w_ple, w_ple_gate)
    h = x
    for i in range(p.shape[0]):
        h = _layer(h, p[i], *(w[i] for w in params))
    return h
```

```python
import functools

import jax
import jax.numpy as jnp
from jax import lax
from jax.experimental import pallas as pl
from jax.experimental.pallas import tpu as pltpu

F32 = jnp.float32
BF16 = jnp.bfloat16

D_MODEL = 1024
D_A = 512
A_GROUPS = 4
A_DH = D_A // A_GROUPS
A_CHUNK = 128
D_B = 512
B_HEADS = 8
B_DH = D_B // B_HEADS
B_BLOCK = 256
B_TOPK = 3
D_PLE = 256
N_GROUPS = 4
EXP_PER_GROUP = 8
N_EXPERTS = N_GROUPS * EXP_PER_GROUP
D_FF_EXP = 256
EPS = 1e-6
NEG = -1e30

LANES = 128
HEAD_PAIR = 2 * B_DH
N_PAIRS = B_HEADS // 2
ROUTER_LANES = LANES
ROUTE_FIELDS = 8
VMEM_LIMIT = 56 * 1024 * 1024

PROJ_TM = 512
OUT_TM = 512
EXP_TM = 256
TOK_TM = 256
ROW_SUB = D_MODEL // LANES


def _rms(x, g):
    return x * lax.rsqrt(jnp.mean(x * x, axis=-1, keepdims=True) + EPS) * g


def _dot(a, b):
    return jnp.dot(a, b, preferred_element_type=F32)


def _dot_nt(a, b):
    return lax.dot_general(a, b, (((1,), (1,)), ((), ())), preferred_element_type=F32)


def _split_bf16(x):
    hi = x.astype(BF16)
    lo = (x - hi.astype(F32)).astype(BF16)
    return hi, lo


def _proj_kernel(x_ref, gmix_ref, wuv_ref, wk_ref, wqt_ref, wvt_ref, gva_ref, ws_ref, bs_ref,
                 goa_ref, gk_ref, gq_ref, bd_ref,
                 ya_ref, k_ref, qt_ref, vt_ref, prod_ref):
    tm = x_ref.shape[1]
    n_chunks = tm // A_CHUNK
    xn = _rms(x_ref[0], gmix_ref[...]).astype(BF16)

    uv = _dot(xn, wuv_ref[...])
    u = jax.nn.gelu(uv[:, :D_A])
    v = jax.nn.gelu(uv[:, D_A:])
    row = lax.broadcasted_iota(jnp.int32, (A_CHUNK, A_CHUNK), 0)
    col = lax.broadcasted_iota(jnp.int32, (A_CHUNK, A_CHUNK), 1)
    causal = col <= row
    for g in range(A_GROUPS):
        gs = slice(g * A_DH, (g + 1) * A_DH)
        vn = _rms(v[:, gs], gva_ref[:, gs]).astype(BF16)
        rhs = jnp.concatenate([vn[c * A_CHUNK:(c + 1) * A_CHUNK] for c in range(n_chunks)], axis=1)
        ws = jnp.where(causal, ws_ref[g], 0.0).astype(BF16)
        mixed = _dot(ws, rhs)
        for c in range(n_chunks):
            cs = slice(c * A_CHUNK, (c + 1) * A_CHUNK)
            prod_ref[cs, gs] = u[cs, gs] * (mixed[:, cs] + bs_ref[g])
    ya_ref[0] = _rms(prod_ref[...], goa_ref[...]).astype(ya_ref.dtype)

    k = _dot(xn, wk_ref[...])
    k2_hi, k2_lo = _split_bf16(k * k)
    k_ms = _dot(k2_hi, bd_ref[...]) + _dot(k2_lo, bd_ref[...])
    k_ref[0] = (k * lax.rsqrt(k_ms + EPS) * gk_ref[...]).astype(k_ref.dtype)

    qt = _dot_nt(wqt_ref[...], xn)
    vt = _dot_nt(wvt_ref[...], xn).astype(vt_ref.dtype)
    qn = []
    for h in range(B_HEADS):
        qh = qt[h * B_DH:(h + 1) * B_DH]
        qn.append(qh * lax.rsqrt(jnp.mean(qh * qh, axis=0, keepdims=True) + EPS))
    qn = (jnp.concatenate(qn, axis=0) * gq_ref[...]).astype(qt_ref.dtype)
    for p in range(N_PAIRS):
        ps = slice(p * HEAD_PAIR, (p + 1) * HEAD_PAIR)
        for j in range(tm // B_BLOCK):
            js = slice(j * B_BLOCK, (j + 1) * B_BLOCK)
            qt_ref[0, p, j] = qn[ps, js]
            vt_ref[0, p, j] = vt[ps, js]


def _proj_call(x, gmix, wuv, wk, wqt, wvt, gva, ws, bs, goa, gk, gq, bd):
    b, s, d = x.shape
    tm = PROJ_TM
    nb = s // B_BLOCK
    jb = tm // B_BLOCK
    const = lambda *shape: pl.BlockSpec(shape, lambda bi, ti: (0,) * len(shape))
    return pl.pallas_call(
        _proj_kernel,
        grid=(b, s // tm),
        in_specs=[
            pl.BlockSpec((1, tm, d), lambda bi, ti: (bi, ti, 0)),
            const(1, d), const(d, 2 * D_A), const(d, D_B), const(D_B, d), const(D_B, d),
            const(1, D_A), const(A_GROUPS, A_CHUNK, A_CHUNK), const(A_GROUPS, A_CHUNK, A_DH),
            const(1, D_A), const(1, D_B), const(D_B, 1), const(D_B, D_B),
        ],
        out_specs=[
            pl.BlockSpec((1, tm, D_A), lambda bi, ti: (bi, ti, 0)),
            pl.BlockSpec((1, tm, D_B), lambda bi, ti: (bi, ti, 0)),
            pl.BlockSpec((1, N_PAIRS, jb, HEAD_PAIR, B_BLOCK), lambda bi, ti: (bi, 0, ti, 0, 0)),
            pl.BlockSpec((1, N_PAIRS, jb, HEAD_PAIR, B_BLOCK), lambda bi, ti: (bi, 0, ti, 0, 0)),
        ],
        out_shape=[
            jax.ShapeDtypeStruct((b, s, D_A), BF16),
            jax.ShapeDtypeStruct((b, s, D_B), BF16),
            jax.ShapeDtypeStruct((b, N_PAIRS, nb, HEAD_PAIR, B_BLOCK), BF16),
            jax.ShapeDtypeStruct((b, N_PAIRS, nb, HEAD_PAIR, B_BLOCK), BF16),
        ],
        scratch_shapes=[pltpu.VMEM((tm, D_A), F32)],
        compiler_params=pltpu.CompilerParams(
            dimension_semantics=("parallel", "parallel"), vmem_limit_bytes=VMEM_LIMIT),
        name="proj",
    )(x, gmix, wuv, wk, wqt, wvt, gva, ws, bs, goa, gk, gq, bd)


def _moba_kernel(qt_ref, k_ref, vt_ref, gob_ref, y_ref, kmean_ref, qz_ref, sel_ref, m_ref, l_ref, acc_ref,
                 s_ref):
    qb = pl.program_id(1)
    nb = k_ref.shape[1] // B_BLOCK
    nbp = sel_ref.shape[1]

    @pl.when(qb == 0)
    def _():
        rows = [jnp.mean(k_ref[0, j * B_BLOCK:(j + 1) * B_BLOCK, :].astype(F32), axis=0, keepdims=True)
                for j in range(nb)]
        rows += [jnp.zeros((1, D_B), F32)] * (nbp - nb)
        km = jnp.concatenate(rows, axis=0)
        hi = km.astype(BF16)
        rest = km - hi.astype(F32)
        mid = rest.astype(BF16)
        lo = (rest - mid.astype(F32)).astype(BF16)
        for p in range(N_PAIRS):
            ps = slice(p * HEAD_PAIR, (p + 1) * HEAD_PAIR)
            kmean_ref[p] = jnp.concatenate([hi[:, ps], mid[:, ps], lo[:, ps]], axis=0)

    head_row = lax.broadcasted_iota(jnp.int32, (HEAD_PAIR, B_BLOCK), 0) // B_DH
    blk = lax.broadcasted_iota(jnp.int32, (nbp, B_BLOCK), 0)
    past = blk < qb

    for h in range(B_HEADS):
        p, hh = divmod(h, 2)
        q_pair = qt_ref[0, p, 0]
        qz = jnp.where(head_row == hh, q_pair, jnp.zeros_like(q_pair))
        qz_ref[h] = qz
        g3 = _dot(kmean_ref[p], qz)
        gate = (g3[:nbp] + g3[nbp:2 * nbp] + g3[2 * nbp:]) * float(B_DH) ** 0.5
        gate = jnp.where(past, gate, NEG)
        rank = jnp.zeros((nbp, B_BLOCK), F32)
        for j in range(nb):
            gj = gate[j:j + 1]
            ahead = (gj > gate) | ((gj == gate) & (blk > j))
            rank = rank + jnp.where(ahead, 1.0, 0.0)
        chosen = (rank < float(B_TOPK)) & (gate > NEG * 0.5)
        sel_ref[h] = jnp.where(chosen, 1.0, 0.0)

    def scores(h, j):
        p = h // 2
        kj = k_ref[0, pl.ds(pl.multiple_of(j * B_BLOCK, B_BLOCK), B_BLOCK), p * HEAD_PAIR:(p + 1) * HEAD_PAIR]
        return _dot(kj, qz_ref[h])

    def values(h, j):
        p, hh = divmod(h, 2)
        return vt_ref[0, p, j, hh * B_DH:(hh + 1) * B_DH, :]

    key_pos = lax.broadcasted_iota(jnp.int32, (B_BLOCK, B_BLOCK), 0)
    q_pos = lax.broadcasted_iota(jnp.int32, (B_BLOCK, B_BLOCK), 1)
    causal = key_pos <= q_pos
    for h in range(B_HEADS):
        s = jnp.where(causal, scores(h, qb), NEG)
        s_ref[h, qb] = s
        m_ref[h] = jnp.max(s, axis=0, keepdims=True)
        l_ref[h] = jnp.zeros((1, B_BLOCK), F32)
        acc_ref[h] = jnp.zeros((B_DH, B_BLOCK), F32)

    def score_pass(j, carry):
        for h in range(B_HEADS):
            s = jnp.where(sel_ref[h, pl.ds(j, 1), :] > 0.5, scores(h, j), NEG)
            s_ref[h, j] = s
            m_ref[h] = jnp.maximum(m_ref[h], jnp.max(s, axis=0, keepdims=True))
        return carry

    lax.fori_loop(0, qb, score_pass, 0)

    def value_pass(j, carry):
        for h in range(B_HEADS):
            pr = jnp.exp(s_ref[h, j] - m_ref[h])
            l_ref[h] += jnp.sum(pr, axis=0, keepdims=True)
            acc_ref[h] += _dot(values(h, j), pr.astype(BF16))
        return carry

    lax.fori_loop(0, qb + 1, value_pass, 0)

    yt = jnp.concatenate([acc_ref[h] / l_ref[h] for h in range(B_HEADS)], axis=0)
    yt = yt * lax.rsqrt(jnp.mean(yt * yt, axis=0, keepdims=True) + EPS)
    y_ref[0] = (yt.T * gob_ref[...]).astype(y_ref.dtype)


def _moba_call(qt, k, vt, gob):
    b, s, _ = k.shape
    nb = s // B_BLOCK
    nbp = -(-nb // 8) * 8
    return pl.pallas_call(
        _moba_kernel,
        grid=(b, nb),
        in_specs=[
            pl.BlockSpec((1, N_PAIRS, 1, HEAD_PAIR, B_BLOCK), lambda bi, qi: (bi, 0, qi, 0, 0)),
            pl.BlockSpec((1, s, D_B), lambda bi, qi: (bi, 0, 0)),
            pl.BlockSpec((1, N_PAIRS, nb, HEAD_PAIR, B_BLOCK), lambda bi, qi: (bi, 0, 0, 0, 0)),
            pl.BlockSpec((1, D_B), lambda bi, qi: (0, 0)),
        ],
        out_specs=pl.BlockSpec((1, B_BLOCK, D_B), lambda bi, qi: (bi, qi, 0)),
        out_shape=jax.ShapeDtypeStruct((b, s, D_B), BF16),
        scratch_shapes=[
            pltpu.VMEM((N_PAIRS, 3 * nbp, HEAD_PAIR), BF16),
            pltpu.VMEM((B_HEADS, HEAD_PAIR, B_BLOCK), BF16),
            pltpu.VMEM((B_HEADS, nbp, B_BLOCK), F32),
            pltpu.VMEM((B_HEADS, 1, B_BLOCK), F32),
            pltpu.VMEM((B_HEADS, 1, B_BLOCK), F32),
            pltpu.VMEM((B_HEADS, B_DH, B_BLOCK), F32),
            pltpu.VMEM((B_HEADS, nb, B_BLOCK, B_BLOCK), F32),
        ],
        compiler_params=pltpu.CompilerParams(
            dimension_semantics=("parallel", "arbitrary"), vmem_limit_bytes=VMEM_LIMIT),
        name="moba",
    )(qt, k, vt, gob)


def _outmix_kernel(x_ref, ya_ref, yb_ref, wout_ref, gffn_ref, wr_hi_ref, wr_lo_ref,
                   h_ref, hn_ref, route_ref, route_t_ref, counts_ref, cnt_ref):
    h = x_ref[...] + _dot(ya_ref[...], wout_ref[:D_A]) + _dot(yb_ref[...], wout_ref[D_A:])
    h_ref[...] = h
    hn = _rms(h, gffn_ref[...])
    _to_row_tiles(hn_ref, hn)

    hn_hi, hn_lo = _split_bf16(hn)
    lg = _dot(hn_hi, wr_hi_ref[...]) + (_dot(hn_lo, wr_hi_ref[...]) + _dot(hn_hi, wr_lo_ref[...]))
    lane = lax.broadcasted_iota(jnp.int32, lg.shape, 1)
    big = ROUTER_LANES

    def first_lane(mask):
        return jnp.min(jnp.where(mask, lane, big), axis=-1, keepdims=True)

    is_g = lane < N_GROUPS
    g_max = jnp.max(jnp.where(is_g, lg, -jnp.inf), axis=-1, keepdims=True)
    p_g = 1.0 / jnp.sum(jnp.where(is_g, jnp.exp(lg - g_max), 0.0), axis=-1, keepdims=True)
    g_sel = first_lane(is_g & (lg == g_max))
    e_lane = lane - N_GROUPS
    in_grp = (e_lane >= 0) & (e_lane < N_EXPERTS) & (lax.shift_right_arithmetic(e_lane, 3) == g_sel)
    l1 = jnp.max(jnp.where(in_grp, lg, -jnp.inf), axis=-1, keepdims=True)
    i1 = first_lane(in_grp & (lg == l1))
    rest = in_grp & (lane != i1)
    l2 = jnp.max(jnp.where(rest, lg, -jnp.inf), axis=-1, keepdims=True)
    i2 = first_lane(rest & (lg == l2))
    e2 = jnp.exp(l2 - l1)
    w1 = 1.0 / (1.0 + e2)
    w2 = e2 / (1.0 + e2)

    @pl.when(pl.program_id(0) == 0)
    def _():
        cnt_ref[...] = jnp.zeros_like(cnt_ref)

    tm = lg.shape[0]
    picked = jnp.where((lane == i1) | (lane == i2), 1.0, 0.0)
    tok_r = lax.broadcasted_iota(jnp.int32, (tm, tm), 0)
    tok_c = lax.broadcasted_iota(jnp.int32, (tm, tm), 1)
    earlier = jnp.where(tok_c < tok_r, 1.0, 0.0).astype(BF16)
    before = _dot(earlier, picked.astype(BF16)) + cnt_ref[...]
    r1 = jnp.sum(jnp.where(lane == i1, before, 0.0), axis=-1, keepdims=True)
    r2 = jnp.sum(jnp.where(lane == i2, before, 0.0), axis=-1, keepdims=True)
    cnt_ref[...] += jnp.sum(picked, axis=0, keepdims=True)
    counts_ref[...] = cnt_ref[...]

    fields = ((i1 - N_GROUPS).astype(F32), (i2 - N_GROUPS).astype(F32), p_g * w1, p_g * w2, r1, r2)
    rec = jnp.zeros_like(lg)
    for n, val in enumerate(fields):
        rec = jnp.where(lane == n, val, rec)
    route_ref[...] = rec
    route_t_ref[...] = rec.T[:ROUTE_FIELDS]


def _outmix_call(x2, ya2, yb2, wout, gffn, wr_hi, wr_lo):
    t, d = x2.shape
    tm = OUT_TM
    const = lambda *shape: pl.BlockSpec(shape, lambda ti: (0,) * len(shape))
    tok = lambda w: pl.BlockSpec((tm, w), lambda ti: (ti, 0))
    return pl.pallas_call(
        _outmix_kernel,
        grid=(t // tm,),
        in_specs=[tok(d), tok(D_A), tok(D_B), const(d, d), const(1, d),
                  const(d, ROUTER_LANES), const(d, ROUTER_LANES)],
        out_specs=[tok(d), pl.BlockSpec((tm * ROW_SUB, LANES), lambda ti: (ti, 0)), tok(ROUTER_LANES),
                   pl.BlockSpec((ROUTE_FIELDS, tm), lambda ti: (0, ti)), const(1, ROUTER_LANES)],
        out_shape=[jax.ShapeDtypeStruct((t, d), F32), jax.ShapeDtypeStruct((t * ROW_SUB, LANES), F32),
                   jax.ShapeDtypeStruct((t, ROUTER_LANES), F32),
                   jax.ShapeDtypeStruct((ROUTE_FIELDS, t), F32), jax.ShapeDtypeStruct((1, ROUTER_LANES), F32)],
        scratch_shapes=[pltpu.VMEM((1, ROUTER_LANES), F32)],
        compiler_params=pltpu.CompilerParams(
            dimension_semantics=("arbitrary",), vmem_limit_bytes=VMEM_LIMIT),
        name="outmix",
    )(x2, ya2, yb2, wout, gffn, wr_hi, wr_lo)


def _to_row_tiles(ref, x):
    n = x.shape[0]
    for s in range(ROW_SUB):
        ref[pl.ds(s, n, stride=ROW_SUB), :] = x[:, s * LANES:(s + 1) * LANES]


def _from_row_tiles(ref, first, n):
    return jnp.concatenate(
        [ref[pl.ds(first * ROW_SUB + s, n, stride=ROW_SUB), :] for s in range(ROW_SUB)], axis=1)


def _tile_at(ref, row):
    return ref.at[pl.ds(pl.multiple_of(row * ROW_SUB, ROW_SUB), ROW_SUB), :]


def _wait_tiles(hbm, vmem_ref, n_tiles, sem):
    cap = vmem_ref.shape[0] // ROW_SUB
    while n_tiles > 0:
        n = min(n_tiles, cap)
        pltpu.make_async_copy(hbm.at[pl.ds(0, n * ROW_SUB), :], vmem_ref.at[pl.ds(0, n * ROW_SUB), :], sem).wait()
        n_tiles -= n


def _dispatch_kernel(pos_ref, pad_ref, hn_ref, xs_hbm, zero_ref, sem):
    n_rows = pos_ref.shape[2]
    tm = n_rows // 2
    n_pad = pad_ref.shape[2]
    zero_ref[...] = jnp.zeros_like(zero_ref)
    for r in range(n_rows):
        pltpu.make_async_copy(_tile_at(hn_ref, r % tm), _tile_at(xs_hbm, pos_ref[0, 0, r]),
                              sem).start(priority=r % 2)
    for r in range(n_pad):
        pltpu.make_async_copy(zero_ref, _tile_at(xs_hbm, pad_ref[0, 0, r]), sem).start(priority=r % 2)
    _wait_tiles(xs_hbm, hn_ref, n_rows + n_pad, sem)


def _dispatch_call(pos_rows, pad_rows, hn_tiles, n_sorted_rows):
    n_steps, _, n_rows = pos_rows.shape
    tm = n_rows // 2
    smem = lambda w: pl.BlockSpec((1, 1, w), lambda i: (i, 0, 0), memory_space=pltpu.SMEM)
    return pl.pallas_call(
        _dispatch_kernel,
        grid=(n_steps,),
        in_specs=[smem(n_rows), smem(pad_rows.shape[2]),
                  pl.BlockSpec((tm * ROW_SUB, LANES), lambda i: (i, 0))],
        out_specs=pl.BlockSpec(memory_space=pl.ANY),
        out_shape=jax.ShapeDtypeStruct((n_sorted_rows * ROW_SUB, LANES), F32),
        scratch_shapes=[pltpu.VMEM((ROW_SUB, LANES), F32), pltpu.SemaphoreType.DMA(())],
        compiler_params=pltpu.CompilerParams(
            dimension_semantics=("arbitrary",), vmem_limit_bytes=VMEM_LIMIT),
        name="dispatch",
    )(pos_rows, pad_rows, hn_tiles)


def _experts_kernel(tile_e_ref, n_tiles_ref, xs_ref, wg_ref, wu_ref, wd_ref, y_ref, wg16, wu16, wd16):
    i = pl.program_id(0)
    tm = xs_ref.shape[0] // ROW_SUB

    @pl.when((i == 0) | (tile_e_ref[i] != tile_e_ref[jnp.maximum(i - 1, 0)]))
    def _():
        wg16[...] = wg_ref[0].astype(BF16)
        wu16[...] = wu_ref[0].astype(BF16)
        wd16[...] = wd_ref[0].astype(BF16)

    @pl.when(i < n_tiles_ref[0])
    def _():
        x = _from_row_tiles(xs_ref, 0, tm).astype(BF16)
        a = jax.nn.silu(_dot(x, wg16[...])) * _dot(x, wu16[...])
        _to_row_tiles(y_ref, _dot(a.astype(BF16), wd16[...]))

    @pl.when(i >= n_tiles_ref[0])
    def _():
        y_ref[...] = jnp.zeros_like(y_ref)


def _experts_call(tile_e, n_tiles, xs, wg, wu, wd):
    d = wg.shape[1]
    tm = EXP_TM
    nt = xs.shape[0] // (tm * ROW_SUB)
    used = lambda i, n: jnp.minimum(i, n[0] - 1)
    w_spec = lambda *shape: pl.BlockSpec((1,) + shape, lambda i, te, n: (te[used(i, n)], 0, 0))
    return pl.pallas_call(
        _experts_kernel,
        grid_spec=pltpu.PrefetchScalarGridSpec(
            num_scalar_prefetch=2,
            grid=(nt,),
            in_specs=[pl.BlockSpec((tm * ROW_SUB, LANES), lambda i, te, n: (used(i, n), 0)),
                      w_spec(d, D_FF_EXP), w_spec(d, D_FF_EXP), w_spec(D_FF_EXP, d)],
            out_specs=pl.BlockSpec((tm * ROW_SUB, LANES), lambda i, te, n: (i, 0)),
            scratch_shapes=[pltpu.VMEM((d, D_FF_EXP), BF16), pltpu.VMEM((d, D_FF_EXP), BF16),
                            pltpu.VMEM((D_FF_EXP, d), BF16)],
        ),
        out_shape=jax.ShapeDtypeStruct(xs.shape, F32),
        compiler_params=pltpu.CompilerParams(
            dimension_semantics=("arbitrary",), vmem_limit_bytes=VMEM_LIMIT),
        name="experts",
    )(tile_e, n_tiles, xs, wg, wu, wd)


def _ple_kernel(cur_ref, nxt_ref, ys_hbm, h_ref, route_ref, p_ref, wple_ref, gple_ref, wpg_ref,
                o_ref, ybuf0, ybuf1, sem):
    tm = h_ref.shape[0]
    i = pl.program_id(0)
    n_steps = pl.num_programs(0)
    bufs = (ybuf0, ybuf1)

    def start_gather(idx_ref, slot):
        for r in range(2 * tm):
            pltpu.make_async_copy(_tile_at(ys_hbm, idx_ref[0, 0, r]), _tile_at(bufs[slot], r),
                                  sem.at[slot]).start(priority=r % 2)

    @pl.when(i == 0)
    def _():
        start_gather(cur_ref, 0)

    def step(slot):
        _wait_tiles(ys_hbm, bufs[slot], 2 * tm, sem.at[slot])
        start_gather(nxt_ref, 1 - slot)
        rec = route_ref[...]
        h = (h_ref[...] + rec[:, 2:3] * _from_row_tiles(bufs[slot], 0, tm)
             + rec[:, 3:4] * _from_row_tiles(bufs[slot], tm, tm))
        gate = jax.nn.sigmoid(_dot(_rms(h, gple_ref[...]).astype(BF16), wpg_ref[...]))
        o_ref[...] = h + _dot(p_ref[...].astype(BF16), wple_ref[...]) * gate

    for slot in range(2):
        pl.when(lax.rem(i, 2) == slot)(functools.partial(step, slot))

    for slot in range(2):
        pl.when((i == n_steps - 1) & (lax.rem(n_steps, 2) == slot))(
            functools.partial(_wait_tiles, ys_hbm, bufs[slot], 2 * tm, sem.at[slot]))


def _ple_call(pos_rows, ys, h, route, p2, wple, gple, wpg):
    t, d = h.shape
    n_steps, _, n_rows = pos_rows.shape
    tm = n_rows // 2
    idx_spec = lambda f: pl.BlockSpec((1, 1, n_rows), lambda i: (f(i), 0, 0), memory_space=pltpu.SMEM)
    const = lambda *shape: pl.BlockSpec(shape, lambda i: (0,) * len(shape))
    tok = lambda w: pl.BlockSpec((tm, w), lambda i: (i, 0))
    return pl.pallas_call(
        _ple_kernel,
        grid=(n_steps,),
        in_specs=[idx_spec(lambda i: i), idx_spec(lambda i: lax.rem(i + 1, n_steps)),
                  pl.BlockSpec(memory_space=pl.ANY),
                  tok(d), tok(ROUTER_LANES), tok(D_PLE), const(D_PLE, d), const(1, d), const(d, d)],
        out_specs=tok(d),
        out_shape=jax.ShapeDtypeStruct((t, d), F32),
        scratch_shapes=[pltpu.VMEM((n_rows * ROW_SUB, LANES), F32), pltpu.VMEM((n_rows * ROW_SUB, LANES), F32),
                        pltpu.SemaphoreType.DMA((2,))],
        compiler_params=pltpu.CompilerParams(
            dimension_semantics=("arbitrary",), vmem_limit_bytes=VMEM_LIMIT),
        name="ple",
    )(pos_rows, pos_rows, ys, h, route, p2, wple, gple, wpg)


def _routing_tables(route_t, counts, t):
    tm = EXP_TM
    n_pairs = 2 * t
    n_steps = t // TOK_TM
    nt = n_pairs // tm + N_EXPERTS
    n_fill = nt * tm - n_pairs
    padded = ((counts + tm - 1) // tm) * tm
    seg_end = jnp.cumsum(padded)
    seg_start = seg_end - padded
    experts = jnp.arange(N_EXPERTS, dtype=jnp.int32)[:, None, None]
    e12 = route_t[0:2].astype(jnp.int32)
    start12 = jnp.sum(jnp.where(e12[None] == experts, seg_start[:, None, None], 0), axis=0)
    pos = start12 + route_t[4:6].astype(jnp.int32)
    pos_rows = pos.reshape(2, n_steps, TOK_TM).transpose(1, 0, 2).reshape(n_steps, 1, 2 * TOK_TM)
    gap_len = jnp.concatenate([padded - counts, nt * tm - seg_end[-1:]])
    gap_start = jnp.concatenate([seg_start + counts, seg_end[-1:]])
    gap_end = jnp.cumsum(gap_len)
    m = jnp.arange(n_fill, dtype=jnp.int32)
    gap = jnp.sum(m[:, None] >= gap_end[None, :], axis=1)
    pad_rows = (gap_start[gap] + m - (gap_end - gap_len)[gap]).reshape(n_steps, 1, n_fill // n_steps)
    tile_start = jnp.arange(nt, dtype=jnp.int32) * tm
    tile_e = jnp.minimum(jnp.sum(tile_start[:, None] >= seg_end[None, :], axis=1), N_EXPERTS - 1)
    n_tiles = (seg_end[-1] // tm).reshape(1)
    return tile_e.astype(jnp.int32), n_tiles.astype(jnp.int32), pos_rows, pad_rows.astype(jnp.int32), nt * tm


def _layer(h, p_i, g_mix, w_in, g_v_a, w_s, b_s, g_q, g_k, g_out_a, g_out_b, w_out, g_ffn, w_group,
           w_expert, w_gate_e, w_up_e, w_down_e, g_ple, w_ple, w_ple_gate):
    b, s, d = h.shape
    t = b * s
    row = lambda g: g.reshape(1, -1).astype(F32)
    wuv = w_in[:, :2 * D_A].astype(BF16)
    wqt = w_in[:, 2 * D_A:2 * D_A + D_B].T.astype(BF16)
    wk = w_in[:, 2 * D_A + D_B:2 * D_A + 2 * D_B].astype(BF16)
    wvt = w_in[:, 2 * D_A + 2 * D_B:].T.astype(BF16)
    bs = jnp.broadcast_to(b_s[:, :, None], (A_GROUPS, A_CHUNK, A_DH)).astype(F32)
    gk = row(jnp.tile(g_k, B_HEADS))
    gq = (jnp.tile(g_q, B_HEADS) * (1.0 / float(B_DH) ** 0.5)).reshape(D_B, 1).astype(F32)
    head = jnp.arange(D_B) // B_DH
    bd = jnp.where(head[:, None] == head[None, :], 1.0 / B_DH, 0.0).astype(BF16)
    w_r = jnp.zeros((d, ROUTER_LANES), F32)
    w_r = w_r.at[:, :N_GROUPS].set(w_group).at[:, N_GROUPS:N_GROUPS + N_EXPERTS].set(w_expert)
    wr_hi, wr_lo = _split_bf16(w_r)

    ya, k, qt, vt = _proj_call(h, row(g_mix), wuv, wk, wqt, wvt, row(g_v_a), w_s.astype(F32), bs,
                               row(g_out_a), gk, gq, bd)
    yb = _moba_call(qt, k, vt, row(g_out_b))
    h1, hn_tiles, route, route_t, counts = _outmix_call(
        h.reshape(t, d), ya.reshape(t, D_A), yb.reshape(t, D_B), w_out.astype(BF16), row(g_ffn), wr_hi, wr_lo)
    counts = counts[0, N_GROUPS:N_GROUPS + N_EXPERTS].astype(jnp.int32)
    tile_e, n_tiles, pos_rows, pad_rows, n_sorted_rows = _routing_tables(route_t, counts, t)
    xs = _dispatch_call(pos_rows, pad_rows, hn_tiles, n_sorted_rows)
    ys = _experts_call(tile_e, n_tiles, xs, w_gate_e, w_up_e, w_down_e)
    out = _ple_call(pos_rows, ys, h1, route, p_i.reshape(t, D_PLE), w_ple.astype(BF16), row(g_ple),
                    w_ple_gate.astype(BF16))
    return out.reshape(b, s, d)


def kernel(x, p, g_mix, w_in, g_v_a, w_s, b_s, g_q, g_k, g_out_a, g_out_b, w_out, g_ffn, w_group, w_expert,
           w_gate_e, w_up_e, w_down_e, g_ple, w_ple, w_ple_gate):
    params = (g_mix, w_in, g_v_a, w_s, b_s, g_q, g_k, g_out_a, g_out_b, w_out, g_ffn, w_group, w_expert,
              w_gate_e, w_up_e, w_down_e, g_ple, w_ple, w_ple_gate)
    h = x
    for i in range(p.shape[0]):
        h = _layer(h, p[i], *(w[i] for w in params))
    return h
```

```python
import functools

import jax
import jax.numpy as jnp
from jax import lax
from jax.experimental import pallas as pl
from jax.experimental.pallas import tpu as pltpu

F32 = jnp.float32
BF16 = jnp.bfloat16

D_MODEL = 1024
D_A = 512
A_GROUPS = 4
A_DH = D_A // A_GROUPS
A_CHUNK = 128
D_B = 512
B_HEADS = 8
B_DH = D_B // B_HEADS
B_BLOCK = 256
B_TOPK = 3
D_PLE = 256
N_GROUPS = 4
EXP_PER_GROUP = 8
N_EXPERTS = N_GROUPS * EXP_PER_GROUP
D_FF_EXP = 256
EPS = 1e-6
NEG = -1e30
LOG2E = 1.4426950408889634

LANES = 128
HEAD_PAIR = 2 * B_DH
N_PAIRS = B_HEADS // 2
ROUTER_LANES = LANES
ROUTE_FIELDS = 8
VMEM_LIMIT = 56 * 1024 * 1024

PROJ_TM = 512
OUT_TM = 512
EXP_TM = 256
MOBA_UNROLL = 2
TOK_TM = 256
ROW_SUB = D_MODEL // LANES


def _rms(x, g):
    return x * lax.rsqrt(jnp.mean(x * x, axis=-1, keepdims=True) + EPS) * g


def _dot(a, b):
    return jnp.dot(a, b, preferred_element_type=F32)


def _dot_nt(a, b):
    return lax.dot_general(a, b, (((1,), (1,)), ((), ())), preferred_element_type=F32)


def _split_bf16(x):
    hi = x.astype(BF16)
    lo = (x - hi.astype(F32)).astype(BF16)
    return hi, lo


def _proj_kernel(x_ref, gmix_ref, wuv_ref, wk_ref, wqt_ref, wvt_ref, gva_ref, ws_ref, bs_ref,
                 goa_ref, gk_ref, gq_ref, bd_ref,
                 ya_ref, k_ref, qt_ref, vt_ref, prod_ref):
    tm = x_ref.shape[1]
    n_chunks = tm // A_CHUNK
    xn = _rms(x_ref[0], gmix_ref[...]).astype(BF16)

    uv = _dot(xn, wuv_ref[...])
    u = jax.nn.gelu(uv[:, :D_A])
    v = jax.nn.gelu(uv[:, D_A:])
    row = lax.broadcasted_iota(jnp.int32, (A_CHUNK, A_CHUNK), 0)
    col = lax.broadcasted_iota(jnp.int32, (A_CHUNK, A_CHUNK), 1)
    causal = col <= row
    for g in range(A_GROUPS):
        gs = slice(g * A_DH, (g + 1) * A_DH)
        vn = _rms(v[:, gs], gva_ref[:, gs]).astype(BF16)
        rhs = jnp.concatenate([vn[c * A_CHUNK:(c + 1) * A_CHUNK] for c in range(n_chunks)], axis=1)
        ws = jnp.where(causal, ws_ref[g], 0.0).astype(BF16)
        mixed = _dot(ws, rhs)
        for c in range(n_chunks):
            cs = slice(c * A_CHUNK, (c + 1) * A_CHUNK)
            prod_ref[cs, gs] = u[cs, gs] * (mixed[:, cs] + bs_ref[g])
    ya_ref[0] = _rms(prod_ref[...], goa_ref[...]).astype(ya_ref.dtype)

    k = _dot(xn, wk_ref[...])
    k2_hi, k2_lo = _split_bf16(k * k)
    k_ms = _dot(k2_hi, bd_ref[...]) + _dot(k2_lo, bd_ref[...])
    k_ref[0] = (k * lax.rsqrt(k_ms + EPS) * gk_ref[...]).astype(k_ref.dtype)

    qt = _dot_nt(wqt_ref[...], xn)
    vt = _dot_nt(wvt_ref[...], xn).astype(vt_ref.dtype)
    qn = []
    for h in range(B_HEADS):
        qh = qt[h * B_DH:(h + 1) * B_DH]
        qn.append(qh * lax.rsqrt(jnp.mean(qh * qh, axis=0, keepdims=True) + EPS))
    qn = (jnp.concatenate(qn, axis=0) * gq_ref[...]).astype(qt_ref.dtype)
    for p in range(N_PAIRS):
        ps = slice(p * HEAD_PAIR, (p + 1) * HEAD_PAIR)
        for j in range(tm // B_BLOCK):
            js = slice(j * B_BLOCK, (j + 1) * B_BLOCK)
            qt_ref[0, p, j] = qn[ps, js]
            vt_ref[0, p, j] = vt[ps, js]


def _proj_call(x, gmix, wuv, wk, wqt, wvt, gva, ws, bs, goa, gk, gq, bd):
    b, s, d = x.shape
    tm = PROJ_TM
    nb = s // B_BLOCK
    jb = tm // B_BLOCK
    const = lambda *shape: pl.BlockSpec(shape, lambda bi, ti: (0,) * len(shape))
    return pl.pallas_call(
        _proj_kernel,
        grid=(b, s // tm),
        in_specs=[
            pl.BlockSpec((1, tm, d), lambda bi, ti: (bi, ti, 0)),
            const(1, d), const(d, 2 * D_A), const(d, D_B), const(D_B, d), const(D_B, d),
            const(1, D_A), const(A_GROUPS, A_CHUNK, A_CHUNK), const(A_GROUPS, A_CHUNK, A_DH),
            const(1, D_A), const(1, D_B), const(D_B, 1), const(D_B, D_B),
        ],
        out_specs=[
            pl.BlockSpec((1, tm, D_A), lambda bi, ti: (bi, ti, 0)),
            pl.BlockSpec((1, tm, D_B), lambda bi, ti: (bi, ti, 0)),
            pl.BlockSpec((1, N_PAIRS, jb, HEAD_PAIR, B_BLOCK), lambda bi, ti: (bi, 0, ti, 0, 0)),
            pl.BlockSpec((1, N_PAIRS, jb, HEAD_PAIR, B_BLOCK), lambda bi, ti: (bi, 0, ti, 0, 0)),
        ],
        out_shape=[
            jax.ShapeDtypeStruct((b, s, D_A), BF16),
            jax.ShapeDtypeStruct((b, s, D_B), BF16),
            jax.ShapeDtypeStruct((b, N_PAIRS, nb, HEAD_PAIR, B_BLOCK), BF16),
            jax.ShapeDtypeStruct((b, N_PAIRS, nb, HEAD_PAIR, B_BLOCK), BF16),
        ],
        scratch_shapes=[pltpu.VMEM((tm, D_A), F32)],
        compiler_params=pltpu.CompilerParams(
            dimension_semantics=("parallel", "parallel"), vmem_limit_bytes=VMEM_LIMIT),
        name="proj",
    )(x, gmix, wuv, wk, wqt, wvt, gva, ws, bs, goa, gk, gq, bd)


def _moba_kernel(qt_ref, k_ref, vt_ref, gob_ref, y_ref, kmean_ref, qz_ref, sel_ref, m_ref, l_ref, acc_ref,
                 s_ref):
    qb = pl.program_id(1)
    nb = k_ref.shape[1] // B_BLOCK
    nbp = sel_ref.shape[1]

    @pl.when(qb == 0)
    def _():
        rows = [jnp.mean(k_ref[0, j * B_BLOCK:(j + 1) * B_BLOCK, :].astype(F32), axis=0, keepdims=True)
                for j in range(nb)]
        rows += [jnp.zeros((1, D_B), F32)] * (nbp - nb)
        km = jnp.concatenate(rows, axis=0)
        hi = km.astype(BF16)
        rest = km - hi.astype(F32)
        mid = rest.astype(BF16)
        lo = (rest - mid.astype(F32)).astype(BF16)
        for p in range(N_PAIRS):
            ps = slice(p * HEAD_PAIR, (p + 1) * HEAD_PAIR)
            kmean_ref[p] = jnp.concatenate([hi[:, ps], mid[:, ps], lo[:, ps]], axis=0)

    head_row = lax.broadcasted_iota(jnp.int32, (HEAD_PAIR, B_BLOCK), 0) // B_DH
    blk = lax.broadcasted_iota(jnp.int32, (nbp, B_BLOCK), 0)
    past = blk < qb

    for h in range(B_HEADS):
        p, hh = divmod(h, 2)
        q_pair = qt_ref[0, p, 0]
        qz = jnp.where(head_row == hh, q_pair, jnp.zeros_like(q_pair))
        qz_ref[h] = qz
        g3 = _dot(kmean_ref[p], qz)
        gate = (g3[:nbp] + g3[nbp:2 * nbp] + g3[2 * nbp:]) * float(B_DH) ** 0.5
        gate = jnp.where(past, gate, NEG)
        rank = jnp.zeros((nbp, B_BLOCK), F32)
        for j in range(nb):
            gj = gate[j:j + 1]
            ahead = (gj > gate) | ((gj == gate) & (blk > j))
            rank = rank + jnp.where(ahead, 1.0, 0.0)
        chosen = (rank < float(B_TOPK)) & (gate > NEG * 0.5)
        sel_ref[h] = jnp.where(chosen, 1.0, 0.0)

    def scores(h, j):
        p = h // 2
        kj = k_ref[0, pl.ds(pl.multiple_of(j * B_BLOCK, B_BLOCK), B_BLOCK), p * HEAD_PAIR:(p + 1) * HEAD_PAIR]
        return _dot(kj, qz_ref[h])

    def values(h, j):
        p, hh = divmod(h, 2)
        return vt_ref[0, p, j, hh * B_DH:(hh + 1) * B_DH, :]

    key_pos = lax.broadcasted_iota(jnp.int32, (B_BLOCK, B_BLOCK), 0)
    q_pos = lax.broadcasted_iota(jnp.int32, (B_BLOCK, B_BLOCK), 1)
    causal = key_pos <= q_pos
    def stage_scores(h, j, visible):
        s = jnp.where(visible, scores(h, j) * LOG2E, NEG)
        s_ref[h, j] = s
        return jnp.max(s, axis=0, keepdims=True)

    for h in range(B_HEADS):
        m_ref[h] = stage_scores(h, qb, causal)
        l_ref[h] = jnp.zeros((1, B_BLOCK), F32)
        acc_ref[h] = jnp.zeros((B_DH, B_BLOCK), F32)

    def score_pass(n_blocks, j0):
        for h in range(B_HEADS):
            m = m_ref[h]
            for u in range(n_blocks):
                m = jnp.maximum(m, stage_scores(h, j0 + u, sel_ref[h, pl.ds(j0 + u, 1), :] > 0.5))
            m_ref[h] = m

    def value_pass(n_blocks, j0):
        for h in range(B_HEADS):
            l, acc = l_ref[h], acc_ref[h]
            for u in range(n_blocks):
                pr = jnp.exp2(s_ref[h, j0 + u] - m_ref[h])
                l = l + jnp.sum(pr, axis=0, keepdims=True)
                acc = acc + _dot(values(h, j0 + u), pr.astype(BF16))
            l_ref[h], acc_ref[h] = l, acc

    def run(n, body):
        u = MOBA_UNROLL

        def trips(i, carry):
            body(u, i * u)
            return carry

        def rest(j, carry):
            body(1, j)
            return carry

        lax.fori_loop(0, n // u, trips, 0)
        lax.fori_loop((n // u) * u, n, rest, 0)

    run(qb, score_pass)
    run(qb + 1, value_pass)

    yt = jnp.concatenate([acc_ref[h] / l_ref[h] for h in range(B_HEADS)], axis=0)
    yt = yt * lax.rsqrt(jnp.mean(yt * yt, axis=0, keepdims=True) + EPS)
    y_ref[0] = (yt.T * gob_ref[...]).astype(y_ref.dtype)


def _moba_call(qt, k, vt, gob):
    b, s, _ = k.shape
    nb = s // B_BLOCK
    nbp = -(-nb // 8) * 8
    return pl.pallas_call(
        _moba_kernel,
        grid=(b, nb),
        in_specs=[
            pl.BlockSpec((1, N_PAIRS, 1, HEAD_PAIR, B_BLOCK), lambda bi, qi: (bi, 0, qi, 0, 0)),
            pl.BlockSpec((1, s, D_B), lambda bi, qi: (bi, 0, 0)),
            pl.BlockSpec((1, N_PAIRS, nb, HEAD_PAIR, B_BLOCK), lambda bi, qi: (bi, 0, 0, 0, 0)),
            pl.BlockSpec((1, D_B), lambda bi, qi: (0, 0)),
        ],
        out_specs=pl.BlockSpec((1, B_BLOCK, D_B), lambda bi, qi: (bi, qi, 0)),
        out_shape=jax.ShapeDtypeStruct((b, s, D_B), BF16),
        scratch_shapes=[
            pltpu.VMEM((N_PAIRS, 3 * nbp, HEAD_PAIR), BF16),
            pltpu.VMEM((B_HEADS, HEAD_PAIR, B_BLOCK), BF16),
            pltpu.VMEM((B_HEADS, nbp, B_BLOCK), F32),
            pltpu.VMEM((B_HEADS, 1, B_BLOCK), F32),
            pltpu.VMEM((B_HEADS, 1, B_BLOCK), F32),
            pltpu.VMEM((B_HEADS, B_DH, B_BLOCK), F32),
            pltpu.VMEM((B_HEADS, nb, B_BLOCK, B_BLOCK), F32),
        ],
        compiler_params=pltpu.CompilerParams(
            dimension_semantics=("parallel", "arbitrary"), vmem_limit_bytes=VMEM_LIMIT),
        name="moba",
    )(qt, k, vt, gob)


def _outmix_kernel(x_ref, ya_ref, yb_ref, wout_ref, gffn_ref, wr_hi_ref, wr_lo_ref,
                   h_ref, hn_ref, route_ref, route_t_ref, counts_ref, cnt_ref):
    h = x_ref[...] + _dot(ya_ref[...], wout_ref[:D_A]) + _dot(yb_ref[...], wout_ref[D_A:])
    h_ref[...] = h
    hn = _rms(h, gffn_ref[...])
    _to_row_tiles(hn_ref, hn)

    hn_hi, hn_lo = _split_bf16(hn)
    lg = _dot(hn_hi, wr_hi_ref[...]) + (_dot(hn_lo, wr_hi_ref[...]) + _dot(hn_hi, wr_lo_ref[...]))
    lane = lax.broadcasted_iota(jnp.int32, lg.shape, 1)
    big = ROUTER_LANES

    def first_lane(mask):
        return jnp.min(jnp.where(mask, lane, big), axis=-1, keepdims=True)

    is_g = lane < N_GROUPS
    g_max = jnp.max(jnp.where(is_g, lg, -jnp.inf), axis=-1, keepdims=True)
    p_g = 1.0 / jnp.sum(jnp.where(is_g, jnp.exp(lg - g_max), 0.0), axis=-1, keepdims=True)
    g_sel = first_lane(is_g & (lg == g_max))
    e_lane = lane - N_GROUPS
    in_grp = (e_lane >= 0) & (e_lane < N_EXPERTS) & (lax.shift_right_arithmetic(e_lane, 3) == g_sel)
    l1 = jnp.max(jnp.where(in_grp, lg, -jnp.inf), axis=-1, keepdims=True)
    i1 = first_lane(in_grp & (lg == l1))
    rest = in_grp & (lane != i1)
    l2 = jnp.max(jnp.where(rest, lg, -jnp.inf), axis=-1, keepdims=True)
    i2 = first_lane(rest & (lg == l2))
    e2 = jnp.exp(l2 - l1)
    w1 = 1.0 / (1.0 + e2)
    w2 = e2 / (1.0 + e2)

    @pl.when(pl.program_id(0) == 0)
    def _():
        cnt_ref[...] = jnp.zeros_like(cnt_ref)

    tm = lg.shape[0]
    picked = jnp.where((lane == i1) | (lane == i2), 1.0, 0.0)
    tok_r = lax.broadcasted_iota(jnp.int32, (tm, tm), 0)
    tok_c = lax.broadcasted_iota(jnp.int32, (tm, tm), 1)
    earlier = jnp.where(tok_c < tok_r, 1.0, 0.0).astype(BF16)
    before = _dot(earlier, picked.astype(BF16)) + cnt_ref[...]
    r1 = jnp.sum(jnp.where(lane == i1, before, 0.0), axis=-1, keepdims=True)
    r2 = jnp.sum(jnp.where(lane == i2, before, 0.0), axis=-1, keepdims=True)
    cnt_ref[...] += jnp.sum(picked, axis=0, keepdims=True)
    counts_ref[...] = cnt_ref[...]

    fields = ((i1 - N_GROUPS).astype(F32), (i2 - N_GROUPS).astype(F32), p_g * w1, p_g * w2, r1, r2)
    rec = jnp.zeros_like(lg)
    for n, val in enumerate(fields):
        rec = jnp.where(lane == n, val, rec)
    route_ref[...] = rec
    route_t_ref[...] = rec.T[:ROUTE_FIELDS]


def _outmix_call(x2, ya2, yb2, wout, gffn, wr_hi, wr_lo):
    t, d = x2.shape
    tm = OUT_TM
    const = lambda *shape: pl.BlockSpec(shape, lambda ti: (0,) * len(shape))
    tok = lambda w: pl.BlockSpec((tm, w), lambda ti: (ti, 0))
    return pl.pallas_call(
        _outmix_kernel,
        grid=(t // tm,),
        in_specs=[tok(d), tok(D_A), tok(D_B), const(d, d), const(1, d),
                  const(d, ROUTER_LANES), const(d, ROUTER_LANES)],
        out_specs=[tok(d), pl.BlockSpec((tm * ROW_SUB, LANES), lambda ti: (ti, 0)), tok(ROUTER_LANES),
                   pl.BlockSpec((ROUTE_FIELDS, tm), lambda ti: (0, ti)), const(1, ROUTER_LANES)],
        out_shape=[jax.ShapeDtypeStruct((t, d), F32), jax.ShapeDtypeStruct((t * ROW_SUB, LANES), F32),
                   jax.ShapeDtypeStruct((t, ROUTER_LANES), F32),
                   jax.ShapeDtypeStruct((ROUTE_FIELDS, t), F32), jax.ShapeDtypeStruct((1, ROUTER_LANES), F32)],
        scratch_shapes=[pltpu.VMEM((1, ROUTER_LANES), F32)],
        compiler_params=pltpu.CompilerParams(
            dimension_semantics=("arbitrary",), vmem_limit_bytes=VMEM_LIMIT),
        name="outmix",
    )(x2, ya2, yb2, wout, gffn, wr_hi, wr_lo)


def _to_row_tiles(ref, x):
    n = x.shape[0]
    for s in range(ROW_SUB):
        ref[pl.ds(s, n, stride=ROW_SUB), :] = x[:, s * LANES:(s + 1) * LANES]


def _from_row_tiles(ref, first, n):
    return jnp.concatenate(
        [ref[pl.ds(first * ROW_SUB + s, n, stride=ROW_SUB), :] for s in range(ROW_SUB)], axis=1)


def _tile_at(ref, row):
    return ref.at[pl.ds(pl.multiple_of(row * ROW_SUB, ROW_SUB), ROW_SUB), :]


def _wait_tiles(hbm, vmem_ref, n_tiles, sem):
    cap = vmem_ref.shape[0] // ROW_SUB
    while n_tiles > 0:
        n = min(n_tiles, cap)
        pltpu.make_async_copy(hbm.at[pl.ds(0, n * ROW_SUB), :], vmem_ref.at[pl.ds(0, n * ROW_SUB), :], sem).wait()
        n_tiles -= n


def _dispatch_kernel(pos_ref, pad_ref, hn_ref, xs_hbm, zero_ref, sem):
    n_rows = pos_ref.shape[2]
    tm = n_rows // 2
    n_pad = pad_ref.shape[2]
    zero_ref[...] = jnp.zeros_like(zero_ref)
    for r in range(n_rows):
        pltpu.make_async_copy(_tile_at(hn_ref, r % tm), _tile_at(xs_hbm, pos_ref[0, 0, r]),
                              sem).start(priority=r % 2)
    for r in range(n_pad):
        pltpu.make_async_copy(zero_ref, _tile_at(xs_hbm, pad_ref[0, 0, r]), sem).start(priority=r % 2)
    _wait_tiles(xs_hbm, hn_ref, n_rows + n_pad, sem)


def _dispatch_call(pos_rows, pad_rows, hn_tiles, n_sorted_rows):
    n_steps, _, n_rows = pos_rows.shape
    tm = n_rows // 2
    smem = lambda w: pl.BlockSpec((1, 1, w), lambda i: (i, 0, 0), memory_space=pltpu.SMEM)
    return pl.pallas_call(
        _dispatch_kernel,
        grid=(n_steps,),
        in_specs=[smem(n_rows), smem(pad_rows.shape[2]),
                  pl.BlockSpec((tm * ROW_SUB, LANES), lambda i: (i, 0))],
        out_specs=pl.BlockSpec(memory_space=pl.ANY),
        out_shape=jax.ShapeDtypeStruct((n_sorted_rows * ROW_SUB, LANES), F32),
        scratch_shapes=[pltpu.VMEM((ROW_SUB, LANES), F32), pltpu.SemaphoreType.DMA(())],
        compiler_params=pltpu.CompilerParams(
            dimension_semantics=("arbitrary",), vmem_limit_bytes=VMEM_LIMIT),
        name="dispatch",
    )(pos_rows, pad_rows, hn_tiles)


def _experts_kernel(tile_e_ref, n_tiles_ref, xs_ref, wg_ref, wu_ref, wd_ref, y_ref, wg16, wu16, wd16):
    i = pl.program_id(0)
    tm = xs_ref.shape[0] // ROW_SUB

    @pl.when((i == 0) | (tile_e_ref[i] != tile_e_ref[jnp.maximum(i - 1, 0)]))
    def _():
        wg16[...] = wg_ref[0].astype(BF16)
        wu16[...] = wu_ref[0].astype(BF16)
        wd16[...] = wd_ref[0].astype(BF16)

    @pl.when(i < n_tiles_ref[0])
    def _():
        x = _from_row_tiles(xs_ref, 0, tm).astype(BF16)
        a = jax.nn.silu(_dot(x, wg16[...])) * _dot(x, wu16[...])
        _to_row_tiles(y_ref, _dot(a.astype(BF16), wd16[...]))

    @pl.when(i >= n_tiles_ref[0])
    def _():
        y_ref[...] = jnp.zeros_like(y_ref)


def _experts_call(tile_e, n_tiles, xs, wg, wu, wd):
    d = wg.shape[1]
    tm = EXP_TM
    nt = xs.shape[0] // (tm * ROW_SUB)
    used = lambda i, n: jnp.minimum(i, n[0] - 1)
    w_spec = lambda *shape: pl.BlockSpec((1,) + shape, lambda i, te, n: (te[used(i, n)], 0, 0))
    return pl.pallas_call(
        _experts_kernel,
        grid_spec=pltpu.PrefetchScalarGridSpec(
            num_scalar_prefetch=2,
            grid=(nt,),
            in_specs=[pl.BlockSpec((tm * ROW_SUB, LANES), lambda i, te, n: (used(i, n), 0)),
                      w_spec(d, D_FF_EXP), w_spec(d, D_FF_EXP), w_spec(D_FF_EXP, d)],
            out_specs=pl.BlockSpec((tm * ROW_SUB, LANES), lambda i, te, n: (i, 0)),
            scratch_shapes=[pltpu.VMEM((d, D_FF_EXP), BF16), pltpu.VMEM((d, D_FF_EXP), BF16),
                            pltpu.VMEM((D_FF_EXP, d), BF16)],
        ),
        out_shape=jax.ShapeDtypeStruct(xs.shape, F32),
        compiler_params=pltpu.CompilerParams(
            dimension_semantics=("arbitrary",), vmem_limit_bytes=VMEM_LIMIT),
        name="experts",
    )(tile_e, n_tiles, xs, wg, wu, wd)


def _ple_kernel(cur_ref, nxt_ref, ys_hbm, h_ref, route_ref, p_ref, wple_ref, gple_ref, wpg_ref,
                o_ref, ybuf0, ybuf1, sem):
    tm = h_ref.shape[0]
    i = pl.program_id(0)
    n_steps = pl.num_programs(0)
    bufs = (ybuf0, ybuf1)

    def start_gather(idx_ref, slot):
        for r in range(2 * tm):
            pltpu.make_async_copy(_tile_at(ys_hbm, idx_ref[0, 0, r]), _tile_at(bufs[slot], r),
                                  sem.at[slot]).start(priority=r % 2)

    @pl.when(i == 0)
    def _():
        start_gather(cur_ref, 0)

    def step(slot):
        _wait_tiles(ys_hbm, bufs[slot], 2 * tm, sem.at[slot])
        start_gather(nxt_ref, 1 - slot)
        rec = route_ref[...]
        h = (h_ref[...] + rec[:, 2:3] * _from_row_tiles(bufs[slot], 0, tm)
             + rec[:, 3:4] * _from_row_tiles(bufs[slot], tm, tm))
        gate = jax.nn.sigmoid(_dot(_rms(h, gple_ref[...]).astype(BF16), wpg_ref[...]))
        o_ref[...] = h + _dot(p_ref[...].astype(BF16), wple_ref[...]) * gate

    for slot in range(2):
        pl.when(lax.rem(i, 2) == slot)(functools.partial(step, slot))

    for slot in range(2):
        pl.when((i == n_steps - 1) & (lax.rem(n_steps, 2) == slot))(
            functools.partial(_wait_tiles, ys_hbm, bufs[slot], 2 * tm, sem.at[slot]))


def _ple_call(pos_rows, ys, h, route, p2, wple, gple, wpg):
    t, d = h.shape
    n_steps, _, n_rows = pos_rows.shape
    tm = n_rows // 2
    idx_spec = lambda f: pl.BlockSpec((1, 1, n_rows), lambda i: (f(i), 0, 0), memory_space=pltpu.SMEM)
    const = lambda *shape: pl.BlockSpec(shape, lambda i: (0,) * len(shape))
    tok = lambda w: pl.BlockSpec((tm, w), lambda i: (i, 0))
    return pl.pallas_call(
        _ple_kernel,
        grid=(n_steps,),
        in_specs=[idx_spec(lambda i: i), idx_spec(lambda i: lax.rem(i + 1, n_steps)),
                  pl.BlockSpec(memory_space=pl.ANY),
                  tok(d), tok(ROUTER_LANES), tok(D_PLE), const(D_PLE, d), const(1, d), const(d, d)],
        out_specs=tok(d),
        out_shape=jax.ShapeDtypeStruct((t, d), F32),
        scratch_shapes=[pltpu.VMEM((n_rows * ROW_SUB, LANES), F32), pltpu.VMEM((n_rows * ROW_SUB, LANES), F32),
                        pltpu.SemaphoreType.DMA((2,))],
        compiler_params=pltpu.CompilerParams(
            dimension_semantics=("arbitrary",), vmem_limit_bytes=VMEM_LIMIT),
        name="ple",
    )(pos_rows, pos_rows, ys, h, route, p2, wple, gple, wpg)


def _routing_tables(route_t, counts, t):
    tm = EXP_TM
    n_pairs = 2 * t
    n_steps = t // TOK_TM
    nt = n_pairs // tm + N_EXPERTS
    n_fill = nt * tm - n_pairs
    padded = ((counts + tm - 1) // tm) * tm
    seg_end = jnp.cumsum(padded)
    seg_start = seg_end - padded
    experts = jnp.arange(N_EXPERTS, dtype=jnp.int32)[:, None, None]
    e12 = route_t[0:2].astype(jnp.int32)
    start12 = jnp.sum(jnp.where(e12[None] == experts, seg_start[:, None, None], 0), axis=0)
    pos = start12 + route_t[4:6].astype(jnp.int32)
    pos_rows = pos.reshape(2, n_steps, TOK_TM).transpose(1, 0, 2).reshape(n_steps, 1, 2 * TOK_TM)
    gap_len = jnp.concatenate([padded - counts, nt * tm - seg_end[-1:]])
    gap_start = jnp.concatenate([seg_start + counts, seg_end[-1:]])
    gap_end = jnp.cumsum(gap_len)
    m = jnp.arange(n_fill, dtype=jnp.int32)
    gap = jnp.sum(m[:, None] >= gap_end[None, :], axis=1)
    pad_rows = (gap_start[gap] + m - (gap_end - gap_len)[gap]).reshape(n_steps, 1, n_fill // n_steps)
    tile_start = jnp.arange(nt, dtype=jnp.int32) * tm
    tile_e = jnp.minimum(jnp.sum(tile_start[:, None] >= seg_end[None, :], axis=1), N_EXPERTS - 1)
    n_tiles = (seg_end[-1] // tm).reshape(1)
    return tile_e.astype(jnp.int32), n_tiles.astype(jnp.int32), pos_rows, pad_rows.astype(jnp.int32), nt * tm


def _layer(h, p_i, g_mix, w_in, g_v_a, w_s, b_s, g_q, g_k, g_out_a, g_out_b, w_out, g_ffn, w_group,
           w_expert, w_gate_e, w_up_e, w_down_e, g_ple, w_ple, w_ple_gate):
    b, s, d = h.shape
    t = b * s
    row = lambda g: g.reshape(1, -1).astype(F32)
    wuv = w_in[:, :2 * D_A].astype(BF16)
    wqt = w_in[:, 2 * D_A:2 * D_A + D_B].T.astype(BF16)
    wk = w_in[:, 2 * D_A + D_B:2 * D_A + 2 * D_B].astype(BF16)
    wvt = w_in[:, 2 * D_A + 2 * D_B:].T.astype(BF16)
    bs = jnp.broadcast_to(b_s[:, :, None], (A_GROUPS, A_CHUNK, A_DH)).astype(F32)
    gk = row(jnp.tile(g_k, B_HEADS))
    gq = (jnp.tile(g_q, B_HEADS) * (1.0 / float(B_DH) ** 0.5)).reshape(D_B, 1).astype(F32)
    head = jnp.arange(D_B) // B_DH
    bd = jnp.where(head[:, None] == head[None, :], 1.0 / B_DH, 0.0).astype(BF16)
    w_r = jnp.zeros((d, ROUTER_LANES), F32)
    w_r = w_r.at[:, :N_GROUPS].set(w_group).at[:, N_GROUPS:N_GROUPS + N_EXPERTS].set(w_expert)
    wr_hi, wr_lo = _split_bf16(w_r)

    ya, k, qt, vt = _proj_call(h, row(g_mix), wuv, wk, wqt, wvt, row(g_v_a), w_s.astype(F32), bs,
                               row(g_out_a), gk, gq, bd)
    yb = _moba_call(qt, k, vt, row(g_out_b))
    h1, hn_tiles, route, route_t, counts = _outmix_call(
        h.reshape(t, d), ya.reshape(t, D_A), yb.reshape(t, D_B), w_out.astype(BF16), row(g_ffn), wr_hi, wr_lo)
    counts = counts[0, N_GROUPS:N_GROUPS + N_EXPERTS].astype(jnp.int32)
    tile_e, n_tiles, pos_rows, pad_rows, n_sorted_rows = _routing_tables(route_t, counts, t)
    xs = _dispatch_call(pos_rows, pad_rows, hn_tiles, n_sorted_rows)
    ys = _experts_call(tile_e, n_tiles, xs, w_gate_e, w_up_e, w_down_e)
    out = _ple_call(pos_rows, ys, h1, route, p_i.reshape(t, D_PLE), w_ple.astype(BF16), row(g_ple),
                    w_ple_gate.astype(BF16))
    return out.reshape(b, s, d)


def kernel(x, p, g_mix, w_in, g_v_a, w_s, b_s, g_q, g_k, g_out_a, g_out_b, w_out, g_ffn, w_group, w_expert,
           w_gate_e, w_up_e, w_down_e, g_ple, w_ple, w_ple_gate):
    params = (g_mix, w_in, g_v_a, w_s, b_s, g_q, g_k, g_out_a, g_out_b, w_out, g_ffn, w_group, w_expert,
              w_gate_e, w_up_e, w_down_e, g_ple, w_ple, w_ple_gate)
    h = x
    for i in range(p.shape[0]):
        h = _layer(h, p[i], *(w[i] for w in params))
    return h
```

```python
import functools

import jax
import jax.numpy as jnp
from jax import lax
from jax.experimental import pallas as pl
from jax.experimental.pallas import tpu as pltpu

F32 = jnp.float32
BF16 = jnp.bfloat16

D_MODEL = 1024
D_A = 512
A_GROUPS = 4
A_DH = D_A // A_GROUPS
A_CHUNK = 128
D_B = 512
B_HEADS = 8
B_DH = D_B // B_HEADS
B_BLOCK = 256
B_TOPK = 3
D_PLE = 256
N_GROUPS = 4
EXP_PER_GROUP = 8
N_EXPERTS = N_GROUPS * EXP_PER_GROUP
D_FF_EXP = 256
EPS = 1e-6
NEG = -1e30
LOG2E = 1.4426950408889634

LANES = 128
HEAD_PAIR = 2 * B_DH
N_PAIRS = B_HEADS // 2
ROUTER_LANES = LANES
ROUTE_FIELDS = 8
VMEM_LIMIT = 56 * 1024 * 1024

PROJ_TM = 512
OUT_TM = 512
EXP_TM = 256
MOBA_UNROLL = 2
TOK_TM = 256
PLE_TM = 512
ROW_SUB = D_MODEL // LANES


def _rms(x, g):
    return x * lax.rsqrt(jnp.mean(x * x, axis=-1, keepdims=True) + EPS) * g


def _dot(a, b):
    return jnp.dot(a, b, preferred_element_type=F32)


def _dot_nt(a, b):
    return lax.dot_general(a, b, (((1,), (1,)), ((), ())), preferred_element_type=F32)


def _split_bf16(x):
    hi = x.astype(BF16)
    lo = (x - hi.astype(F32)).astype(BF16)
    return hi, lo


def _proj_kernel(x_ref, gmix_ref, wuv_ref, wk_ref, wqt_ref, wvt_ref, gva_ref, ws_ref, bs_ref,
                 goa_ref, gk_ref, gq_ref, bd_ref,
                 ya_ref, k_ref, qt_ref, vt_ref, prod_ref):
    tm = x_ref.shape[1]
    n_chunks = tm // A_CHUNK
    xn = _rms(x_ref[0], gmix_ref[...]).astype(BF16)

    uv = _dot(xn, wuv_ref[...])
    u = jax.nn.gelu(uv[:, :D_A])
    v = jax.nn.gelu(uv[:, D_A:])
    row = lax.broadcasted_iota(jnp.int32, (A_CHUNK, A_CHUNK), 0)
    col = lax.broadcasted_iota(jnp.int32, (A_CHUNK, A_CHUNK), 1)
    causal = col <= row
    for g in range(A_GROUPS):
        gs = slice(g * A_DH, (g + 1) * A_DH)
        vn = _rms(v[:, gs], gva_ref[:, gs]).astype(BF16)
        rhs = jnp.concatenate([vn[c * A_CHUNK:(c + 1) * A_CHUNK] for c in range(n_chunks)], axis=1)
        ws = jnp.where(causal, ws_ref[g], 0.0).astype(BF16)
        mixed = _dot(ws, rhs)
        for c in range(n_chunks):
            cs = slice(c * A_CHUNK, (c + 1) * A_CHUNK)
            prod_ref[cs, gs] = u[cs, gs] * (mixed[:, cs] + bs_ref[g])
    ya_ref[0] = _rms(prod_ref[...], goa_ref[...]).astype(ya_ref.dtype)

    k = _dot(xn, wk_ref[...])
    k2_hi, k2_lo = _split_bf16(k * k)
    k_ms = _dot(k2_hi, bd_ref[...]) + _dot(k2_lo, bd_ref[...])
    k_ref[0] = (k * lax.rsqrt(k_ms + EPS) * gk_ref[...]).astype(k_ref.dtype)

    qt = _dot_nt(wqt_ref[...], xn)
    vt = _dot_nt(wvt_ref[...], xn).astype(vt_ref.dtype)
    qn = []
    for h in range(B_HEADS):
        qh = qt[h * B_DH:(h + 1) * B_DH]
        qn.append(qh * lax.rsqrt(jnp.mean(qh * qh, axis=0, keepdims=True) + EPS))
    qn = (jnp.concatenate(qn, axis=0) * gq_ref[...]).astype(qt_ref.dtype)
    for p in range(N_PAIRS):
        ps = slice(p * HEAD_PAIR, (p + 1) * HEAD_PAIR)
        for j in range(tm // B_BLOCK):
            js = slice(j * B_BLOCK, (j + 1) * B_BLOCK)
            qt_ref[0, p, j] = qn[ps, js]
            vt_ref[0, p, j] = vt[ps, js]


def _proj_call(x, gmix, wuv, wk, wqt, wvt, gva, ws, bs, goa, gk, gq, bd):
    b, s, d = x.shape
    tm = PROJ_TM
    nb = s // B_BLOCK
    jb = tm // B_BLOCK
    const = lambda *shape: pl.BlockSpec(shape, lambda bi, ti: (0,) * len(shape))
    return pl.pallas_call(
        _proj_kernel,
        grid=(b, s // tm),
        in_specs=[
            pl.BlockSpec((1, tm, d), lambda bi, ti: (bi, ti, 0)),
            const(1, d), const(d, 2 * D_A), const(d, D_B), const(D_B, d), const(D_B, d),
            const(1, D_A), const(A_GROUPS, A_CHUNK, A_CHUNK), const(A_GROUPS, A_CHUNK, A_DH),
            const(1, D_A), const(1, D_B), const(D_B, 1), const(D_B, D_B),
        ],
        out_specs=[
            pl.BlockSpec((1, tm, D_A), lambda bi, ti: (bi, ti, 0)),
            pl.BlockSpec((1, tm, D_B), lambda bi, ti: (bi, ti, 0)),
            pl.BlockSpec((1, N_PAIRS, jb, HEAD_PAIR, B_BLOCK), lambda bi, ti: (bi, 0, ti, 0, 0)),
            pl.BlockSpec((1, N_PAIRS, jb, HEAD_PAIR, B_BLOCK), lambda bi, ti: (bi, 0, ti, 0, 0)),
        ],
        out_shape=[
            jax.ShapeDtypeStruct((b, s, D_A), BF16),
            jax.ShapeDtypeStruct((b, s, D_B), BF16),
            jax.ShapeDtypeStruct((b, N_PAIRS, nb, HEAD_PAIR, B_BLOCK), BF16),
            jax.ShapeDtypeStruct((b, N_PAIRS, nb, HEAD_PAIR, B_BLOCK), BF16),
        ],
        scratch_shapes=[pltpu.VMEM((tm, D_A), F32)],
        compiler_params=pltpu.CompilerParams(
            dimension_semantics=("parallel", "parallel"), vmem_limit_bytes=VMEM_LIMIT),
        name="proj",
    )(x, gmix, wuv, wk, wqt, wvt, gva, ws, bs, goa, gk, gq, bd)


def _moba_kernel(qt_ref, k_ref, vt_ref, gob_ref, y_ref, kmean_ref, qz_ref, sel_ref, m_ref, l_ref, acc_ref,
                 s_ref):
    qb = pl.program_id(1)
    nb = k_ref.shape[1] // B_BLOCK
    nbp = sel_ref.shape[1]

    @pl.when(qb == 0)
    def _():
        rows = [jnp.mean(k_ref[0, j * B_BLOCK:(j + 1) * B_BLOCK, :].astype(F32), axis=0, keepdims=True)
                for j in range(nb)]
        rows += [jnp.zeros((1, D_B), F32)] * (nbp - nb)
        km = jnp.concatenate(rows, axis=0)
        hi = km.astype(BF16)
        rest = km - hi.astype(F32)
        mid = rest.astype(BF16)
        lo = (rest - mid.astype(F32)).astype(BF16)
        for p in range(N_PAIRS):
            ps = slice(p * HEAD_PAIR, (p + 1) * HEAD_PAIR)
            kmean_ref[p] = jnp.concatenate([hi[:, ps], mid[:, ps], lo[:, ps]], axis=0)

    head_row = lax.broadcasted_iota(jnp.int32, (HEAD_PAIR, B_BLOCK), 0) // B_DH
    blk = lax.broadcasted_iota(jnp.int32, (nbp, B_BLOCK), 0)
    past = blk < qb

    for h in range(B_HEADS):
        p, hh = divmod(h, 2)
        q_pair = qt_ref[0, p, 0]
        qz = jnp.where(head_row == hh, q_pair, jnp.zeros_like(q_pair))
        qz_ref[h] = qz
        g3 = _dot(kmean_ref[p], qz)
        gate = (g3[:nbp] + g3[nbp:2 * nbp] + g3[2 * nbp:]) * float(B_DH) ** 0.5
        gate = jnp.where(past, gate, NEG)
        rank = jnp.zeros((nbp, B_BLOCK), F32)
        for j in range(nb):
            gj = gate[j:j + 1]
            ahead = (gj > gate) | ((gj == gate) & (blk > j))
            rank = rank + jnp.where(ahead, 1.0, 0.0)
        chosen = (rank < float(B_TOPK)) & (gate > NEG * 0.5)
        sel_ref[h] = jnp.where(chosen, 1.0, 0.0)

    def scores(h, j):
        p = h // 2
        kj = k_ref[0, pl.ds(pl.multiple_of(j * B_BLOCK, B_BLOCK), B_BLOCK), p * HEAD_PAIR:(p + 1) * HEAD_PAIR]
        return _dot(kj, qz_ref[h])

    def values(h, j):
        p, hh = divmod(h, 2)
        return vt_ref[0, p, j, hh * B_DH:(hh + 1) * B_DH, :]

    key_pos = lax.broadcasted_iota(jnp.int32, (B_BLOCK, B_BLOCK), 0)
    q_pos = lax.broadcasted_iota(jnp.int32, (B_BLOCK, B_BLOCK), 1)
    causal = key_pos <= q_pos
    def stage_scores(h, j, visible):
        s = jnp.where(visible, scores(h, j) * LOG2E, NEG)
        s_ref[h, j] = s
        return jnp.max(s, axis=0, keepdims=True)

    for h in range(B_HEADS):
        m_ref[h] = stage_scores(h, qb, causal)
        l_ref[h] = jnp.zeros((1, B_BLOCK), F32)
        acc_ref[h] = jnp.zeros((B_DH, B_BLOCK), F32)

    def score_pass(n_blocks, j0):
        for h in range(B_HEADS):
            m = m_ref[h]
            for u in range(n_blocks):
                m = jnp.maximum(m, stage_scores(h, j0 + u, sel_ref[h, pl.ds(j0 + u, 1), :] > 0.5))
            m_ref[h] = m

    def value_pass(n_blocks, j0):
        for h in range(B_HEADS):
            l, acc = l_ref[h], acc_ref[h]
            for u in range(n_blocks):
                pr = jnp.exp2(s_ref[h, j0 + u] - m_ref[h])
                l = l + jnp.sum(pr, axis=0, keepdims=True)
                acc = acc + _dot(values(h, j0 + u), pr.astype(BF16))
            l_ref[h], acc_ref[h] = l, acc

    def run(n, body):
        u = MOBA_UNROLL

        def trips(i, carry):
            body(u, i * u)
            return carry

        def rest(j, carry):
            body(1, j)
            return carry

        lax.fori_loop(0, n // u, trips, 0)
        lax.fori_loop((n // u) * u, n, rest, 0)

    run(qb, score_pass)
    run(qb + 1, value_pass)

    yt = jnp.concatenate([acc_ref[h] / l_ref[h] for h in range(B_HEADS)], axis=0)
    yt = yt * lax.rsqrt(jnp.mean(yt * yt, axis=0, keepdims=True) + EPS)
    y_ref[0] = (yt.T * gob_ref[...]).astype(y_ref.dtype)


def _moba_call(qt, k, vt, gob):
    b, s, _ = k.shape
    nb = s // B_BLOCK
    nbp = -(-nb // 8) * 8
    return pl.pallas_call(
        _moba_kernel,
        grid=(b, nb),
        in_specs=[
            pl.BlockSpec((1, N_PAIRS, 1, HEAD_PAIR, B_BLOCK), lambda bi, qi: (bi, 0, qi, 0, 0)),
            pl.BlockSpec((1, s, D_B), lambda bi, qi: (bi, 0, 0)),
            pl.BlockSpec((1, N_PAIRS, nb, HEAD_PAIR, B_BLOCK), lambda bi, qi: (bi, 0, 0, 0, 0)),
            pl.BlockSpec((1, D_B), lambda bi, qi: (0, 0)),
        ],
        out_specs=pl.BlockSpec((1, B_BLOCK, D_B), lambda bi, qi: (bi, qi, 0)),
        out_shape=jax.ShapeDtypeStruct((b, s, D_B), BF16),
        scratch_shapes=[
            pltpu.VMEM((N_PAIRS, 3 * nbp, HEAD_PAIR), BF16),
            pltpu.VMEM((B_HEADS, HEAD_PAIR, B_BLOCK), BF16),
            pltpu.VMEM((B_HEADS, nbp, B_BLOCK), F32),
            pltpu.VMEM((B_HEADS, 1, B_BLOCK), F32),
            pltpu.VMEM((B_HEADS, 1, B_BLOCK), F32),
            pltpu.VMEM((B_HEADS, B_DH, B_BLOCK), F32),
            pltpu.VMEM((B_HEADS, nb, B_BLOCK, B_BLOCK), F32),
        ],
        compiler_params=pltpu.CompilerParams(
            dimension_semantics=("parallel", "arbitrary"), vmem_limit_bytes=VMEM_LIMIT),
        name="moba",
    )(qt, k, vt, gob)


def _outmix_kernel(x_ref, ya_ref, yb_ref, wout_ref, gffn_ref, wr_hi_ref, wr_lo_ref,
                   h_ref, hn_ref, route_ref, route_t_ref, counts_ref, cnt_ref):
    h = x_ref[...] + _dot(ya_ref[...], wout_ref[:D_A]) + _dot(yb_ref[...], wout_ref[D_A:])
    h_ref[...] = h
    hn = _rms(h, gffn_ref[...])
    _to_row_tiles(hn_ref, hn)

    hn_hi, hn_lo = _split_bf16(hn)
    lg = _dot(hn_hi, wr_hi_ref[...]) + (_dot(hn_lo, wr_hi_ref[...]) + _dot(hn_hi, wr_lo_ref[...]))
    lane = lax.broadcasted_iota(jnp.int32, lg.shape, 1)
    big = ROUTER_LANES

    def first_lane(mask):
        return jnp.min(jnp.where(mask, lane, big), axis=-1, keepdims=True)

    is_g = lane < N_GROUPS
    g_max = jnp.max(jnp.where(is_g, lg, -jnp.inf), axis=-1, keepdims=True)
    p_g = 1.0 / jnp.sum(jnp.where(is_g, jnp.exp(lg - g_max), 0.0), axis=-1, keepdims=True)
    g_sel = first_lane(is_g & (lg == g_max))
    e_lane = lane - N_GROUPS
    in_grp = (e_lane >= 0) & (e_lane < N_EXPERTS) & (lax.shift_right_arithmetic(e_lane, 3) == g_sel)
    l1 = jnp.max(jnp.where(in_grp, lg, -jnp.inf), axis=-1, keepdims=True)
    i1 = first_lane(in_grp & (lg == l1))
    rest = in_grp & (lane != i1)
    l2 = jnp.max(jnp.where(rest, lg, -jnp.inf), axis=-1, keepdims=True)
    i2 = first_lane(rest & (lg == l2))
    e2 = jnp.exp(l2 - l1)
    w1 = 1.0 / (1.0 + e2)
    w2 = e2 / (1.0 + e2)

    @pl.when(pl.program_id(0) == 0)
    def _():
        cnt_ref[...] = jnp.zeros_like(cnt_ref)

    tm = lg.shape[0]
    picked = jnp.where((lane == i1) | (lane == i2), 1.0, 0.0)
    tok_r = lax.broadcasted_iota(jnp.int32, (tm, tm), 0)
    tok_c = lax.broadcasted_iota(jnp.int32, (tm, tm), 1)
    earlier = jnp.where(tok_c < tok_r, 1.0, 0.0).astype(BF16)
    before = _dot(earlier, picked.astype(BF16)) + cnt_ref[...]
    r1 = jnp.sum(jnp.where(lane == i1, before, 0.0), axis=-1, keepdims=True)
    r2 = jnp.sum(jnp.where(lane == i2, before, 0.0), axis=-1, keepdims=True)
    cnt_ref[...] += jnp.sum(picked, axis=0, keepdims=True)
    counts_ref[...] = cnt_ref[...]

    fields = ((i1 - N_GROUPS).astype(F32), (i2 - N_GROUPS).astype(F32), p_g * w1, p_g * w2, r1, r2)
    rec = jnp.zeros_like(lg)
    for n, val in enumerate(fields):
        rec = jnp.where(lane == n, val, rec)
    route_ref[...] = rec
    route_t_ref[...] = rec.T[:ROUTE_FIELDS]


def _outmix_call(x2, ya2, yb2, wout, gffn, wr_hi, wr_lo):
    t, d = x2.shape
    tm = OUT_TM
    const = lambda *shape: pl.BlockSpec(shape, lambda ti: (0,) * len(shape))
    tok = lambda w: pl.BlockSpec((tm, w), lambda ti: (ti, 0))
    return pl.pallas_call(
        _outmix_kernel,
        grid=(t // tm,),
        in_specs=[tok(d), tok(D_A), tok(D_B), const(d, d), const(1, d),
                  const(d, ROUTER_LANES), const(d, ROUTER_LANES)],
        out_specs=[tok(d), pl.BlockSpec((tm * ROW_SUB, LANES), lambda ti: (ti, 0)), tok(ROUTER_LANES),
                   pl.BlockSpec((ROUTE_FIELDS, tm), lambda ti: (0, ti)), const(1, ROUTER_LANES)],
        out_shape=[jax.ShapeDtypeStruct((t, d), F32), jax.ShapeDtypeStruct((t * ROW_SUB, LANES), F32),
                   jax.ShapeDtypeStruct((t, ROUTER_LANES), F32),
                   jax.ShapeDtypeStruct((ROUTE_FIELDS, t), F32), jax.ShapeDtypeStruct((1, ROUTER_LANES), F32)],
        scratch_shapes=[pltpu.VMEM((1, ROUTER_LANES), F32)],
        compiler_params=pltpu.CompilerParams(
            dimension_semantics=("arbitrary",), vmem_limit_bytes=VMEM_LIMIT),
        name="outmix",
    )(x2, ya2, yb2, wout, gffn, wr_hi, wr_lo)


def _to_row_tiles(ref, x):
    n = x.shape[0]
    for s in range(ROW_SUB):
        ref[pl.ds(s, n, stride=ROW_SUB), :] = x[:, s * LANES:(s + 1) * LANES]


def _from_row_tiles(ref, first, n):
    return jnp.concatenate(
        [ref[pl.ds(first * ROW_SUB + s, n, stride=ROW_SUB), :] for s in range(ROW_SUB)], axis=1)


def _tile_at(ref, row):
    return ref.at[pl.ds(pl.multiple_of(row * ROW_SUB, ROW_SUB), ROW_SUB), :]


def _wait_tiles(hbm, vmem_ref, n_tiles, sem):
    cap = vmem_ref.shape[0] // ROW_SUB
    while n_tiles > 0:
        n = min(n_tiles, cap)
        pltpu.make_async_copy(hbm.at[pl.ds(0, n * ROW_SUB), :], vmem_ref.at[pl.ds(0, n * ROW_SUB), :], sem).wait()
        n_tiles -= n


def _dispatch_kernel(pos_ref, pad_ref, hn_ref, xs_hbm, back_ref, zero_ref, sem):
    n_rows = pos_ref.shape[2]
    tm = n_rows // 2
    n_pad = pad_ref.shape[2]
    i = pl.program_id(0)
    n_tok = pl.num_programs(0) * tm
    zero_ref[...] = jnp.zeros_like(zero_ref)
    for r in range(n_rows):
        slot = pos_ref[0, 0, r]
        pltpu.make_async_copy(_tile_at(hn_ref, r % tm), _tile_at(xs_hbm, slot), sem).start(priority=r % 2)
        back_ref[slot] = (r // tm) * n_tok + i * tm + r % tm
    for r in range(n_pad):
        slot = pad_ref[0, 0, r]
        pltpu.make_async_copy(zero_ref, _tile_at(xs_hbm, slot), sem).start(priority=r % 2)
        back_ref[slot] = 2 * n_tok + i * n_pad + r
    _wait_tiles(xs_hbm, hn_ref, n_rows + n_pad, sem)


def _dispatch_call(pos_rows, pad_rows, hn_tiles, n_sorted_rows):
    n_steps, _, n_rows = pos_rows.shape
    tm = n_rows // 2
    smem = lambda w: pl.BlockSpec((1, 1, w), lambda i: (i, 0, 0), memory_space=pltpu.SMEM)
    return pl.pallas_call(
        _dispatch_kernel,
        grid=(n_steps,),
        in_specs=[smem(n_rows), smem(pad_rows.shape[2]),
                  pl.BlockSpec((tm * ROW_SUB, LANES), lambda i: (i, 0))],
        out_specs=[pl.BlockSpec(memory_space=pl.ANY), pl.BlockSpec(memory_space=pltpu.SMEM)],
        out_shape=[jax.ShapeDtypeStruct((n_sorted_rows * ROW_SUB, LANES), F32),
                   jax.ShapeDtypeStruct((n_sorted_rows,), jnp.int32)],
        scratch_shapes=[pltpu.VMEM((ROW_SUB, LANES), F32), pltpu.SemaphoreType.DMA(())],
        compiler_params=pltpu.CompilerParams(
            dimension_semantics=("arbitrary",), vmem_limit_bytes=VMEM_LIMIT),
        name="dispatch",
    )(pos_rows, pad_rows, hn_tiles)


def _experts_kernel(tile_e_ref, n_tiles_ref, back_ref, xs_ref, wg_ref, wu_ref, wd_ref, yt_hbm,
                    wg16, wu16, wd16, ybuf0, ybuf1, sem):
    i = pl.program_id(0)
    nt = pl.num_programs(0) - 1
    n_used = n_tiles_ref[0]
    tm = xs_ref.shape[0] // ROW_SUB
    bufs = (ybuf0, ybuf1)
    cur = jnp.minimum(i, nt - 1)

    @pl.when((i == 0) | (tile_e_ref[cur] != tile_e_ref[jnp.maximum(cur - 1, 0)]))
    def _():
        wg16[...] = wg_ref[0].astype(BF16)
        wu16[...] = wu_ref[0].astype(BF16)
        wd16[...] = wd_ref[0].astype(BF16)

    def compute(slot):
        x = _from_row_tiles(xs_ref, 0, tm).astype(BF16)
        a = jax.nn.silu(_dot(x, wg16[...])) * _dot(x, wu16[...])
        _to_row_tiles(bufs[slot], _dot(a.astype(BF16), wd16[...]))

    def send_previous(slot):
        for r in range(tm):
            pltpu.make_async_copy(_tile_at(bufs[slot], r), _tile_at(yt_hbm, back_ref[0, 0, r]),
                                  sem.at[slot]).start(priority=r % 2)

    def drain(slot):
        _wait_tiles(yt_hbm, bufs[slot], tm, sem.at[slot])

    for slot in range(2):
        mine = lax.rem(i, 2) == slot

        @pl.when(mine & (i >= 2))
        def _(slot=slot):
            drain(slot)

        @pl.when(mine & (i == 0))
        def _(slot=slot):
            compute(slot)

        @pl.when(mine & (i >= 1) & (i < n_used))
        def _(slot=slot):
            compute(slot)
            send_previous(1 - slot)

        @pl.when(mine & (i >= n_used) & (i < nt))
        def _(slot=slot):
            bufs[slot][...] = jnp.zeros_like(bufs[slot])
            send_previous(1 - slot)

        @pl.when(mine & (i == nt))
        def _(slot=slot):
            send_previous(1 - slot)
            drain(1 - slot)


def _experts_call(tile_e, n_tiles, back_rows, xs, wg, wu, wd, n_out_rows):
    d = wg.shape[1]
    tm = EXP_TM
    nt = xs.shape[0] // (tm * ROW_SUB)
    used = lambda i, n: jnp.minimum(i, n[0] - 1)
    w_spec = lambda *shape: pl.BlockSpec((1,) + shape, lambda i, te, n: (te[used(i, n)], 0, 0))
    return pl.pallas_call(
        _experts_kernel,
        grid_spec=pltpu.PrefetchScalarGridSpec(
            num_scalar_prefetch=2,
            grid=(nt + 1,),
            in_specs=[pl.BlockSpec((1, 1, tm), lambda i, te, n: (jnp.maximum(i - 1, 0), 0, 0),
                                   memory_space=pltpu.SMEM),
                      pl.BlockSpec((tm * ROW_SUB, LANES), lambda i, te, n: (used(i, n), 0)),
                      w_spec(d, D_FF_EXP), w_spec(d, D_FF_EXP), w_spec(D_FF_EXP, d)],
            out_specs=pl.BlockSpec(memory_space=pl.ANY),
            scratch_shapes=[pltpu.VMEM((d, D_FF_EXP), BF16), pltpu.VMEM((d, D_FF_EXP), BF16),
                            pltpu.VMEM((D_FF_EXP, d), BF16),
                            pltpu.VMEM((tm * ROW_SUB, LANES), F32), pltpu.VMEM((tm * ROW_SUB, LANES), F32),
                            pltpu.SemaphoreType.DMA((2,))],
        ),
        out_shape=jax.ShapeDtypeStruct((n_out_rows * ROW_SUB, LANES), F32),
        compiler_params=pltpu.CompilerParams(
            dimension_semantics=("arbitrary",), vmem_limit_bytes=VMEM_LIMIT),
        name="experts",
    )(tile_e, n_tiles, back_rows, xs, wg, wu, wd)


def _ple_kernel(y1_ref, y2_ref, h_ref, route_ref, p_ref, wple_ref, gple_ref, wpg_ref, o_ref):
    tm = h_ref.shape[0]
    rec = route_ref[...]
    h = (h_ref[...] + rec[:, 2:3] * _from_row_tiles(y1_ref, 0, tm)
         + rec[:, 3:4] * _from_row_tiles(y2_ref, 0, tm))
    gate = jax.nn.sigmoid(_dot(_rms(h, gple_ref[...]).astype(BF16), wpg_ref[...]))
    o_ref[...] = h + _dot(p_ref[...].astype(BF16), wple_ref[...]) * gate


def _ple_call(yt, h, route, p2, wple, gple, wpg):
    t, d = h.shape
    tm = PLE_TM
    n_steps = t // tm
    const = lambda *shape: pl.BlockSpec(shape, lambda i: (0,) * len(shape))
    tok = lambda w: pl.BlockSpec((tm, w), lambda i: (i, 0))
    y_spec = lambda k: pl.BlockSpec((tm * ROW_SUB, LANES), lambda i: (k * n_steps + i, 0))
    return pl.pallas_call(
        _ple_kernel,
        grid=(n_steps,),
        in_specs=[y_spec(0), y_spec(1), tok(d), tok(ROUTER_LANES), tok(D_PLE),
                  const(D_PLE, d), const(1, d), const(d, d)],
        out_specs=tok(d),
        out_shape=jax.ShapeDtypeStruct((t, d), F32),
        compiler_params=pltpu.CompilerParams(
            dimension_semantics=("parallel",), vmem_limit_bytes=VMEM_LIMIT),
        name="ple",
    )(yt, yt, h, route, p2, wple, gple, wpg)


def _routing_tables(route_t, counts, t):
    tm = EXP_TM
    n_pairs = 2 * t
    n_steps = t // TOK_TM
    nt = n_pairs // tm + N_EXPERTS
    n_fill = nt * tm - n_pairs
    padded = ((counts + tm - 1) // tm) * tm
    seg_end = jnp.cumsum(padded)
    seg_start = seg_end - padded
    experts = jnp.arange(N_EXPERTS, dtype=jnp.int32)[:, None, None]
    e12 = route_t[0:2].astype(jnp.int32)
    start12 = jnp.sum(jnp.where(e12[None] == experts, seg_start[:, None, None], 0), axis=0)
    pos = start12 + route_t[4:6].astype(jnp.int32)
    pos_rows = pos.reshape(2, n_steps, TOK_TM).transpose(1, 0, 2).reshape(n_steps, 1, 2 * TOK_TM)
    gap_len = jnp.concatenate([padded - counts, nt * tm - seg_end[-1:]])
    gap_start = jnp.concatenate([seg_start + counts, seg_end[-1:]])
    gap_end = jnp.cumsum(gap_len)
    m = jnp.arange(n_fill, dtype=jnp.int32)
    gap = jnp.sum(m[:, None] >= gap_end[None, :], axis=1)
    pad_rows = (gap_start[gap] + m - (gap_end - gap_len)[gap]).reshape(n_steps, 1, n_fill // n_steps)
    tile_start = jnp.arange(nt, dtype=jnp.int32) * tm
    tile_e = jnp.minimum(jnp.sum(tile_start[:, None] >= seg_end[None, :], axis=1), N_EXPERTS - 1)
    n_tiles = (seg_end[-1] // tm).reshape(1)
    return tile_e.astype(jnp.int32), n_tiles.astype(jnp.int32), pos_rows, pad_rows.astype(jnp.int32), nt * tm


def _layer(h, p_i, g_mix, w_in, g_v_a, w_s, b_s, g_q, g_k, g_out_a, g_out_b, w_out, g_ffn, w_group,
           w_expert, w_gate_e, w_up_e, w_down_e, g_ple, w_ple, w_ple_gate):
    b, s, d = h.shape
    t = b * s
    row = lambda g: g.reshape(1, -1).astype(F32)
    wuv = w_in[:, :2 * D_A].astype(BF16)
    wqt = w_in[:, 2 * D_A:2 * D_A + D_B].T.astype(BF16)
    wk = w_in[:, 2 * D_A + D_B:2 * D_A + 2 * D_B].astype(BF16)
    wvt = w_in[:, 2 * D_A + 2 * D_B:].T.astype(BF16)
    bs = jnp.broadcast_to(b_s[:, :, None], (A_GROUPS, A_CHUNK, A_DH)).astype(F32)
    gk = row(jnp.tile(g_k, B_HEADS))
    gq = (jnp.tile(g_q, B_HEADS) * (1.0 / float(B_DH) ** 0.5)).reshape(D_B, 1).astype(F32)
    head = jnp.arange(D_B) // B_DH
    bd = jnp.where(head[:, None] == head[None, :], 1.0 / B_DH, 0.0).astype(BF16)
    w_r = jnp.zeros((d, ROUTER_LANES), F32)
    w_r = w_r.at[:, :N_GROUPS].set(w_group).at[:, N_GROUPS:N_GROUPS + N_EXPERTS].set(w_expert)
    wr_hi, wr_lo = _split_bf16(w_r)

    ya, k, qt, vt = _proj_call(h, row(g_mix), wuv, wk, wqt, wvt, row(g_v_a), w_s.astype(F32), bs,
                               row(g_out_a), gk, gq, bd)
    yb = _moba_call(qt, k, vt, row(g_out_b))
    h1, hn_tiles, route, route_t, counts = _outmix_call(
        h.reshape(t, d), ya.reshape(t, D_A), yb.reshape(t, D_B), w_out.astype(BF16), row(g_ffn), wr_hi, wr_lo)
    counts = counts[0, N_GROUPS:N_GROUPS + N_EXPERTS].astype(jnp.int32)
    tile_e, n_tiles, pos_rows, pad_rows, n_sorted_rows = _routing_tables(route_t, counts, t)
    xs, back_rows = _dispatch_call(pos_rows, pad_rows, hn_tiles, n_sorted_rows)
    yt = _experts_call(tile_e, n_tiles, back_rows.reshape(n_sorted_rows // EXP_TM, 1, EXP_TM), xs,
                       w_gate_e, w_up_e, w_down_e, n_sorted_rows)
    out = _ple_call(yt, h1, route, p_i.reshape(t, D_PLE), w_ple.astype(BF16), row(g_ple),
                    w_ple_gate.astype(BF16))
    return out.reshape(b, s, d)


def kernel(x, p, g_mix, w_in, g_v_a, w_s, b_s, g_q, g_k, g_out_a, g_out_b, w_out, g_ffn, w_group, w_expert,
           w_gate_e, w_up_e, w_down_e, g_ple, w_ple, w_ple_gate):
    params = (g_mix, w_in, g_v_a, w_s, b_s, g_q, g_k, g_out_a, g_out_b, w_out, g_ffn, w_group, w_expert,
              w_gate_e, w_up_e, w_down_e, g_ple, w_ple, w_ple_gate)
    h = x
    for i in range(p.shape[0]):
        h = _layer(h, p[i], *(w[i] for w in params))
    return h
```

```python
import functools

import jax
import jax.numpy as jnp
from jax import lax
from jax.experimental import pallas as pl
from jax.experimental.pallas import tpu as pltpu

F32 = jnp.float32
BF16 = jnp.bfloat16

D_MODEL = 1024
D_A = 512
A_GROUPS = 4
A_DH = D_A // A_GROUPS
A_CHUNK = 128
D_B = 512
B_HEADS = 8
B_DH = D_B // B_HEADS
B_BLOCK = 256
B_TOPK = 3
D_PLE = 256
N_GROUPS = 4
EXP_PER_GROUP = 8
N_EXPERTS = N_GROUPS * EXP_PER_GROUP
D_FF_EXP = 256
EPS = 1e-6
NEG = -1e30
LOG2E = 1.4426950408889634

LANES = 128
HEAD_PAIR = 2 * B_DH
N_PAIRS = B_HEADS // 2
ROUTER_ROWS = 48
ROUTE_FIELDS = 8
VMEM_LIMIT = 56 * 1024 * 1024

PROJ_TM = 512
OUT_TM = 512
EXP_TM = 256
MOBA_UNROLL = 2
TOK_TM = 256
ROW_SUB = D_MODEL // LANES


def _rms(x, g):
    return x * lax.rsqrt(jnp.mean(x * x, axis=-1, keepdims=True) + EPS) * g


def _dot(a, b):
    return jnp.dot(a, b, preferred_element_type=F32)


def _dot_nt(a, b):
    return lax.dot_general(a, b, (((1,), (1,)), ((), ())), preferred_element_type=F32)


def _split_bf16(x):
    hi = x.astype(BF16)
    lo = (x - hi.astype(F32)).astype(BF16)
    return hi, lo


def _proj_kernel(x_ref, gmix_ref, wuv_ref, wk_ref, wqt_ref, wvt_ref, gva_ref, ws_ref, bs_ref,
                 goa_ref, gk_ref, gq_ref, bd_ref,
                 ya_ref, k_ref, qt_ref, vt_ref, prod_ref):
    tm = x_ref.shape[1]
    n_chunks = tm // A_CHUNK
    xn = _rms(x_ref[0], gmix_ref[...]).astype(BF16)

    uv = _dot(xn, wuv_ref[...])
    u = jax.nn.gelu(uv[:, :D_A])
    v = jax.nn.gelu(uv[:, D_A:])
    row = lax.broadcasted_iota(jnp.int32, (A_CHUNK, A_CHUNK), 0)
    col = lax.broadcasted_iota(jnp.int32, (A_CHUNK, A_CHUNK), 1)
    causal = col <= row
    for g in range(A_GROUPS):
        gs = slice(g * A_DH, (g + 1) * A_DH)
        vn = _rms(v[:, gs], gva_ref[:, gs]).astype(BF16)
        rhs = jnp.concatenate([vn[c * A_CHUNK:(c + 1) * A_CHUNK] for c in range(n_chunks)], axis=1)
        ws = jnp.where(causal, ws_ref[g], 0.0).astype(BF16)
        mixed = _dot(ws, rhs)
        for c in range(n_chunks):
            cs = slice(c * A_CHUNK, (c + 1) * A_CHUNK)
            prod_ref[cs, gs] = u[cs, gs] * (mixed[:, cs] + bs_ref[g])
    ya_ref[0] = _rms(prod_ref[...], goa_ref[...]).astype(ya_ref.dtype)

    k = _dot(xn, wk_ref[...])
    k2_hi, k2_lo = _split_bf16(k * k)
    k_ms = _dot(k2_hi, bd_ref[...]) + _dot(k2_lo, bd_ref[...])
    k_ref[0] = (k * lax.rsqrt(k_ms + EPS) * gk_ref[...]).astype(k_ref.dtype)

    qt = _dot_nt(wqt_ref[...], xn)
    vt = _dot_nt(wvt_ref[...], xn).astype(vt_ref.dtype)
    qn = []
    for h in range(B_HEADS):
        qh = qt[h * B_DH:(h + 1) * B_DH]
        qn.append(qh * lax.rsqrt(jnp.mean(qh * qh, axis=0, keepdims=True) + EPS))
    qn = (jnp.concatenate(qn, axis=0) * gq_ref[...]).astype(qt_ref.dtype)
    for p in range(N_PAIRS):
        ps = slice(p * HEAD_PAIR, (p + 1) * HEAD_PAIR)
        for j in range(tm // B_BLOCK):
            js = slice(j * B_BLOCK, (j + 1) * B_BLOCK)
            qt_ref[0, p, j] = qn[ps, js]
            vt_ref[0, p, j] = vt[ps, js]


def _proj_call(x, gmix, wuv, wk, wqt, wvt, gva, ws, bs, goa, gk, gq, bd):
    b, s, d = x.shape
    tm = PROJ_TM
    nb = s // B_BLOCK
    jb = tm // B_BLOCK
    const = lambda *shape: pl.BlockSpec(shape, lambda bi, ti: (0,) * len(shape))
    return pl.pallas_call(
        _proj_kernel,
        grid=(b, s // tm),
        in_specs=[
            pl.BlockSpec((1, tm, d), lambda bi, ti: (bi, ti, 0)),
            const(1, d), const(d, 2 * D_A), const(d, D_B), const(D_B, d), const(D_B, d),
            const(1, D_A), const(A_GROUPS, A_CHUNK, A_CHUNK), const(A_GROUPS, A_CHUNK, A_DH),
            const(1, D_A), const(1, D_B), const(D_B, 1), const(D_B, D_B),
        ],
        out_specs=[
            pl.BlockSpec((1, tm, D_A), lambda bi, ti: (bi, ti, 0)),
            pl.BlockSpec((1, tm, D_B), lambda bi, ti: (bi, ti, 0)),
            pl.BlockSpec((1, N_PAIRS, jb, HEAD_PAIR, B_BLOCK), lambda bi, ti: (bi, 0, ti, 0, 0)),
            pl.BlockSpec((1, N_PAIRS, jb, HEAD_PAIR, B_BLOCK), lambda bi, ti: (bi, 0, ti, 0, 0)),
        ],
        out_shape=[
            jax.ShapeDtypeStruct((b, s, D_A), BF16),
            jax.ShapeDtypeStruct((b, s, D_B), BF16),
            jax.ShapeDtypeStruct((b, N_PAIRS, nb, HEAD_PAIR, B_BLOCK), BF16),
            jax.ShapeDtypeStruct((b, N_PAIRS, nb, HEAD_PAIR, B_BLOCK), BF16),
        ],
        scratch_shapes=[pltpu.VMEM((tm, D_A), F32)],
        compiler_params=pltpu.CompilerParams(
            dimension_semantics=("parallel", "parallel"), vmem_limit_bytes=VMEM_LIMIT),
        name="proj",
    )(x, gmix, wuv, wk, wqt, wvt, gva, ws, bs, goa, gk, gq, bd)


def _moba_kernel(qt_ref, k_ref, vt_ref, gob_ref, y_ref, kmean_ref, qz_ref, sel_ref, m_ref, l_ref, acc_ref,
                 s_ref):
    qb = pl.program_id(1)
    nb = k_ref.shape[1] // B_BLOCK
    nbp = sel_ref.shape[1]

    @pl.when(qb == 0)
    def _():
        rows = [jnp.mean(k_ref[0, j * B_BLOCK:(j + 1) * B_BLOCK, :].astype(F32), axis=0, keepdims=True)
                for j in range(nb)]
        rows += [jnp.zeros((1, D_B), F32)] * (nbp - nb)
        km = jnp.concatenate(rows, axis=0)
        hi = km.astype(BF16)
        rest = km - hi.astype(F32)
        mid = rest.astype(BF16)
        lo = (rest - mid.astype(F32)).astype(BF16)
        for p in range(N_PAIRS):
            ps = slice(p * HEAD_PAIR, (p + 1) * HEAD_PAIR)
            kmean_ref[p] = jnp.concatenate([hi[:, ps], mid[:, ps], lo[:, ps]], axis=0)

    head_row = lax.broadcasted_iota(jnp.int32, (HEAD_PAIR, B_BLOCK), 0) // B_DH
    blk = lax.broadcasted_iota(jnp.int32, (nbp, B_BLOCK), 0)
    past = blk < qb

    for h in range(B_HEADS):
        p, hh = divmod(h, 2)
        q_pair = qt_ref[0, p, 0]
        qz = jnp.where(head_row == hh, q_pair, jnp.zeros_like(q_pair))
        qz_ref[h] = qz
        g3 = _dot(kmean_ref[p], qz)
        gate = (g3[:nbp] + g3[nbp:2 * nbp] + g3[2 * nbp:]) * float(B_DH) ** 0.5
        gate = jnp.where(past, gate, NEG)
        rank = jnp.zeros((nbp, B_BLOCK), F32)
        for j in range(nb):
            gj = gate[j:j + 1]
            ahead = (gj > gate) | ((gj == gate) & (blk > j))
            rank = rank + jnp.where(ahead, 1.0, 0.0)
        chosen = (rank < float(B_TOPK)) & (gate > NEG * 0.5)
        sel_ref[h] = jnp.where(chosen, 1.0, 0.0)

    def scores(h, j):
        p = h // 2
        kj = k_ref[0, pl.ds(pl.multiple_of(j * B_BLOCK, B_BLOCK), B_BLOCK), p * HEAD_PAIR:(p + 1) * HEAD_PAIR]
        return _dot(kj, qz_ref[h])

    def values(h, j):
        p, hh = divmod(h, 2)
        return vt_ref[0, p, j, hh * B_DH:(hh + 1) * B_DH, :]

    key_pos = lax.broadcasted_iota(jnp.int32, (B_BLOCK, B_BLOCK), 0)
    q_pos = lax.broadcasted_iota(jnp.int32, (B_BLOCK, B_BLOCK), 1)
    causal = key_pos <= q_pos

    def stage_scores(h, j, visible):
        s = jnp.where(visible, scores(h, j) * LOG2E, NEG)
        s_ref[h, j] = s
        return jnp.max(s, axis=0, keepdims=True)

    for h in range(B_HEADS):
        m_ref[h] = stage_scores(h, qb, causal)
        l_ref[h] = jnp.zeros((1, B_BLOCK), F32)
        acc_ref[h] = jnp.zeros((B_DH, B_BLOCK), F32)

    def score_pass(n_blocks, j0):
        for h in range(B_HEADS):
            m = m_ref[h]
            for u in range(n_blocks):
                m = jnp.maximum(m, stage_scores(h, j0 + u, sel_ref[h, pl.ds(j0 + u, 1), :] > 0.5))
            m_ref[h] = m

    def value_pass(n_blocks, j0):
        for h in range(B_HEADS):
            l, acc = l_ref[h], acc_ref[h]
            for u in range(n_blocks):
                pr = jnp.exp2(s_ref[h, j0 + u] - m_ref[h])
                l = l + jnp.sum(pr, axis=0, keepdims=True)
                acc = acc + _dot(values(h, j0 + u), pr.astype(BF16))
            l_ref[h], acc_ref[h] = l, acc

    def run(n, body):
        u = MOBA_UNROLL

        def trips(i, carry):
            body(u, i * u)
            return carry

        def rest(j, carry):
            body(1, j)
            return carry

        lax.fori_loop(0, n // u, trips, 0)
        lax.fori_loop((n // u) * u, n, rest, 0)

    run(qb, score_pass)
    run(qb + 1, value_pass)

    yt = jnp.concatenate([acc_ref[h] / l_ref[h] for h in range(B_HEADS)], axis=0)
    yt = yt * lax.rsqrt(jnp.mean(yt * yt, axis=0, keepdims=True) + EPS)
    y_ref[0] = (yt.T * gob_ref[...]).astype(y_ref.dtype)


def _moba_call(qt, k, vt, gob):
    b, s, _ = k.shape
    nb = s // B_BLOCK
    nbp = -(-nb // 8) * 8
    return pl.pallas_call(
        _moba_kernel,
        grid=(b, nb),
        in_specs=[
            pl.BlockSpec((1, N_PAIRS, 1, HEAD_PAIR, B_BLOCK), lambda bi, qi: (bi, 0, qi, 0, 0)),
            pl.BlockSpec((1, s, D_B), lambda bi, qi: (bi, 0, 0)),
            pl.BlockSpec((1, N_PAIRS, nb, HEAD_PAIR, B_BLOCK), lambda bi, qi: (bi, 0, 0, 0, 0)),
            pl.BlockSpec((1, D_B), lambda bi, qi: (0, 0)),
        ],
        out_specs=pl.BlockSpec((1, B_BLOCK, D_B), lambda bi, qi: (bi, qi, 0)),
        out_shape=jax.ShapeDtypeStruct((b, s, D_B), BF16),
        scratch_shapes=[
            pltpu.VMEM((N_PAIRS, 3 * nbp, HEAD_PAIR), BF16),
            pltpu.VMEM((B_HEADS, HEAD_PAIR, B_BLOCK), BF16),
            pltpu.VMEM((B_HEADS, nbp, B_BLOCK), F32),
            pltpu.VMEM((B_HEADS, 1, B_BLOCK), F32),
            pltpu.VMEM((B_HEADS, 1, B_BLOCK), F32),
            pltpu.VMEM((B_HEADS, B_DH, B_BLOCK), F32),
            pltpu.VMEM((B_HEADS, nb, B_BLOCK, B_BLOCK), F32),
        ],
        compiler_params=pltpu.CompilerParams(
            dimension_semantics=("parallel", "arbitrary"), vmem_limit_bytes=VMEM_LIMIT),
        name="moba",
    )(qt, k, vt, gob)


def _outmix_kernel(x_ref, ya_ref, yb_ref, wout_ref, gffn_ref, wrt_hi_ref, wrt_lo_ref,
                   h_ref, hn_ref, route_ref, route_i_ref, counts_ref, cnt_ref):
    h = x_ref[...] + _dot(ya_ref[...], wout_ref[:D_A]) + _dot(yb_ref[...], wout_ref[D_A:])
    h_ref[...] = h
    hn = _rms(h, gffn_ref[...])
    _to_row_tiles(hn_ref, hn)
    tm = hn.shape[0]

    hn_hi, hn_lo = _split_bf16(hn)
    lg = _dot_nt(wrt_hi_ref[...], hn_hi) + (_dot_nt(wrt_hi_ref[...], hn_lo) + _dot_nt(wrt_lo_ref[...], hn_hi))
    row = lax.broadcasted_iota(jnp.int32, lg.shape, 0)

    def first_row(mask):
        return jnp.min(jnp.where(mask, row, ROUTER_ROWS), axis=0, keepdims=True)

    is_g = row < N_GROUPS
    g_max = jnp.max(jnp.where(is_g, lg, -jnp.inf), axis=0, keepdims=True)
    p_g = 1.0 / jnp.sum(jnp.where(is_g, jnp.exp(lg - g_max), 0.0), axis=0, keepdims=True)
    g_sel = first_row(is_g & (lg == g_max))
    e_row = row - N_GROUPS
    in_grp = (e_row >= 0) & (e_row < N_EXPERTS) & (lax.shift_right_arithmetic(e_row, 3) == g_sel)
    l1 = jnp.max(jnp.where(in_grp, lg, -jnp.inf), axis=0, keepdims=True)
    i1 = first_row(in_grp & (lg == l1))
    rest = in_grp & (row != i1)
    l2 = jnp.max(jnp.where(rest, lg, -jnp.inf), axis=0, keepdims=True)
    i2 = first_row(rest & (lg == l2))
    e2 = jnp.exp(l2 - l1)
    w1 = 1.0 / (1.0 + e2)
    w2 = e2 / (1.0 + e2)

    @pl.when(pl.program_id(0) == 0)
    def _():
        cnt_ref[...] = jnp.zeros_like(cnt_ref)

    picked = jnp.where((row == i1) | (row == i2), 1.0, 0.0)
    tok_r = lax.broadcasted_iota(jnp.int32, (tm, tm), 0)
    tok_c = lax.broadcasted_iota(jnp.int32, (tm, tm), 1)
    earlier = jnp.where(tok_r < tok_c, 1.0, 0.0).astype(BF16)
    before = _dot(picked.astype(BF16), earlier) + cnt_ref[:, 0:1]
    r1 = jnp.sum(jnp.where(row == i1, before, 0.0), axis=0, keepdims=True)
    r2 = jnp.sum(jnp.where(row == i2, before, 0.0), axis=0, keepdims=True)
    cnt_ref[...] += jnp.sum(picked, axis=1, keepdims=True)
    counts_ref[...] = cnt_ref[...]

    e1, e2x = i1 - N_GROUPS, i2 - N_GROUPS
    route_i_ref[...] = jnp.concatenate(
        [e1, e2x, r1.astype(jnp.int32), r2.astype(jnp.int32), jnp.zeros((ROUTE_FIELDS - 4, tm), jnp.int32)], axis=0)
    wts = jnp.concatenate([p_g * w1, p_g * w2, jnp.zeros((LANES - 2, tm), F32)], axis=0)
    route_ref[...] = wts.T


def _outmix_call(x2, ya2, yb2, wout, gffn, wrt_hi, wrt_lo):
    t, d = x2.shape
    tm = OUT_TM
    const = lambda *shape: pl.BlockSpec(shape, lambda ti: (0,) * len(shape))
    tok = lambda w: pl.BlockSpec((tm, w), lambda ti: (ti, 0))
    return pl.pallas_call(
        _outmix_kernel,
        grid=(t // tm,),
        in_specs=[tok(d), tok(D_A), tok(D_B), const(d, d), const(1, d),
                  const(ROUTER_ROWS, d), const(ROUTER_ROWS, d)],
        out_specs=[tok(d), pl.BlockSpec((tm * ROW_SUB, LANES), lambda ti: (ti, 0)), tok(LANES),
                   pl.BlockSpec((ROUTE_FIELDS, tm), lambda ti: (0, ti)), const(ROUTER_ROWS, LANES)],
        out_shape=[jax.ShapeDtypeStruct((t, d), F32), jax.ShapeDtypeStruct((t * ROW_SUB, LANES), F32),
                   jax.ShapeDtypeStruct((t, LANES), F32),
                   jax.ShapeDtypeStruct((ROUTE_FIELDS, t), jnp.int32),
                   jax.ShapeDtypeStruct((ROUTER_ROWS, LANES), F32)],
        scratch_shapes=[pltpu.VMEM((ROUTER_ROWS, LANES), F32)],
        compiler_params=pltpu.CompilerParams(
            dimension_semantics=("arbitrary",), vmem_limit_bytes=VMEM_LIMIT),
        name="outmix",
    )(x2, ya2, yb2, wout, gffn, wrt_hi, wrt_lo)


def _to_row_tiles(ref, x):
    n = x.shape[0]
    for s in range(ROW_SUB):
        ref[pl.ds(s, n, stride=ROW_SUB), :] = x[:, s * LANES:(s + 1) * LANES]


def _from_row_tiles(ref, first, n):
    return jnp.concatenate(
        [ref[pl.ds(first * ROW_SUB + s, n, stride=ROW_SUB), :] for s in range(ROW_SUB)], axis=1)


def _tiles_at(ref, row, n=1):
    return ref.at[pl.ds(pl.multiple_of(row * ROW_SUB, ROW_SUB), n * ROW_SUB), :]


def _wait_tiles(hbm, vmem_ref, n_tiles, sem):
    cap = vmem_ref.shape[0] // ROW_SUB
    while n_tiles > 0:
        n = min(n_tiles, cap)
        pltpu.make_async_copy(hbm.at[pl.ds(0, n * ROW_SUB), :], vmem_ref.at[pl.ds(0, n * ROW_SUB), :], sem).wait()
        n_tiles -= n


def _dispatch_kernel(seg_start_ref, gap_start_ref, gap_len_ref, route_i_ref, hn_ref, xs_hbm, pos_ref,
                     zero_ref, sem, fill_sem):
    tm = hn_ref.shape[0] // ROW_SUB

    @pl.when(pl.program_id(0) == 0)
    def _():
        zero_ref[...] = jnp.zeros_like(zero_ref)
        fills = []
        for g in range(N_EXPERTS):
            row, n = gap_start_ref[g], gap_len_ref[g]
            size = EXP_TM // 2
            while size >= 1:
                fills.append(((n & size) != 0, row, size))
                row = row + (n & size)
                size //= 2
        for tile in range(N_EXPERTS):
            fills.append((tile * EXP_TM < gap_len_ref[N_EXPERTS], gap_start_ref[N_EXPERTS] + tile * EXP_TM, EXP_TM))
        copies = [(cond, pltpu.make_async_copy(_tiles_at(zero_ref, 0, size), _tiles_at(xs_hbm, row, size), fill_sem))
                  for cond, row, size in fills]
        for cond, copy in copies:
            pl.when(cond)(copy.start)
        for cond, copy in copies:
            pl.when(cond)(copy.wait)

    for r in range(2 * tm):
        k, tok = divmod(r, tm)
        slot = seg_start_ref[route_i_ref[k, tok]] + route_i_ref[2 + k, tok]
        pos_ref[0, 0, r] = slot
        pltpu.make_async_copy(_tiles_at(hn_ref, tok), _tiles_at(xs_hbm, slot), sem).start(priority=r % 2)
    _wait_tiles(xs_hbm, hn_ref, 2 * tm, sem)


def _dispatch_call(seg_start, gap_start, gap_len, route_i, hn_tiles, n_sorted_rows):
    tm = TOK_TM
    n_steps = route_i.shape[1] // tm
    return pl.pallas_call(
        _dispatch_kernel,
        grid_spec=pltpu.PrefetchScalarGridSpec(
            num_scalar_prefetch=3,
            grid=(n_steps,),
            in_specs=[pl.BlockSpec((ROUTE_FIELDS, tm), lambda i, *_: (0, i), memory_space=pltpu.SMEM),
                      pl.BlockSpec((tm * ROW_SUB, LANES), lambda i, *_: (i, 0))],
            out_specs=[pl.BlockSpec(memory_space=pl.ANY),
                       pl.BlockSpec((1, 1, 2 * tm), lambda i, *_: (i, 0, 0), memory_space=pltpu.SMEM)],
            scratch_shapes=[pltpu.VMEM((EXP_TM * ROW_SUB, LANES), F32), pltpu.SemaphoreType.DMA(()),
                            pltpu.SemaphoreType.DMA(())],
        ),
        out_shape=[jax.ShapeDtypeStruct((n_sorted_rows * ROW_SUB, LANES), F32),
                   jax.ShapeDtypeStruct((n_steps, 1, 2 * tm), jnp.int32)],
        compiler_params=pltpu.CompilerParams(
            dimension_semantics=("arbitrary",), vmem_limit_bytes=VMEM_LIMIT),
        name="dispatch",
    )(seg_start, gap_start, gap_len, route_i, hn_tiles)


def _experts_kernel(tile_e_ref, n_tiles_ref, xs_ref, wg_ref, wu_ref, wd_ref, y_ref, wg16, wu16, wd16):
    i = pl.program_id(0)
    tm = xs_ref.shape[0] // ROW_SUB

    @pl.when((i == 0) | (tile_e_ref[i] != tile_e_ref[jnp.maximum(i - 1, 0)]))
    def _():
        wg16[...] = wg_ref[0].astype(BF16)
        wu16[...] = wu_ref[0].astype(BF16)
        wd16[...] = wd_ref[0].astype(BF16)

    @pl.when(i < n_tiles_ref[0])
    def _():
        x = _from_row_tiles(xs_ref, 0, tm).astype(BF16)
        a = jax.nn.silu(_dot(x, wg16[...])) * _dot(x, wu16[...])
        _to_row_tiles(y_ref, _dot(a.astype(BF16), wd16[...]))

    @pl.when(i >= n_tiles_ref[0])
    def _():
        y_ref[...] = jnp.zeros_like(y_ref)


def _experts_call(tile_e, n_tiles, xs, wg, wu, wd):
    d = wg.shape[1]
    tm = EXP_TM
    nt = xs.shape[0] // (tm * ROW_SUB)
    used = lambda i, n: jnp.minimum(i, n[0] - 1)
    w_spec = lambda *shape: pl.BlockSpec((1,) + shape, lambda i, te, n: (te[used(i, n)], 0, 0))
    return pl.pallas_call(
        _experts_kernel,
        grid_spec=pltpu.PrefetchScalarGridSpec(
            num_scalar_prefetch=2,
            grid=(nt,),
            in_specs=[pl.BlockSpec((tm * ROW_SUB, LANES), lambda i, te, n: (used(i, n), 0)),
                      w_spec(d, D_FF_EXP), w_spec(d, D_FF_EXP), w_spec(D_FF_EXP, d)],
            out_specs=pl.BlockSpec((tm * ROW_SUB, LANES), lambda i, te, n: (i, 0)),
            scratch_shapes=[pltpu.VMEM((d, D_FF_EXP), BF16), pltpu.VMEM((d, D_FF_EXP), BF16),
                            pltpu.VMEM((D_FF_EXP, d), BF16)],
        ),
        out_shape=jax.ShapeDtypeStruct(xs.shape, F32),
        compiler_params=pltpu.CompilerParams(
            dimension_semantics=("arbitrary",), vmem_limit_bytes=VMEM_LIMIT),
        name="experts",
    )(tile_e, n_tiles, xs, wg, wu, wd)


def _ple_kernel(cur_ref, nxt_ref, ys_hbm, h_ref, route_ref, p_ref, wple_ref, gple_ref, wpg_ref,
                o_ref, ybuf0, ybuf1, sem):
    tm = h_ref.shape[0]
    i = pl.program_id(0)
    n_steps = pl.num_programs(0)
    bufs = (ybuf0, ybuf1)

    def start_gather(idx_ref, slot):
        for r in range(2 * tm):
            pltpu.make_async_copy(_tiles_at(ys_hbm, idx_ref[0, 0, r]), _tiles_at(bufs[slot], r),
                                  sem.at[slot]).start(priority=r % 2)

    @pl.when(i == 0)
    def _():
        start_gather(cur_ref, 0)

    def step(slot):
        _wait_tiles(ys_hbm, bufs[slot], 2 * tm, sem.at[slot])
        start_gather(nxt_ref, 1 - slot)
        wts = route_ref[...]
        h = (h_ref[...] + wts[:, 0:1] * _from_row_tiles(bufs[slot], 0, tm)
             + wts[:, 1:2] * _from_row_tiles(bufs[slot], tm, tm))
        gate = jax.nn.sigmoid(_dot(_rms(h, gple_ref[...]).astype(BF16), wpg_ref[...]))
        o_ref[...] = h + _dot(p_ref[...].astype(BF16), wple_ref[...]) * gate

    for slot in range(2):
        pl.when(lax.rem(i, 2) == slot)(functools.partial(step, slot))

    for slot in range(2):
        pl.when((i == n_steps - 1) & (lax.rem(n_steps, 2) == slot))(
            functools.partial(_wait_tiles, ys_hbm, bufs[slot], 2 * tm, sem.at[slot]))


def _ple_call(pos_rows, ys, h, route, p2, wple, gple, wpg):
    t, d = h.shape
    n_steps, _, n_rows = pos_rows.shape
    tm = n_rows // 2
    idx_spec = lambda f: pl.BlockSpec((1, 1, n_rows), lambda i: (f(i), 0, 0), memory_space=pltpu.SMEM)
    const = lambda *shape: pl.BlockSpec(shape, lambda i: (0,) * len(shape))
    tok = lambda w: pl.BlockSpec((tm, w), lambda i: (i, 0))
    return pl.pallas_call(
        _ple_kernel,
        grid=(n_steps,),
        in_specs=[idx_spec(lambda i: i), idx_spec(lambda i: lax.rem(i + 1, n_steps)),
                  pl.BlockSpec(memory_space=pl.ANY),
                  tok(d), tok(LANES), tok(D_PLE), const(D_PLE, d), const(1, d), const(d, d)],
        out_specs=tok(d),
        out_shape=jax.ShapeDtypeStruct((t, d), F32),
        scratch_shapes=[pltpu.VMEM((n_rows * ROW_SUB, LANES), F32), pltpu.VMEM((n_rows * ROW_SUB, LANES), F32),
                        pltpu.SemaphoreType.DMA((2,))],
        compiler_params=pltpu.CompilerParams(
            dimension_semantics=("arbitrary",), vmem_limit_bytes=VMEM_LIMIT),
        name="ple",
    )(pos_rows, pos_rows, ys, h, route, p2, wple, gple, wpg)


def _segment_tables(counts, t):
    tm = EXP_TM
    nt = 2 * t // tm + N_EXPERTS
    padded = ((counts + tm - 1) // tm) * tm
    seg_end = jnp.cumsum(padded)
    seg_start = seg_end - padded
    gap_len = jnp.concatenate([padded - counts, nt * tm - seg_end[-1:]])
    gap_start = jnp.concatenate([seg_start + counts, seg_end[-1:]])
    tile_start = jnp.arange(nt, dtype=jnp.int32) * tm
    tile_e = jnp.minimum(jnp.sum(tile_start[:, None] >= seg_end[None, :], axis=1), N_EXPERTS - 1)
    n_tiles = (seg_end[-1] // tm).reshape(1)
    i32 = lambda a: a.astype(jnp.int32)
    return i32(seg_start), i32(gap_start), i32(gap_len), i32(tile_e), i32(n_tiles), nt * tm


def _layer(h, p_i, g_mix, w_in, g_v_a, w_s, b_s, g_q, g_k, g_out_a, g_out_b, w_out, g_ffn, w_group,
           w_expert, w_gate_e, w_up_e, w_down_e, g_ple, w_ple, w_ple_gate):
    b, s, d = h.shape
    t = b * s
    row = lambda g: g.reshape(1, -1).astype(F32)
    wuv = w_in[:, :2 * D_A].astype(BF16)
    wqt = w_in[:, 2 * D_A:2 * D_A + D_B].T.astype(BF16)
    wk = w_in[:, 2 * D_A + D_B:2 * D_A + 2 * D_B].astype(BF16)
    wvt = w_in[:, 2 * D_A + 2 * D_B:].T.astype(BF16)
    bs = jnp.broadcast_to(b_s[:, :, None], (A_GROUPS, A_CHUNK, A_DH)).astype(F32)
    gk = row(jnp.tile(g_k, B_HEADS))
    gq = (jnp.tile(g_q, B_HEADS) * (1.0 / float(B_DH) ** 0.5)).reshape(D_B, 1).astype(F32)
    head = jnp.arange(D_B) // B_DH
    bd = jnp.where(head[:, None] == head[None, :], 1.0 / B_DH, 0.0).astype(BF16)
    w_rt = jnp.zeros((ROUTER_ROWS, d), F32)
    w_rt = w_rt.at[:N_GROUPS].set(w_group.T).at[N_GROUPS:N_GROUPS + N_EXPERTS].set(w_expert.T)
    wrt_hi, wrt_lo = _split_bf16(w_rt)

    ya, k, qt, vt = _proj_call(h, row(g_mix), wuv, wk, wqt, wvt, row(g_v_a), w_s.astype(F32), bs,
                               row(g_out_a), gk, gq, bd)
    yb = _moba_call(qt, k, vt, row(g_out_b))
    h1, hn_tiles, route, route_i, counts = _outmix_call(
        h.reshape(t, d), ya.reshape(t, D_A), yb.reshape(t, D_B), w_out.astype(BF16), row(g_ffn), wrt_hi, wrt_lo)
    counts = counts[N_GROUPS:N_GROUPS + N_EXPERTS, 0].astype(jnp.int32)
    seg_start, gap_start, gap_len, tile_e, n_tiles, n_sorted_rows = _segment_tables(counts, t)
    xs, pos_rows = _dispatch_call(seg_start, gap_start, gap_len, route_i, hn_tiles, n_sorted_rows)
    ys = _experts_call(tile_e, n_tiles, xs, w_gate_e, w_up_e, w_down_e)
    out = _ple_call(pos_rows, ys, h1, route, p_i.reshape(t, D_PLE), w_ple.astype(BF16), row(g_ple),
                    w_ple_gate.astype(BF16))
    return out.reshape(b, s, d)


def kernel(x, p, g_mix, w_in, g_v_a, w_s, b_s, g_q, g_k, g_out_a, g_out_b, w_out, g_ffn, w_group, w_expert,
           w_gate_e, w_up_e, w_down_e, g_ple, w_ple, w_ple_gate):
    params = (g_mix, w_in, g_v_a, w_s, b_s, g_q, g_k, g_out_a, g_out_b, w_out, g_ffn, w_group, w_expert,
              w_gate_e, w_up_e, w_down_e, g_ple, w_ple, w_ple_gate)
    h = x
    for i in range(p.shape[0]):
        h = _layer(h, p[i], *(w[i] for w in params))
    return h
```

```python
import functools

import jax
import jax.numpy as jnp
from jax import lax
from jax.experimental import pallas as pl
from jax.experimental.pallas import tpu as pltpu

F32 = jnp.float32
BF16 = jnp.bfloat16

D_MODEL = 1024
D_A = 512
A_GROUPS = 4
A_DH = D_A // A_GROUPS
A_CHUNK = 128
D_B = 512
B_HEADS = 8
B_DH = D_B // B_HEADS
B_BLOCK = 256
B_TOPK = 3
D_PLE = 256
N_GROUPS = 4
EXP_PER_GROUP = 8
N_EXPERTS = N_GROUPS * EXP_PER_GROUP
D_FF_EXP = 256
EPS = 1e-6
NEG = -1e30
LOG2E = 1.4426950408889634

LANES = 128
HEAD_PAIR = 2 * B_DH
N_PAIRS = B_HEADS // 2
ROUTER_ROWS = 48
ROUTE_FIELDS = 8
VMEM_LIMIT = 56 * 1024 * 1024

PROJ_TM = 512
OUT_TM = 512
EXP_TM = 256
MOBA_UNROLL = 2
TOK_TM = 256
ROW_SUB = D_MODEL // LANES


def _rms(x, g):
    return x * lax.rsqrt(jnp.mean(x * x, axis=-1, keepdims=True) + EPS) * g


def _dot(a, b):
    return jnp.dot(a, b, preferred_element_type=F32)


def _dot_nt(a, b):
    return lax.dot_general(a, b, (((1,), (1,)), ((), ())), preferred_element_type=F32)


def _split_bf16(x):
    hi = x.astype(BF16)
    lo = (x - hi.astype(F32)).astype(BF16)
    return hi, lo


def _proj_kernel(x_ref, gmix_ref, wuv_ref, wk_ref, wqt_ref, wvt_ref, gva_ref, ws_ref, bs_ref,
                 goa_ref, gk_ref, gq_ref, bd_ref,
                 ya_ref, k_ref, qt_ref, vt_ref, prod_ref):
    tm = x_ref.shape[1]
    n_chunks = tm // A_CHUNK
    xn = _rms(x_ref[0], gmix_ref[...]).astype(BF16)

    uv = _dot(xn, wuv_ref[...])
    u = jax.nn.gelu(uv[:, :D_A])
    v = jax.nn.gelu(uv[:, D_A:])
    row = lax.broadcasted_iota(jnp.int32, (A_CHUNK, A_CHUNK), 0)
    col = lax.broadcasted_iota(jnp.int32, (A_CHUNK, A_CHUNK), 1)
    causal = col <= row
    for g in range(A_GROUPS):
        gs = slice(g * A_DH, (g + 1) * A_DH)
        vn = _rms(v[:, gs], gva_ref[:, gs]).astype(BF16)
        rhs = jnp.concatenate([vn[c * A_CHUNK:(c + 1) * A_CHUNK] for c in range(n_chunks)], axis=1)
        ws = jnp.where(causal, ws_ref[g], 0.0).astype(BF16)
        mixed = _dot(ws, rhs)
        for c in range(n_chunks):
            cs = slice(c * A_CHUNK, (c + 1) * A_CHUNK)
            prod_ref[cs, gs] = u[cs, gs] * (mixed[:, cs] + bs_ref[g])
    ya_ref[0] = _rms(prod_ref[...], goa_ref[...]).astype(ya_ref.dtype)

    k = _dot(xn, wk_ref[...])
    k2_hi, k2_lo = _split_bf16(k * k)
    k_ms = _dot(k2_hi, bd_ref[...]) + _dot(k2_lo, bd_ref[...])
    k_ref[0] = (k * lax.rsqrt(k_ms + EPS) * gk_ref[...]).astype(k_ref.dtype)

    qt = _dot_nt(wqt_ref[...], xn)
    vt = _dot_nt(wvt_ref[...], xn).astype(vt_ref.dtype)
    qn = []
    for h in range(B_HEADS):
        qh = qt[h * B_DH:(h + 1) * B_DH]
        qn.append(qh * lax.rsqrt(jnp.mean(qh * qh, axis=0, keepdims=True) + EPS))
    qn = (jnp.concatenate(qn, axis=0) * gq_ref[...]).astype(qt_ref.dtype)
    for p in range(N_PAIRS):
        ps = slice(p * HEAD_PAIR, (p + 1) * HEAD_PAIR)
        for j in range(tm // B_BLOCK):
            js = slice(j * B_BLOCK, (j + 1) * B_BLOCK)
            qt_ref[0, p, j] = qn[ps, js]
            vt_ref[0, p, j] = vt[ps, js]


def _proj_call(x, gmix, wuv, wk, wqt, wvt, gva, ws, bs, goa, gk, gq, bd):
    b, s, d = x.shape
    tm = PROJ_TM
    nb = s // B_BLOCK
    jb = tm // B_BLOCK
    const = lambda *shape: pl.BlockSpec(shape, lambda bi, ti: (0,) * len(shape))
    return pl.pallas_call(
        _proj_kernel,
        grid=(b, s // tm),
        in_specs=[
            pl.BlockSpec((1, tm, d), lambda bi, ti: (bi, ti, 0)),
            const(1, d), const(d, 2 * D_A), const(d, D_B), const(D_B, d), const(D_B, d),
            const(1, D_A), const(A_GROUPS, A_CHUNK, A_CHUNK), const(A_GROUPS, A_CHUNK, A_DH),
            const(1, D_A), const(1, D_B), const(D_B, 1), const(D_B, D_B),
        ],
        out_specs=[
            pl.BlockSpec((1, tm, D_A), lambda bi, ti: (bi, ti, 0)),
            pl.BlockSpec((1, tm, D_B), lambda bi, ti: (bi, ti, 0)),
            pl.BlockSpec((1, N_PAIRS, jb, HEAD_PAIR, B_BLOCK), lambda bi, ti: (bi, 0, ti, 0, 0)),
            pl.BlockSpec((1, N_PAIRS, jb, HEAD_PAIR, B_BLOCK), lambda bi, ti: (bi, 0, ti, 0, 0)),
        ],
        out_shape=[
            jax.ShapeDtypeStruct((b, s, D_A), BF16),
            jax.ShapeDtypeStruct((b, s, D_B), BF16),
            jax.ShapeDtypeStruct((b, N_PAIRS, nb, HEAD_PAIR, B_BLOCK), BF16),
            jax.ShapeDtypeStruct((b, N_PAIRS, nb, HEAD_PAIR, B_BLOCK), BF16),
        ],
        scratch_shapes=[pltpu.VMEM((tm, D_A), F32)],
        compiler_params=pltpu.CompilerParams(
            dimension_semantics=("parallel", "parallel"), vmem_limit_bytes=VMEM_LIMIT),
        name="proj",
    )(x, gmix, wuv, wk, wqt, wvt, gva, ws, bs, goa, gk, gq, bd)


def _moba_kernel(qt_ref, k_ref, vt_ref, gob_ref, y_ref, kmean_ref, qz_ref, sel_ref, m_ref, l_ref, acc_ref,
                 s_ref):
    qb = pl.program_id(1)
    nb = k_ref.shape[1] // B_BLOCK
    nbp = sel_ref.shape[1]

    @pl.when(qb == 0)
    def _():
        rows = [jnp.mean(k_ref[0, j * B_BLOCK:(j + 1) * B_BLOCK, :].astype(F32), axis=0, keepdims=True)
                for j in range(nb)]
        rows += [jnp.zeros((1, D_B), F32)] * (nbp - nb)
        km = jnp.concatenate(rows, axis=0)
        hi = km.astype(BF16)
        rest = km - hi.astype(F32)
        mid = rest.astype(BF16)
        lo = (rest - mid.astype(F32)).astype(BF16)
        for p in range(N_PAIRS):
            ps = slice(p * HEAD_PAIR, (p + 1) * HEAD_PAIR)
            kmean_ref[p] = jnp.concatenate([hi[:, ps], mid[:, ps], lo[:, ps]], axis=0)

    head_row = lax.broadcasted_iota(jnp.int32, (HEAD_PAIR, B_BLOCK), 0) // B_DH
    blk = lax.broadcasted_iota(jnp.int32, (nbp, B_BLOCK), 0)
    past = blk < qb

    for h in range(B_HEADS):
        p, hh = divmod(h, 2)
        q_pair = qt_ref[0, p, 0]
        qz = jnp.where(head_row == hh, q_pair, jnp.zeros_like(q_pair))
        qz_ref[h] = qz
        g3 = _dot(kmean_ref[p], qz)
        gate = (g3[:nbp] + g3[nbp:2 * nbp] + g3[2 * nbp:]) * float(B_DH) ** 0.5
        gate = jnp.where(past, gate, NEG)
        rank = jnp.zeros((nbp, B_BLOCK), F32)
        for j in range(nb):
            gj = gate[j:j + 1]
            ahead = (gj > gate) | ((gj == gate) & (blk > j))
            rank = rank + jnp.where(ahead, 1.0, 0.0)
        chosen = (rank < float(B_TOPK)) & (gate > NEG * 0.5)
        sel_ref[h] = jnp.where(chosen, 1.0, 0.0)

    def scores(h, j):
        p = h // 2
        kj = k_ref[0, pl.ds(pl.multiple_of(j * B_BLOCK, B_BLOCK), B_BLOCK), p * HEAD_PAIR:(p + 1) * HEAD_PAIR]
        return _dot(kj, qz_ref[h])

    def values(h, j):
        p, hh = divmod(h, 2)
        return vt_ref[0, p, j, hh * B_DH:(hh + 1) * B_DH, :]

    key_pos = lax.broadcasted_iota(jnp.int32, (B_BLOCK, B_BLOCK), 0)
    q_pos = lax.broadcasted_iota(jnp.int32, (B_BLOCK, B_BLOCK), 1)
    causal = key_pos <= q_pos

    def stage_scores(h, j, visible):
        s = jnp.where(visible, scores(h, j) * LOG2E, NEG)
        s_ref[h, j] = s
        return jnp.max(s, axis=0, keepdims=True)

    for h in range(B_HEADS):
        m_ref[h] = stage_scores(h, qb, causal)
        l_ref[h] = jnp.zeros((1, B_BLOCK), F32)
        acc_ref[h] = jnp.zeros((B_DH, B_BLOCK), F32)

    def score_pass(n_blocks, j0):
        for h in range(B_HEADS):
            m = m_ref[h]
            for u in range(n_blocks):
                m = jnp.maximum(m, stage_scores(h, j0 + u, sel_ref[h, pl.ds(j0 + u, 1), :] > 0.5))
            m_ref[h] = m

    def value_pass(n_blocks, j0):
        for h in range(B_HEADS):
            l, acc = l_ref[h], acc_ref[h]
            for u in range(n_blocks):
                pr = jnp.exp2(s_ref[h, j0 + u] - m_ref[h])
                l = l + jnp.sum(pr, axis=0, keepdims=True)
                acc = acc + _dot(values(h, j0 + u), pr.astype(BF16))
            l_ref[h], acc_ref[h] = l, acc

    def run(n, body):
        u = MOBA_UNROLL

        def trips(i, carry):
            body(u, i * u)
            return carry

        def rest(j, carry):
            body(1, j)
            return carry

        lax.fori_loop(0, n // u, trips, 0)
        lax.fori_loop((n // u) * u, n, rest, 0)

    run(qb, score_pass)
    run(qb + 1, value_pass)

    yt = jnp.concatenate([acc_ref[h] / l_ref[h] for h in range(B_HEADS)], axis=0)
    yt = yt * lax.rsqrt(jnp.mean(yt * yt, axis=0, keepdims=True) + EPS)
    y_ref[0] = (yt.T * gob_ref[...]).astype(y_ref.dtype)


def _moba_call(qt, k, vt, gob):
    b, s, _ = k.shape
    nb = s // B_BLOCK
    nbp = -(-nb // 8) * 8
    return pl.pallas_call(
        _moba_kernel,
        grid=(b, nb),
        in_specs=[
            pl.BlockSpec((1, N_PAIRS, 1, HEAD_PAIR, B_BLOCK), lambda bi, qi: (bi, 0, qi, 0, 0)),
            pl.BlockSpec((1, s, D_B), lambda bi, qi: (bi, 0, 0)),
            pl.BlockSpec((1, N_PAIRS, nb, HEAD_PAIR, B_BLOCK), lambda bi, qi: (bi, 0, 0, 0, 0)),
            pl.BlockSpec((1, D_B), lambda bi, qi: (0, 0)),
        ],
        out_specs=pl.BlockSpec((1, B_BLOCK, D_B), lambda bi, qi: (bi, qi, 0)),
        out_shape=jax.ShapeDtypeStruct((b, s, D_B), BF16),
        scratch_shapes=[
            pltpu.VMEM((N_PAIRS, 3 * nbp, HEAD_PAIR), BF16),
            pltpu.VMEM((B_HEADS, HEAD_PAIR, B_BLOCK), BF16),
            pltpu.VMEM((B_HEADS, nbp, B_BLOCK), F32),
            pltpu.VMEM((B_HEADS, 1, B_BLOCK), F32),
            pltpu.VMEM((B_HEADS, 1, B_BLOCK), F32),
            pltpu.VMEM((B_HEADS, B_DH, B_BLOCK), F32),
            pltpu.VMEM((B_HEADS, nb, B_BLOCK, B_BLOCK), F32),
        ],
        compiler_params=pltpu.CompilerParams(
            dimension_semantics=("parallel", "arbitrary"), vmem_limit_bytes=VMEM_LIMIT),
        name="moba",
    )(qt, k, vt, gob)


def _outmix_kernel(x_ref, ya_ref, yb_ref, wout_ref, gffn_ref, wrt_hi_ref, wrt_lo_ref,
                   h_ref, hn_ref, route_ref, route_i_ref, counts_ref, cnt_ref):
    h = x_ref[...] + _dot(ya_ref[...], wout_ref[:D_A]) + _dot(yb_ref[...], wout_ref[D_A:])
    h_ref[...] = h
    hn = _rms(h, gffn_ref[...])
    _to_row_tiles(hn_ref, hn)
    tm = hn.shape[0]

    hn_hi, hn_lo = _split_bf16(hn)
    lg = _dot_nt(wrt_hi_ref[...], hn_hi) + (_dot_nt(wrt_hi_ref[...], hn_lo) + _dot_nt(wrt_lo_ref[...], hn_hi))
    row = lax.broadcasted_iota(jnp.int32, lg.shape, 0)

    def first_row(mask):
        return jnp.min(jnp.where(mask, row, ROUTER_ROWS), axis=0, keepdims=True)

    is_g = row < N_GROUPS
    g_max = jnp.max(jnp.where(is_g, lg, -jnp.inf), axis=0, keepdims=True)
    p_g = 1.0 / jnp.sum(jnp.where(is_g, jnp.exp(lg - g_max), 0.0), axis=0, keepdims=True)
    g_sel = first_row(is_g & (lg == g_max))
    e_row = row - N_GROUPS
    in_grp = (e_row >= 0) & (e_row < N_EXPERTS) & (lax.shift_right_arithmetic(e_row, 3) == g_sel)
    l1 = jnp.max(jnp.where(in_grp, lg, -jnp.inf), axis=0, keepdims=True)
    i1 = first_row(in_grp & (lg == l1))
    rest = in_grp & (row != i1)
    l2 = jnp.max(jnp.where(rest, lg, -jnp.inf), axis=0, keepdims=True)
    i2 = first_row(rest & (lg == l2))
    e2 = jnp.exp(l2 - l1)
    w1 = 1.0 / (1.0 + e2)
    w2 = e2 / (1.0 + e2)

    @pl.when(pl.program_id(0) == 0)
    def _():
        cnt_ref[...] = jnp.zeros_like(cnt_ref)

    picked = jnp.where((row == i1) | (row == i2), 1.0, 0.0)
    tok_r = lax.broadcasted_iota(jnp.int32, (tm, tm), 0)
    tok_c = lax.broadcasted_iota(jnp.int32, (tm, tm), 1)
    earlier = jnp.where(tok_r < tok_c, 1.0, 0.0).astype(BF16)
    before = _dot(picked.astype(BF16), earlier) + cnt_ref[:, 0:1]
    r1 = jnp.sum(jnp.where(row == i1, before, 0.0), axis=0, keepdims=True)
    r2 = jnp.sum(jnp.where(row == i2, before, 0.0), axis=0, keepdims=True)
    cnt_ref[...] += jnp.sum(picked, axis=1, keepdims=True)
    counts_ref[...] = cnt_ref[...]

    e1, e2x = i1 - N_GROUPS, i2 - N_GROUPS
    route_i_ref[...] = jnp.concatenate(
        [e1, e2x, r1.astype(jnp.int32), r2.astype(jnp.int32), jnp.zeros((ROUTE_FIELDS - 4, tm), jnp.int32)], axis=0)
    wts = jnp.concatenate([p_g * w1, p_g * w2, jnp.zeros((LANES - 2, tm), F32)], axis=0)
    route_ref[...] = wts.T


def _outmix_call(x2, ya2, yb2, wout, gffn, wrt_hi, wrt_lo):
    t, d = x2.shape
    tm = OUT_TM
    const = lambda *shape: pl.BlockSpec(shape, lambda ti: (0,) * len(shape))
    tok = lambda w: pl.BlockSpec((tm, w), lambda ti: (ti, 0))
    return pl.pallas_call(
        _outmix_kernel,
        grid=(t // tm,),
        in_specs=[tok(d), tok(D_A), tok(D_B), const(d, d), const(1, d),
                  const(ROUTER_ROWS, d), const(ROUTER_ROWS, d)],
        out_specs=[tok(d), pl.BlockSpec((tm * ROW_SUB, LANES), lambda ti: (ti, 0)), tok(LANES),
                   pl.BlockSpec((ROUTE_FIELDS, tm), lambda ti: (0, ti)), const(ROUTER_ROWS, LANES)],
        out_shape=[jax.ShapeDtypeStruct((t, d), F32), jax.ShapeDtypeStruct((t * ROW_SUB, LANES), F32),
                   jax.ShapeDtypeStruct((t, LANES), F32),
                   jax.ShapeDtypeStruct((ROUTE_FIELDS, t), jnp.int32),
                   jax.ShapeDtypeStruct((ROUTER_ROWS, LANES), F32)],
        scratch_shapes=[pltpu.VMEM((ROUTER_ROWS, LANES), F32)],
        compiler_params=pltpu.CompilerParams(
            dimension_semantics=("arbitrary",), vmem_limit_bytes=VMEM_LIMIT),
        name="outmix",
    )(x2, ya2, yb2, wout, gffn, wrt_hi, wrt_lo)


def _to_row_tiles(ref, x):
    n = x.shape[0]
    for s in range(ROW_SUB):
        ref[pl.ds(s, n, stride=ROW_SUB), :] = x[:, s * LANES:(s + 1) * LANES]


def _from_row_tiles(ref, first, n):
    return jnp.concatenate(
        [ref[pl.ds(first * ROW_SUB + s, n, stride=ROW_SUB), :] for s in range(ROW_SUB)], axis=1)


def _tiles_at(ref, row, n=1):
    return ref.at[pl.ds(pl.multiple_of(row * ROW_SUB, ROW_SUB), n * ROW_SUB), :]


def _wait_tiles(hbm, vmem_ref, n_tiles, sem):
    cap = vmem_ref.shape[0] // ROW_SUB
    while n_tiles > 0:
        n = min(n_tiles, cap)
        pltpu.make_async_copy(hbm.at[pl.ds(0, n * ROW_SUB), :], vmem_ref.at[pl.ds(0, n * ROW_SUB), :], sem).wait()
        n_tiles -= n


def _start_gather(idx_ref, src_hbm, dst_ref, sem):
    for r in range(dst_ref.shape[0] // ROW_SUB):
        pltpu.make_async_copy(_tiles_at(src_hbm, idx_ref[0, 0, r]), _tiles_at(dst_ref, r), sem).start(priority=r % 2)


def _slots_kernel(seg_start_ref, route_i_ref, pos_ref):
    experts = route_i_ref[0:2, :]
    start = jnp.zeros_like(experts)
    for e in range(N_EXPERTS):
        start = jnp.where(experts == e, seg_start_ref[e], start)
    pos_ref[...] = jnp.zeros_like(pos_ref)
    pos_ref[0:2, :] = start + route_i_ref[2:4, :]


def _slots_call(seg_start, route_i):
    return pl.pallas_call(
        _slots_kernel,
        grid_spec=pltpu.PrefetchScalarGridSpec(
            num_scalar_prefetch=1,
            grid=(1,),
            in_specs=[pl.BlockSpec(route_i.shape, lambda i, s: (0, 0))],
            out_specs=pl.BlockSpec(route_i.shape, lambda i, s: (0, 0)),
        ),
        out_shape=jax.ShapeDtypeStruct(route_i.shape, jnp.int32),
        name="slots",
    )(seg_start, route_i)


def _invert_kernel(gap_start_ref, gap_len_ref, pos_ref, src_ref):
    tm = pos_ref.shape[1]
    i = pl.program_id(0)

    @pl.when(i == 0)
    def _():
        for g in range(N_EXPERTS + 1):
            def clear(j, carry, g=g):
                src_ref[gap_start_ref[g] + j] = 0
                return carry
            lax.fori_loop(0, gap_len_ref[g], clear, 0)

    for k in range(2):
        for tok in range(tm):
            src_ref[pos_ref[k, tok]] = i * tm + tok


def _invert_call(gap_start, gap_len, pos, n_sorted_rows):
    tm = TOK_TM
    return pl.pallas_call(
        _invert_kernel,
        grid_spec=pltpu.PrefetchScalarGridSpec(
            num_scalar_prefetch=2,
            grid=(pos.shape[1] // tm,),
            in_specs=[pl.BlockSpec((ROUTE_FIELDS, tm), lambda i, *_: (0, i), memory_space=pltpu.SMEM)],
            out_specs=pl.BlockSpec(memory_space=pltpu.SMEM),
        ),
        out_shape=jax.ShapeDtypeStruct((n_sorted_rows,), jnp.int32),
        compiler_params=pltpu.CompilerParams(dimension_semantics=("arbitrary",)),
        name="invert",
    )(gap_start, gap_len, pos)


def _experts_kernel(tile_e_ref, n_tiles_ref, cur_ref, nxt_ref, hn_hbm, wg_ref, wu_ref, wd_ref, y_ref,
                    wg16, wu16, wd16, xbuf0, xbuf1, sem):
    i = pl.program_id(0)
    n_used = n_tiles_ref[0]
    bufs = (xbuf0, xbuf1)
    tm = xbuf0.shape[0] // ROW_SUB

    @pl.when((i == 0) | (tile_e_ref[i] != tile_e_ref[jnp.maximum(i - 1, 0)]))
    def _():
        wg16[...] = wg_ref[0].astype(BF16)
        wu16[...] = wu_ref[0].astype(BF16)
        wd16[...] = wd_ref[0].astype(BF16)

    def step(slot):
        _wait_tiles(hn_hbm, bufs[slot], tm, sem.at[slot])
        _start_gather(nxt_ref, hn_hbm, bufs[1 - slot], sem.at[1 - slot])
        x = _from_row_tiles(bufs[slot], 0, tm).astype(BF16)
        a = jax.nn.silu(_dot(x, wg16[...])) * _dot(x, wu16[...])
        _to_row_tiles(y_ref, _dot(a.astype(BF16), wd16[...]))

    for slot in range(2):
        mine = lax.rem(i, 2) == slot

        @pl.when(mine & (i == 0))
        def _(slot=slot):
            _start_gather(cur_ref, hn_hbm, bufs[slot], sem.at[slot])

        pl.when(mine & (i < n_used))(functools.partial(step, slot))
        pl.when(mine & (i == n_used))(functools.partial(_wait_tiles, hn_hbm, bufs[slot], tm, sem.at[slot]))

    @pl.when(i >= n_used)
    def _():
        y_ref[...] = jnp.zeros_like(y_ref)


def _experts_call(tile_e, n_tiles, src_rows, hn_tiles, wg, wu, wd):
    d = wg.shape[1]
    tm = EXP_TM
    nt = src_rows.shape[0]
    used = lambda i, n: jnp.minimum(i, n[0] - 1)
    w_spec = lambda *shape: pl.BlockSpec((1,) + shape, lambda i, te, n: (te[used(i, n)], 0, 0))
    idx_spec = lambda f: pl.BlockSpec((1, 1, tm), lambda i, te, n: (f(i, n), 0, 0), memory_space=pltpu.SMEM)
    return pl.pallas_call(
        _experts_kernel,
        grid_spec=pltpu.PrefetchScalarGridSpec(
            num_scalar_prefetch=2,
            grid=(nt,),
            in_specs=[idx_spec(used), idx_spec(lambda i, n: used(i + 1, n)),
                      pl.BlockSpec(memory_space=pl.ANY),
                      w_spec(d, D_FF_EXP), w_spec(d, D_FF_EXP), w_spec(D_FF_EXP, d)],
            out_specs=pl.BlockSpec((tm * ROW_SUB, LANES), lambda i, te, n: (i, 0)),
            scratch_shapes=[pltpu.VMEM((d, D_FF_EXP), BF16), pltpu.VMEM((d, D_FF_EXP), BF16),
                            pltpu.VMEM((D_FF_EXP, d), BF16),
                            pltpu.VMEM((tm * ROW_SUB, LANES), F32), pltpu.VMEM((tm * ROW_SUB, LANES), F32),
                            pltpu.SemaphoreType.DMA((2,))],
        ),
        out_shape=jax.ShapeDtypeStruct((nt * tm * ROW_SUB, LANES), F32),
        compiler_params=pltpu.CompilerParams(
            dimension_semantics=("arbitrary",), vmem_limit_bytes=VMEM_LIMIT),
        name="experts",
    )(tile_e, n_tiles, src_rows, src_rows, hn_tiles, wg, wu, wd)


def _ple_kernel(cur_ref, nxt_ref, ys_hbm, h_ref, route_ref, p_ref, wple_ref, gple_ref, wpg_ref,
                o_ref, ybuf0, ybuf1, sem):
    tm = h_ref.shape[0]
    i = pl.program_id(0)
    n_steps = pl.num_programs(0)
    bufs = (ybuf0, ybuf1)

    def start_gather(pos_ref, slot):
        for r in range(2 * tm):
            pltpu.make_async_copy(_tiles_at(ys_hbm, pos_ref[r // tm, r % tm]), _tiles_at(bufs[slot], r),
                                  sem.at[slot]).start(priority=r % 2)

    @pl.when(i == 0)
    def _():
        start_gather(cur_ref, 0)

    def step(slot):
        _wait_tiles(ys_hbm, bufs[slot], 2 * tm, sem.at[slot])
        start_gather(nxt_ref, 1 - slot)
        wts = route_ref[...]
        h = (h_ref[...] + wts[:, 0:1] * _from_row_tiles(bufs[slot], 0, tm)
             + wts[:, 1:2] * _from_row_tiles(bufs[slot], tm, tm))
        gate = jax.nn.sigmoid(_dot(_rms(h, gple_ref[...]).astype(BF16), wpg_ref[...]))
        o_ref[...] = h + _dot(p_ref[...].astype(BF16), wple_ref[...]) * gate

    for slot in range(2):
        pl.when(lax.rem(i, 2) == slot)(functools.partial(step, slot))

    for slot in range(2):
        pl.when((i == n_steps - 1) & (lax.rem(n_steps, 2) == slot))(
            functools.partial(_wait_tiles, ys_hbm, bufs[slot], 2 * tm, sem.at[slot]))


def _ple_call(pos, ys, h, route, p2, wple, gple, wpg):
    t, d = h.shape
    tm = TOK_TM
    n_steps = t // tm
    n_rows = 2 * tm
    idx_spec = lambda f: pl.BlockSpec((ROUTE_FIELDS, tm), lambda i: (0, f(i)), memory_space=pltpu.SMEM)
    const = lambda *shape: pl.BlockSpec(shape, lambda i: (0,) * len(shape))
    tok = lambda w: pl.BlockSpec((tm, w), lambda i: (i, 0))
    return pl.pallas_call(
        _ple_kernel,
        grid=(n_steps,),
        in_specs=[idx_spec(lambda i: i), idx_spec(lambda i: lax.rem(i + 1, n_steps)),
                  pl.BlockSpec(memory_space=pl.ANY),
                  tok(d), tok(LANES), tok(D_PLE), const(D_PLE, d), const(1, d), const(d, d)],
        out_specs=tok(d),
        out_shape=jax.ShapeDtypeStruct((t, d), F32),
        scratch_shapes=[pltpu.VMEM((n_rows * ROW_SUB, LANES), F32), pltpu.VMEM((n_rows * ROW_SUB, LANES), F32),
                        pltpu.SemaphoreType.DMA((2,))],
        compiler_params=pltpu.CompilerParams(
            dimension_semantics=("arbitrary",), vmem_limit_bytes=VMEM_LIMIT),
        name="ple",
    )(pos, pos, ys, h, route, p2, wple, gple, wpg)


def _segment_tables(counts, t):
    tm = EXP_TM
    nt = 2 * t // tm + N_EXPERTS
    padded = ((counts + tm - 1) // tm) * tm
    seg_end = jnp.cumsum(padded)
    seg_start = seg_end - padded
    gap_len = jnp.concatenate([padded - counts, nt * tm - seg_end[-1:]])
    gap_start = jnp.concatenate([seg_start + counts, seg_end[-1:]])
    tile_start = jnp.arange(nt, dtype=jnp.int32) * tm
    tile_e = jnp.minimum(jnp.sum(tile_start[:, None] >= seg_end[None, :], axis=1), N_EXPERTS - 1)
    n_tiles = (seg_end[-1] // tm).reshape(1)
    i32 = lambda a: a.astype(jnp.int32)
    return i32(seg_start), i32(gap_start), i32(gap_len), i32(tile_e), i32(n_tiles), nt * tm


def _layer(h, p_i, g_mix, w_in, g_v_a, w_s, b_s, g_q, g_k, g_out_a, g_out_b, w_out, g_ffn, w_group,
           w_expert, w_gate_e, w_up_e, w_down_e, g_ple, w_ple, w_ple_gate):
    b, s, d = h.shape
    t = b * s
    row = lambda g: g.reshape(1, -1).astype(F32)
    wuv = w_in[:, :2 * D_A].astype(BF16)
    wqt = w_in[:, 2 * D_A:2 * D_A + D_B].T.astype(BF16)
    wk = w_in[:, 2 * D_A + D_B:2 * D_A + 2 * D_B].astype(BF16)
    wvt = w_in[:, 2 * D_A + 2 * D_B:].T.astype(BF16)
    bs = jnp.broadcast_to(b_s[:, :, None], (A_GROUPS, A_CHUNK, A_DH)).astype(F32)
    gk = row(jnp.tile(g_k, B_HEADS))
    gq = (jnp.tile(g_q, B_HEADS) * (1.0 / float(B_DH) ** 0.5)).reshape(D_B, 1).astype(F32)
    head = jnp.arange(D_B) // B_DH
    bd = jnp.where(head[:, None] == head[None, :], 1.0 / B_DH, 0.0).astype(BF16)
    w_rt = jnp.zeros((ROUTER_ROWS, d), F32)
    w_rt = w_rt.at[:N_GROUPS].set(w_group.T).at[N_GROUPS:N_GROUPS + N_EXPERTS].set(w_expert.T)
    wrt_hi, wrt_lo = _split_bf16(w_rt)

    ya, k, qt, vt = _proj_call(h, row(g_mix), wuv, wk, wqt, wvt, row(g_v_a), w_s.astype(F32), bs,
                               row(g_out_a), gk, gq, bd)
    yb = _moba_call(qt, k, vt, row(g_out_b))
    h1, hn_tiles, route, route_i, counts = _outmix_call(
        h.reshape(t, d), ya.reshape(t, D_A), yb.reshape(t, D_B), w_out.astype(BF16), row(g_ffn), wrt_hi, wrt_lo)
    counts = counts[N_GROUPS:N_GROUPS + N_EXPERTS, 0].astype(jnp.int32)
    seg_start, gap_start, gap_len, tile_e, n_tiles, n_sorted_rows = _segment_tables(counts, t)
    pos = _slots_call(seg_start, route_i)
    src_rows = _invert_call(gap_start, gap_len, pos, n_sorted_rows)
    ys = _experts_call(tile_e, n_tiles, src_rows.reshape(n_sorted_rows // EXP_TM, 1, EXP_TM), hn_tiles,
                       w_gate_e, w_up_e, w_down_e)
    out = _ple_call(pos, ys, h1, route, p_i.reshape(t, D_PLE), w_ple.astype(BF16), row(g_ple),
                    w_ple_gate.astype(BF16))
    return out.reshape(b, s, d)


def kernel(x, p, g_mix, w_in, g_v_a, w_s, b_s, g_q, g_k, g_out_a, g_out_b, w_out, g_ffn, w_group, w_expert,
           w_gate_e, w_up_e, w_down_e, g_ple, w_ple, w_ple_gate):
    params = (g_mix, w_in, g_v_a, w_s, b_s, g_q, g_k, g_out_a, g_out_b, w_out, g_ffn, w_group, w_expert,
              w_gate_e, w_up_e, w_down_e, g_ple, w_ple, w_ple_gate)
    h = x
    for i in range(p.shape[0]):
        h = _layer(h, p[i], *(w[i] for w in params))
    return h
```

```python
import functools

import jax
import jax.numpy as jnp
from jax import lax
from jax.experimental import pallas as pl
from jax.experimental.pallas import tpu as pltpu

F32 = jnp.float32
BF16 = jnp.bfloat16

D_MODEL = 1024
D_A = 512
A_GROUPS = 4
A_DH = D_A // A_GROUPS
A_CHUNK = 128
D_B = 512
B_HEADS = 8
B_DH = D_B // B_HEADS
B_BLOCK = 256
B_TOPK = 3
D_PLE = 256
N_GROUPS = 4
EXP_PER_GROUP = 8
N_EXPERTS = N_GROUPS * EXP_PER_GROUP
D_FF_EXP = 256
EPS = 1e-6
NEG = -1e30
LOG2E = 1.4426950408889634

LANES = 128
HEAD_PAIR = 2 * B_DH
N_PAIRS = B_HEADS // 2
ROUTER_ROWS = 48
ROUTE_FIELDS = 8
VMEM_LIMIT = 56 * 1024 * 1024

PROJ_TM = 512
OUT_TM = 512
EXP_TM = 256
MOBA_UNROLL = 2
TOK_TM = 256
ROW_SUB = D_MODEL // LANES


def _rms(x, g):
    return x * lax.rsqrt(jnp.mean(x * x, axis=-1, keepdims=True) + EPS) * g


def _dot(a, b):
    return jnp.dot(a, b, preferred_element_type=F32)


def _dot_nt(a, b):
    return lax.dot_general(a, b, (((1,), (1,)), ((), ())), preferred_element_type=F32)


def _split_bf16(x):
    hi = x.astype(BF16)
    lo = (x - hi.astype(F32)).astype(BF16)
    return hi, lo


def _proj_kernel(x_ref, gmix_ref, wuv_ref, wk_ref, wqt_ref, wvt_ref, gva_ref, ws_ref, bs_ref,
                 goa_ref, gk_ref, gq_ref, bd_ref,
                 ya_ref, k_ref, qt_ref, vt_ref, prod_ref):
    tm = x_ref.shape[1]
    n_chunks = tm // A_CHUNK
    xn = _rms(x_ref[0], gmix_ref[...]).astype(BF16)

    uv = _dot(xn, wuv_ref[...])
    u = jax.nn.gelu(uv[:, :D_A])
    v = jax.nn.gelu(uv[:, D_A:])
    row = lax.broadcasted_iota(jnp.int32, (A_CHUNK, A_CHUNK), 0)
    col = lax.broadcasted_iota(jnp.int32, (A_CHUNK, A_CHUNK), 1)
    causal = col <= row
    for g in range(A_GROUPS):
        gs = slice(g * A_DH, (g + 1) * A_DH)
        vn = _rms(v[:, gs], gva_ref[:, gs]).astype(BF16)
        rhs = jnp.concatenate([vn[c * A_CHUNK:(c + 1) * A_CHUNK] for c in range(n_chunks)], axis=1)
        ws = jnp.where(causal, ws_ref[g], 0.0).astype(BF16)
        mixed = _dot(ws, rhs)
        for c in range(n_chunks):
            cs = slice(c * A_CHUNK, (c + 1) * A_CHUNK)
            prod_ref[cs, gs] = u[cs, gs] * (mixed[:, cs] + bs_ref[g])
    ya_ref[0] = _rms(prod_ref[...], goa_ref[...]).astype(ya_ref.dtype)

    k = _dot(xn, wk_ref[...])
    k2_hi, k2_lo = _split_bf16(k * k)
    k_ms = _dot(k2_hi, bd_ref[...]) + _dot(k2_lo, bd_ref[...])
    k_ref[0] = (k * lax.rsqrt(k_ms + EPS) * gk_ref[...]).astype(k_ref.dtype)

    qt = _dot_nt(wqt_ref[...], xn)
    vt = _dot_nt(wvt_ref[...], xn).astype(vt_ref.dtype)
    qn = []
    for h in range(B_HEADS):
        qh = qt[h * B_DH:(h + 1) * B_DH]
        qn.append(qh * lax.rsqrt(jnp.mean(qh * qh, axis=0, keepdims=True) + EPS))
    qn = (jnp.concatenate(qn, axis=0) * gq_ref[...]).astype(qt_ref.dtype)
    for p in range(N_PAIRS):
        ps = slice(p * HEAD_PAIR, (p + 1) * HEAD_PAIR)
        for j in range(tm // B_BLOCK):
            js = slice(j * B_BLOCK, (j + 1) * B_BLOCK)
            qt_ref[0, p, j] = qn[ps, js]
            vt_ref[0, p, j] = vt[ps, js]


def _proj_call(x, gmix, wuv, wk, wqt, wvt, gva, ws, bs, goa, gk, gq, bd):
    b, s, d = x.shape
    tm = PROJ_TM
    nb = s // B_BLOCK
    jb = tm // B_BLOCK
    const = lambda *shape: pl.BlockSpec(shape, lambda bi, ti: (0,) * len(shape))
    return pl.pallas_call(
        _proj_kernel,
        grid=(b, s // tm),
        in_specs=[
            pl.BlockSpec((1, tm, d), lambda bi, ti: (bi, ti, 0)),
            const(1, d), const(d, 2 * D_A), const(d, D_B), const(D_B, d), const(D_B, d),
            const(1, D_A), const(A_GROUPS, A_CHUNK, A_CHUNK), const(A_GROUPS, A_CHUNK, A_DH),
            const(1, D_A), const(1, D_B), const(D_B, 1), const(D_B, D_B),
        ],
        out_specs=[
            pl.BlockSpec((1, tm, D_A), lambda bi, ti: (bi, ti, 0)),
            pl.BlockSpec((1, tm, D_B), lambda bi, ti: (bi, ti, 0)),
            pl.BlockSpec((1, N_PAIRS, jb, HEAD_PAIR, B_BLOCK), lambda bi, ti: (bi, 0, ti, 0, 0)),
            pl.BlockSpec((1, N_PAIRS, jb, HEAD_PAIR, B_BLOCK), lambda bi, ti: (bi, 0, ti, 0, 0)),
        ],
        out_shape=[
            jax.ShapeDtypeStruct((b, s, D_A), BF16),
            jax.ShapeDtypeStruct((b, s, D_B), BF16),
            jax.ShapeDtypeStruct((b, N_PAIRS, nb, HEAD_PAIR, B_BLOCK), BF16),
            jax.ShapeDtypeStruct((b, N_PAIRS, nb, HEAD_PAIR, B_BLOCK), BF16),
        ],
        scratch_shapes=[pltpu.VMEM((tm, D_A), F32)],
        compiler_params=pltpu.CompilerParams(
            dimension_semantics=("parallel", "parallel"), vmem_limit_bytes=VMEM_LIMIT),
        name="proj",
    )(x, gmix, wuv, wk, wqt, wvt, gva, ws, bs, goa, gk, gq, bd)


def _moba_kernel(qt_ref, k_ref, vt_ref, gob_ref, y_ref, kmean_ref, qz_ref, sel_ref, m_ref, l_ref, acc_ref,
                 s_ref):
    qb = pl.program_id(1)
    nb = k_ref.shape[1] // B_BLOCK
    nbp = sel_ref.shape[1]

    @pl.when(qb == 0)
    def _():
        rows = [jnp.mean(k_ref[0, j * B_BLOCK:(j + 1) * B_BLOCK, :].astype(F32), axis=0, keepdims=True)
                for j in range(nb)]
        rows += [jnp.zeros((1, D_B), F32)] * (nbp - nb)
        km = jnp.concatenate(rows, axis=0)
        hi = km.astype(BF16)
        rest = km - hi.astype(F32)
        mid = rest.astype(BF16)
        lo = (rest - mid.astype(F32)).astype(BF16)
        for p in range(N_PAIRS):
            ps = slice(p * HEAD_PAIR, (p + 1) * HEAD_PAIR)
            kmean_ref[p] = jnp.concatenate([hi[:, ps], mid[:, ps], lo[:, ps]], axis=0)

    head_row = lax.broadcasted_iota(jnp.int32, (HEAD_PAIR, B_BLOCK), 0) // B_DH
    blk = lax.broadcasted_iota(jnp.int32, (nbp, B_BLOCK), 0)
    past = blk < qb

    for h in range(B_HEADS):
        p, hh = divmod(h, 2)
        q_pair = qt_ref[0, p, 0]
        qz = jnp.where(head_row == hh, q_pair, jnp.zeros_like(q_pair))
        qz_ref[h] = qz
        g3 = _dot(kmean_ref[p], qz)
        gate = (g3[:nbp] + g3[nbp:2 * nbp] + g3[2 * nbp:]) * float(B_DH) ** 0.5
        gate = jnp.where(past, gate, NEG)
        rank = jnp.zeros((nbp, B_BLOCK), F32)
        for j in range(nb):
            gj = gate[j:j + 1]
            ahead = (gj > gate) | ((gj == gate) & (blk > j))
            rank = rank + jnp.where(ahead, 1.0, 0.0)
        chosen = (rank < float(B_TOPK)) & (gate > NEG * 0.5)
        sel_ref[h] = jnp.where(chosen, 1.0, 0.0)

    def scores(h, j):
        p = h // 2
        kj = k_ref[0, pl.ds(pl.multiple_of(j * B_BLOCK, B_BLOCK), B_BLOCK), p * HEAD_PAIR:(p + 1) * HEAD_PAIR]
        return _dot(kj, qz_ref[h])

    def values(h, j):
        p, hh = divmod(h, 2)
        return vt_ref[0, p, j, hh * B_DH:(hh + 1) * B_DH, :]

    key_pos = lax.broadcasted_iota(jnp.int32, (B_BLOCK, B_BLOCK), 0)
    q_pos = lax.broadcasted_iota(jnp.int32, (B_BLOCK, B_BLOCK), 1)
    causal = key_pos <= q_pos

    def stage_scores(h, j, visible):
        s = jnp.where(visible, scores(h, j) * LOG2E, NEG)
        s_ref[h, j] = s
        return jnp.max(s, axis=0, keepdims=True)

    for h in range(B_HEADS):
        m_ref[h] = stage_scores(h, qb, causal)
        l_ref[h] = jnp.zeros((1, B_BLOCK), F32)
        acc_ref[h] = jnp.zeros((B_DH, B_BLOCK), F32)

    def score_pass(n_blocks, j0):
        for h in range(B_HEADS):
            m = m_ref[h]
            for u in range(n_blocks):
                m = jnp.maximum(m, stage_scores(h, j0 + u, sel_ref[h, pl.ds(j0 + u, 1), :] > 0.5))
            m_ref[h] = m

    def value_pass(n_blocks, j0):
        for h in range(B_HEADS):
            l, acc = l_ref[h], acc_ref[h]
            for u in range(n_blocks):
                pr = jnp.exp2(s_ref[h, j0 + u] - m_ref[h])
                l = l + jnp.sum(pr, axis=0, keepdims=True)
                acc = acc + _dot(values(h, j0 + u), pr.astype(BF16))
            l_ref[h], acc_ref[h] = l, acc

    def run(n, body):
        u = MOBA_UNROLL

        def trips(i, carry):
            body(u, i * u)
            return carry

        def rest(j, carry):
            body(1, j)
            return carry

        lax.fori_loop(0, n // u, trips, 0)
        lax.fori_loop((n // u) * u, n, rest, 0)

    run(qb, score_pass)
    run(qb + 1, value_pass)

    yt = jnp.concatenate([acc_ref[h] / l_ref[h] for h in range(B_HEADS)], axis=0)
    yt = yt * lax.rsqrt(jnp.mean(yt * yt, axis=0, keepdims=True) + EPS)
    y_ref[0] = (yt.T * gob_ref[...]).astype(y_ref.dtype)


def _moba_call(qt, k, vt, gob):
    b, s, _ = k.shape
    nb = s // B_BLOCK
    nbp = -(-nb // 8) * 8
    return pl.pallas_call(
        _moba_kernel,
        grid=(b, nb),
        in_specs=[
            pl.BlockSpec((1, N_PAIRS, 1, HEAD_PAIR, B_BLOCK), lambda bi, qi: (bi, 0, qi, 0, 0)),
            pl.BlockSpec((1, s, D_B), lambda bi, qi: (bi, 0, 0)),
            pl.BlockSpec((1, N_PAIRS, nb, HEAD_PAIR, B_BLOCK), lambda bi, qi: (bi, 0, 0, 0, 0)),
            pl.BlockSpec((1, D_B), lambda bi, qi: (0, 0)),
        ],
        out_specs=pl.BlockSpec((1, B_BLOCK, D_B), lambda bi, qi: (bi, qi, 0)),
        out_shape=jax.ShapeDtypeStruct((b, s, D_B), BF16),
        scratch_shapes=[
            pltpu.VMEM((N_PAIRS, 3 * nbp, HEAD_PAIR), BF16),
            pltpu.VMEM((B_HEADS, HEAD_PAIR, B_BLOCK), BF16),
            pltpu.VMEM((B_HEADS, nbp, B_BLOCK), F32),
            pltpu.VMEM((B_HEADS, 1, B_BLOCK), F32),
            pltpu.VMEM((B_HEADS, 1, B_BLOCK), F32),
            pltpu.VMEM((B_HEADS, B_DH, B_BLOCK), F32),
            pltpu.VMEM((B_HEADS, nb, B_BLOCK, B_BLOCK), F32),
        ],
        compiler_params=pltpu.CompilerParams(
            dimension_semantics=("parallel", "arbitrary"), vmem_limit_bytes=VMEM_LIMIT),
        name="moba",
    )(qt, k, vt, gob)


def _outmix_kernel(x_ref, ya_ref, yb_ref, wout_ref, gffn_ref, wrt_hi_ref, wrt_lo_ref,
                   h_ref, hn_ref, route_ref, route_i_ref, counts_ref, cnt_ref):
    h = x_ref[...] + _dot(ya_ref[...], wout_ref[:D_A]) + _dot(yb_ref[...], wout_ref[D_A:])
    h_ref[...] = h
    hn = _rms(h, gffn_ref[...])
    _to_row_tiles(hn_ref, hn)
    tm = hn.shape[0]

    hn_hi, hn_lo = _split_bf16(hn)
    lg = _dot_nt(wrt_hi_ref[...], hn_hi) + (_dot_nt(wrt_hi_ref[...], hn_lo) + _dot_nt(wrt_lo_ref[...], hn_hi))
    row = lax.broadcasted_iota(jnp.int32, lg.shape, 0)

    def first_row(mask):
        return jnp.min(jnp.where(mask, row, ROUTER_ROWS), axis=0, keepdims=True)

    is_g = row < N_GROUPS
    g_max = jnp.max(jnp.where(is_g, lg, -jnp.inf), axis=0, keepdims=True)
    p_g = 1.0 / jnp.sum(jnp.where(is_g, jnp.exp(lg - g_max), 0.0), axis=0, keepdims=True)
    g_sel = first_row(is_g & (lg == g_max))
    e_row = row - N_GROUPS
    in_grp = (e_row >= 0) & (e_row < N_EXPERTS) & (lax.shift_right_arithmetic(e_row, 3) == g_sel)
    l1 = jnp.max(jnp.where(in_grp, lg, -jnp.inf), axis=0, keepdims=True)
    i1 = first_row(in_grp & (lg == l1))
    rest = in_grp & (row != i1)
    l2 = jnp.max(jnp.where(rest, lg, -jnp.inf), axis=0, keepdims=True)
    i2 = first_row(rest & (lg == l2))
    e2 = jnp.exp(l2 - l1)
    w1 = 1.0 / (1.0 + e2)
    w2 = e2 / (1.0 + e2)

    @pl.when(pl.program_id(0) == 0)
    def _():
        cnt_ref[...] = jnp.zeros_like(cnt_ref)

    picked = jnp.where((row == i1) | (row == i2), 1.0, 0.0)
    tok_r = lax.broadcasted_iota(jnp.int32, (tm, tm), 0)
    tok_c = lax.broadcasted_iota(jnp.int32, (tm, tm), 1)
    earlier = jnp.where(tok_r < tok_c, 1.0, 0.0).astype(BF16)
    before = _dot(picked.astype(BF16), earlier) + cnt_ref[:, 0:1]
    r1 = jnp.sum(jnp.where(row == i1, before, 0.0), axis=0, keepdims=True)
    r2 = jnp.sum(jnp.where(row == i2, before, 0.0), axis=0, keepdims=True)
    cnt_ref[...] += jnp.sum(picked, axis=1, keepdims=True)
    counts_ref[...] = cnt_ref[...]

    e1, e2x = i1 - N_GROUPS, i2 - N_GROUPS
    route_i_ref[...] = jnp.concatenate(
        [e1, e2x, r1.astype(jnp.int32), r2.astype(jnp.int32), jnp.zeros((ROUTE_FIELDS - 4, tm), jnp.int32)], axis=0)
    wts = jnp.concatenate([p_g * w1, p_g * w2, jnp.zeros((LANES - 2, tm), F32)], axis=0)
    route_ref[...] = wts.T


def _outmix_call(x2, ya2, yb2, wout, gffn, wrt_hi, wrt_lo):
    t, d = x2.shape
    tm = OUT_TM
    const = lambda *shape: pl.BlockSpec(shape, lambda ti: (0,) * len(shape))
    tok = lambda w: pl.BlockSpec((tm, w), lambda ti: (ti, 0))
    return pl.pallas_call(
        _outmix_kernel,
        grid=(t // tm,),
        in_specs=[tok(d), tok(D_A), tok(D_B), const(d, d), const(1, d),
                  const(ROUTER_ROWS, d), const(ROUTER_ROWS, d)],
        out_specs=[tok(d), pl.BlockSpec((tm * ROW_SUB, LANES), lambda ti: (ti, 0)), tok(LANES),
                   pl.BlockSpec((ROUTE_FIELDS, tm), lambda ti: (0, ti)), const(ROUTER_ROWS, LANES)],
        out_shape=[jax.ShapeDtypeStruct((t, d), F32), jax.ShapeDtypeStruct((t * ROW_SUB, LANES), F32),
                   jax.ShapeDtypeStruct((t, LANES), F32),
                   jax.ShapeDtypeStruct((ROUTE_FIELDS, t), jnp.int32),
                   jax.ShapeDtypeStruct((ROUTER_ROWS, LANES), F32)],
        scratch_shapes=[pltpu.VMEM((ROUTER_ROWS, LANES), F32)],
        compiler_params=pltpu.CompilerParams(
            dimension_semantics=("arbitrary",), vmem_limit_bytes=VMEM_LIMIT),
        name="outmix",
    )(x2, ya2, yb2, wout, gffn, wrt_hi, wrt_lo)


def _to_row_tiles(ref, x):
    n = x.shape[0]
    for s in range(ROW_SUB):
        ref[pl.ds(s, n, stride=ROW_SUB), :] = x[:, s * LANES:(s + 1) * LANES]


def _from_row_tiles(ref, first, n):
    return jnp.concatenate(
        [ref[pl.ds(first * ROW_SUB + s, n, stride=ROW_SUB), :] for s in range(ROW_SUB)], axis=1)


def _tiles_at(ref, row, n=1):
    return ref.at[pl.ds(pl.multiple_of(row * ROW_SUB, ROW_SUB), n * ROW_SUB), :]


def _wait_tiles(hbm, vmem_ref, n_tiles, sem):
    cap = vmem_ref.shape[0] // ROW_SUB
    while n_tiles > 0:
        n = min(n_tiles, cap)
        pltpu.make_async_copy(hbm.at[pl.ds(0, n * ROW_SUB), :], vmem_ref.at[pl.ds(0, n * ROW_SUB), :], sem).wait()
        n_tiles -= n


def _slots_kernel(seg_start_ref, route_i_ref, pos_ref):
    experts = route_i_ref[0:2, :]
    start = jnp.zeros_like(experts)
    for e in range(N_EXPERTS):
        start = jnp.where(experts == e, seg_start_ref[e], start)
    pos_ref[...] = jnp.zeros_like(pos_ref)
    pos_ref[0:2, :] = start + route_i_ref[2:4, :]


def _slots_call(seg_start, route_i):
    return pl.pallas_call(
        _slots_kernel,
        grid_spec=pltpu.PrefetchScalarGridSpec(
            num_scalar_prefetch=1,
            grid=(1,),
            in_specs=[pl.BlockSpec(route_i.shape, lambda i, s: (0, 0))],
            out_specs=pl.BlockSpec(route_i.shape, lambda i, s: (0, 0)),
        ),
        out_shape=jax.ShapeDtypeStruct(route_i.shape, jnp.int32),
        name="slots",
    )(seg_start, route_i)


def _dispatch_kernel(gap_start_ref, gap_len_ref, pos_ref, hn_ref, xs_hbm, zero_ref, sem, fill_sem):
    tm = hn_ref.shape[0] // ROW_SUB

    @pl.when(pl.program_id(0) == 0)
    def _():
        zero_ref[...] = jnp.zeros_like(zero_ref)
        fills = []
        for g in range(N_EXPERTS):
            row, n = gap_start_ref[g], gap_len_ref[g]
            size = EXP_TM // 2
            while size >= 1:
                fills.append(((n & size) != 0, row, size))
                row = row + (n & size)
                size //= 2
        for tile in range(N_EXPERTS):
            fills.append((tile * EXP_TM < gap_len_ref[N_EXPERTS], gap_start_ref[N_EXPERTS] + tile * EXP_TM, EXP_TM))
        copies = [(cond, pltpu.make_async_copy(_tiles_at(zero_ref, 0, size), _tiles_at(xs_hbm, row, size), fill_sem))
                  for cond, row, size in fills]
        for cond, copy in copies:
            pl.when(cond)(copy.start)
        for cond, copy in copies:
            pl.when(cond)(copy.wait)

    for r in range(2 * tm):
        k, tok = divmod(r, tm)
        pltpu.make_async_copy(_tiles_at(hn_ref, tok), _tiles_at(xs_hbm, pos_ref[k, tok]), sem).start(priority=r % 2)
    _wait_tiles(xs_hbm, hn_ref, 2 * tm, sem)


def _dispatch_call(gap_start, gap_len, pos, hn_tiles, n_sorted_rows):
    tm = TOK_TM
    return pl.pallas_call(
        _dispatch_kernel,
        grid_spec=pltpu.PrefetchScalarGridSpec(
            num_scalar_prefetch=2,
            grid=(pos.shape[1] // tm,),
            in_specs=[pl.BlockSpec((ROUTE_FIELDS, tm), lambda i, *_: (0, i), memory_space=pltpu.SMEM),
                      pl.BlockSpec((tm * ROW_SUB, LANES), lambda i, *_: (i, 0))],
            out_specs=pl.BlockSpec(memory_space=pl.ANY),
            scratch_shapes=[pltpu.VMEM((EXP_TM * ROW_SUB, LANES), F32), pltpu.SemaphoreType.DMA(()),
                            pltpu.SemaphoreType.DMA(())],
        ),
        out_shape=jax.ShapeDtypeStruct((n_sorted_rows * ROW_SUB, LANES), F32),
        compiler_params=pltpu.CompilerParams(
            dimension_semantics=("arbitrary",), vmem_limit_bytes=VMEM_LIMIT),
        name="dispatch",
    )(gap_start, gap_len, pos, hn_tiles)


def _experts_kernel(tile_e_ref, n_tiles_ref, xs_ref, wg_ref, wu_ref, wd_ref, y_ref, wg16, wu16, wd16):
    i = pl.program_id(0)
    tm = xs_ref.shape[0] // ROW_SUB

    @pl.when((i == 0) | (tile_e_ref[i] != tile_e_ref[jnp.maximum(i - 1, 0)]))
    def _():
        wg16[...] = wg_ref[0].astype(BF16)
        wu16[...] = wu_ref[0].astype(BF16)
        wd16[...] = wd_ref[0].astype(BF16)

    @pl.when(i < n_tiles_ref[0])
    def _():
        x = _from_row_tiles(xs_ref, 0, tm).astype(BF16)
        a = jax.nn.silu(_dot(x, wg16[...])) * _dot(x, wu16[...])
        _to_row_tiles(y_ref, _dot(a.astype(BF16), wd16[...]))

    @pl.when(i >= n_tiles_ref[0])
    def _():
        y_ref[...] = jnp.zeros_like(y_ref)


def _experts_call(tile_e, n_tiles, xs, wg, wu, wd):
    d = wg.shape[1]
    tm = EXP_TM
    nt = xs.shape[0] // (tm * ROW_SUB)
    used = lambda i, n: jnp.minimum(i, n[0] - 1)
    w_spec = lambda *shape: pl.BlockSpec((1,) + shape, lambda i, te, n: (te[used(i, n)], 0, 0))
    return pl.pallas_call(
        _experts_kernel,
        grid_spec=pltpu.PrefetchScalarGridSpec(
            num_scalar_prefetch=2,
            grid=(nt,),
            in_specs=[pl.BlockSpec((tm * ROW_SUB, LANES), lambda i, te, n: (used(i, n), 0)),
                      w_spec(d, D_FF_EXP), w_spec(d, D_FF_EXP), w_spec(D_FF_EXP, d)],
            out_specs=pl.BlockSpec((tm * ROW_SUB, LANES), lambda i, te, n: (i, 0)),
            scratch_shapes=[pltpu.VMEM((d, D_FF_EXP), BF16), pltpu.VMEM((d, D_FF_EXP), BF16),
                            pltpu.VMEM((D_FF_EXP, d), BF16)],
        ),
        out_shape=jax.ShapeDtypeStruct(xs.shape, F32),
        compiler_params=pltpu.CompilerParams(
            dimension_semantics=("arbitrary",), vmem_limit_bytes=VMEM_LIMIT),
        name="experts",
    )(tile_e, n_tiles, xs, wg, wu, wd)


def _ple_kernel(cur_ref, nxt_ref, ys_hbm, h_ref, route_ref, p_ref, wple_ref, gple_ref, wpg_ref,
                o_ref, ybuf0, ybuf1, sem):
    tm = h_ref.shape[0]
    i = pl.program_id(0)
    n_steps = pl.num_programs(0)
    bufs = (ybuf0, ybuf1)

    def start_gather(pos_ref, slot):
        for r in range(2 * tm):
            pltpu.make_async_copy(_tiles_at(ys_hbm, pos_ref[r // tm, r % tm]), _tiles_at(bufs[slot], r),
                                  sem.at[slot]).start(priority=r % 2)

    @pl.when(i == 0)
    def _():
        start_gather(cur_ref, 0)

    def step(slot):
        _wait_tiles(ys_hbm, bufs[slot], 2 * tm, sem.at[slot])
        start_gather(nxt_ref, 1 - slot)
        wts = route_ref[...]
        h = (h_ref[...] + wts[:, 0:1] * _from_row_tiles(bufs[slot], 0, tm)
             + wts[:, 1:2] * _from_row_tiles(bufs[slot], tm, tm))
        gate = jax.nn.sigmoid(_dot(_rms(h, gple_ref[...]).astype(BF16), wpg_ref[...]))
        o_ref[...] = h + _dot(p_ref[...].astype(BF16), wple_ref[...]) * gate

    for slot in range(2):
        pl.when(lax.rem(i, 2) == slot)(functools.partial(step, slot))

    for slot in range(2):
        pl.when((i == n_steps - 1) & (lax.rem(n_steps, 2) == slot))(
            functools.partial(_wait_tiles, ys_hbm, bufs[slot], 2 * tm, sem.at[slot]))


def _ple_call(pos, ys, h, route, p2, wple, gple, wpg):
    t, d = h.shape
    tm = TOK_TM
    n_steps = t // tm
    n_rows = 2 * tm
    idx_spec = lambda f: pl.BlockSpec((ROUTE_FIELDS, tm), lambda i: (0, f(i)), memory_space=pltpu.SMEM)
    const = lambda *shape: pl.BlockSpec(shape, lambda i: (0,) * len(shape))
    tok = lambda w: pl.BlockSpec((tm, w), lambda i: (i, 0))
    return pl.pallas_call(
        _ple_kernel,
        grid=(n_steps,),
        in_specs=[idx_spec(lambda i: i), idx_spec(lambda i: lax.rem(i + 1, n_steps)),
                  pl.BlockSpec(memory_space=pl.ANY),
                  tok(d), tok(LANES), tok(D_PLE), const(D_PLE, d), const(1, d), const(d, d)],
        out_specs=tok(d),
        out_shape=jax.ShapeDtypeStruct((t, d), F32),
        scratch_shapes=[pltpu.VMEM((n_rows * ROW_SUB, LANES), F32), pltpu.VMEM((n_rows * ROW_SUB, LANES), F32),
                        pltpu.SemaphoreType.DMA((2,))],
        compiler_params=pltpu.CompilerParams(
            dimension_semantics=("arbitrary",), vmem_limit_bytes=VMEM_LIMIT),
        name="ple",
    )(pos, pos, ys, h, route, p2, wple, gple, wpg)


def _segment_tables(counts, t):
    tm = EXP_TM
    nt = 2 * t // tm + N_EXPERTS
    padded = ((counts + tm - 1) // tm) * tm
    seg_end = jnp.cumsum(padded)
    seg_start = seg_end - padded
    gap_len = jnp.concatenate([padded - counts, nt * tm - seg_end[-1:]])
    gap_start = jnp.concatenate([seg_start + counts, seg_end[-1:]])
    tile_start = jnp.arange(nt, dtype=jnp.int32) * tm
    tile_e = jnp.minimum(jnp.sum(tile_start[:, None] >= seg_end[None, :], axis=1), N_EXPERTS - 1)
    n_tiles = (seg_end[-1] // tm).reshape(1)
    i32 = lambda a: a.astype(jnp.int32)
    return i32(seg_start), i32(gap_start), i32(gap_len), i32(tile_e), i32(n_tiles), nt * tm


def _layer(h, p_i, g_mix, w_in, g_v_a, w_s, b_s, g_q, g_k, g_out_a, g_out_b, w_out, g_ffn, w_group,
           w_expert, w_gate_e, w_up_e, w_down_e, g_ple, w_ple, w_ple_gate):
    b, s, d = h.shape
    t = b * s
    row = lambda g: g.reshape(1, -1).astype(F32)
    wuv = w_in[:, :2 * D_A].astype(BF16)
    wqt = w_in[:, 2 * D_A:2 * D_A + D_B].T.astype(BF16)
    wk = w_in[:, 2 * D_A + D_B:2 * D_A + 2 * D_B].astype(BF16)
    wvt = w_in[:, 2 * D_A + 2 * D_B:].T.astype(BF16)
    bs = jnp.broadcast_to(b_s[:, :, None], (A_GROUPS, A_CHUNK, A_DH)).astype(F32)
    gk = row(jnp.tile(g_k, B_HEADS))
    gq = (jnp.tile(g_q, B_HEADS) * (1.0 / float(B_DH) ** 0.5)).reshape(D_B, 1).astype(F32)
    head = jnp.arange(D_B) // B_DH
    bd = jnp.where(head[:, None] == head[None, :], 1.0 / B_DH, 0.0).astype(BF16)
    w_rt = jnp.zeros((ROUTER_ROWS, d), F32)
    w_rt = w_rt.at[:N_GROUPS].set(w_group.T).at[N_GROUPS:N_GROUPS + N_EXPERTS].set(w_expert.T)
    wrt_hi, wrt_lo = _split_bf16(w_rt)

    ya, k, qt, vt = _proj_call(h, row(g_mix), wuv, wk, wqt, wvt, row(g_v_a), w_s.astype(F32), bs,
                               row(g_out_a), gk, gq, bd)
    yb = _moba_call(qt, k, vt, row(g_out_b))
    h1, hn_tiles, route, route_i, counts = _outmix_call(
        h.reshape(t, d), ya.reshape(t, D_A), yb.reshape(t, D_B), w_out.astype(BF16), row(g_ffn), wrt_hi, wrt_lo)
    counts = counts[N_GROUPS:N_GROUPS + N_EXPERTS, 0].astype(jnp.int32)
    seg_start, gap_start, gap_len, tile_e, n_tiles, n_sorted_rows = _segment_tables(counts, t)
    pos = _slots_call(seg_start, route_i)
    xs = _dispatch_call(gap_start, gap_len, pos, hn_tiles, n_sorted_rows)
    ys = _experts_call(tile_e, n_tiles, xs, w_gate_e, w_up_e, w_down_e)
    out = _ple_call(pos, ys, h1, route, p_i.reshape(t, D_PLE), w_ple.astype(BF16), row(g_ple),
                    w_ple_gate.astype(BF16))
    return out.reshape(b, s, d)


def kernel(x, p, g_mix, w_in, g_v_a, w_s, b_s, g_q, g_k, g_out_a, g_out_b, w_out, g_ffn, w_group, w_expert,
           w_gate_e, w_up_e, w_down_e, g_ple, w_ple, w_ple_gate):
    params = (g_mix, w_in, g_v_a, w_s, b_s, g_q, g_k, g_out_a, g_out_b, w_out, g_ffn, w_group, w_expert,
              w_gate_e, w_up_e, w_down_e, g_ple, w_ple, w_ple_gate)
    h = x
    for i in range(p.shape[0]):
        h = _layer(h, p[i], *(w[i] for w in params))
    return h
```

```python
import functools

import jax
import jax.numpy as jnp
from jax import lax
from jax.experimental import pallas as pl
from jax.experimental.pallas import tpu as pltpu

F32 = jnp.float32
BF16 = jnp.bfloat16

D_MODEL = 1024
D_A = 512
A_GROUPS = 4
A_DH = D_A // A_GROUPS
A_CHUNK = 128
D_B = 512
B_HEADS = 8
B_DH = D_B // B_HEADS
B_BLOCK = 256
B_TOPK = 3
D_PLE = 256
N_GROUPS = 4
EXP_PER_GROUP = 8
N_EXPERTS = N_GROUPS * EXP_PER_GROUP
D_FF_EXP = 256
EPS = 1e-6
NEG = -1e30
LOG2E = 1.4426950408889634

LANES = 128
HEAD_PAIR = 2 * B_DH
N_PAIRS = B_HEADS // 2
ROUTER_ROWS = 48
ROUTE_FIELDS = 8
VMEM_LIMIT = 56 * 1024 * 1024

PROJ_TM = 512
OUT_TM = 512
EXP_TM = 256
BOUND_MARGIN = 1.02
MIN_SHIFTED_SUM = 2.0 ** -64
TOK_TM = 256
ROW_SUB = D_MODEL // LANES


def _rms(x, g):
    return x * lax.rsqrt(jnp.mean(x * x, axis=-1, keepdims=True) + EPS) * g


def _dot(a, b):
    return jnp.dot(a, b, preferred_element_type=F32)


def _dot_nt(a, b):
    return lax.dot_general(a, b, (((1,), (1,)), ((), ())), preferred_element_type=F32)


def _split_bf16(x):
    hi = x.astype(BF16)
    lo = (x - hi.astype(F32)).astype(BF16)
    return hi, lo


def _proj_kernel(x_ref, gmix_ref, wuv_ref, wk_ref, wqt_ref, wvt_ref, gva_ref, ws_ref, bs_ref,
                 goa_ref, gk_ref, gq_ref, bd_ref,
                 ya_ref, k_ref, qt_ref, vt_ref, prod_ref):
    tm = x_ref.shape[1]
    n_chunks = tm // A_CHUNK
    xn = _rms(x_ref[0], gmix_ref[...]).astype(BF16)

    uv = _dot(xn, wuv_ref[...])
    u = jax.nn.gelu(uv[:, :D_A])
    v = jax.nn.gelu(uv[:, D_A:])
    row = lax.broadcasted_iota(jnp.int32, (A_CHUNK, A_CHUNK), 0)
    col = lax.broadcasted_iota(jnp.int32, (A_CHUNK, A_CHUNK), 1)
    causal = col <= row
    for g in range(A_GROUPS):
        gs = slice(g * A_DH, (g + 1) * A_DH)
        vn = _rms(v[:, gs], gva_ref[:, gs]).astype(BF16)
        rhs = jnp.concatenate([vn[c * A_CHUNK:(c + 1) * A_CHUNK] for c in range(n_chunks)], axis=1)
        ws = jnp.where(causal, ws_ref[g], 0.0).astype(BF16)
        mixed = _dot(ws, rhs)
        for c in range(n_chunks):
            cs = slice(c * A_CHUNK, (c + 1) * A_CHUNK)
            prod_ref[cs, gs] = u[cs, gs] * (mixed[:, cs] + bs_ref[g])
    ya_ref[0] = _rms(prod_ref[...], goa_ref[...]).astype(ya_ref.dtype)

    k = _dot(xn, wk_ref[...])
    k2_hi, k2_lo = _split_bf16(k * k)
    k_ms = _dot(k2_hi, bd_ref[...]) + _dot(k2_lo, bd_ref[...])
    k_ref[0] = (k * lax.rsqrt(k_ms + EPS) * gk_ref[...]).astype(k_ref.dtype)

    qt = _dot_nt(wqt_ref[...], xn)
    vt = _dot_nt(wvt_ref[...], xn).astype(vt_ref.dtype)
    qn = []
    for h in range(B_HEADS):
        qh = qt[h * B_DH:(h + 1) * B_DH]
        qn.append(qh * lax.rsqrt(jnp.mean(qh * qh, axis=0, keepdims=True) + EPS))
    qn = (jnp.concatenate(qn, axis=0) * gq_ref[...]).astype(qt_ref.dtype)
    for p in range(N_PAIRS):
        ps = slice(p * HEAD_PAIR, (p + 1) * HEAD_PAIR)
        for j in range(tm // B_BLOCK):
            js = slice(j * B_BLOCK, (j + 1) * B_BLOCK)
            qt_ref[0, p, j] = qn[ps, js]
            vt_ref[0, p, j] = vt[ps, js]


def _proj_call(x, gmix, wuv, wk, wqt, wvt, gva, ws, bs, goa, gk, gq, bd):
    b, s, d = x.shape
    tm = PROJ_TM
    nb = s // B_BLOCK
    jb = tm // B_BLOCK
    const = lambda *shape: pl.BlockSpec(shape, lambda bi, ti: (0,) * len(shape))
    return pl.pallas_call(
        _proj_kernel,
        grid=(b, s // tm),
        in_specs=[
            pl.BlockSpec((1, tm, d), lambda bi, ti: (bi, ti, 0)),
            const(1, d), const(d, 2 * D_A), const(d, D_B), const(D_B, d), const(D_B, d),
            const(1, D_A), const(A_GROUPS, A_CHUNK, A_CHUNK), const(A_GROUPS, A_CHUNK, A_DH),
            const(1, D_A), const(1, D_B), const(D_B, 1), const(D_B, D_B),
        ],
        out_specs=[
            pl.BlockSpec((1, tm, D_A), lambda bi, ti: (bi, ti, 0)),
            pl.BlockSpec((1, tm, D_B), lambda bi, ti: (bi, ti, 0)),
            pl.BlockSpec((1, N_PAIRS, jb, HEAD_PAIR, B_BLOCK), lambda bi, ti: (bi, 0, ti, 0, 0)),
            pl.BlockSpec((1, N_PAIRS, jb, HEAD_PAIR, B_BLOCK), lambda bi, ti: (bi, 0, ti, 0, 0)),
        ],
        out_shape=[
            jax.ShapeDtypeStruct((b, s, D_A), BF16),
            jax.ShapeDtypeStruct((b, s, D_B), BF16),
            jax.ShapeDtypeStruct((b, N_PAIRS, nb, HEAD_PAIR, B_BLOCK), BF16),
            jax.ShapeDtypeStruct((b, N_PAIRS, nb, HEAD_PAIR, B_BLOCK), BF16),
        ],
        scratch_shapes=[pltpu.VMEM((tm, D_A), F32)],
        compiler_params=pltpu.CompilerParams(
            dimension_semantics=("parallel", "parallel"), vmem_limit_bytes=VMEM_LIMIT),
        name="proj",
    )(x, gmix, wuv, wk, wqt, wvt, gva, ws, bs, goa, gk, gq, bd)


def _moba_kernel(bound_ref, qt_ref, k_ref, vt_ref, gob_ref, y_ref, kmean_ref, qz_ref, sel_ref, m_ref, l_ref,
                 acc_ref, stage0_ref, stage1_ref):
    qb = pl.program_id(1)
    nb = k_ref.shape[1] // B_BLOCK
    nbp = sel_ref.shape[1]

    @pl.when(qb == 0)
    def _():
        rows = [jnp.mean(k_ref[0, j * B_BLOCK:(j + 1) * B_BLOCK, :].astype(F32), axis=0, keepdims=True)
                for j in range(nb)]
        rows += [jnp.zeros((1, D_B), F32)] * (nbp - nb)
        km = jnp.concatenate(rows, axis=0)
        hi = km.astype(BF16)
        rest = km - hi.astype(F32)
        mid = rest.astype(BF16)
        lo = (rest - mid.astype(F32)).astype(BF16)
        for p in range(N_PAIRS):
            ps = slice(p * HEAD_PAIR, (p + 1) * HEAD_PAIR)
            kmean_ref[p] = jnp.concatenate([hi[:, ps], mid[:, ps], lo[:, ps]], axis=0)

    head_row = lax.broadcasted_iota(jnp.int32, (HEAD_PAIR, B_BLOCK), 0) // B_DH
    blk = lax.broadcasted_iota(jnp.int32, (nbp, B_BLOCK), 0)
    past = blk < qb

    for h in range(B_HEADS):
        p, hh = divmod(h, 2)
        q_pair = qt_ref[0, p, 0]
        qz = jnp.where(head_row == hh, q_pair, jnp.zeros_like(q_pair))
        qz_ref[h] = qz
        g3 = _dot(kmean_ref[p], qz)
        gate = (g3[:nbp] + g3[nbp:2 * nbp] + g3[2 * nbp:]) * float(B_DH) ** 0.5
        gate = jnp.where(past, gate, NEG)
        rank = jnp.zeros((nbp, B_BLOCK), F32)
        for j in range(nb):
            gj = gate[j:j + 1]
            ahead = (gj > gate) | ((gj == gate) & (blk > j))
            rank = rank + jnp.where(ahead, 1.0, 0.0)
        chosen = (rank < float(B_TOPK)) & (gate > NEG * 0.5)
        sel_ref[h] = jnp.where(chosen, -bound_ref[0, 0], NEG)

    def scores(h, j):
        p = h // 2
        kj = k_ref[0, pl.ds(pl.multiple_of(j * B_BLOCK, B_BLOCK), B_BLOCK), p * HEAD_PAIR:(p + 1) * HEAD_PAIR]
        return _dot(kj, qz_ref[h])

    def values(h, j):
        p, hh = divmod(h, 2)
        return vt_ref[0, p, j, hh * B_DH:(hh + 1) * B_DH, :]

    key_pos = lax.broadcasted_iota(jnp.int32, (B_BLOCK, B_BLOCK), 0)
    q_pos = lax.broadcasted_iota(jnp.int32, (B_BLOCK, B_BLOCK), 1)
    causal = key_pos <= q_pos

    bound = bound_ref[0, 0]
    stage = (stage0_ref, stage1_ref)
    own_shift = jnp.where(causal, -bound, NEG)

    def stage_scores(h, block, shift, parity):
        stage[parity][h] = scores(h, block) * LOG2E + shift

    def consume(h, s, parity):
        pr = jnp.exp2(stage[parity][h])
        l_ref[h] += jnp.sum(pr, axis=0, keepdims=True)
        acc_ref[h] += _dot(values(h, jnp.where(s == 0, qb, s - 1)), pr.astype(BF16))

    def full_step(s, parity):
        for h in range(B_HEADS):
            stage_scores(h, s - 1, sel_ref[h, pl.ds(s - 1, 1), :], parity)
            consume(h, s - 1, 1 - parity)

    for h in range(B_HEADS):
        stage_scores(h, qb, own_shift, 0)
        l_ref[h] = jnp.zeros((1, B_BLOCK), F32)
        acc_ref[h] = jnp.zeros((B_DH, B_BLOCK), F32)

    def two_steps(u, carry):
        full_step(2 * u + 1, 1)
        full_step(2 * u + 2, 0)
        return carry

    lax.fori_loop(0, qb // 2, two_steps, 0)

    @pl.when(lax.rem(qb, 2) == 1)
    def _():
        full_step(qb, 1)

    for parity in range(2):
        @pl.when(lax.rem(qb, 2) == parity)
        def _(parity=parity):
            for h in range(B_HEADS):
                consume(h, qb, parity)

    smallest = jnp.min(jnp.concatenate([l_ref[h] for h in range(B_HEADS)], axis=0))

    @pl.when(jnp.logical_not(smallest > MIN_SHIFTED_SUM))
    def _():
        def online(h, j, own, first):
            visible = causal if own else sel_ref[h, pl.ds(j, 1), :] > NEG * 0.5
            sc = jnp.where(visible, scores(h, j) * LOG2E, NEG)
            m_tile = jnp.max(sc, axis=0, keepdims=True)
            m_new = m_tile if first else jnp.maximum(m_ref[h], m_tile)
            pr = jnp.exp2(sc - m_new)
            l_new, acc_new = jnp.sum(pr, axis=0, keepdims=True), _dot(values(h, j), pr.astype(BF16))
            if not first:
                alpha = jnp.exp2(m_ref[h] - m_new)
                l_new, acc_new = alpha * l_ref[h] + l_new, alpha * acc_ref[h] + acc_new
            m_ref[h], l_ref[h], acc_ref[h] = m_new, l_new, acc_new

        for h in range(B_HEADS):
            online(h, qb, True, True)

        def past(j, carry):
            for h in range(B_HEADS):
                online(h, j, False, False)
            return carry

        lax.fori_loop(0, qb, past, 0)

    yt = jnp.concatenate([acc_ref[h] / l_ref[h] for h in range(B_HEADS)], axis=0)
    yt = yt * lax.rsqrt(jnp.mean(yt * yt, axis=0, keepdims=True) + EPS)
    y_ref[0] = (yt.T * gob_ref[...]).astype(y_ref.dtype)


def _moba_call(bound, qt, k, vt, gob):
    b, s, _ = k.shape
    nb = s // B_BLOCK
    nbp = -(-nb // 8) * 8
    return pl.pallas_call(
        _moba_kernel,
        grid=(b, nb),
        in_specs=[
            pl.BlockSpec(memory_space=pltpu.SMEM),
            pl.BlockSpec((1, N_PAIRS, 1, HEAD_PAIR, B_BLOCK), lambda bi, qi: (bi, 0, qi, 0, 0)),
            pl.BlockSpec((1, s, D_B), lambda bi, qi: (bi, 0, 0)),
            pl.BlockSpec((1, N_PAIRS, nb, HEAD_PAIR, B_BLOCK), lambda bi, qi: (bi, 0, 0, 0, 0)),
            pl.BlockSpec((1, D_B), lambda bi, qi: (0, 0)),
        ],
        out_specs=pl.BlockSpec((1, B_BLOCK, D_B), lambda bi, qi: (bi, qi, 0)),
        out_shape=jax.ShapeDtypeStruct((b, s, D_B), BF16),
        scratch_shapes=[
            pltpu.VMEM((N_PAIRS, 3 * nbp, HEAD_PAIR), BF16),
            pltpu.VMEM((B_HEADS, HEAD_PAIR, B_BLOCK), BF16),
            pltpu.VMEM((B_HEADS, nbp, B_BLOCK), F32),
            pltpu.VMEM((B_HEADS, 1, B_BLOCK), F32),
            pltpu.VMEM((B_HEADS, 1, B_BLOCK), F32),
            pltpu.VMEM((B_HEADS, B_DH, B_BLOCK), F32),
            pltpu.VMEM((B_HEADS, B_BLOCK, B_BLOCK), F32),
            pltpu.VMEM((B_HEADS, B_BLOCK, B_BLOCK), F32),
        ],
        compiler_params=pltpu.CompilerParams(
            dimension_semantics=("parallel", "arbitrary"), vmem_limit_bytes=VMEM_LIMIT),
        name="moba",
    )(bound, qt, k, vt, gob)


def _outmix_kernel(x_ref, ya_ref, yb_ref, wout_ref, gffn_ref, wrt_hi_ref, wrt_lo_ref,
                   h_ref, hn_ref, route_ref, route_i_ref, counts_ref, cnt_ref):
    h = x_ref[...] + _dot(ya_ref[...], wout_ref[:D_A]) + _dot(yb_ref[...], wout_ref[D_A:])
    h_ref[...] = h
    hn = _rms(h, gffn_ref[...])
    _to_row_tiles(hn_ref, hn)
    tm = hn.shape[0]

    hn_hi, hn_lo = _split_bf16(hn)
    lg = _dot_nt(wrt_hi_ref[...], hn_hi) + (_dot_nt(wrt_hi_ref[...], hn_lo) + _dot_nt(wrt_lo_ref[...], hn_hi))
    row = lax.broadcasted_iota(jnp.int32, lg.shape, 0)

    def first_row(mask):
        return jnp.min(jnp.where(mask, row, ROUTER_ROWS), axis=0, keepdims=True)

    is_g = row < N_GROUPS
    g_max = jnp.max(jnp.where(is_g, lg, -jnp.inf), axis=0, keepdims=True)
    p_g = 1.0 / jnp.sum(jnp.where(is_g, jnp.exp(lg - g_max), 0.0), axis=0, keepdims=True)
    g_sel = first_row(is_g & (lg == g_max))
    e_row = row - N_GROUPS
    in_grp = (e_row >= 0) & (e_row < N_EXPERTS) & (lax.shift_right_arithmetic(e_row, 3) == g_sel)
    l1 = jnp.max(jnp.where(in_grp, lg, -jnp.inf), axis=0, keepdims=True)
    i1 = first_row(in_grp & (lg == l1))
    rest = in_grp & (row != i1)
    l2 = jnp.max(jnp.where(rest, lg, -jnp.inf), axis=0, keepdims=True)
    i2 = first_row(rest & (lg == l2))
    e2 = jnp.exp(l2 - l1)
    w1 = 1.0 / (1.0 + e2)
    w2 = e2 / (1.0 + e2)

    @pl.when(pl.program_id(0) == 0)
    def _():
        cnt_ref[...] = jnp.zeros_like(cnt_ref)

    picked = jnp.where((row == i1) | (row == i2), 1.0, 0.0)
    tok_r = lax.broadcasted_iota(jnp.int32, (tm, tm), 0)
    tok_c = lax.broadcasted_iota(jnp.int32, (tm, tm), 1)
    earlier = jnp.where(tok_r < tok_c, 1.0, 0.0).astype(BF16)
    before = _dot(picked.astype(BF16), earlier) + cnt_ref[:, 0:1]
    r1 = jnp.sum(jnp.where(row == i1, before, 0.0), axis=0, keepdims=True)
    r2 = jnp.sum(jnp.where(row == i2, before, 0.0), axis=0, keepdims=True)
    cnt_ref[...] += jnp.sum(picked, axis=1, keepdims=True)
    counts_ref[...] = cnt_ref[...]

    e1, e2x = i1 - N_GROUPS, i2 - N_GROUPS
    route_i_ref[...] = jnp.concatenate(
        [e1, e2x, r1.astype(jnp.int32), r2.astype(jnp.int32), jnp.zeros((ROUTE_FIELDS - 4, tm), jnp.int32)], axis=0)
    wts = jnp.concatenate([p_g * w1, p_g * w2, jnp.zeros((LANES - 2, tm), F32)], axis=0)
    route_ref[...] = wts.T


def _outmix_call(x2, ya2, yb2, wout, gffn, wrt_hi, wrt_lo):
    t, d = x2.shape
    tm = OUT_TM
    const = lambda *shape: pl.BlockSpec(shape, lambda ti: (0,) * len(shape))
    tok = lambda w: pl.BlockSpec((tm, w), lambda ti: (ti, 0))
    return pl.pallas_call(
        _outmix_kernel,
        grid=(t // tm,),
        in_specs=[tok(d), tok(D_A), tok(D_B), const(d, d), const(1, d),
                  const(ROUTER_ROWS, d), const(ROUTER_ROWS, d)],
        out_specs=[tok(d), pl.BlockSpec((tm * ROW_SUB, LANES), lambda ti: (ti, 0)), tok(LANES),
                   pl.BlockSpec((ROUTE_FIELDS, tm), lambda ti: (0, ti)), const(ROUTER_ROWS, LANES)],
        out_shape=[jax.ShapeDtypeStruct((t, d), F32), jax.ShapeDtypeStruct((t * ROW_SUB, LANES), F32),
                   jax.ShapeDtypeStruct((t, LANES), F32),
                   jax.ShapeDtypeStruct((ROUTE_FIELDS, t), jnp.int32),
                   jax.ShapeDtypeStruct((ROUTER_ROWS, LANES), F32)],
        scratch_shapes=[pltpu.VMEM((ROUTER_ROWS, LANES), F32)],
        compiler_params=pltpu.CompilerParams(
            dimension_semantics=("arbitrary",), vmem_limit_bytes=VMEM_LIMIT),
        name="outmix",
    )(x2, ya2, yb2, wout, gffn, wrt_hi, wrt_lo)


def _to_row_tiles(ref, x):
    n = x.shape[0]
    for s in range(ROW_SUB):
        ref[pl.ds(s, n, stride=ROW_SUB), :] = x[:, s * LANES:(s + 1) * LANES]


def _from_row_tiles(ref, first, n):
    return jnp.concatenate(
        [ref[pl.ds(first * ROW_SUB + s, n, stride=ROW_SUB), :] for s in range(ROW_SUB)], axis=1)


def _tiles_at(ref, row, n=1):
    return ref.at[pl.ds(pl.multiple_of(row * ROW_SUB, ROW_SUB), n * ROW_SUB), :]


def _wait_tiles(hbm, vmem_ref, n_tiles, sem):
    cap = vmem_ref.shape[0] // ROW_SUB
    while n_tiles > 0:
        n = min(n_tiles, cap)
        pltpu.make_async_copy(hbm.at[pl.ds(0, n * ROW_SUB), :], vmem_ref.at[pl.ds(0, n * ROW_SUB), :], sem).wait()
        n_tiles -= n


def _slots_kernel(seg_start_ref, route_i_ref, pos_ref):
    experts = route_i_ref[0:2, :]
    start = jnp.zeros_like(experts)
    for e in range(N_EXPERTS):
        start = jnp.where(experts == e, seg_start_ref[e], start)
    pos_ref[...] = jnp.zeros_like(pos_ref)
    pos_ref[0:2, :] = start + route_i_ref[2:4, :]


def _slots_call(seg_start, route_i):
    return pl.pallas_call(
        _slots_kernel,
        grid_spec=pltpu.PrefetchScalarGridSpec(
            num_scalar_prefetch=1,
            grid=(1,),
            in_specs=[pl.BlockSpec(route_i.shape, lambda i, s: (0, 0))],
            out_specs=pl.BlockSpec(route_i.shape, lambda i, s: (0, 0)),
        ),
        out_shape=jax.ShapeDtypeStruct(route_i.shape, jnp.int32),
        name="slots",
    )(seg_start, route_i)


def _dispatch_kernel(gap_start_ref, gap_len_ref, pos_ref, hn_ref, xs_hbm, zero_ref, sem, fill_sem):
    tm = hn_ref.shape[0] // ROW_SUB

    @pl.when(pl.program_id(0) == 0)
    def _():
        zero_ref[...] = jnp.zeros_like(zero_ref)
        fills = []
        for g in range(N_EXPERTS):
            row, n = gap_start_ref[g], gap_len_ref[g]
            size = EXP_TM // 2
            while size >= 1:
                fills.append(((n & size) != 0, row, size))
                row = row + (n & size)
                size //= 2
        for tile in range(N_EXPERTS):
            fills.append((tile * EXP_TM < gap_len_ref[N_EXPERTS], gap_start_ref[N_EXPERTS] + tile * EXP_TM, EXP_TM))
        copies = [(cond, pltpu.make_async_copy(_tiles_at(zero_ref, 0, size), _tiles_at(xs_hbm, row, size), fill_sem))
                  for cond, row, size in fills]
        for cond, copy in copies:
            pl.when(cond)(copy.start)
        for cond, copy in copies:
            pl.when(cond)(copy.wait)

    for r in range(2 * tm):
        k, tok = divmod(r, tm)
        pltpu.make_async_copy(_tiles_at(hn_ref, tok), _tiles_at(xs_hbm, pos_ref[k, tok]), sem).start(priority=r % 2)
    _wait_tiles(xs_hbm, hn_ref, 2 * tm, sem)


def _dispatch_call(gap_start, gap_len, pos, hn_tiles, n_sorted_rows):
    tm = TOK_TM
    return pl.pallas_call(
        _dispatch_kernel,
        grid_spec=pltpu.PrefetchScalarGridSpec(
            num_scalar_prefetch=2,
            grid=(pos.shape[1] // tm,),
            in_specs=[pl.BlockSpec((ROUTE_FIELDS, tm), lambda i, *_: (0, i), memory_space=pltpu.SMEM),
                      pl.BlockSpec((tm * ROW_SUB, LANES), lambda i, *_: (i, 0))],
            out_specs=pl.BlockSpec(memory_space=pl.ANY),
            scratch_shapes=[pltpu.VMEM((EXP_TM * ROW_SUB, LANES), F32), pltpu.SemaphoreType.DMA(()),
                            pltpu.SemaphoreType.DMA(())],
        ),
        out_shape=jax.ShapeDtypeStruct((n_sorted_rows * ROW_SUB, LANES), F32),
        compiler_params=pltpu.CompilerParams(
            dimension_semantics=("arbitrary",), vmem_limit_bytes=VMEM_LIMIT),
        name="dispatch",
    )(gap_start, gap_len, pos, hn_tiles)


def _experts_kernel(tile_e_ref, n_tiles_ref, xs_ref, wg_ref, wu_ref, wd_ref, y_ref, wg16, wu16, wd16):
    i = pl.program_id(0)
    tm = xs_ref.shape[0] // ROW_SUB

    @pl.when((i == 0) | (tile_e_ref[i] != tile_e_ref[jnp.maximum(i - 1, 0)]))
    def _():
        wg16[...] = wg_ref[0].astype(BF16)
        wu16[...] = wu_ref[0].astype(BF16)
        wd16[...] = wd_ref[0].astype(BF16)

    @pl.when(i < n_tiles_ref[0])
    def _():
        x = _from_row_tiles(xs_ref, 0, tm).astype(BF16)
        a = jax.nn.silu(_dot(x, wg16[...])) * _dot(x, wu16[...])
        _to_row_tiles(y_ref, _dot(a.astype(BF16), wd16[...]))

    @pl.when(i >= n_tiles_ref[0])
    def _():
        y_ref[...] = jnp.zeros_like(y_ref)


def _experts_call(tile_e, n_tiles, xs, wg, wu, wd):
    d = wg.shape[1]
    tm = EXP_TM
    nt = xs.shape[0] // (tm * ROW_SUB)
    used = lambda i, n: jnp.minimum(i, n[0] - 1)
    w_spec = lambda *shape: pl.BlockSpec((1,) + shape, lambda i, te, n: (te[used(i, n)], 0, 0))
    return pl.pallas_call(
        _experts_kernel,
        grid_spec=pltpu.PrefetchScalarGridSpec(
            num_scalar_prefetch=2,
            grid=(nt,),
            in_specs=[pl.BlockSpec((tm * ROW_SUB, LANES), lambda i, te, n: (used(i, n), 0)),
                      w_spec(d, D_FF_EXP), w_spec(d, D_FF_EXP), w_spec(D_FF_EXP, d)],
            out_specs=pl.BlockSpec((tm * ROW_SUB, LANES), lambda i, te, n: (i, 0)),
            scratch_shapes=[pltpu.VMEM((d, D_FF_EXP), BF16), pltpu.VMEM((d, D_FF_EXP), BF16),
                            pltpu.VMEM((D_FF_EXP, d), BF16)],
        ),
        out_shape=jax.ShapeDtypeStruct(xs.shape, F32),
        compiler_params=pltpu.CompilerParams(
            dimension_semantics=("arbitrary",), vmem_limit_bytes=VMEM_LIMIT),
        name="experts",
    )(tile_e, n_tiles, xs, wg, wu, wd)


def _ple_kernel(cur_ref, nxt_ref, ys_hbm, h_ref, route_ref, p_ref, wple_ref, gple_ref, wpg_ref,
                o_ref, ybuf0, ybuf1, sem):
    tm = h_ref.shape[0]
    i = pl.program_id(0)
    n_steps = pl.num_programs(0)
    bufs = (ybuf0, ybuf1)

    def start_gather(pos_ref, slot):
        for r in range(2 * tm):
            pltpu.make_async_copy(_tiles_at(ys_hbm, pos_ref[r // tm, r % tm]), _tiles_at(bufs[slot], r),
                                  sem.at[slot]).start(priority=r % 2)

    @pl.when(i == 0)
    def _():
        start_gather(cur_ref, 0)

    def step(slot):
        _wait_tiles(ys_hbm, bufs[slot], 2 * tm, sem.at[slot])
        start_gather(nxt_ref, 1 - slot)
        wts = route_ref[...]
        h = (h_ref[...] + wts[:, 0:1] * _from_row_tiles(bufs[slot], 0, tm)
             + wts[:, 1:2] * _from_row_tiles(bufs[slot], tm, tm))
        gate = jax.nn.sigmoid(_dot(_rms(h, gple_ref[...]).astype(BF16), wpg_ref[...]))
        o_ref[...] = h + _dot(p_ref[...].astype(BF16), wple_ref[...]) * gate

    for slot in range(2):
        pl.when(lax.rem(i, 2) == slot)(functools.partial(step, slot))

    for slot in range(2):
        pl.when((i == n_steps - 1) & (lax.rem(n_steps, 2) == slot))(
            functools.partial(_wait_tiles, ys_hbm, bufs[slot], 2 * tm, sem.at[slot]))


def _ple_call(pos, ys, h, route, p2, wple, gple, wpg):
    t, d = h.shape
    tm = TOK_TM
    n_steps = t // tm
    n_rows = 2 * tm
    idx_spec = lambda f: pl.BlockSpec((ROUTE_FIELDS, tm), lambda i: (0, f(i)), memory_space=pltpu.SMEM)
    const = lambda *shape: pl.BlockSpec(shape, lambda i: (0,) * len(shape))
    tok = lambda w: pl.BlockSpec((tm, w), lambda i: (i, 0))
    return pl.pallas_call(
        _ple_kernel,
        grid=(n_steps,),
        in_specs=[idx_spec(lambda i: i), idx_spec(lambda i: lax.rem(i + 1, n_steps)),
                  pl.BlockSpec(memory_space=pl.ANY),
                  tok(d), tok(LANES), tok(D_PLE), const(D_PLE, d), const(1, d), const(d, d)],
        out_specs=tok(d),
        out_shape=jax.ShapeDtypeStruct((t, d), F32),
        scratch_shapes=[pltpu.VMEM((n_rows * ROW_SUB, LANES), F32), pltpu.VMEM((n_rows * ROW_SUB, LANES), F32),
                        pltpu.SemaphoreType.DMA((2,))],
        compiler_params=pltpu.CompilerParams(
            dimension_semantics=("arbitrary",), vmem_limit_bytes=VMEM_LIMIT),
        name="ple",
    )(pos, pos, ys, h, route, p2, wple, gple, wpg)


def _segment_tables(counts, t):
    tm = EXP_TM
    nt = 2 * t // tm + N_EXPERTS
    padded = ((counts + tm - 1) // tm) * tm
    seg_end = jnp.cumsum(padded)
    seg_start = seg_end - padded
    gap_len = jnp.concatenate([padded - counts, nt * tm - seg_end[-1:]])
    gap_start = jnp.concatenate([seg_start + counts, seg_end[-1:]])
    tile_start = jnp.arange(nt, dtype=jnp.int32) * tm
    tile_e = jnp.minimum(jnp.sum(tile_start[:, None] >= seg_end[None, :], axis=1), N_EXPERTS - 1)
    n_tiles = (seg_end[-1] // tm).reshape(1)
    i32 = lambda a: a.astype(jnp.int32)
    return i32(seg_start), i32(gap_start), i32(gap_len), i32(tile_e), i32(n_tiles), nt * tm


def _layer(h, p_i, g_mix, w_in, g_v_a, w_s, b_s, g_q, g_k, g_out_a, g_out_b, w_out, g_ffn, w_group,
           w_expert, w_gate_e, w_up_e, w_down_e, g_ple, w_ple, w_ple_gate):
    b, s, d = h.shape
    t = b * s
    row = lambda g: g.reshape(1, -1).astype(F32)
    wuv = w_in[:, :2 * D_A].astype(BF16)
    wqt = w_in[:, 2 * D_A:2 * D_A + D_B].T.astype(BF16)
    wk = w_in[:, 2 * D_A + D_B:2 * D_A + 2 * D_B].astype(BF16)
    wvt = w_in[:, 2 * D_A + 2 * D_B:].T.astype(BF16)
    bs = jnp.broadcast_to(b_s[:, :, None], (A_GROUPS, A_CHUNK, A_DH)).astype(F32)
    gk = row(jnp.tile(g_k, B_HEADS))
    gq = (jnp.tile(g_q, B_HEADS) * (1.0 / float(B_DH) ** 0.5)).reshape(D_B, 1).astype(F32)
    head = jnp.arange(D_B) // B_DH
    bd = jnp.where(head[:, None] == head[None, :], 1.0 / B_DH, 0.0).astype(BF16)
    w_rt = jnp.zeros((ROUTER_ROWS, d), F32)
    w_rt = w_rt.at[:N_GROUPS].set(w_group.T).at[N_GROUPS:N_GROUPS + N_EXPERTS].set(w_expert.T)
    wrt_hi, wrt_lo = _split_bf16(w_rt)

    ya, k, qt, vt = _proj_call(h, row(g_mix), wuv, wk, wqt, wvt, row(g_v_a), w_s.astype(F32), bs,
                               row(g_out_a), gk, gq, bd)
    bound = (BOUND_MARGIN * LOG2E * float(B_DH) ** 0.5) * jnp.max(jnp.abs(g_q)) * jnp.max(jnp.abs(g_k))
    yb = _moba_call(bound.reshape(1, 1).astype(F32), qt, k, vt, row(g_out_b))
    h1, hn_tiles, route, route_i, counts = _outmix_call(
        h.reshape(t, d), ya.reshape(t, D_A), yb.reshape(t, D_B), w_out.astype(BF16), row(g_ffn), wrt_hi, wrt_lo)
    counts = counts[N_GROUPS:N_GROUPS + N_EXPERTS, 0].astype(jnp.int32)
    seg_start, gap_start, gap_len, tile_e, n_tiles, n_sorted_rows = _segment_tables(counts, t)
    pos = _slots_call(seg_start, route_i)
    xs = _dispatch_call(gap_start, gap_len, pos, hn_tiles, n_sorted_rows)
    ys = _experts_call(tile_e, n_tiles, xs, w_gate_e, w_up_e, w_down_e)
    out = _ple_call(pos, ys, h1, route, p_i.reshape(t, D_PLE), w_ple.astype(BF16), row(g_ple),
                    w_ple_gate.astype(BF16))
    return out.reshape(b, s, d)


def kernel(x, p, g_mix, w_in, g_v_a, w_s, b_s, g_q, g_k, g_out_a, g_out_b, w_out, g_ffn, w_group, w_expert,
           w_gate_e, w_up_e, w_down_e, g_ple, w_ple, w_ple_gate):
    params = (g_mix, w_in, g_v_a, w_s, b_s, g_q, g_k, g_out_a, g_out_b, w_out, g_ffn, w_group, w_expert,
              w_gate_e, w_up_e, w_down_e, g_ple, w_ple, w_ple_gate)
    h = x
    for i in range(p.shape[0]):
        h = _layer(h, p[i], *(w[i] for w in params))
    return h
```

```python
import functools

import jax
import jax.numpy as jnp
from jax import lax
from jax.experimental import pallas as pl
from jax.experimental.pallas import tpu as pltpu

F32 = jnp.float32
BF16 = jnp.bfloat16

D_MODEL = 1024
D_A = 512
A_GROUPS = 4
A_DH = D_A // A_GROUPS
A_CHUNK = 128
D_B = 512
B_HEADS = 8
B_DH = D_B // B_HEADS
B_BLOCK = 256
B_TOPK = 3
D_PLE = 256
N_GROUPS = 4
EXP_PER_GROUP = 8
N_EXPERTS = N_GROUPS * EXP_PER_GROUP
D_FF_EXP = 256
EPS = 1e-6
NEG = -1e30
LOG2E = 1.4426950408889634

LANES = 128
HEAD_PAIR = 2 * B_DH
N_PAIRS = B_HEADS // 2
ROUTER_ROWS = 48
ROUTE_FIELDS = 8
VMEM_LIMIT = 56 * 1024 * 1024

PROJ_TM = 512
OUT_TM = 512
EXP_TM = 256
MOBA_STEPS_PER_TRIP = 4
BOUND_MARGIN = 1.02
MIN_SHIFTED_SUM = 2.0 ** -64
TOK_TM = 256
ROW_SUB = D_MODEL // LANES


def _rms(x, g):
    return x * lax.rsqrt(jnp.mean(x * x, axis=-1, keepdims=True) + EPS) * g


def _dot(a, b):
    return jnp.dot(a, b, preferred_element_type=F32)


def _dot_nt(a, b):
    return lax.dot_general(a, b, (((1,), (1,)), ((), ())), preferred_element_type=F32)


def _split_bf16(x):
    hi = x.astype(BF16)
    lo = (x - hi.astype(F32)).astype(BF16)
    return hi, lo


def _proj_kernel(x_ref, gmix_ref, wuv_ref, wk_ref, wqt_ref, wvt_ref, gva_ref, ws_ref, bs_ref,
                 goa_ref, gk_ref, gq_ref, bd_ref,
                 ya_ref, k_ref, qt_ref, vt_ref, prod_ref):
    tm = x_ref.shape[1]
    n_chunks = tm // A_CHUNK
    xn = _rms(x_ref[0], gmix_ref[...]).astype(BF16)

    uv = _dot(xn, wuv_ref[...])
    u = jax.nn.gelu(uv[:, :D_A])
    v = jax.nn.gelu(uv[:, D_A:])
    row = lax.broadcasted_iota(jnp.int32, (A_CHUNK, A_CHUNK), 0)
    col = lax.broadcasted_iota(jnp.int32, (A_CHUNK, A_CHUNK), 1)
    causal = col <= row
    for g in range(A_GROUPS):
        gs = slice(g * A_DH, (g + 1) * A_DH)
        vn = _rms(v[:, gs], gva_ref[:, gs]).astype(BF16)
        rhs = jnp.concatenate([vn[c * A_CHUNK:(c + 1) * A_CHUNK] for c in range(n_chunks)], axis=1)
        ws = jnp.where(causal, ws_ref[g], 0.0).astype(BF16)
        mixed = _dot(ws, rhs)
        for c in range(n_chunks):
            cs = slice(c * A_CHUNK, (c + 1) * A_CHUNK)
            prod_ref[cs, gs] = u[cs, gs] * (mixed[:, cs] + bs_ref[g])
    ya_ref[0] = _rms(prod_ref[...], goa_ref[...]).astype(ya_ref.dtype)

    k = _dot(xn, wk_ref[...])
    k2_hi, k2_lo = _split_bf16(k * k)
    k_ms = _dot(k2_hi, bd_ref[...]) + _dot(k2_lo, bd_ref[...])
    k_ref[0] = (k * lax.rsqrt(k_ms + EPS) * gk_ref[...]).astype(k_ref.dtype)

    qt = _dot_nt(wqt_ref[...], xn)
    vt = _dot_nt(wvt_ref[...], xn).astype(vt_ref.dtype)
    qn = []
    for h in range(B_HEADS):
        qh = qt[h * B_DH:(h + 1) * B_DH]
        qn.append(qh * lax.rsqrt(jnp.mean(qh * qh, axis=0, keepdims=True) + EPS))
    qn = (jnp.concatenate(qn, axis=0) * gq_ref[...]).astype(qt_ref.dtype)
    for p in range(N_PAIRS):
        ps = slice(p * HEAD_PAIR, (p + 1) * HEAD_PAIR)
        for j in range(tm // B_BLOCK):
            js = slice(j * B_BLOCK, (j + 1) * B_BLOCK)
            qt_ref[0, p, j] = qn[ps, js]
            vt_ref[0, p, j] = vt[ps, js]


def _proj_call(x, gmix, wuv, wk, wqt, wvt, gva, ws, bs, goa, gk, gq, bd):
    b, s, d = x.shape
    tm = PROJ_TM
    nb = s // B_BLOCK
    jb = tm // B_BLOCK
    const = lambda *shape: pl.BlockSpec(shape, lambda bi, ti: (0,) * len(shape))
    return pl.pallas_call(
        _proj_kernel,
        grid=(b, s // tm),
        in_specs=[
            pl.BlockSpec((1, tm, d), lambda bi, ti: (bi, ti, 0)),
            const(1, d), const(d, 2 * D_A), const(d, D_B), const(D_B, d), const(D_B, d),
            const(1, D_A), const(A_GROUPS, A_CHUNK, A_CHUNK), const(A_GROUPS, A_CHUNK, A_DH),
            const(1, D_A), const(1, D_B), const(D_B, 1), const(D_B, D_B),
        ],
        out_specs=[
            pl.BlockSpec((1, tm, D_A), lambda bi, ti: (bi, ti, 0)),
            pl.BlockSpec((1, tm, D_B), lambda bi, ti: (bi, ti, 0)),
            pl.BlockSpec((1, N_PAIRS, jb, HEAD_PAIR, B_BLOCK), lambda bi, ti: (bi, 0, ti, 0, 0)),
            pl.BlockSpec((1, N_PAIRS, jb, HEAD_PAIR, B_BLOCK), lambda bi, ti: (bi, 0, ti, 0, 0)),
        ],
        out_shape=[
            jax.ShapeDtypeStruct((b, s, D_A), BF16),
            jax.ShapeDtypeStruct((b, s, D_B), BF16),
            jax.ShapeDtypeStruct((b, N_PAIRS, nb, HEAD_PAIR, B_BLOCK), BF16),
            jax.ShapeDtypeStruct((b, N_PAIRS, nb, HEAD_PAIR, B_BLOCK), BF16),
        ],
        scratch_shapes=[pltpu.VMEM((tm, D_A), F32)],
        compiler_params=pltpu.CompilerParams(
            dimension_semantics=("parallel", "parallel"), vmem_limit_bytes=VMEM_LIMIT),
        name="proj",
    )(x, gmix, wuv, wk, wqt, wvt, gva, ws, bs, goa, gk, gq, bd)


def _moba_kernel(bound_ref, qt_ref, k_ref, vt_ref, gob_ref, y_ref, kmean_ref, qz_ref, sel_ref, m_ref, l_ref,
                 acc_ref, stage0_ref, stage1_ref):
    qb = pl.program_id(1)
    nb = k_ref.shape[1] // B_BLOCK
    nbp = sel_ref.shape[1]

    @pl.when(qb == 0)
    def _():
        rows = [jnp.mean(k_ref[0, j * B_BLOCK:(j + 1) * B_BLOCK, :].astype(F32), axis=0, keepdims=True)
                for j in range(nb)]
        rows += [jnp.zeros((1, D_B), F32)] * (nbp - nb)
        km = jnp.concatenate(rows, axis=0)
        hi = km.astype(BF16)
        rest = km - hi.astype(F32)
        mid = rest.astype(BF16)
        lo = (rest - mid.astype(F32)).astype(BF16)
        for p in range(N_PAIRS):
            ps = slice(p * HEAD_PAIR, (p + 1) * HEAD_PAIR)
            kmean_ref[p] = jnp.concatenate([hi[:, ps], mid[:, ps], lo[:, ps]], axis=0)

    head_row = lax.broadcasted_iota(jnp.int32, (HEAD_PAIR, B_BLOCK), 0) // B_DH
    blk = lax.broadcasted_iota(jnp.int32, (nbp, B_BLOCK), 0)
    past = blk < qb

    for h in range(B_HEADS):
        p, hh = divmod(h, 2)
        q_pair = qt_ref[0, p, 0]
        qz = jnp.where(head_row == hh, q_pair, jnp.zeros_like(q_pair))
        qz_ref[h] = qz
        g3 = _dot(kmean_ref[p], qz)
        gate = (g3[:nbp] + g3[nbp:2 * nbp] + g3[2 * nbp:]) * float(B_DH) ** 0.5
        gate = jnp.where(past, gate, NEG)
        left = gate
        taken = blk < 0
        for _ in range(min(B_TOPK, nb)):
            top = jnp.max(left, axis=0, keepdims=True)
            first = jnp.min(jnp.where(left == top, blk, nbp), axis=0, keepdims=True)
            taken = taken | (blk == first)
            left = jnp.where(blk == first, -jnp.inf, left)
        chosen = taken & (gate > NEG * 0.5)
        sel_ref[h] = jnp.where(chosen, -bound_ref[0, 0], NEG)

    def scores(h, j):
        p = h // 2
        kj = k_ref[0, pl.ds(pl.multiple_of(j * B_BLOCK, B_BLOCK), B_BLOCK), p * HEAD_PAIR:(p + 1) * HEAD_PAIR]
        return _dot(kj, qz_ref[h])

    def values(h, j):
        p, hh = divmod(h, 2)
        return vt_ref[0, p, j, hh * B_DH:(hh + 1) * B_DH, :]

    key_pos = lax.broadcasted_iota(jnp.int32, (B_BLOCK, B_BLOCK), 0)
    q_pos = lax.broadcasted_iota(jnp.int32, (B_BLOCK, B_BLOCK), 1)
    causal = key_pos <= q_pos

    bound = bound_ref[0, 0]
    stage = (stage0_ref, stage1_ref)
    own_shift = jnp.where(causal, -bound, NEG)

    def stage_scores(h, block, shift, parity):
        stage[parity][h] = scores(h, block) * LOG2E + shift

    def consume(h, s, parity):
        pr = jnp.exp2(stage[parity][h])
        l_ref[h] += jnp.sum(pr, axis=0, keepdims=True)
        acc_ref[h] += _dot(values(h, jnp.where(s == 0, qb, s - 1)), pr.astype(BF16))

    def full_step(s, parity):
        for h in range(B_HEADS):
            stage_scores(h, s - 1, sel_ref[h, pl.ds(s - 1, 1), :], parity)
            consume(h, s - 1, 1 - parity)

    for h in range(B_HEADS):
        stage_scores(h, qb, own_shift, 0)
        l_ref[h] = jnp.zeros((1, B_BLOCK), F32)
        acc_ref[h] = jnp.zeros((B_DH, B_BLOCK), F32)

    def trip(u, carry):
        for v in range(MOBA_STEPS_PER_TRIP):
            full_step(MOBA_STEPS_PER_TRIP * u + v + 1, (v + 1) % 2)
        return carry

    lax.fori_loop(0, qb // MOBA_STEPS_PER_TRIP, trip, 0)
    done = (qb // MOBA_STEPS_PER_TRIP) * MOBA_STEPS_PER_TRIP
    for v in range(MOBA_STEPS_PER_TRIP - 1):
        pl.when(qb - done > v)(functools.partial(full_step, done + v + 1, (v + 1) % 2))

    for parity in range(2):
        @pl.when(lax.rem(qb, 2) == parity)
        def _(parity=parity):
            for h in range(B_HEADS):
                consume(h, qb, parity)

    smallest = jnp.min(jnp.concatenate([l_ref[h] for h in range(B_HEADS)], axis=0))

    @pl.when(jnp.logical_not(smallest > MIN_SHIFTED_SUM))
    def _():
        def online(h, j, own, first):
            visible = causal if own else sel_ref[h, pl.ds(j, 1), :] > NEG * 0.5
            sc = jnp.where(visible, scores(h, j) * LOG2E, NEG)
            m_tile = jnp.max(sc, axis=0, keepdims=True)
            m_new = m_tile if first else jnp.maximum(m_ref[h], m_tile)
            pr = jnp.exp2(sc - m_new)
            l_new, acc_new = jnp.sum(pr, axis=0, keepdims=True), _dot(values(h, j), pr.astype(BF16))
            if not first:
                alpha = jnp.exp2(m_ref[h] - m_new)
                l_new, acc_new = alpha * l_ref[h] + l_new, alpha * acc_ref[h] + acc_new
            m_ref[h], l_ref[h], acc_ref[h] = m_new, l_new, acc_new

        for h in range(B_HEADS):
            online(h, qb, True, True)

        def past(j, carry):
            for h in range(B_HEADS):
                online(h, j, False, False)
            return carry

        lax.fori_loop(0, qb, past, 0)

    yt = jnp.concatenate([acc_ref[h] / l_ref[h] for h in range(B_HEADS)], axis=0)
    yt = yt * lax.rsqrt(jnp.mean(yt * yt, axis=0, keepdims=True) + EPS)
    y_ref[0] = (yt.T * gob_ref[...]).astype(y_ref.dtype)


def _moba_call(bound, qt, k, vt, gob):
    b, s, _ = k.shape
    nb = s // B_BLOCK
    nbp = -(-nb // 8) * 8
    return pl.pallas_call(
        _moba_kernel,
        grid=(b, nb),
        in_specs=[
            pl.BlockSpec(memory_space=pltpu.SMEM),
            pl.BlockSpec((1, N_PAIRS, 1, HEAD_PAIR, B_BLOCK), lambda bi, qi: (bi, 0, qi, 0, 0)),
            pl.BlockSpec((1, s, D_B), lambda bi, qi: (bi, 0, 0)),
            pl.BlockSpec((1, N_PAIRS, nb, HEAD_PAIR, B_BLOCK), lambda bi, qi: (bi, 0, 0, 0, 0)),
            pl.BlockSpec((1, D_B), lambda bi, qi: (0, 0)),
        ],
        out_specs=pl.BlockSpec((1, B_BLOCK, D_B), lambda bi, qi: (bi, qi, 0)),
        out_shape=jax.ShapeDtypeStruct((b, s, D_B), BF16),
        scratch_shapes=[
            pltpu.VMEM((N_PAIRS, 3 * nbp, HEAD_PAIR), BF16),
            pltpu.VMEM((B_HEADS, HEAD_PAIR, B_BLOCK), BF16),
            pltpu.VMEM((B_HEADS, nbp, B_BLOCK), F32),
            pltpu.VMEM((B_HEADS, 1, B_BLOCK), F32),
            pltpu.VMEM((B_HEADS, 1, B_BLOCK), F32),
            pltpu.VMEM((B_HEADS, B_DH, B_BLOCK), F32),
            pltpu.VMEM((B_HEADS, B_BLOCK, B_BLOCK), F32),
            pltpu.VMEM((B_HEADS, B_BLOCK, B_BLOCK), F32),
        ],
        compiler_params=pltpu.CompilerParams(
            dimension_semantics=("parallel", "arbitrary"), vmem_limit_bytes=VMEM_LIMIT),
        name="moba",
    )(bound, qt, k, vt, gob)


def _outmix_kernel(x_ref, ya_ref, yb_ref, wout_ref, gffn_ref, wrt_hi_ref, wrt_lo_ref,
                   h_ref, hn_ref, route_ref, route_i_ref, counts_ref, cnt_ref):
    h = x_ref[...] + _dot(ya_ref[...], wout_ref[:D_A]) + _dot(yb_ref[...], wout_ref[D_A:])
    h_ref[...] = h
    hn = _rms(h, gffn_ref[...])
    _to_row_tiles(hn_ref, hn)
    tm = hn.shape[0]

    hn_hi, hn_lo = _split_bf16(hn)
    lg = _dot_nt(wrt_hi_ref[...], hn_hi) + (_dot_nt(wrt_hi_ref[...], hn_lo) + _dot_nt(wrt_lo_ref[...], hn_hi))
    row = lax.broadcasted_iota(jnp.int32, lg.shape, 0)

    def first_row(mask):
        return jnp.min(jnp.where(mask, row, ROUTER_ROWS), axis=0, keepdims=True)

    is_g = row < N_GROUPS
    g_max = jnp.max(jnp.where(is_g, lg, -jnp.inf), axis=0, keepdims=True)
    p_g = 1.0 / jnp.sum(jnp.where(is_g, jnp.exp(lg - g_max), 0.0), axis=0, keepdims=True)
    g_sel = first_row(is_g & (lg == g_max))
    e_row = row - N_GROUPS
    in_grp = (e_row >= 0) & (e_row < N_EXPERTS) & (lax.shift_right_arithmetic(e_row, 3) == g_sel)
    l1 = jnp.max(jnp.where(in_grp, lg, -jnp.inf), axis=0, keepdims=True)
    i1 = first_row(in_grp & (lg == l1))
    rest = in_grp & (row != i1)
    l2 = jnp.max(jnp.where(rest, lg, -jnp.inf), axis=0, keepdims=True)
    i2 = first_row(rest & (lg == l2))
    e2 = jnp.exp(l2 - l1)
    w1 = 1.0 / (1.0 + e2)
    w2 = e2 / (1.0 + e2)

    @pl.when(pl.program_id(0) == 0)
    def _():
        cnt_ref[...] = jnp.zeros_like(cnt_ref)

    picked = jnp.where((row == i1) | (row == i2), 1.0, 0.0)
    tok_r = lax.broadcasted_iota(jnp.int32, (tm, tm), 0)
    tok_c = lax.broadcasted_iota(jnp.int32, (tm, tm), 1)
    earlier = jnp.where(tok_r < tok_c, 1.0, 0.0).astype(BF16)
    before = _dot(picked.astype(BF16), earlier) + cnt_ref[:, 0:1]
    r1 = jnp.sum(jnp.where(row == i1, before, 0.0), axis=0, keepdims=True)
    r2 = jnp.sum(jnp.where(row == i2, before, 0.0), axis=0, keepdims=True)
    cnt_ref[...] += jnp.sum(picked, axis=1, keepdims=True)
    counts_ref[...] = cnt_ref[...]

    e1, e2x = i1 - N_GROUPS, i2 - N_GROUPS
    route_i_ref[...] = jnp.concatenate(
        [e1, e2x, r1.astype(jnp.int32), r2.astype(jnp.int32), jnp.zeros((ROUTE_FIELDS - 4, tm), jnp.int32)], axis=0)
    wts = jnp.concatenate([p_g * w1, p_g * w2, jnp.zeros((LANES - 2, tm), F32)], axis=0)
    route_ref[...] = wts.T


def _outmix_call(x2, ya2, yb2, wout, gffn, wrt_hi, wrt_lo):
    t, d = x2.shape
    tm = OUT_TM
    const = lambda *shape: pl.BlockSpec(shape, lambda ti: (0,) * len(shape))
    tok = lambda w: pl.BlockSpec((tm, w), lambda ti: (ti, 0))
    return pl.pallas_call(
        _outmix_kernel,
        grid=(t // tm,),
        in_specs=[tok(d), tok(D_A), tok(D_B), const(d, d), const(1, d),
                  const(ROUTER_ROWS, d), const(ROUTER_ROWS, d)],
        out_specs=[tok(d), pl.BlockSpec((tm * ROW_SUB, LANES), lambda ti: (ti, 0)), tok(LANES),
                   pl.BlockSpec((ROUTE_FIELDS, tm), lambda ti: (0, ti)), const(ROUTER_ROWS, LANES)],
        out_shape=[jax.ShapeDtypeStruct((t, d), F32), jax.ShapeDtypeStruct((t * ROW_SUB, LANES), F32),
                   jax.ShapeDtypeStruct((t, LANES), F32),
                   jax.ShapeDtypeStruct((ROUTE_FIELDS, t), jnp.int32),
                   jax.ShapeDtypeStruct((ROUTER_ROWS, LANES), F32)],
        scratch_shapes=[pltpu.VMEM((ROUTER_ROWS, LANES), F32)],
        compiler_params=pltpu.CompilerParams(
            dimension_semantics=("arbitrary",), vmem_limit_bytes=VMEM_LIMIT),
        name="outmix",
    )(x2, ya2, yb2, wout, gffn, wrt_hi, wrt_lo)


def _to_row_tiles(ref, x):
    n = x.shape[0]
    for s in range(ROW_SUB):
        ref[pl.ds(s, n, stride=ROW_SUB), :] = x[:, s * LANES:(s + 1) * LANES]


def _from_row_tiles(ref, first, n):
    return jnp.concatenate(
        [ref[pl.ds(first * ROW_SUB + s, n, stride=ROW_SUB), :] for s in range(ROW_SUB)], axis=1)


def _tiles_at(ref, row, n=1):
    return ref.at[pl.ds(pl.multiple_of(row * ROW_SUB, ROW_SUB), n * ROW_SUB), :]


def _wait_tiles(hbm, vmem_ref, n_tiles, sem):
    cap = vmem_ref.shape[0] // ROW_SUB
    while n_tiles > 0:
        n = min(n_tiles, cap)
        pltpu.make_async_copy(hbm.at[pl.ds(0, n * ROW_SUB), :], vmem_ref.at[pl.ds(0, n * ROW_SUB), :], sem).wait()
        n_tiles -= n


def _slots_kernel(seg_start_ref, route_i_ref, pos_ref):
    experts = route_i_ref[0:2, :]
    start = jnp.zeros_like(experts)
    for e in range(N_EXPERTS):
        start = jnp.where(experts == e, seg_start_ref[e], start)
    pos_ref[...] = jnp.zeros_like(pos_ref)
    pos_ref[0:2, :] = start + route_i_ref[2:4, :]


def _slots_call(seg_start, route_i):
    return pl.pallas_call(
        _slots_kernel,
        grid_spec=pltpu.PrefetchScalarGridSpec(
            num_scalar_prefetch=1,
            grid=(1,),
            in_specs=[pl.BlockSpec(route_i.shape, lambda i, s: (0, 0))],
            out_specs=pl.BlockSpec(route_i.shape, lambda i, s: (0, 0)),
        ),
        out_shape=jax.ShapeDtypeStruct(route_i.shape, jnp.int32),
        name="slots",
    )(seg_start, route_i)


def _dispatch_kernel(gap_start_ref, gap_len_ref, pos_ref, hn_ref, xs_hbm, zero_ref, sem, fill_sem):
    tm = hn_ref.shape[0] // ROW_SUB

    @pl.when(pl.program_id(0) == 0)
    def _():
        zero_ref[...] = jnp.zeros_like(zero_ref)
        fills = []
        for g in range(N_EXPERTS):
            row, n = gap_start_ref[g], gap_len_ref[g]
            size = EXP_TM // 2
            while size >= 1:
                fills.append(((n & size) != 0, row, size))
                row = row + (n & size)
                size //= 2
        for tile in range(N_EXPERTS):
            fills.append((tile * EXP_TM < gap_len_ref[N_EXPERTS], gap_start_ref[N_EXPERTS] + tile * EXP_TM, EXP_TM))
        copies = [(cond, pltpu.make_async_copy(_tiles_at(zero_ref, 0, size), _tiles_at(xs_hbm, row, size), fill_sem))
                  for cond, row, size in fills]
        for cond, copy in copies:
            pl.when(cond)(copy.start)
        for cond, copy in copies:
            pl.when(cond)(copy.wait)

    for r in range(2 * tm):
        k, tok = divmod(r, tm)
        pltpu.make_async_copy(_tiles_at(hn_ref, tok), _tiles_at(xs_hbm, pos_ref[k, tok]), sem).start(priority=r % 2)
    _wait_tiles(xs_hbm, hn_ref, 2 * tm, sem)


def _dispatch_call(gap_start, gap_len, pos, hn_tiles, n_sorted_rows):
    tm = TOK_TM
    return pl.pallas_call(
        _dispatch_kernel,
        grid_spec=pltpu.PrefetchScalarGridSpec(
            num_scalar_prefetch=2,
            grid=(pos.shape[1] // tm,),
            in_specs=[pl.BlockSpec((ROUTE_FIELDS, tm), lambda i, *_: (0, i), memory_space=pltpu.SMEM),
                      pl.BlockSpec((tm * ROW_SUB, LANES), lambda i, *_: (i, 0))],
            out_specs=pl.BlockSpec(memory_space=pl.ANY),
            scratch_shapes=[pltpu.VMEM((EXP_TM * ROW_SUB, LANES), F32), pltpu.SemaphoreType.DMA(()),
                            pltpu.SemaphoreType.DMA(())],
        ),
        out_shape=jax.ShapeDtypeStruct((n_sorted_rows * ROW_SUB, LANES), F32),
        compiler_params=pltpu.CompilerParams(
            dimension_semantics=("arbitrary",), vmem_limit_bytes=VMEM_LIMIT),
        name="dispatch",
    )(gap_start, gap_len, pos, hn_tiles)


def _experts_kernel(tile_e_ref, n_tiles_ref, xs_ref, wg_ref, wu_ref, wd_ref, y_ref, wg16, wu16, wd16):
    i = pl.program_id(0)
    tm = xs_ref.shape[0] // ROW_SUB

    @pl.when((i == 0) | (tile_e_ref[i] != tile_e_ref[jnp.maximum(i - 1, 0)]))
    def _():
        wg16[...] = wg_ref[0].astype(BF16)
        wu16[...] = wu_ref[0].astype(BF16)
        wd16[...] = wd_ref[0].astype(BF16)

    @pl.when(i < n_tiles_ref[0])
    def _():
        x = _from_row_tiles(xs_ref, 0, tm).astype(BF16)
        a = jax.nn.silu(_dot(x, wg16[...])) * _dot(x, wu16[...])
        _to_row_tiles(y_ref, _dot(a.astype(BF16), wd16[...]))

    @pl.when(i >= n_tiles_ref[0])
    def _():
        y_ref[...] = jnp.zeros_like(y_ref)


def _experts_call(tile_e, n_tiles, xs, wg, wu, wd):
    d = wg.shape[1]
    tm = EXP_TM
    nt = xs.shape[0] // (tm * ROW_SUB)
    used = lambda i, n: jnp.minimum(i, n[0] - 1)
    w_spec = lambda *shape: pl.BlockSpec((1,) + shape, lambda i, te, n: (te[used(i, n)], 0, 0))
    return pl.pallas_call(
        _experts_kernel,
        grid_spec=pltpu.PrefetchScalarGridSpec(
            num_scalar_prefetch=2,
            grid=(nt,),
            in_specs=[pl.BlockSpec((tm * ROW_SUB, LANES), lambda i, te, n: (used(i, n), 0)),
                      w_spec(d, D_FF_EXP), w_spec(d, D_FF_EXP), w_spec(D_FF_EXP, d)],
            out_specs=pl.BlockSpec((tm * ROW_SUB, LANES), lambda i, te, n: (i, 0)),
            scratch_shapes=[pltpu.VMEM((d, D_FF_EXP), BF16), pltpu.VMEM((d, D_FF_EXP), BF16),
                            pltpu.VMEM((D_FF_EXP, d), BF16)],
        ),
        out_shape=jax.ShapeDtypeStruct(xs.shape, F32),
        compiler_params=pltpu.CompilerParams(
            dimension_semantics=("arbitrary",), vmem_limit_bytes=VMEM_LIMIT),
        name="experts",
    )(tile_e, n_tiles, xs, wg, wu, wd)


def _ple_kernel(cur_ref, nxt_ref, ys_hbm, h_ref, route_ref, p_ref, wple_ref, gple_ref, wpg_ref,
                o_ref, ybuf0, ybuf1, sem):
    tm = h_ref.shape[0]
    i = pl.program_id(0)
    n_steps = pl.num_programs(0)
    bufs = (ybuf0, ybuf1)

    def start_gather(pos_ref, slot):
        for r in range(2 * tm):
            pltpu.make_async_copy(_tiles_at(ys_hbm, pos_ref[r // tm, r % tm]), _tiles_at(bufs[slot], r),
                                  sem.at[slot]).start(priority=r % 2)

    @pl.when(i == 0)
    def _():
        start_gather(cur_ref, 0)

    def step(slot):
        _wait_tiles(ys_hbm, bufs[slot], 2 * tm, sem.at[slot])
        start_gather(nxt_ref, 1 - slot)
        wts = route_ref[...]
        h = (h_ref[...] + wts[:, 0:1] * _from_row_tiles(bufs[slot], 0, tm)
             + wts[:, 1:2] * _from_row_tiles(bufs[slot], tm, tm))
        gate = jax.nn.sigmoid(_dot(_rms(h, gple_ref[...]).astype(BF16), wpg_ref[...]))
        o_ref[...] = h + _dot(p_ref[...].astype(BF16), wple_ref[...]) * gate

    for slot in range(2):
        pl.when(lax.rem(i, 2) == slot)(functools.partial(step, slot))

    for slot in range(2):
        pl.when((i == n_steps - 1) & (lax.rem(n_steps, 2) == slot))(
            functools.partial(_wait_tiles, ys_hbm, bufs[slot], 2 * tm, sem.at[slot]))


def _ple_call(pos, ys, h, route, p2, wple, gple, wpg):
    t, d = h.shape
    tm = TOK_TM
    n_steps = t // tm
    n_rows = 2 * tm
    idx_spec = lambda f: pl.BlockSpec((ROUTE_FIELDS, tm), lambda i: (0, f(i)), memory_space=pltpu.SMEM)
    const = lambda *shape: pl.BlockSpec(shape, lambda i: (0,) * len(shape))
    tok = lambda w: pl.BlockSpec((tm, w), lambda i: (i, 0))
    return pl.pallas_call(
        _ple_kernel,
        grid=(n_steps,),
        in_specs=[idx_spec(lambda i: i), idx_spec(lambda i: lax.rem(i + 1, n_steps)),
                  pl.BlockSpec(memory_space=pl.ANY),
                  tok(d), tok(LANES), tok(D_PLE), const(D_PLE, d), const(1, d), const(d, d)],
        out_specs=tok(d),
        out_shape=jax.ShapeDtypeStruct((t, d), F32),
        scratch_shapes=[pltpu.VMEM((n_rows * ROW_SUB, LANES), F32), pltpu.VMEM((n_rows * ROW_SUB, LANES), F32),
                        pltpu.SemaphoreType.DMA((2,))],
        compiler_params=pltpu.CompilerParams(
            dimension_semantics=("arbitrary",), vmem_limit_bytes=VMEM_LIMIT),
        name="ple",
    )(pos, pos, ys, h, route, p2, wple, gple, wpg)


def _segment_tables(counts, t):
    tm = EXP_TM
    nt = 2 * t // tm + N_EXPERTS
    padded = ((counts + tm - 1) // tm) * tm
    seg_end = jnp.cumsum(padded)
    seg_start = seg_end - padded
    gap_len = jnp.concatenate([padded - counts, nt * tm - seg_end[-1:]])
    gap_start = jnp.concatenate([seg_start + counts, seg_end[-1:]])
    tile_start = jnp.arange(nt, dtype=jnp.int32) * tm
    tile_e = jnp.minimum(jnp.sum(tile_start[:, None] >= seg_end[None, :], axis=1), N_EXPERTS - 1)
    n_tiles = (seg_end[-1] // tm).reshape(1)
    i32 = lambda a: a.astype(jnp.int32)
    return i32(seg_start), i32(gap_start), i32(gap_len), i32(tile_e), i32(n_tiles), nt * tm


def _layer(h, p_i, g_mix, w_in, g_v_a, w_s, b_s, g_q, g_k, g_out_a, g_out_b, w_out, g_ffn, w_group,
           w_expert, w_gate_e, w_up_e, w_down_e, g_ple, w_ple, w_ple_gate):
    b, s, d = h.shape
    t = b * s
    row = lambda g: g.reshape(1, -1).astype(F32)
    wuv = w_in[:, :2 * D_A].astype(BF16)
    wqt = w_in[:, 2 * D_A:2 * D_A + D_B].T.astype(BF16)
    wk = w_in[:, 2 * D_A + D_B:2 * D_A + 2 * D_B].astype(BF16)
    wvt = w_in[:, 2 * D_A + 2 * D_B:].T.astype(BF16)
    bs = jnp.broadcast_to(b_s[:, :, None], (A_GROUPS, A_CHUNK, A_DH)).astype(F32)
    gk = row(jnp.tile(g_k, B_HEADS))
    gq = (jnp.tile(g_q, B_HEADS) * (1.0 / float(B_DH) ** 0.5)).reshape(D_B, 1).astype(F32)
    head = jnp.arange(D_B) // B_DH
    bd = jnp.where(head[:, None] == head[None, :], 1.0 / B_DH, 0.0).astype(BF16)
    w_rt = jnp.zeros((ROUTER_ROWS, d), F32)
    w_rt = w_rt.at[:N_GROUPS].set(w_group.T).at[N_GROUPS:N_GROUPS + N_EXPERTS].set(w_expert.T)
    wrt_hi, wrt_lo = _split_bf16(w_rt)

    ya, k, qt, vt = _proj_call(h, row(g_mix), wuv, wk, wqt, wvt, row(g_v_a), w_s.astype(F32), bs,
                               row(g_out_a), gk, gq, bd)
    bound = (BOUND_MARGIN * LOG2E * float(B_DH) ** 0.5) * jnp.max(jnp.abs(g_q)) * jnp.max(jnp.abs(g_k))
    yb = _moba_call(bound.reshape(1, 1).astype(F32), qt, k, vt, row(g_out_b))
    h1, hn_tiles, route, route_i, counts = _outmix_call(
        h.reshape(t, d), ya.reshape(t, D_A), yb.reshape(t, D_B), w_out.astype(BF16), row(g_ffn), wrt_hi, wrt_lo)
    counts = counts[N_GROUPS:N_GROUPS + N_EXPERTS, 0].astype(jnp.int32)
    seg_start, gap_start, gap_len, tile_e, n_tiles, n_sorted_rows = _segment_tables(counts, t)
    pos = _slots_call(seg_start, route_i)
    xs = _dispatch_call(gap_start, gap_len, pos, hn_tiles, n_sorted_rows)
    ys = _experts_call(tile_e, n_tiles, xs, w_gate_e, w_up_e, w_down_e)
    out = _ple_call(pos, ys, h1, route, p_i.reshape(t, D_PLE), w_ple.astype(BF16), row(g_ple),
                    w_ple_gate.astype(BF16))
    return out.reshape(b, s, d)


def kernel(x, p, g_mix, w_in, g_v_a, w_s, b_s, g_q, g_k, g_out_a, g_out_b, w_out, g_ffn, w_group, w_expert,
           w_gate_e, w_up_e, w_down_e, g_ple, w_ple, w_ple_gate):
    params = (g_mix, w_in, g_v_a, w_s, b_s, g_q, g_k, g_out_a, g_out_b, w_out, g_ffn, w_group, w_expert,
              w_gate_e, w_up_e, w_down_e, g_ple, w_ple, w_ple_gate)
    h = x
    for i in range(p.shape[0]):
        h = _layer(h, p[i], *(w[i] for w in params))
    return h
```

```python
import functools

import jax
import jax.numpy as jnp
from jax import lax
from jax.experimental import pallas as pl
from jax.experimental.pallas import tpu as pltpu

F32 = jnp.float32
BF16 = jnp.bfloat16

D_MODEL = 1024
D_A = 512
A_GROUPS = 4
A_DH = D_A // A_GROUPS
A_CHUNK = 128
D_B = 512
B_HEADS = 8
B_DH = D_B // B_HEADS
B_BLOCK = 256
B_TOPK = 3
D_PLE = 256
N_GROUPS = 4
EXP_PER_GROUP = 8
N_EXPERTS = N_GROUPS * EXP_PER_GROUP
D_FF_EXP = 256
EPS = 1e-6
NEG = -1e30
LOG2E = 1.4426950408889634

LANES = 128
HEAD_PAIR = 2 * B_DH
N_PAIRS = B_HEADS // 2
ROUTER_ROWS = 48
ROUTE_FIELDS = 8
VMEM_LIMIT = 56 * 1024 * 1024

PROJ_TM = 512
OUT_TM = 512
EXP_TM = 512
MOBA_STEPS_PER_TRIP = 4
BOUND_MARGIN = 1.02
MIN_SHIFTED_SUM = 2.0 ** -64
TOK_TM = 256
ROW_SUB = D_MODEL // LANES


def _rms(x, g):
    return x * lax.rsqrt(jnp.mean(x * x, axis=-1, keepdims=True) + EPS) * g


def _dot(a, b):
    return jnp.dot(a, b, preferred_element_type=F32)


def _dot_nt(a, b):
    return lax.dot_general(a, b, (((1,), (1,)), ((), ())), preferred_element_type=F32)


def _split_bf16(x):
    hi = x.astype(BF16)
    lo = (x - hi.astype(F32)).astype(BF16)
    return hi, lo


def _proj_kernel(x_ref, gmix_ref, wuv_ref, wk_ref, wqt_ref, wvt_ref, gva_ref, ws_ref, bs_ref,
                 goa_ref, gk_ref, gq_ref, bd_ref,
                 ya_ref, k_ref, qt_ref, vt_ref, prod_ref):
    tm = x_ref.shape[1]
    n_chunks = tm // A_CHUNK
    xn = _rms(x_ref[0], gmix_ref[...]).astype(BF16)

    uv = _dot(xn, wuv_ref[...])
    u = jax.nn.gelu(uv[:, :D_A])
    v = jax.nn.gelu(uv[:, D_A:])
    row = lax.broadcasted_iota(jnp.int32, (A_CHUNK, A_CHUNK), 0)
    col = lax.broadcasted_iota(jnp.int32, (A_CHUNK, A_CHUNK), 1)
    causal = col <= row
    for g in range(A_GROUPS):
        gs = slice(g * A_DH, (g + 1) * A_DH)
        vn = _rms(v[:, gs], gva_ref[:, gs]).astype(BF16)
        rhs = jnp.concatenate([vn[c * A_CHUNK:(c + 1) * A_CHUNK] for c in range(n_chunks)], axis=1)
        ws = jnp.where(causal, ws_ref[g], 0.0).astype(BF16)
        mixed = _dot(ws, rhs)
        for c in range(n_chunks):
            cs = slice(c * A_CHUNK, (c + 1) * A_CHUNK)
            prod_ref[cs, gs] = u[cs, gs] * (mixed[:, cs] + bs_ref[g])
    ya_ref[0] = _rms(prod_ref[...], goa_ref[...]).astype(ya_ref.dtype)

    k = _dot(xn, wk_ref[...])
    k2_hi, k2_lo = _split_bf16(k * k)
    k_ms = _dot(k2_hi, bd_ref[...]) + _dot(k2_lo, bd_ref[...])
    k_ref[0] = (k * lax.rsqrt(k_ms + EPS) * gk_ref[...]).astype(k_ref.dtype)

    qt = _dot_nt(wqt_ref[...], xn)
    vt = _dot_nt(wvt_ref[...], xn).astype(vt_ref.dtype)
    qn = []
    for h in range(B_HEADS):
        qh = qt[h * B_DH:(h + 1) * B_DH]
        qn.append(qh * lax.rsqrt(jnp.mean(qh * qh, axis=0, keepdims=True) + EPS))
    qn = (jnp.concatenate(qn, axis=0) * gq_ref[...]).astype(qt_ref.dtype)
    for p in range(N_PAIRS):
        ps = slice(p * HEAD_PAIR, (p + 1) * HEAD_PAIR)
        for j in range(tm // B_BLOCK):
            js = slice(j * B_BLOCK, (j + 1) * B_BLOCK)
            qt_ref[0, p, j] = qn[ps, js]
            vt_ref[0, p, j] = vt[ps, js]


def _proj_call(x, gmix, wuv, wk, wqt, wvt, gva, ws, bs, goa, gk, gq, bd):
    b, s, d = x.shape
    tm = PROJ_TM
    nb = s // B_BLOCK
    jb = tm // B_BLOCK
    const = lambda *shape: pl.BlockSpec(shape, lambda bi, ti: (0,) * len(shape))
    return pl.pallas_call(
        _proj_kernel,
        grid=(b, s // tm),
        in_specs=[
            pl.BlockSpec((1, tm, d), lambda bi, ti: (bi, ti, 0)),
            const(1, d), const(d, 2 * D_A), const(d, D_B), const(D_B, d), const(D_B, d),
            const(1, D_A), const(A_GROUPS, A_CHUNK, A_CHUNK), const(A_GROUPS, A_CHUNK, A_DH),
            const(1, D_A), const(1, D_B), const(D_B, 1), const(D_B, D_B),
        ],
        out_specs=[
            pl.BlockSpec((1, tm, D_A), lambda bi, ti: (bi, ti, 0)),
            pl.BlockSpec((1, tm, D_B), lambda bi, ti: (bi, ti, 0)),
            pl.BlockSpec((1, N_PAIRS, jb, HEAD_PAIR, B_BLOCK), lambda bi, ti: (bi, 0, ti, 0, 0)),
            pl.BlockSpec((1, N_PAIRS, jb, HEAD_PAIR, B_BLOCK), lambda bi, ti: (bi, 0, ti, 0, 0)),
        ],
        out_shape=[
            jax.ShapeDtypeStruct((b, s, D_A), BF16),
            jax.ShapeDtypeStruct((b, s, D_B), BF16),
            jax.ShapeDtypeStruct((b, N_PAIRS, nb, HEAD_PAIR, B_BLOCK), BF16),
            jax.ShapeDtypeStruct((b, N_PAIRS, nb, HEAD_PAIR, B_BLOCK), BF16),
        ],
        scratch_shapes=[pltpu.VMEM((tm, D_A), F32)],
        compiler_params=pltpu.CompilerParams(
            dimension_semantics=("parallel", "parallel"), vmem_limit_bytes=VMEM_LIMIT),
        name="proj",
    )(x, gmix, wuv, wk, wqt, wvt, gva, ws, bs, goa, gk, gq, bd)


def _moba_kernel(bound_ref, qt_ref, k_ref, vt_ref, gob_ref, y_ref, kmean_ref, qz_ref, sel_ref, m_ref, l_ref,
                 acc_ref, stage0_ref, stage1_ref):
    qb = pl.program_id(1)
    nb = k_ref.shape[1] // B_BLOCK
    nbp = sel_ref.shape[1]

    @pl.when(qb == 0)
    def _():
        rows = [jnp.mean(k_ref[0, j * B_BLOCK:(j + 1) * B_BLOCK, :].astype(F32), axis=0, keepdims=True)
                for j in range(nb)]
        rows += [jnp.zeros((1, D_B), F32)] * (nbp - nb)
        km = jnp.concatenate(rows, axis=0)
        hi = km.astype(BF16)
        rest = km - hi.astype(F32)
        mid = rest.astype(BF16)
        lo = (rest - mid.astype(F32)).astype(BF16)
        for p in range(N_PAIRS):
            ps = slice(p * HEAD_PAIR, (p + 1) * HEAD_PAIR)
            kmean_ref[p] = jnp.concatenate([hi[:, ps], mid[:, ps], lo[:, ps]], axis=0)

    head_row = lax.broadcasted_iota(jnp.int32, (HEAD_PAIR, B_BLOCK), 0) // B_DH
    blk = lax.broadcasted_iota(jnp.int32, (nbp, B_BLOCK), 0)
    past = blk < qb

    for h in range(B_HEADS):
        p, hh = divmod(h, 2)
        q_pair = qt_ref[0, p, 0]
        qz = jnp.where(head_row == hh, q_pair, jnp.zeros_like(q_pair))
        qz_ref[h] = qz
        g3 = _dot(kmean_ref[p], qz)
        gate = (g3[:nbp] + g3[nbp:2 * nbp] + g3[2 * nbp:]) * float(B_DH) ** 0.5
        gate = jnp.where(past, gate, NEG)
        left = gate
        taken = blk < 0
        for _ in range(min(B_TOPK, nb)):
            top = jnp.max(left, axis=0, keepdims=True)
            first = jnp.min(jnp.where(left == top, blk, nbp), axis=0, keepdims=True)
            taken = taken | (blk == first)
            left = jnp.where(blk == first, -jnp.inf, left)
        chosen = taken & (gate > NEG * 0.5)
        sel_ref[h] = jnp.where(chosen, -bound_ref[0, 0], NEG)

    def scores(h, j):
        p = h // 2
        kj = k_ref[0, pl.ds(pl.multiple_of(j * B_BLOCK, B_BLOCK), B_BLOCK), p * HEAD_PAIR:(p + 1) * HEAD_PAIR]
        return _dot(kj, qz_ref[h])

    def values(h, j):
        p, hh = divmod(h, 2)
        return vt_ref[0, p, j, hh * B_DH:(hh + 1) * B_DH, :]

    key_pos = lax.broadcasted_iota(jnp.int32, (B_BLOCK, B_BLOCK), 0)
    q_pos = lax.broadcasted_iota(jnp.int32, (B_BLOCK, B_BLOCK), 1)
    causal = key_pos <= q_pos

    bound = bound_ref[0, 0]
    stage = (stage0_ref, stage1_ref)
    own_shift = jnp.where(causal, -bound, NEG)

    def stage_scores(h, block, shift, parity):
        stage[parity][h] = scores(h, block) * LOG2E + shift

    def consume(h, s, parity):
        pr = jnp.exp2(stage[parity][h])
        l_ref[h] += jnp.sum(pr, axis=0, keepdims=True)
        acc_ref[h] += _dot(values(h, jnp.where(s == 0, qb, s - 1)), pr.astype(BF16))

    def full_step(s, parity):
        for h in range(B_HEADS):
            stage_scores(h, s - 1, sel_ref[h, pl.ds(s - 1, 1), :], parity)
            consume(h, s - 1, 1 - parity)

    for h in range(B_HEADS):
        stage_scores(h, qb, own_shift, 0)
        l_ref[h] = jnp.zeros((1, B_BLOCK), F32)
        acc_ref[h] = jnp.zeros((B_DH, B_BLOCK), F32)

    def trip(u, carry):
        for v in range(MOBA_STEPS_PER_TRIP):
            full_step(MOBA_STEPS_PER_TRIP * u + v + 1, (v + 1) % 2)
        return carry

    lax.fori_loop(0, qb // MOBA_STEPS_PER_TRIP, trip, 0)
    done = (qb // MOBA_STEPS_PER_TRIP) * MOBA_STEPS_PER_TRIP
    for v in range(MOBA_STEPS_PER_TRIP - 1):
        pl.when(qb - done > v)(functools.partial(full_step, done + v + 1, (v + 1) % 2))

    for parity in range(2):
        @pl.when(lax.rem(qb, 2) == parity)
        def _(parity=parity):
            for h in range(B_HEADS):
                consume(h, qb, parity)

    smallest = jnp.min(jnp.concatenate([l_ref[h] for h in range(B_HEADS)], axis=0))

    @pl.when(jnp.logical_not(smallest > MIN_SHIFTED_SUM))
    def _():
        def online(h, j, own, first):
            visible = causal if own else sel_ref[h, pl.ds(j, 1), :] > NEG * 0.5
            sc = jnp.where(visible, scores(h, j) * LOG2E, NEG)
            m_tile = jnp.max(sc, axis=0, keepdims=True)
            m_new = m_tile if first else jnp.maximum(m_ref[h], m_tile)
            pr = jnp.exp2(sc - m_new)
            l_new, acc_new = jnp.sum(pr, axis=0, keepdims=True), _dot(values(h, j), pr.astype(BF16))
            if not first:
                alpha = jnp.exp2(m_ref[h] - m_new)
                l_new, acc_new = alpha * l_ref[h] + l_new, alpha * acc_ref[h] + acc_new
            m_ref[h], l_ref[h], acc_ref[h] = m_new, l_new, acc_new

        for h in range(B_HEADS):
            online(h, qb, True, True)

        def past(j, carry):
            for h in range(B_HEADS):
                online(h, j, False, False)
            return carry

        lax.fori_loop(0, qb, past, 0)

    yt = jnp.concatenate([acc_ref[h] / l_ref[h] for h in range(B_HEADS)], axis=0)
    yt = yt * lax.rsqrt(jnp.mean(yt * yt, axis=0, keepdims=True) + EPS)
    y_ref[0] = (yt.T * gob_ref[...]).astype(y_ref.dtype)


def _moba_call(bound, qt, k, vt, gob):
    b, s, _ = k.shape
    nb = s // B_BLOCK
    nbp = -(-nb // 8) * 8
    return pl.pallas_call(
        _moba_kernel,
        grid=(b, nb),
        in_specs=[
            pl.BlockSpec(memory_space=pltpu.SMEM),
            pl.BlockSpec((1, N_PAIRS, 1, HEAD_PAIR, B_BLOCK), lambda bi, qi: (bi, 0, qi, 0, 0)),
            pl.BlockSpec((1, s, D_B), lambda bi, qi: (bi, 0, 0)),
            pl.BlockSpec((1, N_PAIRS, nb, HEAD_PAIR, B_BLOCK), lambda bi, qi: (bi, 0, 0, 0, 0)),
            pl.BlockSpec((1, D_B), lambda bi, qi: (0, 0)),
        ],
        out_specs=pl.BlockSpec((1, B_BLOCK, D_B), lambda bi, qi: (bi, qi, 0)),
        out_shape=jax.ShapeDtypeStruct((b, s, D_B), BF16),
        scratch_shapes=[
            pltpu.VMEM((N_PAIRS, 3 * nbp, HEAD_PAIR), BF16),
            pltpu.VMEM((B_HEADS, HEAD_PAIR, B_BLOCK), BF16),
            pltpu.VMEM((B_HEADS, nbp, B_BLOCK), F32),
            pltpu.VMEM((B_HEADS, 1, B_BLOCK), F32),
            pltpu.VMEM((B_HEADS, 1, B_BLOCK), F32),
            pltpu.VMEM((B_HEADS, B_DH, B_BLOCK), F32),
            pltpu.VMEM((B_HEADS, B_BLOCK, B_BLOCK), F32),
            pltpu.VMEM((B_HEADS, B_BLOCK, B_BLOCK), F32),
        ],
        compiler_params=pltpu.CompilerParams(
            dimension_semantics=("parallel", "arbitrary"), vmem_limit_bytes=VMEM_LIMIT),
        name="moba",
    )(bound, qt, k, vt, gob)


def _outmix_kernel(x_ref, ya_ref, yb_ref, wout_ref, gffn_ref, wrt_hi_ref, wrt_lo_ref,
                   h_ref, hn_ref, route_ref, route_i_ref, counts_ref, cnt_ref):
    h = x_ref[...] + _dot(ya_ref[...], wout_ref[:D_A]) + _dot(yb_ref[...], wout_ref[D_A:])
    h_ref[...] = h
    hn = _rms(h, gffn_ref[...])
    _to_row_tiles(hn_ref, hn)
    tm = hn.shape[0]

    hn_hi, hn_lo = _split_bf16(hn)
    lg = _dot_nt(wrt_hi_ref[...], hn_hi) + (_dot_nt(wrt_hi_ref[...], hn_lo) + _dot_nt(wrt_lo_ref[...], hn_hi))
    row = lax.broadcasted_iota(jnp.int32, lg.shape, 0)

    def first_row(mask):
        return jnp.min(jnp.where(mask, row, ROUTER_ROWS), axis=0, keepdims=True)

    is_g = row < N_GROUPS
    g_max = jnp.max(jnp.where(is_g, lg, -jnp.inf), axis=0, keepdims=True)
    p_g = 1.0 / jnp.sum(jnp.where(is_g, jnp.exp(lg - g_max), 0.0), axis=0, keepdims=True)
    g_sel = first_row(is_g & (lg == g_max))
    e_row = row - N_GROUPS
    in_grp = (e_row >= 0) & (e_row < N_EXPERTS) & (lax.shift_right_arithmetic(e_row, 3) == g_sel)
    l1 = jnp.max(jnp.where(in_grp, lg, -jnp.inf), axis=0, keepdims=True)
    i1 = first_row(in_grp & (lg == l1))
    rest = in_grp & (row != i1)
    l2 = jnp.max(jnp.where(rest, lg, -jnp.inf), axis=0, keepdims=True)
    i2 = first_row(rest & (lg == l2))
    e2 = jnp.exp(l2 - l1)
    w1 = 1.0 / (1.0 + e2)
    w2 = e2 / (1.0 + e2)

    @pl.when(pl.program_id(0) == 0)
    def _():
        cnt_ref[...] = jnp.zeros_like(cnt_ref)

    picked = jnp.where((row == i1) | (row == i2), 1.0, 0.0)
    tok_r = lax.broadcasted_iota(jnp.int32, (tm, tm), 0)
    tok_c = lax.broadcasted_iota(jnp.int32, (tm, tm), 1)
    earlier = jnp.where(tok_r < tok_c, 1.0, 0.0).astype(BF16)
    before = _dot(picked.astype(BF16), earlier) + cnt_ref[:, 0:1]
    r1 = jnp.sum(jnp.where(row == i1, before, 0.0), axis=0, keepdims=True)
    r2 = jnp.sum(jnp.where(row == i2, before, 0.0), axis=0, keepdims=True)
    cnt_ref[...] += jnp.sum(picked, axis=1, keepdims=True)
    counts_ref[...] = cnt_ref[...]

    e1, e2x = i1 - N_GROUPS, i2 - N_GROUPS
    route_i_ref[...] = jnp.concatenate(
        [e1, e2x, r1.astype(jnp.int32), r2.astype(jnp.int32), jnp.zeros((ROUTE_FIELDS - 4, tm), jnp.int32)], axis=0)
    wts = jnp.concatenate([p_g * w1, p_g * w2, jnp.zeros((LANES - 2, tm), F32)], axis=0)
    route_ref[...] = wts.T


def _outmix_call(x2, ya2, yb2, wout, gffn, wrt_hi, wrt_lo):
    t, d = x2.shape
    tm = OUT_TM
    const = lambda *shape: pl.BlockSpec(shape, lambda ti: (0,) * len(shape))
    tok = lambda w: pl.BlockSpec((tm, w), lambda ti: (ti, 0))
    return pl.pallas_call(
        _outmix_kernel,
        grid=(t // tm,),
        in_specs=[tok(d), tok(D_A), tok(D_B), const(d, d), const(1, d),
                  const(ROUTER_ROWS, d), const(ROUTER_ROWS, d)],
        out_specs=[tok(d), pl.BlockSpec((tm * ROW_SUB, LANES), lambda ti: (ti, 0)), tok(LANES),
                   pl.BlockSpec((ROUTE_FIELDS, tm), lambda ti: (0, ti)), const(ROUTER_ROWS, LANES)],
        out_shape=[jax.ShapeDtypeStruct((t, d), F32), jax.ShapeDtypeStruct((t * ROW_SUB, LANES), F32),
                   jax.ShapeDtypeStruct((t, LANES), F32),
                   jax.ShapeDtypeStruct((ROUTE_FIELDS, t), jnp.int32),
                   jax.ShapeDtypeStruct((ROUTER_ROWS, LANES), F32)],
        scratch_shapes=[pltpu.VMEM((ROUTER_ROWS, LANES), F32)],
        compiler_params=pltpu.CompilerParams(
            dimension_semantics=("arbitrary",), vmem_limit_bytes=VMEM_LIMIT),
        name="outmix",
    )(x2, ya2, yb2, wout, gffn, wrt_hi, wrt_lo)


def _to_row_tiles(ref, x):
    n = x.shape[0]
    for s in range(ROW_SUB):
        ref[pl.ds(s, n, stride=ROW_SUB), :] = x[:, s * LANES:(s + 1) * LANES]


def _from_row_tiles(ref, first, n):
    return jnp.concatenate(
        [ref[pl.ds(first * ROW_SUB + s, n, stride=ROW_SUB), :] for s in range(ROW_SUB)], axis=1)


def _tiles_at(ref, row, n=1):
    return ref.at[pl.ds(pl.multiple_of(row * ROW_SUB, ROW_SUB), n * ROW_SUB), :]


def _wait_tiles(hbm, vmem_ref, n_tiles, sem):
    cap = vmem_ref.shape[0] // ROW_SUB
    while n_tiles > 0:
        n = min(n_tiles, cap)
        pltpu.make_async_copy(hbm.at[pl.ds(0, n * ROW_SUB), :], vmem_ref.at[pl.ds(0, n * ROW_SUB), :], sem).wait()
        n_tiles -= n


def _slots_kernel(seg_start_ref, route_i_ref, pos_ref):
    experts = route_i_ref[0:2, :]
    start = jnp.zeros_like(experts)
    for e in range(N_EXPERTS):
        start = jnp.where(experts == e, seg_start_ref[e], start)
    pos_ref[...] = jnp.zeros_like(pos_ref)
    pos_ref[0:2, :] = start + route_i_ref[2:4, :]


def _slots_call(seg_start, route_i):
    return pl.pallas_call(
        _slots_kernel,
        grid_spec=pltpu.PrefetchScalarGridSpec(
            num_scalar_prefetch=1,
            grid=(1,),
            in_specs=[pl.BlockSpec(route_i.shape, lambda i, s: (0, 0))],
            out_specs=pl.BlockSpec(route_i.shape, lambda i, s: (0, 0)),
        ),
        out_shape=jax.ShapeDtypeStruct(route_i.shape, jnp.int32),
        name="slots",
    )(seg_start, route_i)


def _dispatch_kernel(gap_start_ref, gap_len_ref, pos_ref, hn_ref, xs_hbm, zero_ref, sem, fill_sem):
    tm = hn_ref.shape[0] // ROW_SUB

    @pl.when(pl.program_id(0) == 0)
    def _():
        zero_ref[...] = jnp.zeros_like(zero_ref)
        fills = []
        for g in range(N_EXPERTS):
            row, n = gap_start_ref[g], gap_len_ref[g]
            size = EXP_TM // 2
            while size >= 1:
                fills.append(((n & size) != 0, row, size))
                row = row + (n & size)
                size //= 2
        for tile in range(N_EXPERTS):
            fills.append((tile * EXP_TM < gap_len_ref[N_EXPERTS], gap_start_ref[N_EXPERTS] + tile * EXP_TM, EXP_TM))
        copies = [(cond, pltpu.make_async_copy(_tiles_at(zero_ref, 0, size), _tiles_at(xs_hbm, row, size), fill_sem))
                  for cond, row, size in fills]
        for cond, copy in copies:
            pl.when(cond)(copy.start)
        for cond, copy in copies:
            pl.when(cond)(copy.wait)

    for r in range(2 * tm):
        k, tok = divmod(r, tm)
        pltpu.make_async_copy(_tiles_at(hn_ref, tok), _tiles_at(xs_hbm, pos_ref[k, tok]), sem).start(priority=r % 2)
    _wait_tiles(xs_hbm, hn_ref, 2 * tm, sem)


def _dispatch_call(gap_start, gap_len, pos, hn_tiles, n_sorted_rows):
    tm = TOK_TM
    return pl.pallas_call(
        _dispatch_kernel,
        grid_spec=pltpu.PrefetchScalarGridSpec(
            num_scalar_prefetch=2,
            grid=(pos.shape[1] // tm,),
            in_specs=[pl.BlockSpec((ROUTE_FIELDS, tm), lambda i, *_: (0, i), memory_space=pltpu.SMEM),
                      pl.BlockSpec((tm * ROW_SUB, LANES), lambda i, *_: (i, 0))],
            out_specs=pl.BlockSpec(memory_space=pl.ANY),
            scratch_shapes=[pltpu.VMEM((EXP_TM * ROW_SUB, LANES), F32), pltpu.SemaphoreType.DMA(()),
                            pltpu.SemaphoreType.DMA(())],
        ),
        out_shape=jax.ShapeDtypeStruct((n_sorted_rows * ROW_SUB, LANES), F32),
        compiler_params=pltpu.CompilerParams(
            dimension_semantics=("arbitrary",), vmem_limit_bytes=VMEM_LIMIT),
        name="dispatch",
    )(gap_start, gap_len, pos, hn_tiles)


def _experts_kernel(tile_e_ref, n_tiles_ref, xs_ref, wg_ref, wu_ref, wd_ref, y_ref, wg16, wu16, wd16):
    i = pl.program_id(0)
    tm = xs_ref.shape[0] // ROW_SUB

    @pl.when((i == 0) | (tile_e_ref[i] != tile_e_ref[jnp.maximum(i - 1, 0)]))
    def _():
        wg16[...] = wg_ref[0].astype(BF16)
        wu16[...] = wu_ref[0].astype(BF16)
        wd16[...] = wd_ref[0].astype(BF16)

    @pl.when(i < n_tiles_ref[0])
    def _():
        x = _from_row_tiles(xs_ref, 0, tm).astype(BF16)
        a = jax.nn.silu(_dot(x, wg16[...])) * _dot(x, wu16[...])
        _to_row_tiles(y_ref, _dot(a.astype(BF16), wd16[...]))

    @pl.when(i >= n_tiles_ref[0])
    def _():
        y_ref[...] = jnp.zeros_like(y_ref)


def _experts_call(tile_e, n_tiles, xs, wg, wu, wd):
    d = wg.shape[1]
    tm = EXP_TM
    nt = xs.shape[0] // (tm * ROW_SUB)
    used = lambda i, n: jnp.minimum(i, n[0] - 1)
    w_spec = lambda *shape: pl.BlockSpec((1,) + shape, lambda i, te, n: (te[used(i, n)], 0, 0))
    return pl.pallas_call(
        _experts_kernel,
        grid_spec=pltpu.PrefetchScalarGridSpec(
            num_scalar_prefetch=2,
            grid=(nt,),
            in_specs=[pl.BlockSpec((tm * ROW_SUB, LANES), lambda i, te, n: (used(i, n), 0)),
                      w_spec(d, D_FF_EXP), w_spec(d, D_FF_EXP), w_spec(D_FF_EXP, d)],
            out_specs=pl.BlockSpec((tm * ROW_SUB, LANES), lambda i, te, n: (i, 0)),
            scratch_shapes=[pltpu.VMEM((d, D_FF_EXP), BF16), pltpu.VMEM((d, D_FF_EXP), BF16),
                            pltpu.VMEM((D_FF_EXP, d), BF16)],
        ),
        out_shape=jax.ShapeDtypeStruct(xs.shape, F32),
        compiler_params=pltpu.CompilerParams(
            dimension_semantics=("arbitrary",), vmem_limit_bytes=VMEM_LIMIT),
        name="experts",
    )(tile_e, n_tiles, xs, wg, wu, wd)


def _ple_kernel(cur_ref, nxt_ref, ys_hbm, h_ref, route_ref, p_ref, wple_ref, gple_ref, wpg_ref,
                o_ref, ybuf0, ybuf1, sem):
    tm = h_ref.shape[0]
    i = pl.program_id(0)
    n_steps = pl.num_programs(0)
    bufs = (ybuf0, ybuf1)

    def start_gather(pos_ref, slot):
        for r in range(2 * tm):
            pltpu.make_async_copy(_tiles_at(ys_hbm, pos_ref[r // tm, r % tm]), _tiles_at(bufs[slot], r),
                                  sem.at[slot]).start(priority=r % 2)

    @pl.when(i == 0)
    def _():
        start_gather(cur_ref, 0)

    def step(slot):
        _wait_tiles(ys_hbm, bufs[slot], 2 * tm, sem.at[slot])
        start_gather(nxt_ref, 1 - slot)
        wts = route_ref[...]
        h = (h_ref[...] + wts[:, 0:1] * _from_row_tiles(bufs[slot], 0, tm)
             + wts[:, 1:2] * _from_row_tiles(bufs[slot], tm, tm))
        gate = jax.nn.sigmoid(_dot(_rms(h, gple_ref[...]).astype(BF16), wpg_ref[...]))
        o_ref[...] = h + _dot(p_ref[...].astype(BF16), wple_ref[...]) * gate

    for slot in range(2):
        pl.when(lax.rem(i, 2) == slot)(functools.partial(step, slot))

    for slot in range(2):
        pl.when((i == n_steps - 1) & (lax.rem(n_steps, 2) == slot))(
            functools.partial(_wait_tiles, ys_hbm, bufs[slot], 2 * tm, sem.at[slot]))


def _ple_call(pos, ys, h, route, p2, wple, gple, wpg):
    t, d = h.shape
    tm = TOK_TM
    n_steps = t // tm
    n_rows = 2 * tm
    idx_spec = lambda f: pl.BlockSpec((ROUTE_FIELDS, tm), lambda i: (0, f(i)), memory_space=pltpu.SMEM)
    const = lambda *shape: pl.BlockSpec(shape, lambda i: (0,) * len(shape))
    tok = lambda w: pl.BlockSpec((tm, w), lambda i: (i, 0))
    return pl.pallas_call(
        _ple_kernel,
        grid=(n_steps,),
        in_specs=[idx_spec(lambda i: i), idx_spec(lambda i: lax.rem(i + 1, n_steps)),
                  pl.BlockSpec(memory_space=pl.ANY),
                  tok(d), tok(LANES), tok(D_PLE), const(D_PLE, d), const(1, d), const(d, d)],
        out_specs=tok(d),
        out_shape=jax.ShapeDtypeStruct((t, d), F32),
        scratch_shapes=[pltpu.VMEM((n_rows * ROW_SUB, LANES), F32), pltpu.VMEM((n_rows * ROW_SUB, LANES), F32),
                        pltpu.SemaphoreType.DMA((2,))],
        compiler_params=pltpu.CompilerParams(
            dimension_semantics=("arbitrary",), vmem_limit_bytes=VMEM_LIMIT),
        name="ple",
    )(pos, pos, ys, h, route, p2, wple, gple, wpg)


def _segment_tables(counts, t):
    tm = EXP_TM
    nt = 2 * t // tm + N_EXPERTS
    padded = ((counts + tm - 1) // tm) * tm
    seg_end = jnp.cumsum(padded)
    seg_start = seg_end - padded
    gap_len = jnp.concatenate([padded - counts, nt * tm - seg_end[-1:]])
    gap_start = jnp.concatenate([seg_start + counts, seg_end[-1:]])
    tile_start = jnp.arange(nt, dtype=jnp.int32) * tm
    tile_e = jnp.minimum(jnp.sum(tile_start[:, None] >= seg_end[None, :], axis=1), N_EXPERTS - 1)
    n_tiles = (seg_end[-1] // tm).reshape(1)
    i32 = lambda a: a.astype(jnp.int32)
    return i32(seg_start), i32(gap_start), i32(gap_len), i32(tile_e), i32(n_tiles), nt * tm


def _layer(h, p_i, g_mix, w_in, g_v_a, w_s, b_s, g_q, g_k, g_out_a, g_out_b, w_out, g_ffn, w_group,
           w_expert, w_gate_e, w_up_e, w_down_e, g_ple, w_ple, w_ple_gate):
    b, s, d = h.shape
    t = b * s
    row = lambda g: g.reshape(1, -1).astype(F32)
    wuv = w_in[:, :2 * D_A].astype(BF16)
    wqt = w_in[:, 2 * D_A:2 * D_A + D_B].T.astype(BF16)
    wk = w_in[:, 2 * D_A + D_B:2 * D_A + 2 * D_B].astype(BF16)
    wvt = w_in[:, 2 * D_A + 2 * D_B:].T.astype(BF16)
    bs = jnp.broadcast_to(b_s[:, :, None], (A_GROUPS, A_CHUNK, A_DH)).astype(F32)
    gk = row(jnp.tile(g_k, B_HEADS))
    gq = (jnp.tile(g_q, B_HEADS) * (1.0 / float(B_DH) ** 0.5)).reshape(D_B, 1).astype(F32)
    head = jnp.arange(D_B) // B_DH
    bd = jnp.where(head[:, None] == head[None, :], 1.0 / B_DH, 0.0).astype(BF16)
    w_rt = jnp.zeros((ROUTER_ROWS, d), F32)
    w_rt = w_rt.at[:N_GROUPS].set(w_group.T).at[N_GROUPS:N_GROUPS + N_EXPERTS].set(w_expert.T)
    wrt_hi, wrt_lo = _split_bf16(w_rt)

    ya, k, qt, vt = _proj_call(h, row(g_mix), wuv, wk, wqt, wvt, row(g_v_a), w_s.astype(F32), bs,
                               row(g_out_a), gk, gq, bd)
    bound = (BOUND_MARGIN * LOG2E * float(B_DH) ** 0.5) * jnp.max(jnp.abs(g_q)) * jnp.max(jnp.abs(g_k))
    yb = _moba_call(bound.reshape(1, 1).astype(F32), qt, k, vt, row(g_out_b))
    h1, hn_tiles, route, route_i, counts = _outmix_call(
        h.reshape(t, d), ya.reshape(t, D_A), yb.reshape(t, D_B), w_out.astype(BF16), row(g_ffn), wrt_hi, wrt_lo)
    counts = counts[N_GROUPS:N_GROUPS + N_EXPERTS, 0].astype(jnp.int32)
    seg_start, gap_start, gap_len, tile_e, n_tiles, n_sorted_rows = _segment_tables(counts, t)
    pos = _slots_call(seg_start, route_i)
    xs = _dispatch_call(gap_start, gap_len, pos, hn_tiles, n_sorted_rows)
    ys = _experts_call(tile_e, n_tiles, xs, w_gate_e, w_up_e, w_down_e)
    out = _ple_call(pos, ys, h1, route, p_i.reshape(t, D_PLE), w_ple.astype(BF16), row(g_ple),
                    w_ple_gate.astype(BF16))
    return out.reshape(b, s, d)


def kernel(x, p, g_mix, w_in, g_v_a, w_s, b_s, g_q, g_k, g_out_a, g_out_b, w_out, g_ffn, w_group, w_expert,
           w_gate_e, w_up_e, w_down_e, g_ple, w_ple, w_ple_gate):
    params = (g_mix, w_in, g_v_a, w_s, b_s, g_q, g_k, g_out_a, g_out_b, w_out, g_ffn, w_group, w_expert,
              w_gate_e, w_up_e, w_down_e, g_ple, w_ple, w_ple_gate)
    h = x
    for i in range(p.shape[0]):
        h = _layer(h, p[i], *(w[i] for w in params))
    return h
```

```python
import functools

import jax
import jax.numpy as jnp
from jax import lax
from jax.experimental import pallas as pl
from jax.experimental.pallas import tpu as pltpu

F32 = jnp.float32
BF16 = jnp.bfloat16

D_MODEL = 1024
D_A = 512
A_GROUPS = 4
A_DH = D_A // A_GROUPS
A_CHUNK = 128
D_B = 512
B_HEADS = 8
B_DH = D_B // B_HEADS
B_BLOCK = 256
B_TOPK = 3
D_PLE = 256
N_GROUPS = 4
EXP_PER_GROUP = 8
N_EXPERTS = N_GROUPS * EXP_PER_GROUP
D_FF_EXP = 256
EPS = 1e-6
NEG = -1e30
LOG2E = 1.4426950408889634

LANES = 128
HEAD_PAIR = 2 * B_DH
N_PAIRS = B_HEADS // 2
ROUTER_ROWS = 48
ROUTE_FIELDS = 8
VMEM_LIMIT = 56 * 1024 * 1024

PROJ_TM = 512
OUT_TM = 512
EXP_TM = 512
MOBA_STEPS_PER_TRIP = 4
BOUND_MARGIN = 1.02
MIN_SHIFTED_SUM = 2.0 ** -64
TOK_TM = 256
ROW_SUB = D_MODEL // LANES
CHUNK = 8
MAX_CHUNKS = N_EXPERTS + 2 * TOK_TM // CHUNK


def _rms(x, g):
    return x * lax.rsqrt(jnp.mean(x * x, axis=-1, keepdims=True) + EPS) * g


def _dot(a, b):
    return jnp.dot(a, b, preferred_element_type=F32)


def _dot_nt(a, b):
    return lax.dot_general(a, b, (((1,), (1,)), ((), ())), preferred_element_type=F32)


def _split_bf16(x):
    hi = x.astype(BF16)
    lo = (x - hi.astype(F32)).astype(BF16)
    return hi, lo


def _proj_kernel(x_ref, gmix_ref, wuv_ref, wk_ref, wqt_ref, wvt_ref, gva_ref, ws_ref, bs_ref,
                 goa_ref, gk_ref, gq_ref, bd_ref,
                 ya_ref, k_ref, qt_ref, vt_ref, prod_ref):
    tm = x_ref.shape[1]
    n_chunks = tm // A_CHUNK
    xn = _rms(x_ref[0], gmix_ref[...]).astype(BF16)

    uv = _dot(xn, wuv_ref[...])
    u = jax.nn.gelu(uv[:, :D_A])
    v = jax.nn.gelu(uv[:, D_A:])
    row = lax.broadcasted_iota(jnp.int32, (A_CHUNK, A_CHUNK), 0)
    col = lax.broadcasted_iota(jnp.int32, (A_CHUNK, A_CHUNK), 1)
    causal = col <= row
    for g in range(A_GROUPS):
        gs = slice(g * A_DH, (g + 1) * A_DH)
        vn = _rms(v[:, gs], gva_ref[:, gs]).astype(BF16)
        rhs = jnp.concatenate([vn[c * A_CHUNK:(c + 1) * A_CHUNK] for c in range(n_chunks)], axis=1)
        ws = jnp.where(causal, ws_ref[g], 0.0).astype(BF16)
        mixed = _dot(ws, rhs)
        for c in range(n_chunks):
            cs = slice(c * A_CHUNK, (c + 1) * A_CHUNK)
            prod_ref[cs, gs] = u[cs, gs] * (mixed[:, cs] + bs_ref[g])
    ya_ref[0] = _rms(prod_ref[...], goa_ref[...]).astype(ya_ref.dtype)

    k = _dot(xn, wk_ref[...])
    k2_hi, k2_lo = _split_bf16(k * k)
    k_ms = _dot(k2_hi, bd_ref[...]) + _dot(k2_lo, bd_ref[...])
    k_ref[0] = (k * lax.rsqrt(k_ms + EPS) * gk_ref[...]).astype(k_ref.dtype)

    qt = _dot_nt(wqt_ref[...], xn)
    vt = _dot_nt(wvt_ref[...], xn).astype(vt_ref.dtype)
    qn = []
    for h in range(B_HEADS):
        qh = qt[h * B_DH:(h + 1) * B_DH]
        qn.append(qh * lax.rsqrt(jnp.mean(qh * qh, axis=0, keepdims=True) + EPS))
    qn = (jnp.concatenate(qn, axis=0) * gq_ref[...]).astype(qt_ref.dtype)
    for p in range(N_PAIRS):
        ps = slice(p * HEAD_PAIR, (p + 1) * HEAD_PAIR)
        for j in range(tm // B_BLOCK):
            js = slice(j * B_BLOCK, (j + 1) * B_BLOCK)
            qt_ref[0, p, j] = qn[ps, js]
            vt_ref[0, p, j] = vt[ps, js]


def _proj_call(x, gmix, wuv, wk, wqt, wvt, gva, ws, bs, goa, gk, gq, bd):
    b, s, d = x.shape
    tm = PROJ_TM
    nb = s // B_BLOCK
    jb = tm // B_BLOCK
    const = lambda *shape: pl.BlockSpec(shape, lambda bi, ti: (0,) * len(shape))
    return pl.pallas_call(
        _proj_kernel,
        grid=(b, s // tm),
        in_specs=[
            pl.BlockSpec((1, tm, d), lambda bi, ti: (bi, ti, 0)),
            const(1, d), const(d, 2 * D_A), const(d, D_B), const(D_B, d), const(D_B, d),
            const(1, D_A), const(A_GROUPS, A_CHUNK, A_CHUNK), const(A_GROUPS, A_CHUNK, A_DH),
            const(1, D_A), const(1, D_B), const(D_B, 1), const(D_B, D_B),
        ],
        out_specs=[
            pl.BlockSpec((1, tm, D_A), lambda bi, ti: (bi, ti, 0)),
            pl.BlockSpec((1, tm, D_B), lambda bi, ti: (bi, ti, 0)),
            pl.BlockSpec((1, N_PAIRS, jb, HEAD_PAIR, B_BLOCK), lambda bi, ti: (bi, 0, ti, 0, 0)),
            pl.BlockSpec((1, N_PAIRS, jb, HEAD_PAIR, B_BLOCK), lambda bi, ti: (bi, 0, ti, 0, 0)),
        ],
        out_shape=[
            jax.ShapeDtypeStruct((b, s, D_A), BF16),
            jax.ShapeDtypeStruct((b, s, D_B), BF16),
            jax.ShapeDtypeStruct((b, N_PAIRS, nb, HEAD_PAIR, B_BLOCK), BF16),
            jax.ShapeDtypeStruct((b, N_PAIRS, nb, HEAD_PAIR, B_BLOCK), BF16),
        ],
        scratch_shapes=[pltpu.VMEM((tm, D_A), F32)],
        compiler_params=pltpu.CompilerParams(
            dimension_semantics=("parallel", "parallel"), vmem_limit_bytes=VMEM_LIMIT),
        name="proj",
    )(x, gmix, wuv, wk, wqt, wvt, gva, ws, bs, goa, gk, gq, bd)


def _moba_kernel(bound_ref, qt_ref, k_ref, vt_ref, gob_ref, y_ref, kmean_ref, qz_ref, sel_ref, m_ref, l_ref,
                 acc_ref, stage0_ref, stage1_ref):
    qb = pl.program_id(1)
    nb = k_ref.shape[1] // B_BLOCK
    nbp = sel_ref.shape[1]

    @pl.when(qb == 0)
    def _():
        rows = [jnp.mean(k_ref[0, j * B_BLOCK:(j + 1) * B_BLOCK, :].astype(F32), axis=0, keepdims=True)
                for j in range(nb)]
        rows += [jnp.zeros((1, D_B), F32)] * (nbp - nb)
        km = jnp.concatenate(rows, axis=0)
        hi = km.astype(BF16)
        rest = km - hi.astype(F32)
        mid = rest.astype(BF16)
        lo = (rest - mid.astype(F32)).astype(BF16)
        for p in range(N_PAIRS):
            ps = slice(p * HEAD_PAIR, (p + 1) * HEAD_PAIR)
            kmean_ref[p] = jnp.concatenate([hi[:, ps], mid[:, ps], lo[:, ps]], axis=0)

    head_row = lax.broadcasted_iota(jnp.int32, (HEAD_PAIR, B_BLOCK), 0) // B_DH
    blk = lax.broadcasted_iota(jnp.int32, (nbp, B_BLOCK), 0)
    past = blk < qb

    for h in range(B_HEADS):
        p, hh = divmod(h, 2)
        q_pair = qt_ref[0, p, 0]
        qz = jnp.where(head_row == hh, q_pair, jnp.zeros_like(q_pair))
        qz_ref[h] = qz
        g3 = _dot(kmean_ref[p], qz)
        gate = (g3[:nbp] + g3[nbp:2 * nbp] + g3[2 * nbp:]) * float(B_DH) ** 0.5
        gate = jnp.where(past, gate, NEG)
        left = gate
        taken = blk < 0
        for _ in range(min(B_TOPK, nb)):
            top = jnp.max(left, axis=0, keepdims=True)
            first = jnp.min(jnp.where(left == top, blk, nbp), axis=0, keepdims=True)
            taken = taken | (blk == first)
            left = jnp.where(blk == first, -jnp.inf, left)
        chosen = taken & (gate > NEG * 0.5)
        sel_ref[h] = jnp.where(chosen, -bound_ref[0, 0], NEG)

    def scores(h, j):
        p = h // 2
        kj = k_ref[0, pl.ds(pl.multiple_of(j * B_BLOCK, B_BLOCK), B_BLOCK), p * HEAD_PAIR:(p + 1) * HEAD_PAIR]
        return _dot(kj, qz_ref[h])

    def values(h, j):
        p, hh = divmod(h, 2)
        return vt_ref[0, p, j, hh * B_DH:(hh + 1) * B_DH, :]

    key_pos = lax.broadcasted_iota(jnp.int32, (B_BLOCK, B_BLOCK), 0)
    q_pos = lax.broadcasted_iota(jnp.int32, (B_BLOCK, B_BLOCK), 1)
    causal = key_pos <= q_pos

    bound = bound_ref[0, 0]
    stage = (stage0_ref, stage1_ref)
    own_shift = jnp.where(causal, -bound, NEG)

    def stage_scores(h, block, shift, parity):
        stage[parity][h] = scores(h, block) * LOG2E + shift

    def consume(h, s, parity):
        pr = jnp.exp2(stage[parity][h])
        l_ref[h] += jnp.sum(pr, axis=0, keepdims=True)
        acc_ref[h] += _dot(values(h, jnp.where(s == 0, qb, s - 1)), pr.astype(BF16))

    def full_step(s, parity):
        for h in range(B_HEADS):
            stage_scores(h, s - 1, sel_ref[h, pl.ds(s - 1, 1), :], parity)
            consume(h, s - 1, 1 - parity)

    for h in range(B_HEADS):
        stage_scores(h, qb, own_shift, 0)
        l_ref[h] = jnp.zeros((1, B_BLOCK), F32)
        acc_ref[h] = jnp.zeros((B_DH, B_BLOCK), F32)

    def trip(u, carry):
        for v in range(MOBA_STEPS_PER_TRIP):
            full_step(MOBA_STEPS_PER_TRIP * u + v + 1, (v + 1) % 2)
        return carry

    lax.fori_loop(0, qb // MOBA_STEPS_PER_TRIP, trip, 0)
    done = (qb // MOBA_STEPS_PER_TRIP) * MOBA_STEPS_PER_TRIP
    for v in range(MOBA_STEPS_PER_TRIP - 1):
        pl.when(qb - done > v)(functools.partial(full_step, done + v + 1, (v + 1) % 2))

    for parity in range(2):
        @pl.when(lax.rem(qb, 2) == parity)
        def _(parity=parity):
            for h in range(B_HEADS):
                consume(h, qb, parity)

    smallest = jnp.min(jnp.concatenate([l_ref[h] for h in range(B_HEADS)], axis=0))

    @pl.when(jnp.logical_not(smallest > MIN_SHIFTED_SUM))
    def _():
        def online(h, j, own, first):
            visible = causal if own else sel_ref[h, pl.ds(j, 1), :] > NEG * 0.5
            sc = jnp.where(visible, scores(h, j) * LOG2E, NEG)
            m_tile = jnp.max(sc, axis=0, keepdims=True)
            m_new = m_tile if first else jnp.maximum(m_ref[h], m_tile)
            pr = jnp.exp2(sc - m_new)
            l_new, acc_new = jnp.sum(pr, axis=0, keepdims=True), _dot(values(h, j), pr.astype(BF16))
            if not first:
                alpha = jnp.exp2(m_ref[h] - m_new)
                l_new, acc_new = alpha * l_ref[h] + l_new, alpha * acc_ref[h] + acc_new
            m_ref[h], l_ref[h], acc_ref[h] = m_new, l_new, acc_new

        for h in range(B_HEADS):
            online(h, qb, True, True)

        def past(j, carry):
            for h in range(B_HEADS):
                online(h, j, False, False)
            return carry

        lax.fori_loop(0, qb, past, 0)

    yt = jnp.concatenate([acc_ref[h] / l_ref[h] for h in range(B_HEADS)], axis=0)
    yt = yt * lax.rsqrt(jnp.mean(yt * yt, axis=0, keepdims=True) + EPS)
    y_ref[0] = (yt.T * gob_ref[...]).astype(y_ref.dtype)


def _moba_call(bound, qt, k, vt, gob):
    b, s, _ = k.shape
    nb = s // B_BLOCK
    nbp = -(-nb // 8) * 8
    return pl.pallas_call(
        _moba_kernel,
        grid=(b, nb),
        in_specs=[
            pl.BlockSpec(memory_space=pltpu.SMEM),
            pl.BlockSpec((1, N_PAIRS, 1, HEAD_PAIR, B_BLOCK), lambda bi, qi: (bi, 0, qi, 0, 0)),
            pl.BlockSpec((1, s, D_B), lambda bi, qi: (bi, 0, 0)),
            pl.BlockSpec((1, N_PAIRS, nb, HEAD_PAIR, B_BLOCK), lambda bi, qi: (bi, 0, 0, 0, 0)),
            pl.BlockSpec((1, D_B), lambda bi, qi: (0, 0)),
        ],
        out_specs=pl.BlockSpec((1, B_BLOCK, D_B), lambda bi, qi: (bi, qi, 0)),
        out_shape=jax.ShapeDtypeStruct((b, s, D_B), BF16),
        scratch_shapes=[
            pltpu.VMEM((N_PAIRS, 3 * nbp, HEAD_PAIR), BF16),
            pltpu.VMEM((B_HEADS, HEAD_PAIR, B_BLOCK), BF16),
            pltpu.VMEM((B_HEADS, nbp, B_BLOCK), F32),
            pltpu.VMEM((B_HEADS, 1, B_BLOCK), F32),
            pltpu.VMEM((B_HEADS, 1, B_BLOCK), F32),
            pltpu.VMEM((B_HEADS, B_DH, B_BLOCK), F32),
            pltpu.VMEM((B_HEADS, B_BLOCK, B_BLOCK), F32),
            pltpu.VMEM((B_HEADS, B_BLOCK, B_BLOCK), F32),
        ],
        compiler_params=pltpu.CompilerParams(
            dimension_semantics=("parallel", "arbitrary"), vmem_limit_bytes=VMEM_LIMIT),
        name="moba",
    )(bound, qt, k, vt, gob)


def _outmix_kernel(x_ref, ya_ref, yb_ref, wout_ref, gffn_ref, wrt_hi_ref, wrt_lo_ref,
                   h_ref, hn_ref, route_ref, route_i_ref, counts_ref, before_tile_ref, cnt_ref):
    h = x_ref[...] + _dot(ya_ref[...], wout_ref[:D_A]) + _dot(yb_ref[...], wout_ref[D_A:])
    h_ref[...] = h
    hn = _rms(h, gffn_ref[...])
    _to_row_tiles(hn_ref, hn)
    tm = hn.shape[0]

    hn_hi, hn_lo = _split_bf16(hn)
    lg = _dot_nt(wrt_hi_ref[...], hn_hi) + (_dot_nt(wrt_hi_ref[...], hn_lo) + _dot_nt(wrt_lo_ref[...], hn_hi))
    row = lax.broadcasted_iota(jnp.int32, lg.shape, 0)

    def first_row(mask):
        return jnp.min(jnp.where(mask, row, ROUTER_ROWS), axis=0, keepdims=True)

    is_g = row < N_GROUPS
    g_max = jnp.max(jnp.where(is_g, lg, -jnp.inf), axis=0, keepdims=True)
    p_g = 1.0 / jnp.sum(jnp.where(is_g, jnp.exp(lg - g_max), 0.0), axis=0, keepdims=True)
    g_sel = first_row(is_g & (lg == g_max))
    e_row = row - N_GROUPS
    in_grp = (e_row >= 0) & (e_row < N_EXPERTS) & (lax.shift_right_arithmetic(e_row, 3) == g_sel)
    l1 = jnp.max(jnp.where(in_grp, lg, -jnp.inf), axis=0, keepdims=True)
    i1 = first_row(in_grp & (lg == l1))
    rest = in_grp & (row != i1)
    l2 = jnp.max(jnp.where(rest, lg, -jnp.inf), axis=0, keepdims=True)
    i2 = first_row(rest & (lg == l2))
    e2 = jnp.exp(l2 - l1)
    w1 = 1.0 / (1.0 + e2)
    w2 = e2 / (1.0 + e2)

    @pl.when(pl.program_id(0) == 0)
    def _():
        cnt_ref[...] = jnp.zeros_like(cnt_ref)

    picked = jnp.where((row == i1) | (row == i2), 1.0, 0.0)
    tok_r = lax.broadcasted_iota(jnp.int32, (tm, tm), 0)
    tok_c = lax.broadcasted_iota(jnp.int32, (tm, tm), 1)
    earlier = jnp.where(tok_r < tok_c, 1.0, 0.0).astype(BF16)
    before = _dot(picked.astype(BF16), earlier) + cnt_ref[:, 0:1]
    r1 = jnp.sum(jnp.where(row == i1, before, 0.0), axis=0, keepdims=True)
    r2 = jnp.sum(jnp.where(row == i2, before, 0.0), axis=0, keepdims=True)
    cnt_ref[...] += jnp.sum(picked, axis=1, keepdims=True)
    counts_ref[...] = cnt_ref[...]
    for j in range(tm // TOK_TM):
        before_tile_ref[j] = jnp.broadcast_to(before[:, j * TOK_TM:j * TOK_TM + 1], (ROUTER_ROWS, LANES))

    e1, e2x = i1 - N_GROUPS, i2 - N_GROUPS
    route_i_ref[...] = jnp.concatenate(
        [e1, e2x, r1.astype(jnp.int32), r2.astype(jnp.int32), jnp.zeros((ROUTE_FIELDS - 4, tm), jnp.int32)], axis=0)
    wts = jnp.concatenate([p_g * w1, p_g * w2, jnp.zeros((LANES - 2, tm), F32)], axis=0)
    route_ref[...] = wts.T


def _outmix_call(x2, ya2, yb2, wout, gffn, wrt_hi, wrt_lo):
    t, d = x2.shape
    tm = OUT_TM
    const = lambda *shape: pl.BlockSpec(shape, lambda ti: (0,) * len(shape))
    tok = lambda w: pl.BlockSpec((tm, w), lambda ti: (ti, 0))
    return pl.pallas_call(
        _outmix_kernel,
        grid=(t // tm,),
        in_specs=[tok(d), tok(D_A), tok(D_B), const(d, d), const(1, d),
                  const(ROUTER_ROWS, d), const(ROUTER_ROWS, d)],
        out_specs=[tok(d), pl.BlockSpec((tm * ROW_SUB, LANES), lambda ti: (ti, 0)), tok(LANES),
                   pl.BlockSpec((ROUTE_FIELDS, tm), lambda ti: (0, ti)), const(ROUTER_ROWS, LANES),
                   pl.BlockSpec((tm // TOK_TM, ROUTER_ROWS, LANES), lambda ti: (ti, 0, 0))],
        out_shape=[jax.ShapeDtypeStruct((t, d), F32), jax.ShapeDtypeStruct((t * ROW_SUB, LANES), F32),
                   jax.ShapeDtypeStruct((t, LANES), F32),
                   jax.ShapeDtypeStruct((ROUTE_FIELDS, t), jnp.int32),
                   jax.ShapeDtypeStruct((ROUTER_ROWS, LANES), F32),
                   jax.ShapeDtypeStruct((t // TOK_TM, ROUTER_ROWS, LANES), F32)],
        scratch_shapes=[pltpu.VMEM((ROUTER_ROWS, LANES), F32)],
        compiler_params=pltpu.CompilerParams(
            dimension_semantics=("arbitrary",), vmem_limit_bytes=VMEM_LIMIT),
        name="outmix",
    )(x2, ya2, yb2, wout, gffn, wrt_hi, wrt_lo)


def _to_row_tiles(ref, x):
    n = x.shape[0]
    for s in range(ROW_SUB):
        ref[pl.ds(s, n, stride=ROW_SUB), :] = x[:, s * LANES:(s + 1) * LANES]


def _from_row_tiles(ref, first, n):
    return jnp.concatenate(
        [ref[pl.ds(first * ROW_SUB + s, n, stride=ROW_SUB), :] for s in range(ROW_SUB)], axis=1)


def _tiles_at(ref, row, n=1):
    return ref.at[pl.ds(pl.multiple_of(row * ROW_SUB, ROW_SUB), n * ROW_SUB), :]


def _wait_tiles(hbm, vmem_ref, n_tiles, sem):
    cap = vmem_ref.shape[0] // ROW_SUB
    while n_tiles > 0:
        n = min(n_tiles, cap)
        pltpu.make_async_copy(hbm.at[pl.ds(0, n * ROW_SUB), :], vmem_ref.at[pl.ds(0, n * ROW_SUB), :], sem).wait()
        n_tiles -= n


def _slots_kernel(route_i_ref, offset_ref, place_ref):
    experts = route_i_ref[0:2, :]
    offset = jnp.zeros_like(experts)
    for e in range(N_EXPERTS):
        offset = jnp.where(experts == e, offset_ref[e:e + 1, :], offset)
    place_ref[...] = jnp.zeros_like(place_ref)
    place_ref[0:2, :] = offset + route_i_ref[2:4, :]


def _slots_call(route_i, offsets):
    whole = lambda a: pl.BlockSpec(a.shape, lambda i: (0, 0))
    return pl.pallas_call(
        _slots_kernel,
        grid=(1,),
        in_specs=[whole(route_i), whole(offsets)],
        out_specs=whole(route_i),
        out_shape=jax.ShapeDtypeStruct(route_i.shape, jnp.int32),
        name="slots",
    )(route_i, offsets)


def _dispatch_kernel(gap_start_ref, gap_len_ref, place_ref, chunk_ref, hn_ref, xs_hbm, packed_ref, zero_ref,
                     sem, fill_sem):
    tm = hn_ref.shape[0] // ROW_SUB
    n_chunks = packed_ref.shape[0] // (CHUNK * ROW_SUB)

    @pl.when(pl.program_id(0) == 0)
    def _():
        zero_ref[...] = jnp.zeros_like(zero_ref)
        packed_ref[...] = jnp.zeros_like(packed_ref)
        fills = []
        for g in range(N_EXPERTS):
            row, n = gap_start_ref[g], gap_len_ref[g]
            size = EXP_TM
            while size >= 1:
                fills.append(((n & size) != 0, row, size))
                row = row + (n & size)
                size //= 2
        for tile in range(N_EXPERTS + 1):
            fills.append((tile * EXP_TM < gap_len_ref[N_EXPERTS], gap_start_ref[N_EXPERTS] + tile * EXP_TM, EXP_TM))
        spare, left = xs_hbm.shape[0] // ROW_SUB - n_chunks * CHUNK, n_chunks * CHUNK
        while left > 0:
            fills.append((None, spare, min(left, EXP_TM)))
            spare, left = spare + min(left, EXP_TM), left - min(left, EXP_TM)
        copies = [(cond, pltpu.make_async_copy(_tiles_at(zero_ref, 0, size), _tiles_at(xs_hbm, row, size), fill_sem))
                  for cond, row, size in fills]
        for cond, copy in copies:
            copy.start() if cond is None else pl.when(cond)(copy.start)
        for cond, copy in copies:
            copy.wait() if cond is None else pl.when(cond)(copy.wait)

    for r in range(2 * tm):
        k, tok = divmod(r, tm)
        _tiles_at(packed_ref, place_ref[k, tok])[...] = _tiles_at(hn_ref, tok)[...]
    for c in range(n_chunks):
        pltpu.make_async_copy(_tiles_at(packed_ref, c * CHUNK, CHUNK), _tiles_at(xs_hbm, chunk_ref[0, 0, c], CHUNK),
                              sem).start(priority=c % 2)
    _wait_tiles(xs_hbm, packed_ref, n_chunks * CHUNK, sem)


def _dispatch_call(gap_start, gap_len, place, chunk_rows, hn_tiles, n_rows):
    tm = TOK_TM
    return pl.pallas_call(
        _dispatch_kernel,
        grid_spec=pltpu.PrefetchScalarGridSpec(
            num_scalar_prefetch=2,
            grid=(place.shape[1] // tm,),
            in_specs=[pl.BlockSpec((ROUTE_FIELDS, tm), lambda i, *_: (0, i), memory_space=pltpu.SMEM),
                      pl.BlockSpec((1, 1, chunk_rows.shape[2]), lambda i, *_: (i, 0, 0), memory_space=pltpu.SMEM),
                      pl.BlockSpec((tm * ROW_SUB, LANES), lambda i, *_: (i, 0))],
            out_specs=pl.BlockSpec(memory_space=pl.ANY),
            scratch_shapes=[pltpu.VMEM((MAX_CHUNKS * CHUNK * ROW_SUB, LANES), F32),
                            pltpu.VMEM((EXP_TM * ROW_SUB, LANES), F32),
                            pltpu.SemaphoreType.DMA(()), pltpu.SemaphoreType.DMA(())],
        ),
        out_shape=jax.ShapeDtypeStruct((n_rows * ROW_SUB, LANES), F32),
        compiler_params=pltpu.CompilerParams(
            dimension_semantics=("arbitrary",), vmem_limit_bytes=VMEM_LIMIT),
        name="dispatch",
    )(gap_start, gap_len, place, chunk_rows, hn_tiles)


def _experts_kernel(tile_e_ref, n_tiles_ref, xs_ref, wg_ref, wu_ref, wd_ref, y_ref, wg16, wu16, wd16):
    i = pl.program_id(0)
    tm = xs_ref.shape[0] // ROW_SUB

    @pl.when((i == 0) | (tile_e_ref[i] != tile_e_ref[jnp.maximum(i - 1, 0)]))
    def _():
        wg16[...] = wg_ref[0].astype(BF16)
        wu16[...] = wu_ref[0].astype(BF16)
        wd16[...] = wd_ref[0].astype(BF16)

    @pl.when(i < n_tiles_ref[0])
    def _():
        x = _from_row_tiles(xs_ref, 0, tm).astype(BF16)
        a = jax.nn.silu(_dot(x, wg16[...])) * _dot(x, wu16[...])
        _to_row_tiles(y_ref, _dot(a.astype(BF16), wd16[...]))

    @pl.when(i >= n_tiles_ref[0])
    def _():
        y_ref[...] = jnp.zeros_like(y_ref)


def _experts_call(tile_e, n_tiles, xs, wg, wu, wd):
    d = wg.shape[1]
    tm = EXP_TM
    nt = tile_e.shape[0]
    used = lambda i, n: jnp.minimum(i, n[0] - 1)
    w_spec = lambda *shape: pl.BlockSpec((1,) + shape, lambda i, te, n: (te[used(i, n)], 0, 0))
    return pl.pallas_call(
        _experts_kernel,
        grid_spec=pltpu.PrefetchScalarGridSpec(
            num_scalar_prefetch=2,
            grid=(nt,),
            in_specs=[pl.BlockSpec((tm * ROW_SUB, LANES), lambda i, te, n: (used(i, n), 0)),
                      w_spec(d, D_FF_EXP), w_spec(d, D_FF_EXP), w_spec(D_FF_EXP, d)],
            out_specs=pl.BlockSpec((tm * ROW_SUB, LANES), lambda i, te, n: (i, 0)),
            scratch_shapes=[pltpu.VMEM((d, D_FF_EXP), BF16), pltpu.VMEM((d, D_FF_EXP), BF16),
                            pltpu.VMEM((D_FF_EXP, d), BF16)],
        ),
        out_shape=jax.ShapeDtypeStruct((nt * tm * ROW_SUB, LANES), F32),
        compiler_params=pltpu.CompilerParams(
            dimension_semantics=("arbitrary",), vmem_limit_bytes=VMEM_LIMIT),
        name="experts",
    )(tile_e, n_tiles, xs, wg, wu, wd)


def _ple_kernel(place_ref, cur_ref, nxt_ref, ys_hbm, h_ref, route_ref, p_ref, wple_ref, gple_ref, wpg_ref,
                o_ref, stage0, stage1, ybuf, sem):
    tm = h_ref.shape[0]
    i = pl.program_id(0)
    n_steps = pl.num_programs(0)
    stages = (stage0, stage1)
    n_chunks = stage0.shape[0] // (CHUNK * ROW_SUB)

    def start_gather(chunk_ref, slot):
        for c in range(n_chunks):
            pltpu.make_async_copy(_tiles_at(ys_hbm, chunk_ref[0, 0, c], CHUNK), _tiles_at(stages[slot], c * CHUNK, CHUNK),
                                  sem.at[slot]).start(priority=c % 2)

    @pl.when(i == 0)
    def _():
        start_gather(cur_ref, 0)

    def step(slot):
        _wait_tiles(ys_hbm, stages[slot], n_chunks * CHUNK, sem.at[slot])
        start_gather(nxt_ref, 1 - slot)
        for r in range(2 * tm):
            k, tok = divmod(r, tm)
            _tiles_at(ybuf, r)[...] = _tiles_at(stages[slot], place_ref[k, tok])[...]
        wts = route_ref[...]
        h = (h_ref[...] + wts[:, 0:1] * _from_row_tiles(ybuf, 0, tm) + wts[:, 1:2] * _from_row_tiles(ybuf, tm, tm))
        gate = jax.nn.sigmoid(_dot(_rms(h, gple_ref[...]).astype(BF16), wpg_ref[...]))
        o_ref[...] = h + _dot(p_ref[...].astype(BF16), wple_ref[...]) * gate

    for slot in range(2):
        pl.when(lax.rem(i, 2) == slot)(functools.partial(step, slot))

    for slot in range(2):
        pl.when((i == n_steps - 1) & (lax.rem(n_steps, 2) == slot))(
            functools.partial(_wait_tiles, ys_hbm, stages[slot], n_chunks * CHUNK, sem.at[slot]))


def _ple_call(place, chunk_rows, ys, h, route, p2, wple, gple, wpg):
    t, d = h.shape
    tm = TOK_TM
    n_steps = t // tm
    chunk_spec = lambda f: pl.BlockSpec((1, 1, chunk_rows.shape[2]), lambda i: (f(i), 0, 0),
                                        memory_space=pltpu.SMEM)
    const = lambda *shape: pl.BlockSpec(shape, lambda i: (0,) * len(shape))
    tok = lambda w: pl.BlockSpec((tm, w), lambda i: (i, 0))
    stage = pltpu.VMEM((MAX_CHUNKS * CHUNK * ROW_SUB, LANES), F32)
    return pl.pallas_call(
        _ple_kernel,
        grid=(n_steps,),
        in_specs=[pl.BlockSpec((ROUTE_FIELDS, tm), lambda i: (0, i), memory_space=pltpu.SMEM),
                  chunk_spec(lambda i: i), chunk_spec(lambda i: lax.rem(i + 1, n_steps)),
                  pl.BlockSpec(memory_space=pl.ANY),
                  tok(d), tok(LANES), tok(D_PLE), const(D_PLE, d), const(1, d), const(d, d)],
        out_specs=tok(d),
        out_shape=jax.ShapeDtypeStruct((t, d), F32),
        scratch_shapes=[stage, stage, pltpu.VMEM((2 * tm * ROW_SUB, LANES), F32), pltpu.SemaphoreType.DMA((2,))],
        compiler_params=pltpu.CompilerParams(
            dimension_semantics=("arbitrary",), vmem_limit_bytes=VMEM_LIMIT),
        name="ple",
    )(place, chunk_rows, chunk_rows, ys, h, route, p2, wple, gple, wpg)


def _routing_tables(counts, before_tiles, t):
    tm = EXP_TM
    nt = 2 * t // tm + N_EXPERTS + 1
    padded = ((counts + CHUNK + tm - 1) // tm) * tm
    seg_end = jnp.cumsum(padded)
    seg_start = seg_end - padded
    gap_len = jnp.concatenate([padded - counts, nt * tm - seg_end[-1:]])
    gap_start = jnp.concatenate([seg_start + counts, seg_end[-1:]])
    tile_start = jnp.arange(nt, dtype=jnp.int32) * tm
    tile_e = jnp.minimum(jnp.sum(tile_start[:, None] >= seg_end[None, :], axis=1), N_EXPERTS - 1)
    n_tiles = (seg_end[-1] // tm).reshape(1)

    in_tile = jnp.concatenate([before_tiles[1:], counts[None, :]]) - before_tiles
    n_chunks = (in_tile + CHUNK - 1) // CHUNK
    chunk_end = jnp.cumsum(n_chunks, axis=1)
    chunk_base = chunk_end - n_chunks
    c = jnp.arange(LANES, dtype=jnp.int32)[None, :]
    e_of_chunk = jnp.minimum(jnp.sum(c[:, :, None] >= chunk_end[:, None, :], axis=2), N_EXPERTS - 1)
    onehot = e_of_chunk[:, :, None] == jnp.arange(N_EXPERTS, dtype=jnp.int32)
    pick = lambda table: jnp.sum(jnp.where(onehot, table[:, None, :], 0), axis=2)
    first_row = pick(seg_start[None, :] + before_tiles) + CHUNK * (c - pick(chunk_base))
    valid = c < chunk_end[:, -1:]
    read_rows = jnp.where(valid, first_row, 0)
    write_rows = jnp.where(valid, first_row, nt * tm + CHUNK * c)
    offsets = jnp.repeat((chunk_base * CHUNK - before_tiles).T, TOK_TM, axis=1)
    i32 = lambda a: a.astype(jnp.int32)
    return (i32(gap_start), i32(gap_len), i32(tile_e), i32(n_tiles), i32(offsets),
            i32(read_rows)[:, None, :], i32(write_rows)[:, None, :], nt * tm + MAX_CHUNKS * CHUNK)


def _layer(h, p_i, g_mix, w_in, g_v_a, w_s, b_s, g_q, g_k, g_out_a, g_out_b, w_out, g_ffn, w_group,
           w_expert, w_gate_e, w_up_e, w_down_e, g_ple, w_ple, w_ple_gate):
    b, s, d = h.shape
    t = b * s
    row = lambda g: g.reshape(1, -1).astype(F32)
    wuv = w_in[:, :2 * D_A].astype(BF16)
    wqt = w_in[:, 2 * D_A:2 * D_A + D_B].T.astype(BF16)
    wk = w_in[:, 2 * D_A + D_B:2 * D_A + 2 * D_B].astype(BF16)
    wvt = w_in[:, 2 * D_A + 2 * D_B:].T.astype(BF16)
    bs = jnp.broadcast_to(b_s[:, :, None], (A_GROUPS, A_CHUNK, A_DH)).astype(F32)
    gk = row(jnp.tile(g_k, B_HEADS))
    gq = (jnp.tile(g_q, B_HEADS) * (1.0 / float(B_DH) ** 0.5)).reshape(D_B, 1).astype(F32)
    head = jnp.arange(D_B) // B_DH
    bd = jnp.where(head[:, None] == head[None, :], 1.0 / B_DH, 0.0).astype(BF16)
    w_rt = jnp.zeros((ROUTER_ROWS, d), F32)
    w_rt = w_rt.at[:N_GROUPS].set(w_group.T).at[N_GROUPS:N_GROUPS + N_EXPERTS].set(w_expert.T)
    wrt_hi, wrt_lo = _split_bf16(w_rt)

    ya, k, qt, vt = _proj_call(h, row(g_mix), wuv, wk, wqt, wvt, row(g_v_a), w_s.astype(F32), bs,
                               row(g_out_a), gk, gq, bd)
    bound = (BOUND_MARGIN * LOG2E * float(B_DH) ** 0.5) * jnp.max(jnp.abs(g_q)) * jnp.max(jnp.abs(g_k))
    yb = _moba_call(bound.reshape(1, 1).astype(F32), qt, k, vt, row(g_out_b))
    h1, hn_tiles, route, route_i, counts, before_tiles = _outmix_call(
        h.reshape(t, d), ya.reshape(t, D_A), yb.reshape(t, D_B), w_out.astype(BF16), row(g_ffn), wrt_hi, wrt_lo)
    counts = counts[N_GROUPS:N_GROUPS + N_EXPERTS, 0].astype(jnp.int32)
    before_tiles = before_tiles[:, N_GROUPS:N_GROUPS + N_EXPERTS, 0].astype(jnp.int32)
    gap_start, gap_len, tile_e, n_tiles, offsets, read_rows, write_rows, n_rows = _routing_tables(
        counts, before_tiles, t)
    place = _slots_call(route_i, offsets)
    xs = _dispatch_call(gap_start, gap_len, place, write_rows, hn_tiles, n_rows)
    ys = _experts_call(tile_e, n_tiles, xs, w_gate_e, w_up_e, w_down_e)
    out = _ple_call(place, read_rows, ys, h1, route, p_i.reshape(t, D_PLE), w_ple.astype(BF16), row(g_ple),
                    w_ple_gate.astype(BF16))
    return out.reshape(b, s, d)


def kernel(x, p, g_mix, w_in, g_v_a, w_s, b_s, g_q, g_k, g_out_a, g_out_b, w_out, g_ffn, w_group, w_expert,
           w_gate_e, w_up_e, w_down_e, g_ple, w_ple, w_ple_gate):
    params = (g_mix, w_in, g_v_a, w_s, b_s, g_q, g_k, g_out_a, g_out_b, w_out, g_ffn, w_group, w_expert,
              w_gate_e, w_up_e, w_down_e, g_ple, w_ple, w_ple_gate)
    h = x
    for i in range(p.shape[0]):
        h = _layer(h, p[i], *(w[i] for w in params))
    return h
```

```python
import functools

import jax
import jax.numpy as jnp
from jax import lax
from jax.experimental import pallas as pl
from jax.experimental.pallas import tpu as pltpu

F32 = jnp.float32
BF16 = jnp.bfloat16

D_MODEL = 1024
D_A = 512
A_GROUPS = 4
A_DH = D_A // A_GROUPS
A_CHUNK = 128
D_B = 512
B_HEADS = 8
B_DH = D_B // B_HEADS
B_BLOCK = 256
B_TOPK = 3
D_PLE = 256
N_GROUPS = 4
EXP_PER_GROUP = 8
N_EXPERTS = N_GROUPS * EXP_PER_GROUP
D_FF_EXP = 256
EPS = 1e-6
NEG = -1e30
LOG2E = 1.4426950408889634

LANES = 128
HEAD_PAIR = 2 * B_DH
N_PAIRS = B_HEADS // 2
ROUTER_ROWS = 48
ROUTE_FIELDS = 8
VMEM_LIMIT = 56 * 1024 * 1024

PROJ_TM = 512
OUT_TM = 512
EXP_TM = 512
MOBA_STEPS_PER_TRIP = 4
BOUND_MARGIN = 1.02
MIN_SHIFTED_SUM = 2.0 ** -64
TOK_TM = 512
ROW_SUB = D_MODEL // LANES


def _rms(x, g):
    return x * lax.rsqrt(jnp.mean(x * x, axis=-1, keepdims=True) + EPS) * g


def _dot(a, b):
    return jnp.dot(a, b, preferred_element_type=F32)


def _dot_nt(a, b):
    return lax.dot_general(a, b, (((1,), (1,)), ((), ())), preferred_element_type=F32)


def _split_bf16(x):
    hi = x.astype(BF16)
    lo = (x - hi.astype(F32)).astype(BF16)
    return hi, lo


def _proj_kernel(x_ref, gmix_ref, wuv_ref, wk_ref, wqt_ref, wvt_ref, gva_ref, ws_ref, bs_ref,
                 goa_ref, gk_ref, gq_ref, bd_ref,
                 ya_ref, k_ref, qt_ref, vt_ref, prod_ref):
    tm = x_ref.shape[1]
    n_chunks = tm // A_CHUNK
    xn = _rms(x_ref[0], gmix_ref[...]).astype(BF16)

    uv = _dot(xn, wuv_ref[...])
    u = jax.nn.gelu(uv[:, :D_A])
    v = jax.nn.gelu(uv[:, D_A:])
    row = lax.broadcasted_iota(jnp.int32, (A_CHUNK, A_CHUNK), 0)
    col = lax.broadcasted_iota(jnp.int32, (A_CHUNK, A_CHUNK), 1)
    causal = col <= row
    for g in range(A_GROUPS):
        gs = slice(g * A_DH, (g + 1) * A_DH)
        vn = _rms(v[:, gs], gva_ref[:, gs]).astype(BF16)
        rhs = jnp.concatenate([vn[c * A_CHUNK:(c + 1) * A_CHUNK] for c in range(n_chunks)], axis=1)
        ws = jnp.where(causal, ws_ref[g], 0.0).astype(BF16)
        mixed = _dot(ws, rhs)
        for c in range(n_chunks):
            cs = slice(c * A_CHUNK, (c + 1) * A_CHUNK)
            prod_ref[cs, gs] = u[cs, gs] * (mixed[:, cs] + bs_ref[g])
    ya_ref[0] = _rms(prod_ref[...], goa_ref[...]).astype(ya_ref.dtype)

    k = _dot(xn, wk_ref[...])
    k2_hi, k2_lo = _split_bf16(k * k)
    k_ms = _dot(k2_hi, bd_ref[...]) + _dot(k2_lo, bd_ref[...])
    k_ref[0] = (k * lax.rsqrt(k_ms + EPS) * gk_ref[...]).astype(k_ref.dtype)

    qt = _dot_nt(wqt_ref[...], xn)
    vt = _dot_nt(wvt_ref[...], xn).astype(vt_ref.dtype)
    qn = []
    for h in range(B_HEADS):
        qh = qt[h * B_DH:(h + 1) * B_DH]
        qn.append(qh * lax.rsqrt(jnp.mean(qh * qh, axis=0, keepdims=True) + EPS))
    qn = (jnp.concatenate(qn, axis=0) * gq_ref[...]).astype(qt_ref.dtype)
    for p in range(N_PAIRS):
        ps = slice(p * HEAD_PAIR, (p + 1) * HEAD_PAIR)
        for j in range(tm // B_BLOCK):
            js = slice(j * B_BLOCK, (j + 1) * B_BLOCK)
            qt_ref[0, p, j] = qn[ps, js]
            vt_ref[0, p, j] = vt[ps, js]


def _proj_call(x, gmix, wuv, wk, wqt, wvt, gva, ws, bs, goa, gk, gq, bd):
    b, s, d = x.shape
    tm = PROJ_TM
    nb = s // B_BLOCK
    jb = tm // B_BLOCK
    const = lambda *shape: pl.BlockSpec(shape, lambda bi, ti: (0,) * len(shape))
    return pl.pallas_call(
        _proj_kernel,
        grid=(b, s // tm),
        in_specs=[
            pl.BlockSpec((1, tm, d), lambda bi, ti: (bi, ti, 0)),
            const(1, d), const(d, 2 * D_A), const(d, D_B), const(D_B, d), const(D_B, d),
            const(1, D_A), const(A_GROUPS, A_CHUNK, A_CHUNK), const(A_GROUPS, A_CHUNK, A_DH),
            const(1, D_A), const(1, D_B), const(D_B, 1), const(D_B, D_B),
        ],
        out_specs=[
            pl.BlockSpec((1, tm, D_A), lambda bi, ti: (bi, ti, 0)),
            pl.BlockSpec((1, tm, D_B), lambda bi, ti: (bi, ti, 0)),
            pl.BlockSpec((1, N_PAIRS, jb, HEAD_PAIR, B_BLOCK), lambda bi, ti: (bi, 0, ti, 0, 0)),
            pl.BlockSpec((1, N_PAIRS, jb, HEAD_PAIR, B_BLOCK), lambda bi, ti: (bi, 0, ti, 0, 0)),
        ],
        out_shape=[
            jax.ShapeDtypeStruct((b, s, D_A), BF16),
            jax.ShapeDtypeStruct((b, s, D_B), BF16),
            jax.ShapeDtypeStruct((b, N_PAIRS, nb, HEAD_PAIR, B_BLOCK), BF16),
            jax.ShapeDtypeStruct((b, N_PAIRS, nb, HEAD_PAIR, B_BLOCK), BF16),
        ],
        scratch_shapes=[pltpu.VMEM((tm, D_A), F32)],
        compiler_params=pltpu.CompilerParams(
            dimension_semantics=("parallel", "parallel"), vmem_limit_bytes=VMEM_LIMIT),
        name="proj",
    )(x, gmix, wuv, wk, wqt, wvt, gva, ws, bs, goa, gk, gq, bd)


def _moba_kernel(bound_ref, qt_ref, k_ref, vt_ref, gob_ref, y_ref, kmean_ref, qz_ref, sel_ref, m_ref, l_ref,
                 acc_ref, stage0_ref, stage1_ref):
    qb = pl.program_id(1)
    nb = k_ref.shape[1] // B_BLOCK
    nbp = sel_ref.shape[1]

    @pl.when(qb == 0)
    def _():
        rows = [jnp.mean(k_ref[0, j * B_BLOCK:(j + 1) * B_BLOCK, :].astype(F32), axis=0, keepdims=True)
                for j in range(nb)]
        rows += [jnp.zeros((1, D_B), F32)] * (nbp - nb)
        km = jnp.concatenate(rows, axis=0)
        hi = km.astype(BF16)
        rest = km - hi.astype(F32)
        mid = rest.astype(BF16)
        lo = (rest - mid.astype(F32)).astype(BF16)
        for p in range(N_PAIRS):
            ps = slice(p * HEAD_PAIR, (p + 1) * HEAD_PAIR)
            kmean_ref[p] = jnp.concatenate([hi[:, ps], mid[:, ps], lo[:, ps]], axis=0)

    head_row = lax.broadcasted_iota(jnp.int32, (HEAD_PAIR, B_BLOCK), 0) // B_DH
    blk = lax.broadcasted_iota(jnp.int32, (nbp, B_BLOCK), 0)
    past = blk < qb

    for h in range(B_HEADS):
        p, hh = divmod(h, 2)
        q_pair = qt_ref[0, p, 0]
        qz = jnp.where(head_row == hh, q_pair, jnp.zeros_like(q_pair))
        qz_ref[h] = qz
        g3 = _dot(kmean_ref[p], qz)
        gate = (g3[:nbp] + g3[nbp:2 * nbp] + g3[2 * nbp:]) * float(B_DH) ** 0.5
        gate = jnp.where(past, gate, NEG)
        left = gate
        taken = blk < 0
        for _ in range(min(B_TOPK, nb)):
            top = jnp.max(left, axis=0, keepdims=True)
            first = jnp.min(jnp.where(left == top, blk, nbp), axis=0, keepdims=True)
            taken = taken | (blk == first)
            left = jnp.where(blk == first, -jnp.inf, left)
        chosen = taken & (gate > NEG * 0.5)
        sel_ref[h] = jnp.where(chosen, -bound_ref[0, 0], NEG)

    def scores(h, j):
        p = h // 2
        kj = k_ref[0, pl.ds(pl.multiple_of(j * B_BLOCK, B_BLOCK), B_BLOCK), p * HEAD_PAIR:(p + 1) * HEAD_PAIR]
        return _dot(kj, qz_ref[h])

    def values(h, j):
        p, hh = divmod(h, 2)
        return vt_ref[0, p, j, hh * B_DH:(hh + 1) * B_DH, :]

    key_pos = lax.broadcasted_iota(jnp.int32, (B_BLOCK, B_BLOCK), 0)
    q_pos = lax.broadcasted_iota(jnp.int32, (B_BLOCK, B_BLOCK), 1)
    causal = key_pos <= q_pos

    bound = bound_ref[0, 0]
    stage = (stage0_ref, stage1_ref)
    own_shift = jnp.where(causal, -bound, NEG)

    def stage_scores(h, block, shift, parity):
        stage[parity][h] = scores(h, block) * LOG2E + shift

    def consume(h, s, parity):
        pr = jnp.exp2(stage[parity][h])
        l_ref[h] += jnp.sum(pr, axis=0, keepdims=True)
        acc_ref[h] += _dot(values(h, jnp.where(s == 0, qb, s - 1)), pr.astype(BF16))

    def full_step(s, parity):
        for h in range(B_HEADS):
            stage_scores(h, s - 1, sel_ref[h, pl.ds(s - 1, 1), :], parity)
            consume(h, s - 1, 1 - parity)

    for h in range(B_HEADS):
        stage_scores(h, qb, own_shift, 0)
        l_ref[h] = jnp.zeros((1, B_BLOCK), F32)
        acc_ref[h] = jnp.zeros((B_DH, B_BLOCK), F32)

    def trip(u, carry):
        for v in range(MOBA_STEPS_PER_TRIP):
            full_step(MOBA_STEPS_PER_TRIP * u + v + 1, (v + 1) % 2)
        return carry

    lax.fori_loop(0, qb // MOBA_STEPS_PER_TRIP, trip, 0)
    done = (qb // MOBA_STEPS_PER_TRIP) * MOBA_STEPS_PER_TRIP
    for v in range(MOBA_STEPS_PER_TRIP - 1):
        pl.when(qb - done > v)(functools.partial(full_step, done + v + 1, (v + 1) % 2))

    for parity in range(2):
        @pl.when(lax.rem(qb, 2) == parity)
        def _(parity=parity):
            for h in range(B_HEADS):
                consume(h, qb, parity)

    smallest = jnp.min(jnp.concatenate([l_ref[h] for h in range(B_HEADS)], axis=0))

    @pl.when(jnp.logical_not(smallest > MIN_SHIFTED_SUM))
    def _():
        def online(h, j, own, first):
            visible = causal if own else sel_ref[h, pl.ds(j, 1), :] > NEG * 0.5
            sc = jnp.where(visible, scores(h, j) * LOG2E, NEG)
            m_tile = jnp.max(sc, axis=0, keepdims=True)
            m_new = m_tile if first else jnp.maximum(m_ref[h], m_tile)
            pr = jnp.exp2(sc - m_new)
            l_new, acc_new = jnp.sum(pr, axis=0, keepdims=True), _dot(values(h, j), pr.astype(BF16))
            if not first:
                alpha = jnp.exp2(m_ref[h] - m_new)
                l_new, acc_new = alpha * l_ref[h] + l_new, alpha * acc_ref[h] + acc_new
            m_ref[h], l_ref[h], acc_ref[h] = m_new, l_new, acc_new

        for h in range(B_HEADS):
            online(h, qb, True, True)

        def past(j, carry):
            for h in range(B_HEADS):
                online(h, j, False, False)
            return carry

        lax.fori_loop(0, qb, past, 0)

    yt = jnp.concatenate([acc_ref[h] / l_ref[h] for h in range(B_HEADS)], axis=0)
    yt = yt * lax.rsqrt(jnp.mean(yt * yt, axis=0, keepdims=True) + EPS)
    y_ref[0] = (yt.T * gob_ref[...]).astype(y_ref.dtype)


def _moba_call(bound, qt, k, vt, gob):
    b, s, _ = k.shape
    nb = s // B_BLOCK
    nbp = -(-nb // 8) * 8
    return pl.pallas_call(
        _moba_kernel,
        grid=(b, nb),
        in_specs=[
            pl.BlockSpec(memory_space=pltpu.SMEM),
            pl.BlockSpec((1, N_PAIRS, 1, HEAD_PAIR, B_BLOCK), lambda bi, qi: (bi, 0, qi, 0, 0)),
            pl.BlockSpec((1, s, D_B), lambda bi, qi: (bi, 0, 0)),
            pl.BlockSpec((1, N_PAIRS, nb, HEAD_PAIR, B_BLOCK), lambda bi, qi: (bi, 0, 0, 0, 0)),
            pl.BlockSpec((1, D_B), lambda bi, qi: (0, 0)),
        ],
        out_specs=pl.BlockSpec((1, B_BLOCK, D_B), lambda bi, qi: (bi, qi, 0)),
        out_shape=jax.ShapeDtypeStruct((b, s, D_B), BF16),
        scratch_shapes=[
            pltpu.VMEM((N_PAIRS, 3 * nbp, HEAD_PAIR), BF16),
            pltpu.VMEM((B_HEADS, HEAD_PAIR, B_BLOCK), BF16),
            pltpu.VMEM((B_HEADS, nbp, B_BLOCK), F32),
            pltpu.VMEM((B_HEADS, 1, B_BLOCK), F32),
            pltpu.VMEM((B_HEADS, 1, B_BLOCK), F32),
            pltpu.VMEM((B_HEADS, B_DH, B_BLOCK), F32),
            pltpu.VMEM((B_HEADS, B_BLOCK, B_BLOCK), F32),
            pltpu.VMEM((B_HEADS, B_BLOCK, B_BLOCK), F32),
        ],
        compiler_params=pltpu.CompilerParams(
            dimension_semantics=("parallel", "arbitrary"), vmem_limit_bytes=VMEM_LIMIT),
        name="moba",
    )(bound, qt, k, vt, gob)


def _outmix_kernel(x_ref, ya_ref, yb_ref, wout_ref, gffn_ref, wrt_hi_ref, wrt_lo_ref,
                   h_ref, hn_ref, route_ref, route_i_ref, counts_ref, cnt_ref):
    h = x_ref[...] + _dot(ya_ref[...], wout_ref[:D_A]) + _dot(yb_ref[...], wout_ref[D_A:])
    h_ref[...] = h
    hn = _rms(h, gffn_ref[...])
    _to_row_tiles(hn_ref, hn)
    tm = hn.shape[0]

    hn_hi, hn_lo = _split_bf16(hn)
    lg = _dot_nt(wrt_hi_ref[...], hn_hi) + (_dot_nt(wrt_hi_ref[...], hn_lo) + _dot_nt(wrt_lo_ref[...], hn_hi))
    row = lax.broadcasted_iota(jnp.int32, lg.shape, 0)

    def first_row(mask):
        return jnp.min(jnp.where(mask, row, ROUTER_ROWS), axis=0, keepdims=True)

    is_g = row < N_GROUPS
    g_max = jnp.max(jnp.where(is_g, lg, -jnp.inf), axis=0, keepdims=True)
    p_g = 1.0 / jnp.sum(jnp.where(is_g, jnp.exp(lg - g_max), 0.0), axis=0, keepdims=True)
    g_sel = first_row(is_g & (lg == g_max))
    e_row = row - N_GROUPS
    in_grp = (e_row >= 0) & (e_row < N_EXPERTS) & (lax.shift_right_arithmetic(e_row, 3) == g_sel)
    l1 = jnp.max(jnp.where(in_grp, lg, -jnp.inf), axis=0, keepdims=True)
    i1 = first_row(in_grp & (lg == l1))
    rest = in_grp & (row != i1)
    l2 = jnp.max(jnp.where(rest, lg, -jnp.inf), axis=0, keepdims=True)
    i2 = first_row(rest & (lg == l2))
    e2 = jnp.exp(l2 - l1)
    w1 = 1.0 / (1.0 + e2)
    w2 = e2 / (1.0 + e2)

    @pl.when(pl.program_id(0) == 0)
    def _():
        cnt_ref[...] = jnp.zeros_like(cnt_ref)

    picked = jnp.where((row == i1) | (row == i2), 1.0, 0.0)
    tok_r = lax.broadcasted_iota(jnp.int32, (tm, tm), 0)
    tok_c = lax.broadcasted_iota(jnp.int32, (tm, tm), 1)
    earlier = jnp.where(tok_r < tok_c, 1.0, 0.0).astype(BF16)
    before = _dot(picked.astype(BF16), earlier) + cnt_ref[:, 0:1]
    r1 = jnp.sum(jnp.where(row == i1, before, 0.0), axis=0, keepdims=True)
    r2 = jnp.sum(jnp.where(row == i2, before, 0.0), axis=0, keepdims=True)
    cnt_ref[...] += jnp.sum(picked, axis=1, keepdims=True)
    counts_ref[...] = cnt_ref[...]

    e1, e2x = i1 - N_GROUPS, i2 - N_GROUPS
    route_i_ref[...] = jnp.concatenate(
        [e1, e2x, r1.astype(jnp.int32), r2.astype(jnp.int32), jnp.zeros((ROUTE_FIELDS - 4, tm), jnp.int32)], axis=0)
    wts = jnp.concatenate([p_g * w1, p_g * w2, jnp.zeros((LANES - 2, tm), F32)], axis=0)
    route_ref[...] = wts.T


def _outmix_call(x2, ya2, yb2, wout, gffn, wrt_hi, wrt_lo):
    t, d = x2.shape
    tm = OUT_TM
    const = lambda *shape: pl.BlockSpec(shape, lambda ti: (0,) * len(shape))
    tok = lambda w: pl.BlockSpec((tm, w), lambda ti: (ti, 0))
    return pl.pallas_call(
        _outmix_kernel,
        grid=(t // tm,),
        in_specs=[tok(d), tok(D_A), tok(D_B), const(d, d), const(1, d),
                  const(ROUTER_ROWS, d), const(ROUTER_ROWS, d)],
        out_specs=[tok(d), pl.BlockSpec((tm * ROW_SUB, LANES), lambda ti: (ti, 0)), tok(LANES),
                   pl.BlockSpec((ROUTE_FIELDS, tm), lambda ti: (0, ti)), const(ROUTER_ROWS, LANES)],
        out_shape=[jax.ShapeDtypeStruct((t, d), F32), jax.ShapeDtypeStruct((t * ROW_SUB, LANES), F32),
                   jax.ShapeDtypeStruct((t, LANES), F32),
                   jax.ShapeDtypeStruct((ROUTE_FIELDS, t), jnp.int32),
                   jax.ShapeDtypeStruct((ROUTER_ROWS, LANES), F32)],
        scratch_shapes=[pltpu.VMEM((ROUTER_ROWS, LANES), F32)],
        compiler_params=pltpu.CompilerParams(
            dimension_semantics=("arbitrary",), vmem_limit_bytes=VMEM_LIMIT),
        name="outmix",
    )(x2, ya2, yb2, wout, gffn, wrt_hi, wrt_lo)


def _to_row_tiles(ref, x):
    n = x.shape[0]
    for s in range(ROW_SUB):
        ref[pl.ds(s, n, stride=ROW_SUB), :] = x[:, s * LANES:(s + 1) * LANES]


def _from_row_tiles(ref, first, n):
    return jnp.concatenate(
        [ref[pl.ds(first * ROW_SUB + s, n, stride=ROW_SUB), :] for s in range(ROW_SUB)], axis=1)


def _tiles_at(ref, row, n=1):
    return ref.at[pl.ds(pl.multiple_of(row * ROW_SUB, ROW_SUB), n * ROW_SUB), :]


def _wait_tiles(hbm, vmem_ref, n_tiles, sem):
    cap = vmem_ref.shape[0] // ROW_SUB
    while n_tiles > 0:
        n = min(n_tiles, cap)
        pltpu.make_async_copy(hbm.at[pl.ds(0, n * ROW_SUB), :], vmem_ref.at[pl.ds(0, n * ROW_SUB), :], sem).wait()
        n_tiles -= n


def _slots_kernel(seg_start_ref, route_i_ref, pos_ref):
    experts = route_i_ref[0:2, :]
    start = jnp.zeros_like(experts)
    for e in range(N_EXPERTS):
        start = jnp.where(experts == e, seg_start_ref[e], start)
    pos_ref[...] = jnp.zeros_like(pos_ref)
    pos_ref[0:2, :] = start + route_i_ref[2:4, :]


def _slots_call(seg_start, route_i):
    return pl.pallas_call(
        _slots_kernel,
        grid_spec=pltpu.PrefetchScalarGridSpec(
            num_scalar_prefetch=1,
            grid=(1,),
            in_specs=[pl.BlockSpec(route_i.shape, lambda i, s: (0, 0))],
            out_specs=pl.BlockSpec(route_i.shape, lambda i, s: (0, 0)),
        ),
        out_shape=jax.ShapeDtypeStruct(route_i.shape, jnp.int32),
        name="slots",
    )(seg_start, route_i)


def _dispatch_kernel(gap_start_ref, gap_len_ref, pos_ref, hn_ref, xs_hbm, zero_ref, sem, fill_sem):
    tm = hn_ref.shape[0] // ROW_SUB

    @pl.when(pl.program_id(0) == 0)
    def _():
        zero_ref[...] = jnp.zeros_like(zero_ref)
        fills = []
        for g in range(N_EXPERTS):
            row, n = gap_start_ref[g], gap_len_ref[g]
            size = EXP_TM // 2
            while size >= 1:
                fills.append(((n & size) != 0, row, size))
                row = row + (n & size)
                size //= 2
        for tile in range(N_EXPERTS):
            fills.append((tile * EXP_TM < gap_len_ref[N_EXPERTS], gap_start_ref[N_EXPERTS] + tile * EXP_TM, EXP_TM))
        copies = [(cond, pltpu.make_async_copy(_tiles_at(zero_ref, 0, size), _tiles_at(xs_hbm, row, size), fill_sem))
                  for cond, row, size in fills]
        for cond, copy in copies:
            pl.when(cond)(copy.start)
        for cond, copy in copies:
            pl.when(cond)(copy.wait)

    for r in range(2 * tm):
        k, tok = divmod(r, tm)
        pltpu.make_async_copy(_tiles_at(hn_ref, tok), _tiles_at(xs_hbm, pos_ref[k, tok]), sem).start(priority=r % 2)
    _wait_tiles(xs_hbm, hn_ref, 2 * tm, sem)


def _dispatch_call(gap_start, gap_len, pos, hn_tiles, n_sorted_rows):
    tm = TOK_TM
    return pl.pallas_call(
        _dispatch_kernel,
        grid_spec=pltpu.PrefetchScalarGridSpec(
            num_scalar_prefetch=2,
            grid=(pos.shape[1] // tm,),
            in_specs=[pl.BlockSpec((ROUTE_FIELDS, tm), lambda i, *_: (0, i), memory_space=pltpu.SMEM),
                      pl.BlockSpec((tm * ROW_SUB, LANES), lambda i, *_: (i, 0))],
            out_specs=pl.BlockSpec(memory_space=pl.ANY),
            scratch_shapes=[pltpu.VMEM((EXP_TM * ROW_SUB, LANES), F32), pltpu.SemaphoreType.DMA(()),
                            pltpu.SemaphoreType.DMA(())],
        ),
        out_shape=jax.ShapeDtypeStruct((n_sorted_rows * ROW_SUB, LANES), F32),
        compiler_params=pltpu.CompilerParams(
            dimension_semantics=("arbitrary",), vmem_limit_bytes=VMEM_LIMIT),
        name="dispatch",
    )(gap_start, gap_len, pos, hn_tiles)


def _experts_kernel(tile_e_ref, n_tiles_ref, xs_ref, wg_ref, wu_ref, wd_ref, y_ref, wg16, wu16, wd16):
    i = pl.program_id(0)
    tm = xs_ref.shape[0] // ROW_SUB

    @pl.when((i == 0) | (tile_e_ref[i] != tile_e_ref[jnp.maximum(i - 1, 0)]))
    def _():
        wg16[...] = wg_ref[0].astype(BF16)
        wu16[...] = wu_ref[0].astype(BF16)
        wd16[...] = wd_ref[0].astype(BF16)

    @pl.when(i < n_tiles_ref[0])
    def _():
        x = _from_row_tiles(xs_ref, 0, tm).astype(BF16)
        a = jax.nn.silu(_dot(x, wg16[...])) * _dot(x, wu16[...])
        _to_row_tiles(y_ref, _dot(a.astype(BF16), wd16[...]))

    @pl.when(i >= n_tiles_ref[0])
    def _():
        y_ref[...] = jnp.zeros_like(y_ref)


def _experts_call(tile_e, n_tiles, xs, wg, wu, wd):
    d = wg.shape[1]
    tm = EXP_TM
    nt = xs.shape[0] // (tm * ROW_SUB)
    used = lambda i, n: jnp.minimum(i, n[0] - 1)
    w_spec = lambda *shape: pl.BlockSpec((1,) + shape, lambda i, te, n: (te[used(i, n)], 0, 0))
    return pl.pallas_call(
        _experts_kernel,
        grid_spec=pltpu.PrefetchScalarGridSpec(
            num_scalar_prefetch=2,
            grid=(nt,),
            in_specs=[pl.BlockSpec((tm * ROW_SUB, LANES), lambda i, te, n: (used(i, n), 0)),
                      w_spec(d, D_FF_EXP), w_spec(d, D_FF_EXP), w_spec(D_FF_EXP, d)],
            out_specs=pl.BlockSpec((tm * ROW_SUB, LANES), lambda i, te, n: (i, 0)),
            scratch_shapes=[pltpu.VMEM((d, D_FF_EXP), BF16), pltpu.VMEM((d, D_FF_EXP), BF16),
                            pltpu.VMEM((D_FF_EXP, d), BF16)],
        ),
        out_shape=jax.ShapeDtypeStruct(xs.shape, F32),
        compiler_params=pltpu.CompilerParams(
            dimension_semantics=("arbitrary",), vmem_limit_bytes=VMEM_LIMIT),
        name="experts",
    )(tile_e, n_tiles, xs, wg, wu, wd)


def _ple_kernel(cur_ref, nxt_ref, ys_hbm, h_ref, route_ref, p_ref, wple_ref, gple_ref, wpg_ref,
                o_ref, ybuf0, ybuf1, sem):
    tm = h_ref.shape[0]
    i = pl.program_id(0)
    n_steps = pl.num_programs(0)
    bufs = (ybuf0, ybuf1)

    def start_gather(pos_ref, slot):
        for r in range(2 * tm):
            pltpu.make_async_copy(_tiles_at(ys_hbm, pos_ref[r // tm, r % tm]), _tiles_at(bufs[slot], r),
                                  sem.at[slot]).start(priority=r % 2)

    @pl.when(i == 0)
    def _():
        start_gather(cur_ref, 0)

    def step(slot):
        _wait_tiles(ys_hbm, bufs[slot], 2 * tm, sem.at[slot])
        start_gather(nxt_ref, 1 - slot)
        wts = route_ref[...]
        h = (h_ref[...] + wts[:, 0:1] * _from_row_tiles(bufs[slot], 0, tm)
             + wts[:, 1:2] * _from_row_tiles(bufs[slot], tm, tm))
        gate = jax.nn.sigmoid(_dot(_rms(h, gple_ref[...]).astype(BF16), wpg_ref[...]))
        o_ref[...] = h + _dot(p_ref[...].astype(BF16), wple_ref[...]) * gate

    for slot in range(2):
        pl.when(lax.rem(i, 2) == slot)(functools.partial(step, slot))

    for slot in range(2):
        pl.when((i == n_steps - 1) & (lax.rem(n_steps, 2) == slot))(
            functools.partial(_wait_tiles, ys_hbm, bufs[slot], 2 * tm, sem.at[slot]))


def _ple_call(pos, ys, h, route, p2, wple, gple, wpg):
    t, d = h.shape
    tm = TOK_TM
    n_steps = t // tm
    n_rows = 2 * tm
    idx_spec = lambda f: pl.BlockSpec((ROUTE_FIELDS, tm), lambda i: (0, f(i)), memory_space=pltpu.SMEM)
    const = lambda *shape: pl.BlockSpec(shape, lambda i: (0,) * len(shape))
    tok = lambda w: pl.BlockSpec((tm, w), lambda i: (i, 0))
    return pl.pallas_call(
        _ple_kernel,
        grid=(n_steps,),
        in_specs=[idx_spec(lambda i: i), idx_spec(lambda i: lax.rem(i + 1, n_steps)),
                  pl.BlockSpec(memory_space=pl.ANY),
                  tok(d), tok(LANES), tok(D_PLE), const(D_PLE, d), const(1, d), const(d, d)],
        out_specs=tok(d),
        out_shape=jax.ShapeDtypeStruct((t, d), F32),
        scratch_shapes=[pltpu.VMEM((n_rows * ROW_SUB, LANES), F32), pltpu.VMEM((n_rows * ROW_SUB, LANES), F32),
                        pltpu.SemaphoreType.DMA((2,))],
        compiler_params=pltpu.CompilerParams(
            dimension_semantics=("arbitrary",), vmem_limit_bytes=VMEM_LIMIT),
        name="ple",
    )(pos, pos, ys, h, route, p2, wple, gple, wpg)


def _segment_tables(counts, t):
    tm = EXP_TM
    nt = 2 * t // tm + N_EXPERTS
    padded = ((counts + tm - 1) // tm) * tm
    seg_end = jnp.cumsum(padded)
    seg_start = seg_end - padded
    gap_len = jnp.concatenate([padded - counts, nt * tm - seg_end[-1:]])
    gap_start = jnp.concatenate([seg_start + counts, seg_end[-1:]])
    tile_start = jnp.arange(nt, dtype=jnp.int32) * tm
    tile_e = jnp.minimum(jnp.sum(tile_start[:, None] >= seg_end[None, :], axis=1), N_EXPERTS - 1)
    n_tiles = (seg_end[-1] // tm).reshape(1)
    i32 = lambda a: a.astype(jnp.int32)
    return i32(seg_start), i32(gap_start), i32(gap_len), i32(tile_e), i32(n_tiles), nt * tm


def _layer(h, p_i, g_mix, w_in, g_v_a, w_s, b_s, g_q, g_k, g_out_a, g_out_b, w_out, g_ffn, w_group,
           w_expert, w_gate_e, w_up_e, w_down_e, g_ple, w_ple, w_ple_gate):
    b, s, d = h.shape
    t = b * s
    row = lambda g: g.reshape(1, -1).astype(F32)
    wuv = w_in[:, :2 * D_A].astype(BF16)
    wqt = w_in[:, 2 * D_A:2 * D_A + D_B].T.astype(BF16)
    wk = w_in[:, 2 * D_A + D_B:2 * D_A + 2 * D_B].astype(BF16)
    wvt = w_in[:, 2 * D_A + 2 * D_B:].T.astype(BF16)
    bs = jnp.broadcast_to(b_s[:, :, None], (A_GROUPS, A_CHUNK, A_DH)).astype(F32)
    gk = row(jnp.tile(g_k, B_HEADS))
    gq = (jnp.tile(g_q, B_HEADS) * (1.0 / float(B_DH) ** 0.5)).reshape(D_B, 1).astype(F32)
    head = jnp.arange(D_B) // B_DH
    bd = jnp.where(head[:, None] == head[None, :], 1.0 / B_DH, 0.0).astype(BF16)
    w_rt = jnp.zeros((ROUTER_ROWS, d), F32)
    w_rt = w_rt.at[:N_GROUPS].set(w_group.T).at[N_GROUPS:N_GROUPS + N_EXPERTS].set(w_expert.T)
    wrt_hi, wrt_lo = _split_bf16(w_rt)

    ya, k, qt, vt = _proj_call(h, row(g_mix), wuv, wk, wqt, wvt, row(g_v_a), w_s.astype(F32), bs,
                               row(g_out_a), gk, gq, bd)
    bound = (BOUND_MARGIN * LOG2E * float(B_DH) ** 0.5) * jnp.max(jnp.abs(g_q)) * jnp.max(jnp.abs(g_k))
    yb = _moba_call(bound.reshape(1, 1).astype(F32), qt, k, vt, row(g_out_b))
    h1, hn_tiles, route, route_i, counts = _outmix_call(
        h.reshape(t, d), ya.reshape(t, D_A), yb.reshape(t, D_B), w_out.astype(BF16), row(g_ffn), wrt_hi, wrt_lo)
    counts = counts[N_GROUPS:N_GROUPS + N_EXPERTS, 0].astype(jnp.int32)
    seg_start, gap_start, gap_len, tile_e, n_tiles, n_sorted_rows = _segment_tables(counts, t)
    pos = _slots_call(seg_start, route_i)
    xs = _dispatch_call(gap_start, gap_len, pos, hn_tiles, n_sorted_rows)
    ys = _experts_call(tile_e, n_tiles, xs, w_gate_e, w_up_e, w_down_e)
    out = _ple_call(pos, ys, h1, route, p_i.reshape(t, D_PLE), w_ple.astype(BF16), row(g_ple),
                    w_ple_gate.astype(BF16))
    return out.reshape(b, s, d)


def kernel(x, p, g_mix, w_in, g_v_a, w_s, b_s, g_q, g_k, g_out_a, g_out_b, w_out, g_ffn, w_group, w_expert,
           w_gate_e, w_up_e, w_down_e, g_ple, w_ple, w_ple_gate):
    params = (g_mix, w_in, g_v_a, w_s, b_s, g_q, g_k, g_out_a, g_out_b, w_out, g_ffn, w_group, w_expert,
              w_gate_e, w_up_e, w_down_e, g_ple, w_ple, w_ple_gate)
    h = x
    for i in range(p.shape[0]):
        h = _layer(h, p[i], *(w[i] for w in params))
    return h
```

```python
import functools

import jax
import jax.numpy as jnp
from jax import lax
from jax.experimental import pallas as pl
from jax.experimental.pallas import tpu as pltpu

F32 = jnp.float32
BF16 = jnp.bfloat16

D_MODEL = 1024
D_A = 512
A_GROUPS = 4
A_DH = D_A // A_GROUPS
A_CHUNK = 128
D_B = 512
B_HEADS = 8
B_DH = D_B // B_HEADS
B_BLOCK = 256
B_TOPK = 3
D_PLE = 256
N_GROUPS = 4
EXP_PER_GROUP = 8
N_EXPERTS = N_GROUPS * EXP_PER_GROUP
D_FF_EXP = 256
EPS = 1e-6
NEG = -1e30
LOG2E = 1.4426950408889634

LANES = 128
HEAD_PAIR = 2 * B_DH
N_PAIRS = B_HEADS // 2
ROUTER_ROWS = 48
ROUTE_FIELDS = 8
VMEM_LIMIT = 56 * 1024 * 1024

PROJ_TM = 512
OUT_TM = 512
EXP_TM = 512
MOBA_STEPS_PER_TRIP = 4
BOUND_MARGIN = 1.02
MIN_SHIFTED_SUM = 2.0 ** -64
TOK_TM = 1024
ROW_SUB = D_MODEL // LANES


def _rms(x, g):
    return x * lax.rsqrt(jnp.mean(x * x, axis=-1, keepdims=True) + EPS) * g


def _dot(a, b):
    return jnp.dot(a, b, preferred_element_type=F32)


def _dot_nt(a, b):
    return lax.dot_general(a, b, (((1,), (1,)), ((), ())), preferred_element_type=F32)


def _split_bf16(x):
    hi = x.astype(BF16)
    lo = (x - hi.astype(F32)).astype(BF16)
    return hi, lo


def _proj_kernel(x_ref, gmix_ref, wuv_ref, wk_ref, wqt_ref, wvt_ref, gva_ref, ws_ref, bs_ref,
                 goa_ref, gk_ref, gq_ref, bd_ref,
                 ya_ref, k_ref, qt_ref, vt_ref, prod_ref):
    tm = x_ref.shape[1]
    n_chunks = tm // A_CHUNK
    xn = _rms(x_ref[0], gmix_ref[...]).astype(BF16)

    uv = _dot(xn, wuv_ref[...])
    u = jax.nn.gelu(uv[:, :D_A])
    v = jax.nn.gelu(uv[:, D_A:])
    row = lax.broadcasted_iota(jnp.int32, (A_CHUNK, A_CHUNK), 0)
    col = lax.broadcasted_iota(jnp.int32, (A_CHUNK, A_CHUNK), 1)
    causal = col <= row
    for g in range(A_GROUPS):
        gs = slice(g * A_DH, (g + 1) * A_DH)
        vn = _rms(v[:, gs], gva_ref[:, gs]).astype(BF16)
        rhs = jnp.concatenate([vn[c * A_CHUNK:(c + 1) * A_CHUNK] for c in range(n_chunks)], axis=1)
        ws = jnp.where(causal, ws_ref[g], 0.0).astype(BF16)
        mixed = _dot(ws, rhs)
        for c in range(n_chunks):
            cs = slice(c * A_CHUNK, (c + 1) * A_CHUNK)
            prod_ref[cs, gs] = u[cs, gs] * (mixed[:, cs] + bs_ref[g])
    ya_ref[0] = _rms(prod_ref[...], goa_ref[...]).astype(ya_ref.dtype)

    k = _dot(xn, wk_ref[...])
    k2_hi, k2_lo = _split_bf16(k * k)
    k_ms = _dot(k2_hi, bd_ref[...]) + _dot(k2_lo, bd_ref[...])
    k_ref[0] = (k * lax.rsqrt(k_ms + EPS) * gk_ref[...]).astype(k_ref.dtype)

    qt = _dot_nt(wqt_ref[...], xn)
    vt = _dot_nt(wvt_ref[...], xn).astype(vt_ref.dtype)
    qn = []
    for h in range(B_HEADS):
        qh = qt[h * B_DH:(h + 1) * B_DH]
        qn.append(qh * lax.rsqrt(jnp.mean(qh * qh, axis=0, keepdims=True) + EPS))
    qn = (jnp.concatenate(qn, axis=0) * gq_ref[...]).astype(qt_ref.dtype)
    for p in range(N_PAIRS):
        ps = slice(p * HEAD_PAIR, (p + 1) * HEAD_PAIR)
        for j in range(tm // B_BLOCK):
            js = slice(j * B_BLOCK, (j + 1) * B_BLOCK)
            qt_ref[0, p, j] = qn[ps, js]
            vt_ref[0, p, j] = vt[ps, js]


def _proj_call(x, gmix, wuv, wk, wqt, wvt, gva, ws, bs, goa, gk, gq, bd):
    b, s, d = x.shape
    tm = PROJ_TM
    nb = s // B_BLOCK
    jb = tm // B_BLOCK
    const = lambda *shape: pl.BlockSpec(shape, lambda bi, ti: (0,) * len(shape))
    return pl.pallas_call(
        _proj_kernel,
        grid=(b, s // tm),
        in_specs=[
            pl.BlockSpec((1, tm, d), lambda bi, ti: (bi, ti, 0)),
            const(1, d), const(d, 2 * D_A), const(d, D_B), const(D_B, d), const(D_B, d),
            const(1, D_A), const(A_GROUPS, A_CHUNK, A_CHUNK), const(A_GROUPS, A_CHUNK, A_DH),
            const(1, D_A), const(1, D_B), const(D_B, 1), const(D_B, D_B),
        ],
        out_specs=[
            pl.BlockSpec((1, tm, D_A), lambda bi, ti: (bi, ti, 0)),
            pl.BlockSpec((1, tm, D_B), lambda bi, ti: (bi, ti, 0)),
            pl.BlockSpec((1, N_PAIRS, jb, HEAD_PAIR, B_BLOCK), lambda bi, ti: (bi, 0, ti, 0, 0)),
            pl.BlockSpec((1, N_PAIRS, jb, HEAD_PAIR, B_BLOCK), lambda bi, ti: (bi, 0, ti, 0, 0)),
        ],
        out_shape=[
            jax.ShapeDtypeStruct((b, s, D_A), BF16),
            jax.ShapeDtypeStruct((b, s, D_B), BF16),
            jax.ShapeDtypeStruct((b, N_PAIRS, nb, HEAD_PAIR, B_BLOCK), BF16),
            jax.ShapeDtypeStruct((b, N_PAIRS, nb, HEAD_PAIR, B_BLOCK), BF16),
        ],
        scratch_shapes=[pltpu.VMEM((tm, D_A), F32)],
        compiler_params=pltpu.CompilerParams(
            dimension_semantics=("parallel", "parallel"), vmem_limit_bytes=VMEM_LIMIT),
        name="proj",
    )(x, gmix, wuv, wk, wqt, wvt, gva, ws, bs, goa, gk, gq, bd)


def _moba_kernel(bound_ref, qt_ref, k_ref, vt_ref, gob_ref, y_ref, kmean_ref, qz_ref, sel_ref, m_ref, l_ref,
                 acc_ref, stage0_ref, stage1_ref):
    qb = pl.program_id(1)
    nb = k_ref.shape[1] // B_BLOCK
    nbp = sel_ref.shape[1]

    @pl.when(qb == 0)
    def _():
        rows = [jnp.mean(k_ref[0, j * B_BLOCK:(j + 1) * B_BLOCK, :].astype(F32), axis=0, keepdims=True)
                for j in range(nb)]
        rows += [jnp.zeros((1, D_B), F32)] * (nbp - nb)
        km = jnp.concatenate(rows, axis=0)
        hi = km.astype(BF16)
        rest = km - hi.astype(F32)
        mid = rest.astype(BF16)
        lo = (rest - mid.astype(F32)).astype(BF16)
        for p in range(N_PAIRS):
            ps = slice(p * HEAD_PAIR, (p + 1) * HEAD_PAIR)
            kmean_ref[p] = jnp.concatenate([hi[:, ps], mid[:, ps], lo[:, ps]], axis=0)

    head_row = lax.broadcasted_iota(jnp.int32, (HEAD_PAIR, B_BLOCK), 0) // B_DH
    blk = lax.broadcasted_iota(jnp.int32, (nbp, B_BLOCK), 0)
    past = blk < qb

    for h in range(B_HEADS):
        p, hh = divmod(h, 2)
        q_pair = qt_ref[0, p, 0]
        qz = jnp.where(head_row == hh, q_pair, jnp.zeros_like(q_pair))
        qz_ref[h] = qz
        g3 = _dot(kmean_ref[p], qz)
        gate = (g3[:nbp] + g3[nbp:2 * nbp] + g3[2 * nbp:]) * float(B_DH) ** 0.5
        gate = jnp.where(past, gate, NEG)
        left = gate
        taken = blk < 0
        for _ in range(min(B_TOPK, nb)):
            top = jnp.max(left, axis=0, keepdims=True)
            first = jnp.min(jnp.where(left == top, blk, nbp), axis=0, keepdims=True)
            taken = taken | (blk == first)
            left = jnp.where(blk == first, -jnp.inf, left)
        chosen = taken & (gate > NEG * 0.5)
        sel_ref[h] = jnp.where(chosen, -bound_ref[0, 0], NEG)

    def scores(h, j):
        p = h // 2
        kj = k_ref[0, pl.ds(pl.multiple_of(j * B_BLOCK, B_BLOCK), B_BLOCK), p * HEAD_PAIR:(p + 1) * HEAD_PAIR]
        return _dot(kj, qz_ref[h])

    def values(h, j):
        p, hh = divmod(h, 2)
        return vt_ref[0, p, j, hh * B_DH:(hh + 1) * B_DH, :]

    key_pos = lax.broadcasted_iota(jnp.int32, (B_BLOCK, B_BLOCK), 0)
    q_pos = lax.broadcasted_iota(jnp.int32, (B_BLOCK, B_BLOCK), 1)
    causal = key_pos <= q_pos

    bound = bound_ref[0, 0]
    stage = (stage0_ref, stage1_ref)
    own_shift = jnp.where(causal, -bound, NEG)

    def stage_scores(h, block, shift, parity):
        stage[parity][h] = scores(h, block) * LOG2E + shift

    def consume(h, s, parity):
        pr = jnp.exp2(stage[parity][h])
        l_ref[h] += jnp.sum(pr, axis=0, keepdims=True)
        acc_ref[h] += _dot(values(h, jnp.where(s == 0, qb, s - 1)), pr.astype(BF16))

    def full_step(s, parity):
        for h in range(B_HEADS):
            stage_scores(h, s - 1, sel_ref[h, pl.ds(s - 1, 1), :], parity)
            consume(h, s - 1, 1 - parity)

    for h in range(B_HEADS):
        stage_scores(h, qb, own_shift, 0)
        l_ref[h] = jnp.zeros((1, B_BLOCK), F32)
        acc_ref[h] = jnp.zeros((B_DH, B_BLOCK), F32)

    def trip(u, carry):
        for v in range(MOBA_STEPS_PER_TRIP):
            full_step(MOBA_STEPS_PER_TRIP * u + v + 1, (v + 1) % 2)
        return carry

    lax.fori_loop(0, qb // MOBA_STEPS_PER_TRIP, trip, 0)
    done = (qb // MOBA_STEPS_PER_TRIP) * MOBA_STEPS_PER_TRIP
    for v in range(MOBA_STEPS_PER_TRIP - 1):
        pl.when(qb - done > v)(functools.partial(full_step, done + v + 1, (v + 1) % 2))

    for parity in range(2):
        @pl.when(lax.rem(qb, 2) == parity)
        def _(parity=parity):
            for h in range(B_HEADS):
                consume(h, qb, parity)

    smallest = jnp.min(jnp.concatenate([l_ref[h] for h in range(B_HEADS)], axis=0))

    @pl.when(jnp.logical_not(smallest > MIN_SHIFTED_SUM))
    def _():
        def online(h, j, own, first):
            visible = causal if own else sel_ref[h, pl.ds(j, 1), :] > NEG * 0.5
            sc = jnp.where(visible, scores(h, j) * LOG2E, NEG)
            m_tile = jnp.max(sc, axis=0, keepdims=True)
            m_new = m_tile if first else jnp.maximum(m_ref[h], m_tile)
            pr = jnp.exp2(sc - m_new)
            l_new, acc_new = jnp.sum(pr, axis=0, keepdims=True), _dot(values(h, j), pr.astype(BF16))
            if not first:
                alpha = jnp.exp2(m_ref[h] - m_new)
                l_new, acc_new = alpha * l_ref[h] + l_new, alpha * acc_ref[h] + acc_new
            m_ref[h], l_ref[h], acc_ref[h] = m_new, l_new, acc_new

        for h in range(B_HEADS):
            online(h, qb, True, True)

        def past(j, carry):
            for h in range(B_HEADS):
                online(h, j, False, False)
            return carry

        lax.fori_loop(0, qb, past, 0)

    yt = jnp.concatenate([acc_ref[h] / l_ref[h] for h in range(B_HEADS)], axis=0)
    yt = yt * lax.rsqrt(jnp.mean(yt * yt, axis=0, keepdims=True) + EPS)
    y_ref[0] = (yt.T * gob_ref[...]).astype(y_ref.dtype)


def _moba_call(bound, qt, k, vt, gob):
    b, s, _ = k.shape
    nb = s // B_BLOCK
    nbp = -(-nb // 8) * 8
    return pl.pallas_call(
        _moba_kernel,
        grid=(b, nb),
        in_specs=[
            pl.BlockSpec(memory_space=pltpu.SMEM),
            pl.BlockSpec((1, N_PAIRS, 1, HEAD_PAIR, B_BLOCK), lambda bi, qi: (bi, 0, qi, 0, 0)),
            pl.BlockSpec((1, s, D_B), lambda bi, qi: (bi, 0, 0)),
            pl.BlockSpec((1, N_PAIRS, nb, HEAD_PAIR, B_BLOCK), lambda bi, qi: (bi, 0, 0, 0, 0)),
            pl.BlockSpec((1, D_B), lambda bi, qi: (0, 0)),
        ],
        out_specs=pl.BlockSpec((1, B_BLOCK, D_B), lambda bi, qi: (bi, qi, 0)),
        out_shape=jax.ShapeDtypeStruct((b, s, D_B), BF16),
        scratch_shapes=[
            pltpu.VMEM((N_PAIRS, 3 * nbp, HEAD_PAIR), BF16),
            pltpu.VMEM((B_HEADS, HEAD_PAIR, B_BLOCK), BF16),
            pltpu.VMEM((B_HEADS, nbp, B_BLOCK), F32),
            pltpu.VMEM((B_HEADS, 1, B_BLOCK), F32),
            pltpu.VMEM((B_HEADS, 1, B_BLOCK), F32),
            pltpu.VMEM((B_HEADS, B_DH, B_BLOCK), F32),
            pltpu.VMEM((B_HEADS, B_BLOCK, B_BLOCK), F32),
            pltpu.VMEM((B_HEADS, B_BLOCK, B_BLOCK), F32),
        ],
        compiler_params=pltpu.CompilerParams(
            dimension_semantics=("parallel", "arbitrary"), vmem_limit_bytes=VMEM_LIMIT),
        name="moba",
    )(bound, qt, k, vt, gob)


def _outmix_kernel(x_ref, ya_ref, yb_ref, wout_ref, gffn_ref, wrt_hi_ref, wrt_lo_ref,
                   h_ref, hn_ref, route_ref, route_i_ref, counts_ref, cnt_ref):
    h = x_ref[...] + _dot(ya_ref[...], wout_ref[:D_A]) + _dot(yb_ref[...], wout_ref[D_A:])
    h_ref[...] = h
    hn = _rms(h, gffn_ref[...])
    _to_row_tiles(hn_ref, hn)
    tm = hn.shape[0]

    hn_hi, hn_lo = _split_bf16(hn)
    lg = _dot_nt(wrt_hi_ref[...], hn_hi) + (_dot_nt(wrt_hi_ref[...], hn_lo) + _dot_nt(wrt_lo_ref[...], hn_hi))
    row = lax.broadcasted_iota(jnp.int32, lg.shape, 0)

    def first_row(mask):
        return jnp.min(jnp.where(mask, row, ROUTER_ROWS), axis=0, keepdims=True)

    is_g = row < N_GROUPS
    g_max = jnp.max(jnp.where(is_g, lg, -jnp.inf), axis=0, keepdims=True)
    p_g = 1.0 / jnp.sum(jnp.where(is_g, jnp.exp(lg - g_max), 0.0), axis=0, keepdims=True)
    g_sel = first_row(is_g & (lg == g_max))
    e_row = row - N_GROUPS
    in_grp = (e_row >= 0) & (e_row < N_EXPERTS) & (lax.shift_right_arithmetic(e_row, 3) == g_sel)
    l1 = jnp.max(jnp.where(in_grp, lg, -jnp.inf), axis=0, keepdims=True)
    i1 = first_row(in_grp & (lg == l1))
    rest = in_grp & (row != i1)
    l2 = jnp.max(jnp.where(rest, lg, -jnp.inf), axis=0, keepdims=True)
    i2 = first_row(rest & (lg == l2))
    e2 = jnp.exp(l2 - l1)
    w1 = 1.0 / (1.0 + e2)
    w2 = e2 / (1.0 + e2)

    @pl.when(pl.program_id(0) == 0)
    def _():
        cnt_ref[...] = jnp.zeros_like(cnt_ref)

    picked = jnp.where((row == i1) | (row == i2), 1.0, 0.0)
    tok_r = lax.broadcasted_iota(jnp.int32, (tm, tm), 0)
    tok_c = lax.broadcasted_iota(jnp.int32, (tm, tm), 1)
    earlier = jnp.where(tok_r < tok_c, 1.0, 0.0).astype(BF16)
    before = _dot(picked.astype(BF16), earlier) + cnt_ref[:, 0:1]
    r1 = jnp.sum(jnp.where(row == i1, before, 0.0), axis=0, keepdims=True)
    r2 = jnp.sum(jnp.where(row == i2, before, 0.0), axis=0, keepdims=True)
    cnt_ref[...] += jnp.sum(picked, axis=1, keepdims=True)
    counts_ref[...] = cnt_ref[...]

    e1, e2x = i1 - N_GROUPS, i2 - N_GROUPS
    route_i_ref[...] = jnp.concatenate(
        [e1, e2x, r1.astype(jnp.int32), r2.astype(jnp.int32), jnp.zeros((ROUTE_FIELDS - 4, tm), jnp.int32)], axis=0)
    wts = jnp.concatenate([p_g * w1, p_g * w2, jnp.zeros((LANES - 2, tm), F32)], axis=0)
    route_ref[...] = wts.T


def _outmix_call(x2, ya2, yb2, wout, gffn, wrt_hi, wrt_lo):
    t, d = x2.shape
    tm = OUT_TM
    const = lambda *shape: pl.BlockSpec(shape, lambda ti: (0,) * len(shape))
    tok = lambda w: pl.BlockSpec((tm, w), lambda ti: (ti, 0))
    return pl.pallas_call(
        _outmix_kernel,
        grid=(t // tm,),
        in_specs=[tok(d), tok(D_A), tok(D_B), const(d, d), const(1, d),
                  const(ROUTER_ROWS, d), const(ROUTER_ROWS, d)],
        out_specs=[tok(d), pl.BlockSpec((tm * ROW_SUB, LANES), lambda ti: (ti, 0)), tok(LANES),
                   pl.BlockSpec((ROUTE_FIELDS, tm), lambda ti: (0, ti)), const(ROUTER_ROWS, LANES)],
        out_shape=[jax.ShapeDtypeStruct((t, d), F32), jax.ShapeDtypeStruct((t * ROW_SUB, LANES), F32),
                   jax.ShapeDtypeStruct((t, LANES), F32),
                   jax.ShapeDtypeStruct((ROUTE_FIELDS, t), jnp.int32),
                   jax.ShapeDtypeStruct((ROUTER_ROWS, LANES), F32)],
        scratch_shapes=[pltpu.VMEM((ROUTER_ROWS, LANES), F32)],
        compiler_params=pltpu.CompilerParams(
            dimension_semantics=("arbitrary",), vmem_limit_bytes=VMEM_LIMIT),
        name="outmix",
    )(x2, ya2, yb2, wout, gffn, wrt_hi, wrt_lo)


def _to_row_tiles(ref, x):
    n = x.shape[0]
    for s in range(ROW_SUB):
        ref[pl.ds(s, n, stride=ROW_SUB), :] = x[:, s * LANES:(s + 1) * LANES]


def _from_row_tiles(ref, first, n):
    return jnp.concatenate(
        [ref[pl.ds(first * ROW_SUB + s, n, stride=ROW_SUB), :] for s in range(ROW_SUB)], axis=1)


def _tiles_at(ref, row, n=1):
    return ref.at[pl.ds(pl.multiple_of(row * ROW_SUB, ROW_SUB), n * ROW_SUB), :]


def _wait_tiles(hbm, vmem_ref, n_tiles, sem):
    cap = vmem_ref.shape[0] // ROW_SUB
    while n_tiles > 0:
        n = min(n_tiles, cap)
        pltpu.make_async_copy(hbm.at[pl.ds(0, n * ROW_SUB), :], vmem_ref.at[pl.ds(0, n * ROW_SUB), :], sem).wait()
        n_tiles -= n


def _slots_kernel(seg_start_ref, route_i_ref, pos_ref):
    experts = route_i_ref[0:2, :]
    start = jnp.zeros_like(experts)
    for e in range(N_EXPERTS):
        start = jnp.where(experts == e, seg_start_ref[e], start)
    pos_ref[...] = jnp.zeros_like(pos_ref)
    pos_ref[0:2, :] = start + route_i_ref[2:4, :]


def _slots_call(seg_start, route_i):
    return pl.pallas_call(
        _slots_kernel,
        grid_spec=pltpu.PrefetchScalarGridSpec(
            num_scalar_prefetch=1,
            grid=(1,),
            in_specs=[pl.BlockSpec(route_i.shape, lambda i, s: (0, 0))],
            out_specs=pl.BlockSpec(route_i.shape, lambda i, s: (0, 0)),
        ),
        out_shape=jax.ShapeDtypeStruct(route_i.shape, jnp.int32),
        name="slots",
    )(seg_start, route_i)


def _dispatch_kernel(gap_start_ref, gap_len_ref, pos_ref, hn_ref, xs_hbm, zero_ref, sem, fill_sem):
    tm = hn_ref.shape[0] // ROW_SUB

    @pl.when(pl.program_id(0) == 0)
    def _():
        zero_ref[...] = jnp.zeros_like(zero_ref)
        fills = []
        for g in range(N_EXPERTS):
            row, n = gap_start_ref[g], gap_len_ref[g]
            size = EXP_TM // 2
            while size >= 1:
                fills.append(((n & size) != 0, row, size))
                row = row + (n & size)
                size //= 2
        for tile in range(N_EXPERTS):
            fills.append((tile * EXP_TM < gap_len_ref[N_EXPERTS], gap_start_ref[N_EXPERTS] + tile * EXP_TM, EXP_TM))
        copies = [(cond, pltpu.make_async_copy(_tiles_at(zero_ref, 0, size), _tiles_at(xs_hbm, row, size), fill_sem))
                  for cond, row, size in fills]
        for cond, copy in copies:
            pl.when(cond)(copy.start)
        for cond, copy in copies:
            pl.when(cond)(copy.wait)

    for r in range(2 * tm):
        k, tok = divmod(r, tm)
        pltpu.make_async_copy(_tiles_at(hn_ref, tok), _tiles_at(xs_hbm, pos_ref[k, tok]), sem).start(priority=r % 2)
    _wait_tiles(xs_hbm, hn_ref, 2 * tm, sem)


def _dispatch_call(gap_start, gap_len, pos, hn_tiles, n_sorted_rows):
    tm = TOK_TM
    return pl.pallas_call(
        _dispatch_kernel,
        grid_spec=pltpu.PrefetchScalarGridSpec(
            num_scalar_prefetch=2,
            grid=(pos.shape[1] // tm,),
            in_specs=[pl.BlockSpec((ROUTE_FIELDS, tm), lambda i, *_: (0, i), memory_space=pltpu.SMEM),
                      pl.BlockSpec((tm * ROW_SUB, LANES), lambda i, *_: (i, 0))],
            out_specs=pl.BlockSpec(memory_space=pl.ANY),
            scratch_shapes=[pltpu.VMEM((EXP_TM * ROW_SUB, LANES), F32), pltpu.SemaphoreType.DMA(()),
                            pltpu.SemaphoreType.DMA(())],
        ),
        out_shape=jax.ShapeDtypeStruct((n_sorted_rows * ROW_SUB, LANES), F32),
        compiler_params=pltpu.CompilerParams(
            dimension_semantics=("arbitrary",), vmem_limit_bytes=VMEM_LIMIT),
        name="dispatch",
    )(gap_start, gap_len, pos, hn_tiles)


def _experts_kernel(tile_e_ref, n_tiles_ref, xs_ref, wg_ref, wu_ref, wd_ref, y_ref, wg16, wu16, wd16):
    i = pl.program_id(0)
    tm = xs_ref.shape[0] // ROW_SUB

    @pl.when((i == 0) | (tile_e_ref[i] != tile_e_ref[jnp.maximum(i - 1, 0)]))
    def _():
        wg16[...] = wg_ref[0].astype(BF16)
        wu16[...] = wu_ref[0].astype(BF16)
        wd16[...] = wd_ref[0].astype(BF16)

    @pl.when(i < n_tiles_ref[0])
    def _():
        x = _from_row_tiles(xs_ref, 0, tm).astype(BF16)
        a = jax.nn.silu(_dot(x, wg16[...])) * _dot(x, wu16[...])
        _to_row_tiles(y_ref, _dot(a.astype(BF16), wd16[...]))

    @pl.when(i >= n_tiles_ref[0])
    def _():
        y_ref[...] = jnp.zeros_like(y_ref)


def _experts_call(tile_e, n_tiles, xs, wg, wu, wd):
    d = wg.shape[1]
    tm = EXP_TM
    nt = xs.shape[0] // (tm * ROW_SUB)
    used = lambda i, n: jnp.minimum(i, n[0] - 1)
    w_spec = lambda *shape: pl.BlockSpec((1,) + shape, lambda i, te, n: (te[used(i, n)], 0, 0))
    return pl.pallas_call(
        _experts_kernel,
        grid_spec=pltpu.PrefetchScalarGridSpec(
            num_scalar_prefetch=2,
            grid=(nt,),
            in_specs=[pl.BlockSpec((tm * ROW_SUB, LANES), lambda i, te, n: (used(i, n), 0)),
                      w_spec(d, D_FF_EXP), w_spec(d, D_FF_EXP), w_spec(D_FF_EXP, d)],
            out_specs=pl.BlockSpec((tm * ROW_SUB, LANES), lambda i, te, n: (i, 0)),
            scratch_shapes=[pltpu.VMEM((d, D_FF_EXP), BF16), pltpu.VMEM((d, D_FF_EXP), BF16),
                            pltpu.VMEM((D_FF_EXP, d), BF16)],
        ),
        out_shape=jax.ShapeDtypeStruct(xs.shape, F32),
        compiler_params=pltpu.CompilerParams(
            dimension_semantics=("arbitrary",), vmem_limit_bytes=VMEM_LIMIT),
        name="experts",
    )(tile_e, n_tiles, xs, wg, wu, wd)


def _ple_kernel(cur_ref, nxt_ref, ys_hbm, h_ref, route_ref, p_ref, wple_ref, gple_ref, wpg_ref,
                o_ref, ybuf0, ybuf1, sem):
    tm = h_ref.shape[0]
    i = pl.program_id(0)
    n_steps = pl.num_programs(0)
    bufs = (ybuf0, ybuf1)

    def start_gather(pos_ref, slot):
        for r in range(2 * tm):
            pltpu.make_async_copy(_tiles_at(ys_hbm, pos_ref[r // tm, r % tm]), _tiles_at(bufs[slot], r),
                                  sem.at[slot]).start(priority=r % 2)

    @pl.when(i == 0)
    def _():
        start_gather(cur_ref, 0)

    def step(slot):
        _wait_tiles(ys_hbm, bufs[slot], 2 * tm, sem.at[slot])
        start_gather(nxt_ref, 1 - slot)
        wts = route_ref[...]
        h = (h_ref[...] + wts[:, 0:1] * _from_row_tiles(bufs[slot], 0, tm)
             + wts[:, 1:2] * _from_row_tiles(bufs[slot], tm, tm))
        gate = jax.nn.sigmoid(_dot(_rms(h, gple_ref[...]).astype(BF16), wpg_ref[...]))
        o_ref[...] = h + _dot(p_ref[...].astype(BF16), wple_ref[...]) * gate

    for slot in range(2):
        pl.when(lax.rem(i, 2) == slot)(functools.partial(step, slot))

    for slot in range(2):
        pl.when((i == n_steps - 1) & (lax.rem(n_steps, 2) == slot))(
            functools.partial(_wait_tiles, ys_hbm, bufs[slot], 2 * tm, sem.at[slot]))


def _ple_call(pos, ys, h, route, p2, wple, gple, wpg):
    t, d = h.shape
    tm = TOK_TM
    n_steps = t // tm
    n_rows = 2 * tm
    idx_spec = lambda f: pl.BlockSpec((ROUTE_FIELDS, tm), lambda i: (0, f(i)), memory_space=pltpu.SMEM)
    const = lambda *shape: pl.BlockSpec(shape, lambda i: (0,) * len(shape))
    tok = lambda w: pl.BlockSpec((tm, w), lambda i: (i, 0))
    return pl.pallas_call(
        _ple_kernel,
        grid=(n_steps,),
        in_specs=[idx_spec(lambda i: i), idx_spec(lambda i: lax.rem(i + 1, n_steps)),
                  pl.BlockSpec(memory_space=pl.ANY),
                  tok(d), tok(LANES), tok(D_PLE), const(D_PLE, d), const(1, d), const(d, d)],
        out_specs=tok(d),
        out_shape=jax.ShapeDtypeStruct((t, d), F32),
        scratch_shapes=[pltpu.VMEM((n_rows * ROW_SUB, LANES), F32), pltpu.VMEM((n_rows * ROW_SUB, LANES), F32),
                        pltpu.SemaphoreType.DMA((2,))],
        compiler_params=pltpu.CompilerParams(
            dimension_semantics=("arbitrary",), vmem_limit_bytes=VMEM_LIMIT),
        name="ple",
    )(pos, pos, ys, h, route, p2, wple, gple, wpg)


def _segment_tables(counts, t):
    tm = EXP_TM
    nt = 2 * t // tm + N_EXPERTS
    padded = ((counts + tm - 1) // tm) * tm
    seg_end = jnp.cumsum(padded)
    seg_start = seg_end - padded
    gap_len = jnp.concatenate([padded - counts, nt * tm - seg_end[-1:]])
    gap_start = jnp.concatenate([seg_start + counts, seg_end[-1:]])
    tile_start = jnp.arange(nt, dtype=jnp.int32) * tm
    tile_e = jnp.minimum(jnp.sum(tile_start[:, None] >= seg_end[None, :], axis=1), N_EXPERTS - 1)
    n_tiles = (seg_end[-1] // tm).reshape(1)
    i32 = lambda a: a.astype(jnp.int32)
    return i32(seg_start), i32(gap_start), i32(gap_len), i32(tile_e), i32(n_tiles), nt * tm


def _layer(h, p_i, g_mix, w_in, g_v_a, w_s, b_s, g_q, g_k, g_out_a, g_out_b, w_out, g_ffn, w_group,
           w_expert, w_gate_e, w_up_e, w_down_e, g_ple, w_ple, w_ple_gate):
    b, s, d = h.shape
    t = b * s
    row = lambda g: g.reshape(1, -1).astype(F32)
    wuv = w_in[:, :2 * D_A].astype(BF16)
    wqt = w_in[:, 2 * D_A:2 * D_A + D_B].T.astype(BF16)
    wk = w_in[:, 2 * D_A + D_B:2 * D_A + 2 * D_B].astype(BF16)
    wvt = w_in[:, 2 * D_A + 2 * D_B:].T.astype(BF16)
    bs = jnp.broadcast_to(b_s[:, :, None], (A_GROUPS, A_CHUNK, A_DH)).astype(F32)
    gk = row(jnp.tile(g_k, B_HEADS))
    gq = (jnp.tile(g_q, B_HEADS) * (1.0 / float(B_DH) ** 0.5)).reshape(D_B, 1).astype(F32)
    head = jnp.arange(D_B) // B_DH
    bd = jnp.where(head[:, None] == head[None, :], 1.0 / B_DH, 0.0).astype(BF16)
    w_rt = jnp.zeros((ROUTER_ROWS, d), F32)
    w_rt = w_rt.at[:N_GROUPS].set(w_group.T).at[N_GROUPS:N_GROUPS + N_EXPERTS].set(w_expert.T)
    wrt_hi, wrt_lo = _split_bf16(w_rt)

    ya, k, qt, vt = _proj_call(h, row(g_mix), wuv, wk, wqt, wvt, row(g_v_a), w_s.astype(F32), bs,
                               row(g_out_a), gk, gq, bd)
    bound = (BOUND_MARGIN * LOG2E * float(B_DH) ** 0.5) * jnp.max(jnp.abs(g_q)) * jnp.max(jnp.abs(g_k))
    yb = _moba_call(bound.reshape(1, 1).astype(F32), qt, k, vt, row(g_out_b))
    h1, hn_tiles, route, route_i, counts = _outmix_call(
        h.reshape(t, d), ya.reshape(t, D_A), yb.reshape(t, D_B), w_out.astype(BF16), row(g_ffn), wrt_hi, wrt_lo)
    counts = counts[N_GROUPS:N_GROUPS + N_EXPERTS, 0].astype(jnp.int32)
    seg_start, gap_start, gap_len, tile_e, n_tiles, n_sorted_rows = _segment_tables(counts, t)
    pos = _slots_call(seg_start, route_i)
    xs = _dispatch_call(gap_start, gap_len, pos, hn_tiles, n_sorted_rows)
    ys = _experts_call(tile_e, n_tiles, xs, w_gate_e, w_up_e, w_down_e)
    out = _ple_call(pos, ys, h1, route, p_i.reshape(t, D_PLE), w_ple.astype(BF16), row(g_ple),
                    w_ple_gate.astype(BF16))
    return out.reshape(b, s, d)


def kernel(x, p, g_mix, w_in, g_v_a, w_s, b_s, g_q, g_k, g_out_a, g_out_b, w_out, g_ffn, w_group, w_expert,
           w_gate_e, w_up_e, w_down_e, g_ple, w_ple, w_ple_gate):
    params = (g_mix, w_in, g_v_a, w_s, b_s, g_q, g_k, g_out_a, g_out_b, w_out, g_ffn, w_group, w_expert,
              w_gate_e, w_up_e, w_down_e, g_ple, w_ple, w_ple_gate)
    h = x
    for i in range(p.shape[0]):
        h = _layer(h, p[i], *(w[i] for w in params))
    return h
```

```python
import functools

import jax
import jax.numpy as jnp
from jax import lax
from jax.experimental import pallas as pl
from jax.experimental.pallas import tpu as pltpu

F32 = jnp.float32
BF16 = jnp.bfloat16

D_MODEL = 1024
D_A = 512
A_GROUPS = 4
A_DH = D_A // A_GROUPS
A_CHUNK = 128
D_B = 512
B_HEADS = 8
B_DH = D_B // B_HEADS
B_BLOCK = 256
B_TOPK = 3
D_PLE = 256
N_GROUPS = 4
EXP_PER_GROUP = 8
N_EXPERTS = N_GROUPS * EXP_PER_GROUP
D_FF_EXP = 256
EPS = 1e-6
NEG = -1e30
LOG2E = 1.4426950408889634

LANES = 128
HEAD_PAIR = 2 * B_DH
N_PAIRS = B_HEADS // 2
ROUTER_ROWS = 48
ROUTE_FIELDS = 8
VMEM_LIMIT = 56 * 1024 * 1024

PROJ_TM = 1024
OUT_TM = 1024
EXP_TM = 512
MOBA_STEPS_PER_TRIP = 4
BOUND_MARGIN = 1.02
MIN_SHIFTED_SUM = 2.0 ** -64
TOK_TM = 1024
ROW_SUB = D_MODEL // LANES


def _rms(x, g):
    return x * lax.rsqrt(jnp.mean(x * x, axis=-1, keepdims=True) + EPS) * g


def _dot(a, b):
    return jnp.dot(a, b, preferred_element_type=F32)


def _dot_nt(a, b):
    return lax.dot_general(a, b, (((1,), (1,)), ((), ())), preferred_element_type=F32)


def _split_bf16(x):
    hi = x.astype(BF16)
    lo = (x - hi.astype(F32)).astype(BF16)
    return hi, lo


def _proj_kernel(x_ref, gmix_ref, wuv_ref, wk_ref, wqt_ref, wvt_ref, gva_ref, ws_ref, bs_ref,
                 goa_ref, gk_ref, gq_ref, bd_ref,
                 ya_ref, k_ref, qt_ref, vt_ref, prod_ref):
    tm = x_ref.shape[1]
    n_chunks = tm // A_CHUNK
    xn = _rms(x_ref[0], gmix_ref[...]).astype(BF16)

    uv = _dot(xn, wuv_ref[...])
    u = jax.nn.gelu(uv[:, :D_A])
    v = jax.nn.gelu(uv[:, D_A:])
    row = lax.broadcasted_iota(jnp.int32, (A_CHUNK, A_CHUNK), 0)
    col = lax.broadcasted_iota(jnp.int32, (A_CHUNK, A_CHUNK), 1)
    causal = col <= row
    for g in range(A_GROUPS):
        gs = slice(g * A_DH, (g + 1) * A_DH)
        vn = _rms(v[:, gs], gva_ref[:, gs]).astype(BF16)
        rhs = jnp.concatenate([vn[c * A_CHUNK:(c + 1) * A_CHUNK] for c in range(n_chunks)], axis=1)
        ws = jnp.where(causal, ws_ref[g], 0.0).astype(BF16)
        mixed = _dot(ws, rhs)
        for c in range(n_chunks):
            cs = slice(c * A_CHUNK, (c + 1) * A_CHUNK)
            prod_ref[cs, gs] = u[cs, gs] * (mixed[:, cs] + bs_ref[g])
    ya_ref[0] = _rms(prod_ref[...], goa_ref[...]).astype(ya_ref.dtype)

    k = _dot(xn, wk_ref[...])
    k2_hi, k2_lo = _split_bf16(k * k)
    k_ms = _dot(k2_hi, bd_ref[...]) + _dot(k2_lo, bd_ref[...])
    k_ref[0] = (k * lax.rsqrt(k_ms + EPS) * gk_ref[...]).astype(k_ref.dtype)

    qt = _dot_nt(wqt_ref[...], xn)
    vt = _dot_nt(wvt_ref[...], xn).astype(vt_ref.dtype)
    qn = []
    for h in range(B_HEADS):
        qh = qt[h * B_DH:(h + 1) * B_DH]
        qn.append(qh * lax.rsqrt(jnp.mean(qh * qh, axis=0, keepdims=True) + EPS))
    qn = (jnp.concatenate(qn, axis=0) * gq_ref[...]).astype(qt_ref.dtype)
    for p in range(N_PAIRS):
        ps = slice(p * HEAD_PAIR, (p + 1) * HEAD_PAIR)
        for j in range(tm // B_BLOCK):
            js = slice(j * B_BLOCK, (j + 1) * B_BLOCK)
            qt_ref[0, p, j] = qn[ps, js]
            vt_ref[0, p, j] = vt[ps, js]


def _proj_call(x, gmix, wuv, wk, wqt, wvt, gva, ws, bs, goa, gk, gq, bd):
    b, s, d = x.shape
    tm = PROJ_TM
    nb = s // B_BLOCK
    jb = tm // B_BLOCK
    const = lambda *shape: pl.BlockSpec(shape, lambda bi, ti: (0,) * len(shape))
    return pl.pallas_call(
        _proj_kernel,
        grid=(b, s // tm),
        in_specs=[
            pl.BlockSpec((1, tm, d), lambda bi, ti: (bi, ti, 0)),
            const(1, d), const(d, 2 * D_A), const(d, D_B), const(D_B, d), const(D_B, d),
            const(1, D_A), const(A_GROUPS, A_CHUNK, A_CHUNK), const(A_GROUPS, A_CHUNK, A_DH),
            const(1, D_A), const(1, D_B), const(D_B, 1), const(D_B, D_B),
        ],
        out_specs=[
            pl.BlockSpec((1, tm, D_A), lambda bi, ti: (bi, ti, 0)),
            pl.BlockSpec((1, tm, D_B), lambda bi, ti: (bi, ti, 0)),
            pl.BlockSpec((1, N_PAIRS, jb, HEAD_PAIR, B_BLOCK), lambda bi, ti: (bi, 0, ti, 0, 0)),
            pl.BlockSpec((1, N_PAIRS, jb, HEAD_PAIR, B_BLOCK), lambda bi, ti: (bi, 0, ti, 0, 0)),
        ],
        out_shape=[
            jax.ShapeDtypeStruct((b, s, D_A), BF16),
            jax.ShapeDtypeStruct((b, s, D_B), BF16),
            jax.ShapeDtypeStruct((b, N_PAIRS, nb, HEAD_PAIR, B_BLOCK), BF16),
            jax.ShapeDtypeStruct((b, N_PAIRS, nb, HEAD_PAIR, B_BLOCK), BF16),
        ],
        scratch_shapes=[pltpu.VMEM((tm, D_A), F32)],
        compiler_params=pltpu.CompilerParams(
            dimension_semantics=("parallel", "parallel"), vmem_limit_bytes=VMEM_LIMIT),
        name="proj",
    )(x, gmix, wuv, wk, wqt, wvt, gva, ws, bs, goa, gk, gq, bd)


def _moba_kernel(bound_ref, qt_ref, k_ref, vt_ref, gob_ref, y_ref, kmean_ref, qz_ref, sel_ref, m_ref, l_ref,
                 acc_ref, stage0_ref, stage1_ref):
    qb = pl.program_id(1)
    nb = k_ref.shape[1] // B_BLOCK
    nbp = sel_ref.shape[1]

    @pl.when(qb == 0)
    def _():
        rows = [jnp.mean(k_ref[0, j * B_BLOCK:(j + 1) * B_BLOCK, :].astype(F32), axis=0, keepdims=True)
                for j in range(nb)]
        rows += [jnp.zeros((1, D_B), F32)] * (nbp - nb)
        km = jnp.concatenate(rows, axis=0)
        hi = km.astype(BF16)
        rest = km - hi.astype(F32)
        mid = rest.astype(BF16)
        lo = (rest - mid.astype(F32)).astype(BF16)
        for p in range(N_PAIRS):
            ps = slice(p * HEAD_PAIR, (p + 1) * HEAD_PAIR)
            kmean_ref[p] = jnp.concatenate([hi[:, ps], mid[:, ps], lo[:, ps]], axis=0)

    head_row = lax.broadcasted_iota(jnp.int32, (HEAD_PAIR, B_BLOCK), 0) // B_DH
    blk = lax.broadcasted_iota(jnp.int32, (nbp, B_BLOCK), 0)
    past = blk < qb

    for h in range(B_HEADS):
        p, hh = divmod(h, 2)
        q_pair = qt_ref[0, p, 0]
        qz = jnp.where(head_row == hh, q_pair, jnp.zeros_like(q_pair))
        qz_ref[h] = qz
        g3 = _dot(kmean_ref[p], qz)
        gate = (g3[:nbp] + g3[nbp:2 * nbp] + g3[2 * nbp:]) * float(B_DH) ** 0.5
        gate = jnp.where(past, gate, NEG)
        left = gate
        taken = blk < 0
        for _ in range(min(B_TOPK, nb)):
            top = jnp.max(left, axis=0, keepdims=True)
            first = jnp.min(jnp.where(left == top, blk, nbp), axis=0, keepdims=True)
            taken = taken | (blk == first)
            left = jnp.where(blk == first, -jnp.inf, left)
        chosen = taken & (gate > NEG * 0.5)
        sel_ref[h] = jnp.where(chosen, -bound_ref[0, 0], NEG)

    def scores(h, j):
        p = h // 2
        kj = k_ref[0, pl.ds(pl.multiple_of(j * B_BLOCK, B_BLOCK), B_BLOCK), p * HEAD_PAIR:(p + 1) * HEAD_PAIR]
        return _dot(kj, qz_ref[h])

    def values(h, j):
        p, hh = divmod(h, 2)
        return vt_ref[0, p, j, hh * B_DH:(hh + 1) * B_DH, :]

    key_pos = lax.broadcasted_iota(jnp.int32, (B_BLOCK, B_BLOCK), 0)
    q_pos = lax.broadcasted_iota(jnp.int32, (B_BLOCK, B_BLOCK), 1)
    causal = key_pos <= q_pos

    bound = bound_ref[0, 0]
    stage = (stage0_ref, stage1_ref)
    own_shift = jnp.where(causal, -bound, NEG)

    def stage_scores(h, block, shift, parity):
        stage[parity][h] = scores(h, block) * LOG2E + shift

    def consume(h, s, parity):
        pr = jnp.exp2(stage[parity][h])
        l_ref[h] += jnp.sum(pr, axis=0, keepdims=True)
        acc_ref[h] += _dot(values(h, jnp.where(s == 0, qb, s - 1)), pr.astype(BF16))

    def full_step(s, parity):
        for h in range(B_HEADS):
            stage_scores(h, s - 1, sel_ref[h, pl.ds(s - 1, 1), :], parity)
            consume(h, s - 1, 1 - parity)

    for h in range(B_HEADS):
        stage_scores(h, qb, own_shift, 0)
        l_ref[h] = jnp.zeros((1, B_BLOCK), F32)
        acc_ref[h] = jnp.zeros((B_DH, B_BLOCK), F32)

    def trip(u, carry):
        for v in range(MOBA_STEPS_PER_TRIP):
            full_step(MOBA_STEPS_PER_TRIP * u + v + 1, (v + 1) % 2)
        return carry

    lax.fori_loop(0, qb // MOBA_STEPS_PER_TRIP, trip, 0)
    done = (qb // MOBA_STEPS_PER_TRIP) * MOBA_STEPS_PER_TRIP
    for v in range(MOBA_STEPS_PER_TRIP - 1):
        pl.when(qb - done > v)(functools.partial(full_step, done + v + 1, (v + 1) % 2))

    for parity in range(2):
        @pl.when(lax.rem(qb, 2) == parity)
        def _(parity=parity):
            for h in range(B_HEADS):
                consume(h, qb, parity)

    smallest = jnp.min(jnp.concatenate([l_ref[h] for h in range(B_HEADS)], axis=0))

    @pl.when(jnp.logical_not(smallest > MIN_SHIFTED_SUM))
    def _():
        def online(h, j, own, first):
            visible = causal if own else sel_ref[h, pl.ds(j, 1), :] > NEG * 0.5
            sc = jnp.where(visible, scores(h, j) * LOG2E, NEG)
            m_tile = jnp.max(sc, axis=0, keepdims=True)
            m_new = m_tile if first else jnp.maximum(m_ref[h], m_tile)
            pr = jnp.exp2(sc - m_new)
            l_new, acc_new = jnp.sum(pr, axis=0, keepdims=True), _dot(values(h, j), pr.astype(BF16))
            if not first:
                alpha = jnp.exp2(m_ref[h] - m_new)
                l_new, acc_new = alpha * l_ref[h] + l_new, alpha * acc_ref[h] + acc_new
            m_ref[h], l_ref[h], acc_ref[h] = m_new, l_new, acc_new

        for h in range(B_HEADS):
            online(h, qb, True, True)

        def past(j, carry):
            for h in range(B_HEADS):
                online(h, j, False, False)
            return carry

        lax.fori_loop(0, qb, past, 0)

    yt = jnp.concatenate([acc_ref[h] / l_ref[h] for h in range(B_HEADS)], axis=0)
    yt = yt * lax.rsqrt(jnp.mean(yt * yt, axis=0, keepdims=True) + EPS)
    y_ref[0] = (yt.T * gob_ref[...]).astype(y_ref.dtype)


def _moba_call(bound, qt, k, vt, gob):
    b, s, _ = k.shape
    nb = s // B_BLOCK
    nbp = -(-nb // 8) * 8
    return pl.pallas_call(
        _moba_kernel,
        grid=(b, nb),
        in_specs=[
            pl.BlockSpec(memory_space=pltpu.SMEM),
            pl.BlockSpec((1, N_PAIRS, 1, HEAD_PAIR, B_BLOCK), lambda bi, qi: (bi, 0, qi, 0, 0)),
            pl.BlockSpec((1, s, D_B), lambda bi, qi: (bi, 0, 0)),
            pl.BlockSpec((1, N_PAIRS, nb, HEAD_PAIR, B_BLOCK), lambda bi, qi: (bi, 0, 0, 0, 0)),
            pl.BlockSpec((1, D_B), lambda bi, qi: (0, 0)),
        ],
        out_specs=pl.BlockSpec((1, B_BLOCK, D_B), lambda bi, qi: (bi, qi, 0)),
        out_shape=jax.ShapeDtypeStruct((b, s, D_B), BF16),
        scratch_shapes=[
            pltpu.VMEM((N_PAIRS, 3 * nbp, HEAD_PAIR), BF16),
            pltpu.VMEM((B_HEADS, HEAD_PAIR, B_BLOCK), BF16),
            pltpu.VMEM((B_HEADS, nbp, B_BLOCK), F32),
            pltpu.VMEM((B_HEADS, 1, B_BLOCK), F32),
            pltpu.VMEM((B_HEADS, 1, B_BLOCK), F32),
            pltpu.VMEM((B_HEADS, B_DH, B_BLOCK), F32),
            pltpu.VMEM((B_HEADS, B_BLOCK, B_BLOCK), F32),
            pltpu.VMEM((B_HEADS, B_BLOCK, B_BLOCK), F32),
        ],
        compiler_params=pltpu.CompilerParams(
            dimension_semantics=("parallel", "arbitrary"), vmem_limit_bytes=VMEM_LIMIT),
        name="moba",
    )(bound, qt, k, vt, gob)


def _outmix_kernel(x_ref, ya_ref, yb_ref, wout_ref, gffn_ref, wrt_hi_ref, wrt_lo_ref,
                   h_ref, hn_ref, route_ref, route_i_ref, counts_ref, cnt_ref):
    h = x_ref[...] + _dot(ya_ref[...], wout_ref[:D_A]) + _dot(yb_ref[...], wout_ref[D_A:])
    h_ref[...] = h
    hn = _rms(h, gffn_ref[...])
    _to_row_tiles(hn_ref, hn)
    tm = hn.shape[0]

    hn_hi, hn_lo = _split_bf16(hn)
    lg = _dot_nt(wrt_hi_ref[...], hn_hi) + (_dot_nt(wrt_hi_ref[...], hn_lo) + _dot_nt(wrt_lo_ref[...], hn_hi))
    row = lax.broadcasted_iota(jnp.int32, lg.shape, 0)

    def first_row(mask):
        return jnp.min(jnp.where(mask, row, ROUTER_ROWS), axis=0, keepdims=True)

    is_g = row < N_GROUPS
    g_max = jnp.max(jnp.where(is_g, lg, -jnp.inf), axis=0, keepdims=True)
    p_g = 1.0 / jnp.sum(jnp.where(is_g, jnp.exp(lg - g_max), 0.0), axis=0, keepdims=True)
    g_sel = first_row(is_g & (lg == g_max))
    e_row = row - N_GROUPS
    in_grp = (e_row >= 0) & (e_row < N_EXPERTS) & (lax.shift_right_arithmetic(e_row, 3) == g_sel)
    l1 = jnp.max(jnp.where(in_grp, lg, -jnp.inf), axis=0, keepdims=True)
    i1 = first_row(in_grp & (lg == l1))
    rest = in_grp & (row != i1)
    l2 = jnp.max(jnp.where(rest, lg, -jnp.inf), axis=0, keepdims=True)
    i2 = first_row(rest & (lg == l2))
    e2 = jnp.exp(l2 - l1)
    w1 = 1.0 / (1.0 + e2)
    w2 = e2 / (1.0 + e2)

    @pl.when(pl.program_id(0) == 0)
    def _():
        cnt_ref[...] = jnp.zeros_like(cnt_ref)

    picked = jnp.where((row == i1) | (row == i2), 1.0, 0.0)
    tok_r = lax.broadcasted_iota(jnp.int32, (tm, tm), 0)
    tok_c = lax.broadcasted_iota(jnp.int32, (tm, tm), 1)
    earlier = jnp.where(tok_r < tok_c, 1.0, 0.0).astype(BF16)
    before = _dot(picked.astype(BF16), earlier) + cnt_ref[:, 0:1]
    r1 = jnp.sum(jnp.where(row == i1, before, 0.0), axis=0, keepdims=True)
    r2 = jnp.sum(jnp.where(row == i2, before, 0.0), axis=0, keepdims=True)
    cnt_ref[...] += jnp.sum(picked, axis=1, keepdims=True)
    counts_ref[...] = cnt_ref[...]

    e1, e2x = i1 - N_GROUPS, i2 - N_GROUPS
    route_i_ref[...] = jnp.concatenate(
        [e1, e2x, r1.astype(jnp.int32), r2.astype(jnp.int32), jnp.zeros((ROUTE_FIELDS - 4, tm), jnp.int32)], axis=0)
    wts = jnp.concatenate([p_g * w1, p_g * w2, jnp.zeros((LANES - 2, tm), F32)], axis=0)
    route_ref[...] = wts.T


def _outmix_call(x2, ya2, yb2, wout, gffn, wrt_hi, wrt_lo):
    t, d = x2.shape
    tm = OUT_TM
    const = lambda *shape: pl.BlockSpec(shape, lambda ti: (0,) * len(shape))
    tok = lambda w: pl.BlockSpec((tm, w), lambda ti: (ti, 0))
    return pl.pallas_call(
        _outmix_kernel,
        grid=(t // tm,),
        in_specs=[tok(d), tok(D_A), tok(D_B), const(d, d), const(1, d),
                  const(ROUTER_ROWS, d), const(ROUTER_ROWS, d)],
        out_specs=[tok(d), pl.BlockSpec((tm * ROW_SUB, LANES), lambda ti: (ti, 0)), tok(LANES),
                   pl.BlockSpec((ROUTE_FIELDS, tm), lambda ti: (0, ti)), const(ROUTER_ROWS, LANES)],
        out_shape=[jax.ShapeDtypeStruct((t, d), F32), jax.ShapeDtypeStruct((t * ROW_SUB, LANES), F32),
                   jax.ShapeDtypeStruct((t, LANES), F32),
                   jax.ShapeDtypeStruct((ROUTE_FIELDS, t), jnp.int32),
                   jax.ShapeDtypeStruct((ROUTER_ROWS, LANES), F32)],
        scratch_shapes=[pltpu.VMEM((ROUTER_ROWS, LANES), F32)],
        compiler_params=pltpu.CompilerParams(
            dimension_semantics=("arbitrary",), vmem_limit_bytes=VMEM_LIMIT),
        name="outmix",
    )(x2, ya2, yb2, wout, gffn, wrt_hi, wrt_lo)


def _to_row_tiles(ref, x):
    n = x.shape[0]
    for s in range(ROW_SUB):
        ref[pl.ds(s, n, stride=ROW_SUB), :] = x[:, s * LANES:(s + 1) * LANES]


def _from_row_tiles(ref, first, n):
    return jnp.concatenate(
        [ref[pl.ds(first * ROW_SUB + s, n, stride=ROW_SUB), :] for s in range(ROW_SUB)], axis=1)


def _tiles_at(ref, row, n=1):
    return ref.at[pl.ds(pl.multiple_of(row * ROW_SUB, ROW_SUB), n * ROW_SUB), :]


def _wait_tiles(hbm, vmem_ref, n_tiles, sem):
    cap = vmem_ref.shape[0] // ROW_SUB
    while n_tiles > 0:
        n = min(n_tiles, cap)
        pltpu.make_async_copy(hbm.at[pl.ds(0, n * ROW_SUB), :], vmem_ref.at[pl.ds(0, n * ROW_SUB), :], sem).wait()
        n_tiles -= n


def _slots_kernel(seg_start_ref, route_i_ref, pos_ref):
    experts = route_i_ref[0:2, :]
    start = jnp.zeros_like(experts)
    for e in range(N_EXPERTS):
        start = jnp.where(experts == e, seg_start_ref[e], start)
    pos_ref[...] = jnp.zeros_like(pos_ref)
    pos_ref[0:2, :] = start + route_i_ref[2:4, :]


def _slots_call(seg_start, route_i):
    return pl.pallas_call(
        _slots_kernel,
        grid_spec=pltpu.PrefetchScalarGridSpec(
            num_scalar_prefetch=1,
            grid=(1,),
            in_specs=[pl.BlockSpec(route_i.shape, lambda i, s: (0, 0))],
            out_specs=pl.BlockSpec(route_i.shape, lambda i, s: (0, 0)),
        ),
        out_shape=jax.ShapeDtypeStruct(route_i.shape, jnp.int32),
        name="slots",
    )(seg_start, route_i)


def _dispatch_kernel(gap_start_ref, gap_len_ref, pos_ref, hn_ref, xs_hbm, zero_ref, sem, fill_sem):
    tm = hn_ref.shape[0] // ROW_SUB

    @pl.when(pl.program_id(0) == 0)
    def _():
        zero_ref[...] = jnp.zeros_like(zero_ref)
        fills = []
        for g in range(N_EXPERTS):
            row, n = gap_start_ref[g], gap_len_ref[g]
            size = EXP_TM // 2
            while size >= 1:
                fills.append(((n & size) != 0, row, size))
                row = row + (n & size)
                size //= 2
        for tile in range(N_EXPERTS):
            fills.append((tile * EXP_TM < gap_len_ref[N_EXPERTS], gap_start_ref[N_EXPERTS] + tile * EXP_TM, EXP_TM))
        copies = [(cond, pltpu.make_async_copy(_tiles_at(zero_ref, 0, size), _tiles_at(xs_hbm, row, size), fill_sem))
                  for cond, row, size in fills]
        for cond, copy in copies:
            pl.when(cond)(copy.start)
        for cond, copy in copies:
            pl.when(cond)(copy.wait)

    for r in range(2 * tm):
        k, tok = divmod(r, tm)
        pltpu.make_async_copy(_tiles_at(hn_ref, tok), _tiles_at(xs_hbm, pos_ref[k, tok]), sem).start(priority=r % 2)
    _wait_tiles(xs_hbm, hn_ref, 2 * tm, sem)


def _dispatch_call(gap_start, gap_len, pos, hn_tiles, n_sorted_rows):
    tm = TOK_TM
    return pl.pallas_call(
        _dispatch_kernel,
        grid_spec=pltpu.PrefetchScalarGridSpec(
            num_scalar_prefetch=2,
            grid=(pos.shape[1] // tm,),
            in_specs=[pl.BlockSpec((ROUTE_FIELDS, tm), lambda i, *_: (0, i), memory_space=pltpu.SMEM),
                      pl.BlockSpec((tm * ROW_SUB, LANES), lambda i, *_: (i, 0))],
            out_specs=pl.BlockSpec(memory_space=pl.ANY),
            scratch_shapes=[pltpu.VMEM((EXP_TM * ROW_SUB, LANES), F32), pltpu.SemaphoreType.DMA(()),
                            pltpu.SemaphoreType.DMA(())],
        ),
        out_shape=jax.ShapeDtypeStruct((n_sorted_rows * ROW_SUB, LANES), F32),
        compiler_params=pltpu.CompilerParams(
            dimension_semantics=("arbitrary",), vmem_limit_bytes=VMEM_LIMIT),
        name="dispatch",
    )(gap_start, gap_len, pos, hn_tiles)


def _experts_kernel(tile_e_ref, n_tiles_ref, xs_ref, wg_ref, wu_ref, wd_ref, y_ref, wg16, wu16, wd16):
    i = pl.program_id(0)
    tm = xs_ref.shape[0] // ROW_SUB

    @pl.when((i == 0) | (tile_e_ref[i] != tile_e_ref[jnp.maximum(i - 1, 0)]))
    def _():
        wg16[...] = wg_ref[0].astype(BF16)
        wu16[...] = wu_ref[0].astype(BF16)
        wd16[...] = wd_ref[0].astype(BF16)

    @pl.when(i < n_tiles_ref[0])
    def _():
        x = _from_row_tiles(xs_ref, 0, tm).astype(BF16)
        a = jax.nn.silu(_dot(x, wg16[...])) * _dot(x, wu16[...])
        _to_row_tiles(y_ref, _dot(a.astype(BF16), wd16[...]))

    @pl.when(i >= n_tiles_ref[0])
    def _():
        y_ref[...] = jnp.zeros_like(y_ref)


def _experts_call(tile_e, n_tiles, xs, wg, wu, wd):
    d = wg.shape[1]
    tm = EXP_TM
    nt = xs.shape[0] // (tm * ROW_SUB)
    used = lambda i, n: jnp.minimum(i, n[0] - 1)
    w_spec = lambda *shape: pl.BlockSpec((1,) + shape, lambda i, te, n: (te[used(i, n)], 0, 0))
    return pl.pallas_call(
        _experts_kernel,
        grid_spec=pltpu.PrefetchScalarGridSpec(
            num_scalar_prefetch=2,
            grid=(nt,),
            in_specs=[pl.BlockSpec((tm * ROW_SUB, LANES), lambda i, te, n: (used(i, n), 0)),
                      w_spec(d, D_FF_EXP), w_spec(d, D_FF_EXP), w_spec(D_FF_EXP, d)],
            out_specs=pl.BlockSpec((tm * ROW_SUB, LANES), lambda i, te, n: (i, 0)),
            scratch_shapes=[pltpu.VMEM((d, D_FF_EXP), BF16), pltpu.VMEM((d, D_FF_EXP), BF16),
                            pltpu.VMEM((D_FF_EXP, d), BF16)],
        ),
        out_shape=jax.ShapeDtypeStruct(xs.shape, F32),
        compiler_params=pltpu.CompilerParams(
            dimension_semantics=("arbitrary",), vmem_limit_bytes=VMEM_LIMIT),
        name="experts",
    )(tile_e, n_tiles, xs, wg, wu, wd)


def _ple_kernel(cur_ref, nxt_ref, ys_hbm, h_ref, route_ref, p_ref, wple_ref, gple_ref, wpg_ref,
                o_ref, ybuf0, ybuf1, sem):
    tm = h_ref.shape[0]
    i = pl.program_id(0)
    n_steps = pl.num_programs(0)
    bufs = (ybuf0, ybuf1)

    def start_gather(pos_ref, slot):
        for r in range(2 * tm):
            pltpu.make_async_copy(_tiles_at(ys_hbm, pos_ref[r // tm, r % tm]), _tiles_at(bufs[slot], r),
                                  sem.at[slot]).start(priority=r % 2)

    @pl.when(i == 0)
    def _():
        start_gather(cur_ref, 0)

    def step(slot):
        _wait_tiles(ys_hbm, bufs[slot], 2 * tm, sem.at[slot])
        start_gather(nxt_ref, 1 - slot)
        wts = route_ref[...]
        h = (h_ref[...] + wts[:, 0:1] * _from_row_tiles(bufs[slot], 0, tm)
             + wts[:, 1:2] * _from_row_tiles(bufs[slot], tm, tm))
        gate = jax.nn.sigmoid(_dot(_rms(h, gple_ref[...]).astype(BF16), wpg_ref[...]))
        o_ref[...] = h + _dot(p_ref[...].astype(BF16), wple_ref[...]) * gate

    for slot in range(2):
        pl.when(lax.rem(i, 2) == slot)(functools.partial(step, slot))

    for slot in range(2):
        pl.when((i == n_steps - 1) & (lax.rem(n_steps, 2) == slot))(
            functools.partial(_wait_tiles, ys_hbm, bufs[slot], 2 * tm, sem.at[slot]))


def _ple_call(pos, ys, h, route, p2, wple, gple, wpg):
    t, d = h.shape
    tm = TOK_TM
    n_steps = t // tm
    n_rows = 2 * tm
    idx_spec = lambda f: pl.BlockSpec((ROUTE_FIELDS, tm), lambda i: (0, f(i)), memory_space=pltpu.SMEM)
    const = lambda *shape: pl.BlockSpec(shape, lambda i: (0,) * len(shape))
    tok = lambda w: pl.BlockSpec((tm, w), lambda i: (i, 0))
    return pl.pallas_call(
        _ple_kernel,
        grid=(n_steps,),
        in_specs=[idx_spec(lambda i: i), idx_spec(lambda i: lax.rem(i + 1, n_steps)),
                  pl.BlockSpec(memory_space=pl.ANY),
                  tok(d), tok(LANES), tok(D_PLE), const(D_PLE, d), const(1, d), const(d, d)],
        out_specs=tok(d),
        out_shape=jax.ShapeDtypeStruct((t, d), F32),
        scratch_shapes=[pltpu.VMEM((n_rows * ROW_SUB, LANES), F32), pltpu.VMEM((n_rows * ROW_SUB, LANES), F32),
                        pltpu.SemaphoreType.DMA((2,))],
        compiler_params=pltpu.CompilerParams(
            dimension_semantics=("arbitrary",), vmem_limit_bytes=VMEM_LIMIT),
        name="ple",
    )(pos, pos, ys, h, route, p2, wple, gple, wpg)


def _segment_tables(counts, t):
    tm = EXP_TM
    nt = 2 * t // tm + N_EXPERTS
    padded = ((counts + tm - 1) // tm) * tm
    seg_end = jnp.cumsum(padded)
    seg_start = seg_end - padded
    gap_len = jnp.concatenate([padded - counts, nt * tm - seg_end[-1:]])
    gap_start = jnp.concatenate([seg_start + counts, seg_end[-1:]])
    tile_start = jnp.arange(nt, dtype=jnp.int32) * tm
    tile_e = jnp.minimum(jnp.sum(tile_start[:, None] >= seg_end[None, :], axis=1), N_EXPERTS - 1)
    n_tiles = (seg_end[-1] // tm).reshape(1)
    i32 = lambda a: a.astype(jnp.int32)
    return i32(seg_start), i32(gap_start), i32(gap_len), i32(tile_e), i32(n_tiles), nt * tm


def _layer(h, p_i, g_mix, w_in, g_v_a, w_s, b_s, g_q, g_k, g_out_a, g_out_b, w_out, g_ffn, w_group,
           w_expert, w_gate_e, w_up_e, w_down_e, g_ple, w_ple, w_ple_gate):
    b, s, d = h.shape
    t = b * s
    row = lambda g: g.reshape(1, -1).astype(F32)
    wuv = w_in[:, :2 * D_A].astype(BF16)
    wqt = w_in[:, 2 * D_A:2 * D_A + D_B].T.astype(BF16)
    wk = w_in[:, 2 * D_A + D_B:2 * D_A + 2 * D_B].astype(BF16)
    wvt = w_in[:, 2 * D_A + 2 * D_B:].T.astype(BF16)
    bs = jnp.broadcast_to(b_s[:, :, None], (A_GROUPS, A_CHUNK, A_DH)).astype(F32)
    gk = row(jnp.tile(g_k, B_HEADS))
    gq = (jnp.tile(g_q, B_HEADS) * (1.0 / float(B_DH) ** 0.5)).reshape(D_B, 1).astype(F32)
    head = jnp.arange(D_B) // B_DH
    bd = jnp.where(head[:, None] == head[None, :], 1.0 / B_DH, 0.0).astype(BF16)
    w_rt = jnp.zeros((ROUTER_ROWS, d), F32)
    w_rt = w_rt.at[:N_GROUPS].set(w_group.T).at[N_GROUPS:N_GROUPS + N_EXPERTS].set(w_expert.T)
    wrt_hi, wrt_lo = _split_bf16(w_rt)

    ya, k, qt, vt = _proj_call(h, row(g_mix), wuv, wk, wqt, wvt, row(g_v_a), w_s.astype(F32), bs,
                               row(g_out_a), gk, gq, bd)
    bound = (BOUND_MARGIN * LOG2E * float(B_DH) ** 0.5) * jnp.max(jnp.abs(g_q)) * jnp.max(jnp.abs(g_k))
    yb = _moba_call(bound.reshape(1, 1).astype(F32), qt, k, vt, row(g_out_b))
    h1, hn_tiles, route, route_i, counts = _outmix_call(
        h.reshape(t, d), ya.reshape(t, D_A), yb.reshape(t, D_B), w_out.astype(BF16), row(g_ffn), wrt_hi, wrt_lo)
    counts = counts[N_GROUPS:N_GROUPS + N_EXPERTS, 0].astype(jnp.int32)
    seg_start, gap_start, gap_len, tile_e, n_tiles, n_sorted_rows = _segment_tables(counts, t)
    pos = _slots_call(seg_start, route_i)
    xs = _dispatch_call(gap_start, gap_len, pos, hn_tiles, n_sorted_rows)
    ys = _experts_call(tile_e, n_tiles, xs, w_gate_e, w_up_e, w_down_e)
    out = _ple_call(pos, ys, h1, route, p_i.reshape(t, D_PLE), w_ple.astype(BF16), row(g_ple),
                    w_ple_gate.astype(BF16))
    return out.reshape(b, s, d)


def kernel(x, p, g_mix, w_in, g_v_a, w_s, b_s, g_q, g_k, g_out_a, g_out_b, w_out, g_ffn, w_group, w_expert,
           w_gate_e, w_up_e, w_down_e, g_ple, w_ple, w_ple_gate):
    params = (g_mix, w_in, g_v_a, w_s, b_s, g_q, g_k, g_out_a, g_out_b, w_out, g_ffn, w_group, w_expert,
              w_gate_e, w_up_e, w_down_e, g_ple, w_ple, w_ple_gate)
    h = x
    for i in range(p.shape[0]):
        h = _layer(h, p[i], *(w[i] for w in params))
    return h
```

```python
import functools

import jax
import jax.numpy as jnp
from jax import lax
from jax.experimental import pallas as pl
from jax.experimental.pallas import tpu as pltpu

F32 = jnp.float32
BF16 = jnp.bfloat16

D_MODEL = 1024
D_A = 512
A_GROUPS = 4
A_DH = D_A // A_GROUPS
A_CHUNK = 128
D_B = 512
B_HEADS = 8
B_DH = D_B // B_HEADS
B_BLOCK = 256
B_TOPK = 3
D_PLE = 256
N_GROUPS = 4
EXP_PER_GROUP = 8
N_EXPERTS = N_GROUPS * EXP_PER_GROUP
D_FF_EXP = 256
EPS = 1e-6
NEG = -1e30
LOG2E = 1.4426950408889634

LANES = 128
HEAD_PAIR = 2 * B_DH
N_PAIRS = B_HEADS // 2
ROUTER_ROWS = 48
ROUTE_FIELDS = 8
VMEM_LIMIT = 56 * 1024 * 1024

PROJ_TM = 1024
OUT_TM = 1024
EXP_TM = 512
MOBA_STEPS_PER_TRIP = 4
BOUND_MARGIN = 1.02
MIN_SHIFTED_SUM = 2.0 ** -64
TOK_TM = 1024
ROW_SUB = D_MODEL // LANES


def _rms(x, g):
    return x * lax.rsqrt(jnp.mean(x * x, axis=-1, keepdims=True) + EPS) * g


def _dot(a, b):
    return jnp.dot(a, b, preferred_element_type=F32)


def _dot_nt(a, b):
    return lax.dot_general(a, b, (((1,), (1,)), ((), ())), preferred_element_type=F32)


def _split_bf16(x):
    hi = x.astype(BF16)
    lo = (x - hi.astype(F32)).astype(BF16)
    return hi, lo


def _proj_kernel(x_ref, gmix_ref, win_ref, gva_ref, ws_ref, bs_ref, goa_ref, gk_ref, gq_ref, bd_ref,
                 ya_ref, k_ref, qt_ref, vt_ref, prod_ref, wuv_ref, wk_ref, wqt_ref, wvt_ref):
    tm = x_ref.shape[1]
    n_chunks = tm // A_CHUNK

    @pl.when((pl.program_id(0) == 0) & (pl.program_id(1) == 0))
    def _():
        q0, k0, v0 = 2 * D_A, 2 * D_A + D_B, 2 * D_A + 2 * D_B
        wuv_ref[...] = win_ref[:, :q0].astype(BF16)
        wk_ref[...] = win_ref[:, k0:v0].astype(BF16)
        wqt_ref[...] = win_ref[:, q0:k0].T.astype(BF16)
        wvt_ref[...] = win_ref[:, v0:].T.astype(BF16)

    xn = _rms(x_ref[0], gmix_ref[...]).astype(BF16)

    uv = _dot(xn, wuv_ref[...])
    u = jax.nn.gelu(uv[:, :D_A])
    v = jax.nn.gelu(uv[:, D_A:])
    row = lax.broadcasted_iota(jnp.int32, (A_CHUNK, A_CHUNK), 0)
    col = lax.broadcasted_iota(jnp.int32, (A_CHUNK, A_CHUNK), 1)
    causal = col <= row
    for g in range(A_GROUPS):
        gs = slice(g * A_DH, (g + 1) * A_DH)
        vn = _rms(v[:, gs], gva_ref[:, gs]).astype(BF16)
        rhs = jnp.concatenate([vn[c * A_CHUNK:(c + 1) * A_CHUNK] for c in range(n_chunks)], axis=1)
        ws = jnp.where(causal, ws_ref[g], 0.0).astype(BF16)
        mixed = _dot(ws, rhs)
        for c in range(n_chunks):
            cs = slice(c * A_CHUNK, (c + 1) * A_CHUNK)
            prod_ref[cs, gs] = u[cs, gs] * (mixed[:, cs] + bs_ref[g])
    ya_ref[0] = _rms(prod_ref[...], goa_ref[...]).astype(ya_ref.dtype)

    k = _dot(xn, wk_ref[...])
    k2_hi, k2_lo = _split_bf16(k * k)
    k_ms = _dot(k2_hi, bd_ref[...]) + _dot(k2_lo, bd_ref[...])
    k_ref[0] = (k * lax.rsqrt(k_ms + EPS) * gk_ref[...]).astype(k_ref.dtype)

    qt = _dot_nt(wqt_ref[...], xn)
    vt = _dot_nt(wvt_ref[...], xn).astype(vt_ref.dtype)
    qn = []
    for h in range(B_HEADS):
        qh = qt[h * B_DH:(h + 1) * B_DH]
        qn.append(qh * lax.rsqrt(jnp.mean(qh * qh, axis=0, keepdims=True) + EPS))
    qn = (jnp.concatenate(qn, axis=0) * gq_ref[...]).astype(qt_ref.dtype)
    for p in range(N_PAIRS):
        ps = slice(p * HEAD_PAIR, (p + 1) * HEAD_PAIR)
        for j in range(tm // B_BLOCK):
            js = slice(j * B_BLOCK, (j + 1) * B_BLOCK)
            qt_ref[0, p, j] = qn[ps, js]
            vt_ref[0, p, j] = vt[ps, js]


def _proj_call(x, gmix, w_in, gva, ws, bs, goa, gk, gq, bd):
    b, s, d = x.shape
    tm = PROJ_TM
    nb = s // B_BLOCK
    jb = tm // B_BLOCK
    const = lambda *shape: pl.BlockSpec(shape, lambda bi, ti: (0,) * len(shape))
    return pl.pallas_call(
        _proj_kernel,
        grid=(b, s // tm),
        in_specs=[
            pl.BlockSpec((1, tm, d), lambda bi, ti: (bi, ti, 0)),
            const(1, d), const(d, 2 * D_A + 3 * D_B),
            const(1, D_A), const(A_GROUPS, A_CHUNK, A_CHUNK), const(A_GROUPS, A_CHUNK, A_DH),
            const(1, D_A), const(1, D_B), const(D_B, 1), const(D_B, D_B),
        ],
        out_specs=[
            pl.BlockSpec((1, tm, D_A), lambda bi, ti: (bi, ti, 0)),
            pl.BlockSpec((1, tm, D_B), lambda bi, ti: (bi, ti, 0)),
            pl.BlockSpec((1, N_PAIRS, jb, HEAD_PAIR, B_BLOCK), lambda bi, ti: (bi, 0, ti, 0, 0)),
            pl.BlockSpec((1, N_PAIRS, jb, HEAD_PAIR, B_BLOCK), lambda bi, ti: (bi, 0, ti, 0, 0)),
        ],
        out_shape=[
            jax.ShapeDtypeStruct((b, s, D_A), BF16),
            jax.ShapeDtypeStruct((b, s, D_B), BF16),
            jax.ShapeDtypeStruct((b, N_PAIRS, nb, HEAD_PAIR, B_BLOCK), BF16),
            jax.ShapeDtypeStruct((b, N_PAIRS, nb, HEAD_PAIR, B_BLOCK), BF16),
        ],
        scratch_shapes=[pltpu.VMEM((tm, D_A), F32), pltpu.VMEM((d, 2 * D_A), BF16), pltpu.VMEM((d, D_B), BF16),
                        pltpu.VMEM((D_B, d), BF16), pltpu.VMEM((D_B, d), BF16)],
        compiler_params=pltpu.CompilerParams(
            dimension_semantics=("arbitrary", "arbitrary"), vmem_limit_bytes=VMEM_LIMIT),
        name="proj",
    )(x, gmix, w_in, gva, ws, bs, goa, gk, gq, bd)


def _moba_kernel(bound_ref, qt_ref, k_ref, vt_ref, gob_ref, y_ref, kmean_ref, qz_ref, sel_ref, m_ref, l_ref,
                 acc_ref, stage0_ref, stage1_ref):
    qb = pl.program_id(1)
    nb = k_ref.shape[1] // B_BLOCK
    nbp = sel_ref.shape[1]

    @pl.when(qb == 0)
    def _():
        rows = [jnp.mean(k_ref[0, j * B_BLOCK:(j + 1) * B_BLOCK, :].astype(F32), axis=0, keepdims=True)
                for j in range(nb)]
        rows += [jnp.zeros((1, D_B), F32)] * (nbp - nb)
        km = jnp.concatenate(rows, axis=0)
        hi = km.astype(BF16)
        rest = km - hi.astype(F32)
        mid = rest.astype(BF16)
        lo = (rest - mid.astype(F32)).astype(BF16)
        for p in range(N_PAIRS):
            ps = slice(p * HEAD_PAIR, (p + 1) * HEAD_PAIR)
            kmean_ref[p] = jnp.concatenate([hi[:, ps], mid[:, ps], lo[:, ps]], axis=0)

    head_row = lax.broadcasted_iota(jnp.int32, (HEAD_PAIR, B_BLOCK), 0) // B_DH
    blk = lax.broadcasted_iota(jnp.int32, (nbp, B_BLOCK), 0)
    past = blk < qb

    for h in range(B_HEADS):
        p, hh = divmod(h, 2)
        q_pair = qt_ref[0, p, 0]
        qz = jnp.where(head_row == hh, q_pair, jnp.zeros_like(q_pair))
        qz_ref[h] = qz
        g3 = _dot(kmean_ref[p], qz)
        gate = (g3[:nbp] + g3[nbp:2 * nbp] + g3[2 * nbp:]) * float(B_DH) ** 0.5
        gate = jnp.where(past, gate, NEG)
        left = gate
        taken = blk < 0
        for _ in range(min(B_TOPK, nb)):
            top = jnp.max(left, axis=0, keepdims=True)
            first = jnp.min(jnp.where(left == top, blk, nbp), axis=0, keepdims=True)
            taken = taken | (blk == first)
            left = jnp.where(blk == first, -jnp.inf, left)
        chosen = taken & (gate > NEG * 0.5)
        sel_ref[h] = jnp.where(chosen, -bound_ref[0, 0], NEG)

    def scores(h, j):
        p = h // 2
        kj = k_ref[0, pl.ds(pl.multiple_of(j * B_BLOCK, B_BLOCK), B_BLOCK), p * HEAD_PAIR:(p + 1) * HEAD_PAIR]
        return _dot(kj, qz_ref[h])

    def values(h, j):
        p, hh = divmod(h, 2)
        return vt_ref[0, p, j, hh * B_DH:(hh + 1) * B_DH, :]

    key_pos = lax.broadcasted_iota(jnp.int32, (B_BLOCK, B_BLOCK), 0)
    q_pos = lax.broadcasted_iota(jnp.int32, (B_BLOCK, B_BLOCK), 1)
    causal = key_pos <= q_pos

    bound = bound_ref[0, 0]
    stage = (stage0_ref, stage1_ref)
    own_shift = jnp.where(causal, -bound, NEG)

    def stage_scores(h, block, shift, parity):
        stage[parity][h] = scores(h, block) * LOG2E + shift

    def consume(h, s, parity):
        pr = jnp.exp2(stage[parity][h])
        l_ref[h] += jnp.sum(pr, axis=0, keepdims=True)
        acc_ref[h] += _dot(values(h, jnp.where(s == 0, qb, s - 1)), pr.astype(BF16))

    def full_step(s, parity):
        for h in range(B_HEADS):
            stage_scores(h, s - 1, sel_ref[h, pl.ds(s - 1, 1), :], parity)
            consume(h, s - 1, 1 - parity)

    for h in range(B_HEADS):
        stage_scores(h, qb, own_shift, 0)
        l_ref[h] = jnp.zeros((1, B_BLOCK), F32)
        acc_ref[h] = jnp.zeros((B_DH, B_BLOCK), F32)

    def trip(u, carry):
        for v in range(MOBA_STEPS_PER_TRIP):
            full_step(MOBA_STEPS_PER_TRIP * u + v + 1, (v + 1) % 2)
        return carry

    lax.fori_loop(0, qb // MOBA_STEPS_PER_TRIP, trip, 0)
    done = (qb // MOBA_STEPS_PER_TRIP) * MOBA_STEPS_PER_TRIP
    for v in range(MOBA_STEPS_PER_TRIP - 1):
        pl.when(qb - done > v)(functools.partial(full_step, done + v + 1, (v + 1) % 2))

    for parity in range(2):
        @pl.when(lax.rem(qb, 2) == parity)
        def _(parity=parity):
            for h in range(B_HEADS):
                consume(h, qb, parity)

    smallest = jnp.min(jnp.concatenate([l_ref[h] for h in range(B_HEADS)], axis=0))

    @pl.when(jnp.logical_not(smallest > MIN_SHIFTED_SUM))
    def _():
        def online(h, j, own, first):
            visible = causal if own else sel_ref[h, pl.ds(j, 1), :] > NEG * 0.5
            sc = jnp.where(visible, scores(h, j) * LOG2E, NEG)
            m_tile = jnp.max(sc, axis=0, keepdims=True)
            m_new = m_tile if first else jnp.maximum(m_ref[h], m_tile)
            pr = jnp.exp2(sc - m_new)
            l_new, acc_new = jnp.sum(pr, axis=0, keepdims=True), _dot(values(h, j), pr.astype(BF16))
            if not first:
                alpha = jnp.exp2(m_ref[h] - m_new)
                l_new, acc_new = alpha * l_ref[h] + l_new, alpha * acc_ref[h] + acc_new
            m_ref[h], l_ref[h], acc_ref[h] = m_new, l_new, acc_new

        for h in range(B_HEADS):
            online(h, qb, True, True)

        def past(j, carry):
            for h in range(B_HEADS):
                online(h, j, False, False)
            return carry

        lax.fori_loop(0, qb, past, 0)

    yt = jnp.concatenate([acc_ref[h] / l_ref[h] for h in range(B_HEADS)], axis=0)
    yt = yt * lax.rsqrt(jnp.mean(yt * yt, axis=0, keepdims=True) + EPS)
    y_ref[0] = (yt.T * gob_ref[...]).astype(y_ref.dtype)


def _moba_call(bound, qt, k, vt, gob):
    b, s, _ = k.shape
    nb = s // B_BLOCK
    nbp = -(-nb // 8) * 8
    return pl.pallas_call(
        _moba_kernel,
        grid=(b, nb),
        in_specs=[
            pl.BlockSpec(memory_space=pltpu.SMEM),
            pl.BlockSpec((1, N_PAIRS, 1, HEAD_PAIR, B_BLOCK), lambda bi, qi: (bi, 0, qi, 0, 0)),
            pl.BlockSpec((1, s, D_B), lambda bi, qi: (bi, 0, 0)),
            pl.BlockSpec((1, N_PAIRS, nb, HEAD_PAIR, B_BLOCK), lambda bi, qi: (bi, 0, 0, 0, 0)),
            pl.BlockSpec((1, D_B), lambda bi, qi: (0, 0)),
        ],
        out_specs=pl.BlockSpec((1, B_BLOCK, D_B), lambda bi, qi: (bi, qi, 0)),
        out_shape=jax.ShapeDtypeStruct((b, s, D_B), BF16),
        scratch_shapes=[
            pltpu.VMEM((N_PAIRS, 3 * nbp, HEAD_PAIR), BF16),
            pltpu.VMEM((B_HEADS, HEAD_PAIR, B_BLOCK), BF16),
            pltpu.VMEM((B_HEADS, nbp, B_BLOCK), F32),
            pltpu.VMEM((B_HEADS, 1, B_BLOCK), F32),
            pltpu.VMEM((B_HEADS, 1, B_BLOCK), F32),
            pltpu.VMEM((B_HEADS, B_DH, B_BLOCK), F32),
            pltpu.VMEM((B_HEADS, B_BLOCK, B_BLOCK), F32),
            pltpu.VMEM((B_HEADS, B_BLOCK, B_BLOCK), F32),
        ],
        compiler_params=pltpu.CompilerParams(
            dimension_semantics=("parallel", "arbitrary"), vmem_limit_bytes=VMEM_LIMIT),
        name="moba",
    )(bound, qt, k, vt, gob)


def _outmix_kernel(x_ref, ya_ref, yb_ref, wout_ref, gffn_ref, wrt_hi_ref, wrt_lo_ref,
                   h_ref, hn_ref, route_ref, route_i_ref, counts_ref, cnt_ref):
    h = x_ref[...] + _dot(ya_ref[...], wout_ref[:D_A]) + _dot(yb_ref[...], wout_ref[D_A:])
    h_ref[...] = h
    hn = _rms(h, gffn_ref[...])
    _to_row_tiles(hn_ref, hn)
    tm = hn.shape[0]

    hn_hi, hn_lo = _split_bf16(hn)
    lg = _dot_nt(wrt_hi_ref[...], hn_hi) + (_dot_nt(wrt_hi_ref[...], hn_lo) + _dot_nt(wrt_lo_ref[...], hn_hi))
    row = lax.broadcasted_iota(jnp.int32, lg.shape, 0)

    def first_row(mask):
        return jnp.min(jnp.where(mask, row, ROUTER_ROWS), axis=0, keepdims=True)

    is_g = row < N_GROUPS
    g_max = jnp.max(jnp.where(is_g, lg, -jnp.inf), axis=0, keepdims=True)
    p_g = 1.0 / jnp.sum(jnp.where(is_g, jnp.exp(lg - g_max), 0.0), axis=0, keepdims=True)
    g_sel = first_row(is_g & (lg == g_max))
    e_row = row - N_GROUPS
    in_grp = (e_row >= 0) & (e_row < N_EXPERTS) & (lax.shift_right_arithmetic(e_row, 3) == g_sel)
    l1 = jnp.max(jnp.where(in_grp, lg, -jnp.inf), axis=0, keepdims=True)
    i1 = first_row(in_grp & (lg == l1))
    rest = in_grp & (row != i1)
    l2 = jnp.max(jnp.where(rest, lg, -jnp.inf), axis=0, keepdims=True)
    i2 = first_row(rest & (lg == l2))
    e2 = jnp.exp(l2 - l1)
    w1 = 1.0 / (1.0 + e2)
    w2 = e2 / (1.0 + e2)

    @pl.when(pl.program_id(0) == 0)
    def _():
        cnt_ref[...] = jnp.zeros_like(cnt_ref)

    picked = jnp.where((row == i1) | (row == i2), 1.0, 0.0)
    tok_r = lax.broadcasted_iota(jnp.int32, (tm, tm), 0)
    tok_c = lax.broadcasted_iota(jnp.int32, (tm, tm), 1)
    earlier = jnp.where(tok_r < tok_c, 1.0, 0.0).astype(BF16)
    before = _dot(picked.astype(BF16), earlier) + cnt_ref[:, 0:1]
    r1 = jnp.sum(jnp.where(row == i1, before, 0.0), axis=0, keepdims=True)
    r2 = jnp.sum(jnp.where(row == i2, before, 0.0), axis=0, keepdims=True)
    cnt_ref[...] += jnp.sum(picked, axis=1, keepdims=True)
    counts_ref[...] = cnt_ref[...]

    e1, e2x = i1 - N_GROUPS, i2 - N_GROUPS
    route_i_ref[...] = jnp.concatenate(
        [e1, e2x, r1.astype(jnp.int32), r2.astype(jnp.int32), jnp.zeros((ROUTE_FIELDS - 4, tm), jnp.int32)], axis=0)
    wts = jnp.concatenate([p_g * w1, p_g * w2, jnp.zeros((LANES - 2, tm), F32)], axis=0)
    route_ref[...] = wts.T


def _outmix_call(x2, ya2, yb2, wout, gffn, wrt_hi, wrt_lo):
    t, d = x2.shape
    tm = OUT_TM
    const = lambda *shape: pl.BlockSpec(shape, lambda ti: (0,) * len(shape))
    tok = lambda w: pl.BlockSpec((tm, w), lambda ti: (ti, 0))
    return pl.pallas_call(
        _outmix_kernel,
        grid=(t // tm,),
        in_specs=[tok(d), tok(D_A), tok(D_B), const(d, d), const(1, d),
                  const(ROUTER_ROWS, d), const(ROUTER_ROWS, d)],
        out_specs=[tok(d), pl.BlockSpec((tm * ROW_SUB, LANES), lambda ti: (ti, 0)), tok(LANES),
                   pl.BlockSpec((ROUTE_FIELDS, tm), lambda ti: (0, ti)), const(ROUTER_ROWS, LANES)],
        out_shape=[jax.ShapeDtypeStruct((t, d), F32), jax.ShapeDtypeStruct((t * ROW_SUB, LANES), F32),
                   jax.ShapeDtypeStruct((t, LANES), F32),
                   jax.ShapeDtypeStruct((ROUTE_FIELDS, t), jnp.int32),
                   jax.ShapeDtypeStruct((ROUTER_ROWS, LANES), F32)],
        scratch_shapes=[pltpu.VMEM((ROUTER_ROWS, LANES), F32)],
        compiler_params=pltpu.CompilerParams(
            dimension_semantics=("arbitrary",), vmem_limit_bytes=VMEM_LIMIT),
        name="outmix",
    )(x2, ya2, yb2, wout, gffn, wrt_hi, wrt_lo)


def _to_row_tiles(ref, x):
    n = x.shape[0]
    for s in range(ROW_SUB):
        ref[pl.ds(s, n, stride=ROW_SUB), :] = x[:, s * LANES:(s + 1) * LANES]


def _from_row_tiles(ref, first, n):
    return jnp.concatenate(
        [ref[pl.ds(first * ROW_SUB + s, n, stride=ROW_SUB), :] for s in range(ROW_SUB)], axis=1)


def _tiles_at(ref, row, n=1):
    return ref.at[pl.ds(pl.multiple_of(row * ROW_SUB, ROW_SUB), n * ROW_SUB), :]


def _wait_tiles(hbm, vmem_ref, n_tiles, sem):
    cap = vmem_ref.shape[0] // ROW_SUB
    while n_tiles > 0:
        n = min(n_tiles, cap)
        pltpu.make_async_copy(hbm.at[pl.ds(0, n * ROW_SUB), :], vmem_ref.at[pl.ds(0, n * ROW_SUB), :], sem).wait()
        n_tiles -= n


def _slots_kernel(seg_start_ref, route_i_ref, pos_ref):
    experts = route_i_ref[0:2, :]
    start = jnp.zeros_like(experts)
    for e in range(N_EXPERTS):
        start = jnp.where(experts == e, seg_start_ref[e], start)
    pos_ref[...] = jnp.zeros_like(pos_ref)
    pos_ref[0:2, :] = start + route_i_ref[2:4, :]


def _slots_call(seg_start, route_i):
    return pl.pallas_call(
        _slots_kernel,
        grid_spec=pltpu.PrefetchScalarGridSpec(
            num_scalar_prefetch=1,
            grid=(1,),
            in_specs=[pl.BlockSpec(route_i.shape, lambda i, s: (0, 0))],
            out_specs=pl.BlockSpec(route_i.shape, lambda i, s: (0, 0)),
        ),
        out_shape=jax.ShapeDtypeStruct(route_i.shape, jnp.int32),
        name="slots",
    )(seg_start, route_i)


def _dispatch_kernel(gap_start_ref, gap_len_ref, pos_ref, hn_ref, xs_hbm, zero_ref, sem, fill_sem):
    tm = hn_ref.shape[0] // ROW_SUB

    @pl.when(pl.program_id(0) == 0)
    def _():
        zero_ref[...] = jnp.zeros_like(zero_ref)
        fills = []
        for g in range(N_EXPERTS):
            row, n = gap_start_ref[g], gap_len_ref[g]
            size = EXP_TM // 2
            while size >= 1:
                fills.append(((n & size) != 0, row, size))
                row = row + (n & size)
                size //= 2
        for tile in range(N_EXPERTS):
            fills.append((tile * EXP_TM < gap_len_ref[N_EXPERTS], gap_start_ref[N_EXPERTS] + tile * EXP_TM, EXP_TM))
        copies = [(cond, pltpu.make_async_copy(_tiles_at(zero_ref, 0, size), _tiles_at(xs_hbm, row, size), fill_sem))
                  for cond, row, size in fills]
        for cond, copy in copies:
            pl.when(cond)(copy.start)
        for cond, copy in copies:
            pl.when(cond)(copy.wait)

    for r in range(2 * tm):
        k, tok = divmod(r, tm)
        pltpu.make_async_copy(_tiles_at(hn_ref, tok), _tiles_at(xs_hbm, pos_ref[k, tok]), sem).start(priority=r % 2)
    _wait_tiles(xs_hbm, hn_ref, 2 * tm, sem)


def _dispatch_call(gap_start, gap_len, pos, hn_tiles, n_sorted_rows):
    tm = TOK_TM
    return pl.pallas_call(
        _dispatch_kernel,
        grid_spec=pltpu.PrefetchScalarGridSpec(
            num_scalar_prefetch=2,
            grid=(pos.shape[1] // tm,),
            in_specs=[pl.BlockSpec((ROUTE_FIELDS, tm), lambda i, *_: (0, i), memory_space=pltpu.SMEM),
                      pl.BlockSpec((tm * ROW_SUB, LANES), lambda i, *_: (i, 0))],
            out_specs=pl.BlockSpec(memory_space=pl.ANY),
            scratch_shapes=[pltpu.VMEM((EXP_TM * ROW_SUB, LANES), F32), pltpu.SemaphoreType.DMA(()),
                            pltpu.SemaphoreType.DMA(())],
        ),
        out_shape=jax.ShapeDtypeStruct((n_sorted_rows * ROW_SUB, LANES), F32),
        compiler_params=pltpu.CompilerParams(
            dimension_semantics=("arbitrary",), vmem_limit_bytes=VMEM_LIMIT),
        name="dispatch",
    )(gap_start, gap_len, pos, hn_tiles)


def _experts_kernel(tile_e_ref, n_tiles_ref, xs_ref, wg_ref, wu_ref, wd_ref, y_ref, wg16, wu16, wd16):
    i = pl.program_id(0)
    tm = xs_ref.shape[0] // ROW_SUB

    @pl.when((i == 0) | (tile_e_ref[i] != tile_e_ref[jnp.maximum(i - 1, 0)]))
    def _():
        wg16[...] = wg_ref[0].astype(BF16)
        wu16[...] = wu_ref[0].astype(BF16)
        wd16[...] = wd_ref[0].astype(BF16)

    @pl.when(i < n_tiles_ref[0])
    def _():
        x = _from_row_tiles(xs_ref, 0, tm).astype(BF16)
        a = jax.nn.silu(_dot(x, wg16[...])) * _dot(x, wu16[...])
        _to_row_tiles(y_ref, _dot(a.astype(BF16), wd16[...]))

    @pl.when(i >= n_tiles_ref[0])
    def _():
        y_ref[...] = jnp.zeros_like(y_ref)


def _experts_call(tile_e, n_tiles, xs, wg, wu, wd):
    d = wg.shape[1]
    tm = EXP_TM
    nt = xs.shape[0] // (tm * ROW_SUB)
    used = lambda i, n: jnp.minimum(i, n[0] - 1)
    w_spec = lambda *shape: pl.BlockSpec((1,) + shape, lambda i, te, n: (te[used(i, n)], 0, 0))
    return pl.pallas_call(
        _experts_kernel,
        grid_spec=pltpu.PrefetchScalarGridSpec(
            num_scalar_prefetch=2,
            grid=(nt,),
            in_specs=[pl.BlockSpec((tm * ROW_SUB, LANES), lambda i, te, n: (used(i, n), 0)),
                      w_spec(d, D_FF_EXP), w_spec(d, D_FF_EXP), w_spec(D_FF_EXP, d)],
            out_specs=pl.BlockSpec((tm * ROW_SUB, LANES), lambda i, te, n: (i, 0)),
            scratch_shapes=[pltpu.VMEM((d, D_FF_EXP), BF16), pltpu.VMEM((d, D_FF_EXP), BF16),
                            pltpu.VMEM((D_FF_EXP, d), BF16)],
        ),
        out_shape=jax.ShapeDtypeStruct(xs.shape, F32),
        compiler_params=pltpu.CompilerParams(
            dimension_semantics=("arbitrary",), vmem_limit_bytes=VMEM_LIMIT),
        name="experts",
    )(tile_e, n_tiles, xs, wg, wu, wd)


def _ple_kernel(cur_ref, nxt_ref, ys_hbm, h_ref, route_ref, p_ref, wple_ref, gple_ref, wpg_ref,
                o_ref, ybuf0, ybuf1, sem):
    tm = h_ref.shape[0]
    i = pl.program_id(0)
    n_steps = pl.num_programs(0)
    bufs = (ybuf0, ybuf1)

    def start_gather(pos_ref, slot):
        for r in range(2 * tm):
            pltpu.make_async_copy(_tiles_at(ys_hbm, pos_ref[r // tm, r % tm]), _tiles_at(bufs[slot], r),
                                  sem.at[slot]).start(priority=r % 2)

    @pl.when(i == 0)
    def _():
        start_gather(cur_ref, 0)

    def step(slot):
        _wait_tiles(ys_hbm, bufs[slot], 2 * tm, sem.at[slot])
        start_gather(nxt_ref, 1 - slot)
        wts = route_ref[...]
        h = (h_ref[...] + wts[:, 0:1] * _from_row_tiles(bufs[slot], 0, tm)
             + wts[:, 1:2] * _from_row_tiles(bufs[slot], tm, tm))
        gate = jax.nn.sigmoid(_dot(_rms(h, gple_ref[...]).astype(BF16), wpg_ref[...]))
        o_ref[...] = h + _dot(p_ref[...].astype(BF16), wple_ref[...]) * gate

    for slot in range(2):
        pl.when(lax.rem(i, 2) == slot)(functools.partial(step, slot))

    for slot in range(2):
        pl.when((i == n_steps - 1) & (lax.rem(n_steps, 2) == slot))(
            functools.partial(_wait_tiles, ys_hbm, bufs[slot], 2 * tm, sem.at[slot]))


def _ple_call(pos, ys, h, route, p2, wple, gple, wpg):
    t, d = h.shape
    tm = TOK_TM
    n_steps = t // tm
    n_rows = 2 * tm
    idx_spec = lambda f: pl.BlockSpec((ROUTE_FIELDS, tm), lambda i: (0, f(i)), memory_space=pltpu.SMEM)
    const = lambda *shape: pl.BlockSpec(shape, lambda i: (0,) * len(shape))
    tok = lambda w: pl.BlockSpec((tm, w), lambda i: (i, 0))
    return pl.pallas_call(
        _ple_kernel,
        grid=(n_steps,),
        in_specs=[idx_spec(lambda i: i), idx_spec(lambda i: lax.rem(i + 1, n_steps)),
                  pl.BlockSpec(memory_space=pl.ANY),
                  tok(d), tok(LANES), tok(D_PLE), const(D_PLE, d), const(1, d), const(d, d)],
        out_specs=tok(d),
        out_shape=jax.ShapeDtypeStruct((t, d), F32),
        scratch_shapes=[pltpu.VMEM((n_rows * ROW_SUB, LANES), F32), pltpu.VMEM((n_rows * ROW_SUB, LANES), F32),
                        pltpu.SemaphoreType.DMA((2,))],
        compiler_params=pltpu.CompilerParams(
            dimension_semantics=("arbitrary",), vmem_limit_bytes=VMEM_LIMIT),
        name="ple",
    )(pos, pos, ys, h, route, p2, wple, gple, wpg)


def _segment_tables(counts, t):
    tm = EXP_TM
    nt = 2 * t // tm + N_EXPERTS
    padded = ((counts + tm - 1) // tm) * tm
    seg_end = jnp.cumsum(padded)
    seg_start = seg_end - padded
    gap_len = jnp.concatenate([padded - counts, nt * tm - seg_end[-1:]])
    gap_start = jnp.concatenate([seg_start + counts, seg_end[-1:]])
    tile_start = jnp.arange(nt, dtype=jnp.int32) * tm
    tile_e = jnp.minimum(jnp.sum(tile_start[:, None] >= seg_end[None, :], axis=1), N_EXPERTS - 1)
    n_tiles = (seg_end[-1] // tm).reshape(1)
    i32 = lambda a: a.astype(jnp.int32)
    return i32(seg_start), i32(gap_start), i32(gap_len), i32(tile_e), i32(n_tiles), nt * tm


def _layer(h, p_i, g_mix, w_in, g_v_a, w_s, b_s, g_q, g_k, g_out_a, g_out_b, w_out, g_ffn, w_group,
           w_expert, w_gate_e, w_up_e, w_down_e, g_ple, w_ple, w_ple_gate):
    b, s, d = h.shape
    t = b * s
    row = lambda g: g.reshape(1, -1).astype(F32)
    bs = jnp.broadcast_to(b_s[:, :, None], (A_GROUPS, A_CHUNK, A_DH)).astype(F32)
    gk = row(jnp.tile(g_k, B_HEADS))
    gq = (jnp.tile(g_q, B_HEADS) * (1.0 / float(B_DH) ** 0.5)).reshape(D_B, 1).astype(F32)
    head = jnp.arange(D_B) // B_DH
    bd = jnp.where(head[:, None] == head[None, :], 1.0 / B_DH, 0.0).astype(BF16)
    w_rt = jnp.zeros((ROUTER_ROWS, d), F32)
    w_rt = w_rt.at[:N_GROUPS].set(w_group.T).at[N_GROUPS:N_GROUPS + N_EXPERTS].set(w_expert.T)
    wrt_hi, wrt_lo = _split_bf16(w_rt)

    ya, k, qt, vt = _proj_call(h, row(g_mix), w_in.astype(F32), row(g_v_a), w_s.astype(F32), bs,
                               row(g_out_a), gk, gq, bd)
    bound = (BOUND_MARGIN * LOG2E * float(B_DH) ** 0.5) * jnp.max(jnp.abs(g_q)) * jnp.max(jnp.abs(g_k))
    yb = _moba_call(bound.reshape(1, 1).astype(F32), qt, k, vt, row(g_out_b))
    h1, hn_tiles, route, route_i, counts = _outmix_call(
        h.reshape(t, d), ya.reshape(t, D_A), yb.reshape(t, D_B), w_out.astype(BF16), row(g_ffn), wrt_hi, wrt_lo)
    counts = counts[N_GROUPS:N_GROUPS + N_EXPERTS, 0].astype(jnp.int32)
    seg_start, gap_start, gap_len, tile_e, n_tiles, n_sorted_rows = _segment_tables(counts, t)
    pos = _slots_call(seg_start, route_i)
    xs = _dispatch_call(gap_start, gap_len, pos, hn_tiles, n_sorted_rows)
    ys = _experts_call(tile_e, n_tiles, xs, w_gate_e, w_up_e, w_down_e)
    out = _ple_call(pos, ys, h1, route, p_i.reshape(t, D_PLE), w_ple.astype(BF16), row(g_ple),
                    w_ple_gate.astype(BF16))
    return out.reshape(b, s, d)


def kernel(x, p, g_mix, w_in, g_v_a, w_s, b_s, g_q, g_k, g_out_a, g_out_b, w_out, g_ffn, w_group, w_expert,
           w_gate_e, w_up_e, w_down_e, g_ple, w_ple, w_ple_gate):
    params = (g_mix, w_in, g_v_a, w_s, b_s, g_q, g_k, g_out_a, g_out_b, w_out, g_ffn, w_group, w_expert,
              w_gate_e, w_up_e, w_down_e, g_ple, w_ple, w_ple_gate)
    h = x
    for i in range(p.shape[0]):
        h = _layer(h, p[i], *(w[i] for w in params))
    return h
```

```python
import functools

import jax
import jax.numpy as jnp
from jax import lax
from jax.experimental import pallas as pl
from jax.experimental.pallas import tpu as pltpu

F32 = jnp.float32
BF16 = jnp.bfloat16

D_MODEL = 1024
D_A = 512
A_GROUPS = 4
A_DH = D_A // A_GROUPS
A_CHUNK = 128
D_B = 512
B_HEADS = 8
B_DH = D_B // B_HEADS
B_BLOCK = 256
B_TOPK = 3
D_PLE = 256
N_GROUPS = 4
EXP_PER_GROUP = 8
N_EXPERTS = N_GROUPS * EXP_PER_GROUP
D_FF_EXP = 256
EPS = 1e-6
NEG = -1e30
LOG2E = 1.4426950408889634

LANES = 128
HEAD_PAIR = 2 * B_DH
N_PAIRS = B_HEADS // 2
ROUTER_ROWS = 48
ROUTE_FIELDS = 8
VMEM_LIMIT = 56 * 1024 * 1024

PROJ_TM = 1024
OUT_TM = 1024
EXP_TM = 512
MOBA_STEPS_PER_TRIP = 4
BOUND_MARGIN = 1.02
MIN_SHIFTED_SUM = 2.0 ** -64
TOK_TM = 1024
ROW_SUB = D_MODEL // LANES


def _rms(x, g):
    return x * lax.rsqrt(jnp.mean(x * x, axis=-1, keepdims=True) + EPS) * g


def _dot(a, b):
    return jnp.dot(a, b, preferred_element_type=F32)


def _dot_nt(a, b):
    return lax.dot_general(a, b, (((1,), (1,)), ((), ())), preferred_element_type=F32)


def _split_bf16(x):
    hi = x.astype(BF16)
    lo = (x - hi.astype(F32)).astype(BF16)
    return hi, lo


def _proj_kernel(x_ref, gmix_ref, win_ref, gva_ref, ws_ref, bs_ref, goa_ref, gk_ref, gq_ref, bd_ref,
                 ya_ref, k_ref, qt_ref, vt_ref, prod_ref, wuv_ref, wk_ref, wqt_ref, wvt_ref):
    tm = x_ref.shape[1]
    n_chunks = tm // A_CHUNK

    @pl.when((pl.program_id(0) == 0) & (pl.program_id(1) == 0))
    def _():
        q0, k0, v0 = 2 * D_A, 2 * D_A + D_B, 2 * D_A + 2 * D_B
        wuv_ref[...] = win_ref[:, :q0].astype(BF16)
        wk_ref[...] = win_ref[:, k0:v0].astype(BF16)
        wqt_ref[...] = win_ref[:, q0:k0].T.astype(BF16)
        wvt_ref[...] = win_ref[:, v0:].T.astype(BF16)

    xn = _rms(x_ref[0], gmix_ref[...]).astype(BF16)

    uv = _dot(xn, wuv_ref[...])
    u = jax.nn.gelu(uv[:, :D_A])
    v = jax.nn.gelu(uv[:, D_A:])
    row = lax.broadcasted_iota(jnp.int32, (A_CHUNK, A_CHUNK), 0)
    col = lax.broadcasted_iota(jnp.int32, (A_CHUNK, A_CHUNK), 1)
    causal = col <= row
    for g in range(A_GROUPS):
        gs = slice(g * A_DH, (g + 1) * A_DH)
        vn = _rms(v[:, gs], gva_ref[:, gs]).astype(BF16)
        rhs = jnp.concatenate([vn[c * A_CHUNK:(c + 1) * A_CHUNK] for c in range(n_chunks)], axis=1)
        ws = jnp.where(causal, ws_ref[g], 0.0).astype(BF16)
        mixed = _dot(ws, rhs)
        for c in range(n_chunks):
            cs = slice(c * A_CHUNK, (c + 1) * A_CHUNK)
            prod_ref[cs, gs] = u[cs, gs] * (mixed[:, cs] + bs_ref[g])
    ya_ref[0] = _rms(prod_ref[...], goa_ref[...]).astype(ya_ref.dtype)

    k = _dot(xn, wk_ref[...])
    k2_hi, k2_lo = _split_bf16(k * k)
    k_ms = _dot(k2_hi, bd_ref[...]) + _dot(k2_lo, bd_ref[...])
    k_ref[0] = (k * lax.rsqrt(k_ms + EPS) * gk_ref[...]).astype(k_ref.dtype)

    qt = _dot_nt(wqt_ref[...], xn)
    vt = _dot_nt(wvt_ref[...], xn).astype(vt_ref.dtype)
    qn = []
    for h in range(B_HEADS):
        qh = qt[h * B_DH:(h + 1) * B_DH]
        qn.append(qh * lax.rsqrt(jnp.mean(qh * qh, axis=0, keepdims=True) + EPS))
    qn = (jnp.concatenate(qn, axis=0) * gq_ref[...]).astype(qt_ref.dtype)
    for p in range(N_PAIRS):
        ps = slice(p * HEAD_PAIR, (p + 1) * HEAD_PAIR)
        for j in range(tm // B_BLOCK):
            js = slice(j * B_BLOCK, (j + 1) * B_BLOCK)
            qt_ref[0, p, j] = qn[ps, js]
            vt_ref[0, p, j] = vt[ps, js]


def _proj_call(x, gmix, w_in, gva, ws, bs, goa, gk, gq, bd):
    b, s, d = x.shape
    tm = PROJ_TM
    nb = s // B_BLOCK
    jb = tm // B_BLOCK
    const = lambda *shape: pl.BlockSpec(shape, lambda bi, ti: (0,) * len(shape))
    return pl.pallas_call(
        _proj_kernel,
        grid=(b, s // tm),
        in_specs=[
            pl.BlockSpec((1, tm, d), lambda bi, ti: (bi, ti, 0)),
            const(1, d), const(d, 2 * D_A + 3 * D_B),
            const(1, D_A), const(A_GROUPS, A_CHUNK, A_CHUNK), const(A_GROUPS, A_CHUNK, A_DH),
            const(1, D_A), const(1, D_B), const(D_B, 1), const(D_B, D_B),
        ],
        out_specs=[
            pl.BlockSpec((1, tm, D_A), lambda bi, ti: (bi, ti, 0)),
            pl.BlockSpec((1, tm, D_B), lambda bi, ti: (bi, ti, 0)),
            pl.BlockSpec((1, N_PAIRS, jb, HEAD_PAIR, B_BLOCK), lambda bi, ti: (bi, 0, ti, 0, 0)),
            pl.BlockSpec((1, N_PAIRS, jb, HEAD_PAIR, B_BLOCK), lambda bi, ti: (bi, 0, ti, 0, 0)),
        ],
        out_shape=[
            jax.ShapeDtypeStruct((b, s, D_A), BF16),
            jax.ShapeDtypeStruct((b, s, D_B), BF16),
            jax.ShapeDtypeStruct((b, N_PAIRS, nb, HEAD_PAIR, B_BLOCK), BF16),
            jax.ShapeDtypeStruct((b, N_PAIRS, nb, HEAD_PAIR, B_BLOCK), BF16),
        ],
        scratch_shapes=[pltpu.VMEM((tm, D_A), F32), pltpu.VMEM((d, 2 * D_A), BF16), pltpu.VMEM((d, D_B), BF16),
                        pltpu.VMEM((D_B, d), BF16), pltpu.VMEM((D_B, d), BF16)],
        compiler_params=pltpu.CompilerParams(
            dimension_semantics=("arbitrary", "arbitrary"), vmem_limit_bytes=VMEM_LIMIT),
        name="proj",
    )(x, gmix, w_in, gva, ws, bs, goa, gk, gq, bd)


def _moba_kernel(bound_ref, qt_ref, k_ref, vt_ref, gob_ref, y_ref, kmean_ref, qz_ref, sel_ref, m_ref, l_ref,
                 acc_ref, stage0_ref, stage1_ref):
    qb = pl.program_id(1)
    nb = k_ref.shape[1] // B_BLOCK
    nbp = sel_ref.shape[1]

    @pl.when(qb == 0)
    def _():
        rows = [jnp.mean(k_ref[0, j * B_BLOCK:(j + 1) * B_BLOCK, :].astype(F32), axis=0, keepdims=True)
                for j in range(nb)]
        rows += [jnp.zeros((1, D_B), F32)] * (nbp - nb)
        km = jnp.concatenate(rows, axis=0)
        hi = km.astype(BF16)
        rest = km - hi.astype(F32)
        mid = rest.astype(BF16)
        lo = (rest - mid.astype(F32)).astype(BF16)
        for p in range(N_PAIRS):
            ps = slice(p * HEAD_PAIR, (p + 1) * HEAD_PAIR)
            kmean_ref[p] = jnp.concatenate([hi[:, ps], mid[:, ps], lo[:, ps]], axis=0)

    head_row = lax.broadcasted_iota(jnp.int32, (HEAD_PAIR, B_BLOCK), 0) // B_DH
    blk = lax.broadcasted_iota(jnp.int32, (nbp, B_BLOCK), 0)
    past = blk < qb

    for h in range(B_HEADS):
        p, hh = divmod(h, 2)
        q_pair = qt_ref[0, p, 0]
        qz = jnp.where(head_row == hh, q_pair, jnp.zeros_like(q_pair))
        qz_ref[h] = qz
        g3 = _dot(kmean_ref[p], qz)
        gate = (g3[:nbp] + g3[nbp:2 * nbp] + g3[2 * nbp:]) * float(B_DH) ** 0.5
        gate = jnp.where(past, gate, NEG)
        left = gate
        taken = blk < 0
        for _ in range(min(B_TOPK, nb)):
            top = jnp.max(left, axis=0, keepdims=True)
            first = jnp.min(jnp.where(left == top, blk, nbp), axis=0, keepdims=True)
            taken = taken | (blk == first)
            left = jnp.where(blk == first, -jnp.inf, left)
        chosen = taken & (gate > NEG * 0.5)
        sel_ref[h] = jnp.where(chosen, -bound_ref[0, 0], NEG)

    def scores(h, j):
        p = h // 2
        kj = k_ref[0, pl.ds(pl.multiple_of(j * B_BLOCK, B_BLOCK), B_BLOCK), p * HEAD_PAIR:(p + 1) * HEAD_PAIR]
        return _dot(kj, qz_ref[h])

    def values(h, j):
        p, hh = divmod(h, 2)
        return vt_ref[0, p, j, hh * B_DH:(hh + 1) * B_DH, :]

    key_pos = lax.broadcasted_iota(jnp.int32, (B_BLOCK, B_BLOCK), 0)
    q_pos = lax.broadcasted_iota(jnp.int32, (B_BLOCK, B_BLOCK), 1)
    causal = key_pos <= q_pos

    bound = bound_ref[0, 0]
    stage = (stage0_ref, stage1_ref)
    own_shift = jnp.where(causal, -bound, NEG)

    def stage_scores(h, block, shift, parity):
        stage[parity][h] = scores(h, block) * LOG2E + shift

    def consume(h, s, parity):
        pr = jnp.exp2(stage[parity][h])
        l_ref[h] += jnp.sum(pr, axis=0, keepdims=True)
        acc_ref[h] += _dot(values(h, jnp.where(s == 0, qb, s - 1)), pr.astype(BF16))

    def full_step(s, parity):
        for h in range(B_HEADS):
            stage_scores(h, s - 1, sel_ref[h, pl.ds(s - 1, 1), :], parity)
            consume(h, s - 1, 1 - parity)

    for h in range(B_HEADS):
        stage_scores(h, qb, own_shift, 0)
        l_ref[h] = jnp.zeros((1, B_BLOCK), F32)
        acc_ref[h] = jnp.zeros((B_DH, B_BLOCK), F32)

    def trip(u, carry):
        for v in range(MOBA_STEPS_PER_TRIP):
            full_step(MOBA_STEPS_PER_TRIP * u + v + 1, (v + 1) % 2)
        return carry

    lax.fori_loop(0, qb // MOBA_STEPS_PER_TRIP, trip, 0)
    done = (qb // MOBA_STEPS_PER_TRIP) * MOBA_STEPS_PER_TRIP
    for v in range(MOBA_STEPS_PER_TRIP - 1):
        pl.when(qb - done > v)(functools.partial(full_step, done + v + 1, (v + 1) % 2))

    def finish():
        yt = jnp.concatenate([acc_ref[h] / l_ref[h] for h in range(B_HEADS)], axis=0)
        yt = yt * lax.rsqrt(jnp.mean(yt * yt, axis=0, keepdims=True) + EPS)
        y_ref[0] = (yt.T * gob_ref[...]).astype(y_ref.dtype)

    for parity in range(2):
        @pl.when(lax.rem(qb, 2) == parity)
        def _(parity=parity):
            for h in range(B_HEADS):
                consume(h, qb, parity)
            finish()

    smallest = jnp.min(jnp.concatenate([l_ref[h] for h in range(B_HEADS)], axis=0))

    @pl.when(jnp.logical_not(smallest > MIN_SHIFTED_SUM))
    def _():
        def online(h, j, own, first):
            visible = causal if own else sel_ref[h, pl.ds(j, 1), :] > NEG * 0.5
            sc = jnp.where(visible, scores(h, j) * LOG2E, NEG)
            m_tile = jnp.max(sc, axis=0, keepdims=True)
            m_new = m_tile if first else jnp.maximum(m_ref[h], m_tile)
            pr = jnp.exp2(sc - m_new)
            l_new, acc_new = jnp.sum(pr, axis=0, keepdims=True), _dot(values(h, j), pr.astype(BF16))
            if not first:
                alpha = jnp.exp2(m_ref[h] - m_new)
                l_new, acc_new = alpha * l_ref[h] + l_new, alpha * acc_ref[h] + acc_new
            m_ref[h], l_ref[h], acc_ref[h] = m_new, l_new, acc_new

        for h in range(B_HEADS):
            online(h, qb, True, True)

        def past(j, carry):
            for h in range(B_HEADS):
                online(h, j, False, False)
            return carry

        lax.fori_loop(0, qb, past, 0)
        finish()


def _moba_call(bound, qt, k, vt, gob):
    b, s, _ = k.shape
    nb = s // B_BLOCK
    nbp = -(-nb // 8) * 8
    return pl.pallas_call(
        _moba_kernel,
        grid=(b, nb),
        in_specs=[
            pl.BlockSpec(memory_space=pltpu.SMEM),
            pl.BlockSpec((1, N_PAIRS, 1, HEAD_PAIR, B_BLOCK), lambda bi, qi: (bi, 0, qi, 0, 0)),
            pl.BlockSpec((1, s, D_B), lambda bi, qi: (bi, 0, 0)),
            pl.BlockSpec((1, N_PAIRS, nb, HEAD_PAIR, B_BLOCK), lambda bi, qi: (bi, 0, 0, 0, 0)),
            pl.BlockSpec((1, D_B), lambda bi, qi: (0, 0)),
        ],
        out_specs=pl.BlockSpec((1, B_BLOCK, D_B), lambda bi, qi: (bi, qi, 0)),
        out_shape=jax.ShapeDtypeStruct((b, s, D_B), BF16),
        scratch_shapes=[
            pltpu.VMEM((N_PAIRS, 3 * nbp, HEAD_PAIR), BF16),
            pltpu.VMEM((B_HEADS, HEAD_PAIR, B_BLOCK), BF16),
            pltpu.VMEM((B_HEADS, nbp, B_BLOCK), F32),
            pltpu.VMEM((B_HEADS, 1, B_BLOCK), F32),
            pltpu.VMEM((B_HEADS, 1, B_BLOCK), F32),
            pltpu.VMEM((B_HEADS, B_DH, B_BLOCK), F32),
            pltpu.VMEM((B_HEADS, B_BLOCK, B_BLOCK), F32),
            pltpu.VMEM((B_HEADS, B_BLOCK, B_BLOCK), F32),
        ],
        compiler_params=pltpu.CompilerParams(
            dimension_semantics=("parallel", "arbitrary"), vmem_limit_bytes=VMEM_LIMIT),
        name="moba",
    )(bound, qt, k, vt, gob)


def _outmix_kernel(x_ref, ya_ref, yb_ref, wout_ref, gffn_ref, wrt_hi_ref, wrt_lo_ref,
                   h_ref, hn_ref, route_ref, route_i_ref, counts_ref, cnt_ref):
    h = x_ref[...] + _dot(ya_ref[...], wout_ref[:D_A]) + _dot(yb_ref[...], wout_ref[D_A:])
    h_ref[...] = h
    hn = _rms(h, gffn_ref[...])
    _to_row_tiles(hn_ref, hn)
    tm = hn.shape[0]

    hn_hi, hn_lo = _split_bf16(hn)
    lg = _dot_nt(wrt_hi_ref[...], hn_hi) + (_dot_nt(wrt_hi_ref[...], hn_lo) + _dot_nt(wrt_lo_ref[...], hn_hi))
    row = lax.broadcasted_iota(jnp.int32, lg.shape, 0)

    def first_row(mask):
        return jnp.min(jnp.where(mask, row, ROUTER_ROWS), axis=0, keepdims=True)

    is_g = row < N_GROUPS
    g_max = jnp.max(jnp.where(is_g, lg, -jnp.inf), axis=0, keepdims=True)
    p_g = 1.0 / jnp.sum(jnp.where(is_g, jnp.exp(lg - g_max), 0.0), axis=0, keepdims=True)
    g_sel = first_row(is_g & (lg == g_max))
    e_row = row - N_GROUPS
    in_grp = (e_row >= 0) & (e_row < N_EXPERTS) & (lax.shift_right_arithmetic(e_row, 3) == g_sel)
    l1 = jnp.max(jnp.where(in_grp, lg, -jnp.inf), axis=0, keepdims=True)
    i1 = first_row(in_grp & (lg == l1))
    rest = in_grp & (row != i1)
    l2 = jnp.max(jnp.where(rest, lg, -jnp.inf), axis=0, keepdims=True)
    i2 = first_row(rest & (lg == l2))
    e2 = jnp.exp(l2 - l1)
    w1 = 1.0 / (1.0 + e2)
    w2 = e2 / (1.0 + e2)

    @pl.when(pl.program_id(0) == 0)
    def _():
        cnt_ref[...] = jnp.zeros_like(cnt_ref)

    picked = jnp.where((row == i1) | (row == i2), 1.0, 0.0)
    tok_r = lax.broadcasted_iota(jnp.int32, (tm, tm), 0)
    tok_c = lax.broadcasted_iota(jnp.int32, (tm, tm), 1)
    earlier = jnp.where(tok_r < tok_c, 1.0, 0.0).astype(BF16)
    before = _dot(picked.astype(BF16), earlier) + cnt_ref[:, 0:1]
    r1 = jnp.sum(jnp.where(row == i1, before, 0.0), axis=0, keepdims=True)
    r2 = jnp.sum(jnp.where(row == i2, before, 0.0), axis=0, keepdims=True)
    cnt_ref[...] += jnp.sum(picked, axis=1, keepdims=True)
    counts_ref[...] = cnt_ref[...]

    e1, e2x = i1 - N_GROUPS, i2 - N_GROUPS
    route_i_ref[...] = jnp.concatenate(
        [e1, e2x, r1.astype(jnp.int32), r2.astype(jnp.int32), jnp.zeros((ROUTE_FIELDS - 4, tm), jnp.int32)], axis=0)
    wts = jnp.concatenate([p_g * w1, p_g * w2, jnp.zeros((LANES - 2, tm), F32)], axis=0)
    route_ref[...] = wts.T


def _outmix_call(x2, ya2, yb2, wout, gffn, wrt_hi, wrt_lo):
    t, d = x2.shape
    tm = OUT_TM
    const = lambda *shape: pl.BlockSpec(shape, lambda ti: (0,) * len(shape))
    tok = lambda w: pl.BlockSpec((tm, w), lambda ti: (ti, 0))
    return pl.pallas_call(
        _outmix_kernel,
        grid=(t // tm,),
        in_specs=[tok(d), tok(D_A), tok(D_B), const(d, d), const(1, d),
                  const(ROUTER_ROWS, d), const(ROUTER_ROWS, d)],
        out_specs=[tok(d), pl.BlockSpec((tm * ROW_SUB, LANES), lambda ti: (ti, 0)), tok(LANES),
                   pl.BlockSpec((ROUTE_FIELDS, tm), lambda ti: (0, ti)), const(ROUTER_ROWS, LANES)],
        out_shape=[jax.ShapeDtypeStruct((t, d), F32), jax.ShapeDtypeStruct((t * ROW_SUB, LANES), F32),
                   jax.ShapeDtypeStruct((t, LANES), F32),
                   jax.ShapeDtypeStruct((ROUTE_FIELDS, t), jnp.int32),
                   jax.ShapeDtypeStruct((ROUTER_ROWS, LANES), F32)],
        scratch_shapes=[pltpu.VMEM((ROUTER_ROWS, LANES), F32)],
        compiler_params=pltpu.CompilerParams(
            dimension_semantics=("arbitrary",), vmem_limit_bytes=VMEM_LIMIT),
        name="outmix",
    )(x2, ya2, yb2, wout, gffn, wrt_hi, wrt_lo)


def _to_row_tiles(ref, x):
    n = x.shape[0]
    for s in range(ROW_SUB):
        ref[pl.ds(s, n, stride=ROW_SUB), :] = x[:, s * LANES:(s + 1) * LANES]


def _from_row_tiles(ref, first, n):
    return jnp.concatenate(
        [ref[pl.ds(first * ROW_SUB + s, n, stride=ROW_SUB), :] for s in range(ROW_SUB)], axis=1)


def _tiles_at(ref, row, n=1):
    return ref.at[pl.ds(pl.multiple_of(row * ROW_SUB, ROW_SUB), n * ROW_SUB), :]


def _wait_tiles(hbm, vmem_ref, n_tiles, sem):
    cap = vmem_ref.shape[0] // ROW_SUB
    while n_tiles > 0:
        n = min(n_tiles, cap)
        pltpu.make_async_copy(hbm.at[pl.ds(0, n * ROW_SUB), :], vmem_ref.at[pl.ds(0, n * ROW_SUB), :], sem).wait()
        n_tiles -= n


def _slots_kernel(seg_start_ref, route_i_ref, pos_ref):
    experts = route_i_ref[0:2, :]
    start = jnp.zeros_like(experts)
    for e in range(N_EXPERTS):
        start = jnp.where(experts == e, seg_start_ref[e], start)
    pos_ref[...] = jnp.zeros_like(pos_ref)
    pos_ref[0:2, :] = start + route_i_ref[2:4, :]


def _slots_call(seg_start, route_i):
    return pl.pallas_call(
        _slots_kernel,
        grid_spec=pltpu.PrefetchScalarGridSpec(
            num_scalar_prefetch=1,
            grid=(1,),
            in_specs=[pl.BlockSpec(route_i.shape, lambda i, s: (0, 0))],
            out_specs=pl.BlockSpec(route_i.shape, lambda i, s: (0, 0)),
        ),
        out_shape=jax.ShapeDtypeStruct(route_i.shape, jnp.int32),
        name="slots",
    )(seg_start, route_i)


def _dispatch_kernel(gap_start_ref, gap_len_ref, pos_ref, hn_ref, xs_hbm, zero_ref, sem, fill_sem):
    tm = hn_ref.shape[0] // ROW_SUB

    @pl.when(pl.program_id(0) == 0)
    def _():
        zero_ref[...] = jnp.zeros_like(zero_ref)
        fills = []
        for g in range(N_EXPERTS):
            row, n = gap_start_ref[g], gap_len_ref[g]
            size = EXP_TM // 2
            while size >= 1:
                fills.append(((n & size) != 0, row, size))
                row = row + (n & size)
                size //= 2
        for tile in range(N_EXPERTS):
            fills.append((tile * EXP_TM < gap_len_ref[N_EXPERTS], gap_start_ref[N_EXPERTS] + tile * EXP_TM, EXP_TM))
        copies = [(cond, pltpu.make_async_copy(_tiles_at(zero_ref, 0, size), _tiles_at(xs_hbm, row, size), fill_sem))
                  for cond, row, size in fills]
        for cond, copy in copies:
            pl.when(cond)(copy.start)
        for cond, copy in copies:
            pl.when(cond)(copy.wait)

    for r in range(2 * tm):
        k, tok = divmod(r, tm)
        pltpu.make_async_copy(_tiles_at(hn_ref, tok), _tiles_at(xs_hbm, pos_ref[k, tok]), sem).start(priority=r % 2)
    _wait_tiles(xs_hbm, hn_ref, 2 * tm, sem)


def _dispatch_call(gap_start, gap_len, pos, hn_tiles, n_sorted_rows):
    tm = TOK_TM
    return pl.pallas_call(
        _dispatch_kernel,
        grid_spec=pltpu.PrefetchScalarGridSpec(
            num_scalar_prefetch=2,
            grid=(pos.shape[1] // tm,),
            in_specs=[pl.BlockSpec((ROUTE_FIELDS, tm), lambda i, *_: (0, i), memory_space=pltpu.SMEM),
                      pl.BlockSpec((tm * ROW_SUB, LANES), lambda i, *_: (i, 0))],
            out_specs=pl.BlockSpec(memory_space=pl.ANY),
            scratch_shapes=[pltpu.VMEM((EXP_TM * ROW_SUB, LANES), F32), pltpu.SemaphoreType.DMA(()),
                            pltpu.SemaphoreType.DMA(())],
        ),
        out_shape=jax.ShapeDtypeStruct((n_sorted_rows * ROW_SUB, LANES), F32),
        compiler_params=pltpu.CompilerParams(
            dimension_semantics=("arbitrary",), vmem_limit_bytes=VMEM_LIMIT),
        name="dispatch",
    )(gap_start, gap_len, pos, hn_tiles)


def _experts_kernel(tile_e_ref, n_tiles_ref, xs_ref, wg_ref, wu_ref, wd_ref, y_ref, wg16, wu16, wd16):
    i = pl.program_id(0)
    tm = xs_ref.shape[0] // ROW_SUB

    @pl.when((i == 0) | (tile_e_ref[i] != tile_e_ref[jnp.maximum(i - 1, 0)]))
    def _():
        wg16[...] = wg_ref[0].astype(BF16)
        wu16[...] = wu_ref[0].astype(BF16)
        wd16[...] = wd_ref[0].astype(BF16)

    @pl.when(i < n_tiles_ref[0])
    def _():
        x = _from_row_tiles(xs_ref, 0, tm).astype(BF16)
        a = jax.nn.silu(_dot(x, wg16[...])) * _dot(x, wu16[...])
        _to_row_tiles(y_ref, _dot(a.astype(BF16), wd16[...]))

    @pl.when(i >= n_tiles_ref[0])
    def _():
        y_ref[...] = jnp.zeros_like(y_ref)


def _experts_call(tile_e, n_tiles, xs, wg, wu, wd):
    d = wg.shape[1]
    tm = EXP_TM
    nt = xs.shape[0] // (tm * ROW_SUB)
    used = lambda i, n: jnp.minimum(i, n[0] - 1)
    w_spec = lambda *shape: pl.BlockSpec((1,) + shape, lambda i, te, n: (te[used(i, n)], 0, 0))
    return pl.pallas_call(
        _experts_kernel,
        grid_spec=pltpu.PrefetchScalarGridSpec(
            num_scalar_prefetch=2,
            grid=(nt,),
            in_specs=[pl.BlockSpec((tm * ROW_SUB, LANES), lambda i, te, n: (used(i, n), 0)),
                      w_spec(d, D_FF_EXP), w_spec(d, D_FF_EXP), w_spec(D_FF_EXP, d)],
            out_specs=pl.BlockSpec((tm * ROW_SUB, LANES), lambda i, te, n: (i, 0)),
            scratch_shapes=[pltpu.VMEM((d, D_FF_EXP), BF16), pltpu.VMEM((d, D_FF_EXP), BF16),
                            pltpu.VMEM((D_FF_EXP, d), BF16)],
        ),
        out_shape=jax.ShapeDtypeStruct(xs.shape, F32),
        compiler_params=pltpu.CompilerParams(
            dimension_semantics=("arbitrary",), vmem_limit_bytes=VMEM_LIMIT),
        name="experts",
    )(tile_e, n_tiles, xs, wg, wu, wd)


def _ple_kernel(cur_ref, nxt_ref, ys_hbm, h_ref, route_ref, p_ref, wple_ref, gple_ref, wpg_ref,
                o_ref, ybuf0, ybuf1, sem):
    tm = h_ref.shape[0]
    i = pl.program_id(0)
    n_steps = pl.num_programs(0)
    bufs = (ybuf0, ybuf1)

    def start_gather(pos_ref, slot):
        for r in range(2 * tm):
            pltpu.make_async_copy(_tiles_at(ys_hbm, pos_ref[r // tm, r % tm]), _tiles_at(bufs[slot], r),
                                  sem.at[slot]).start(priority=r % 2)

    @pl.when(i == 0)
    def _():
        start_gather(cur_ref, 0)

    def step(slot):
        _wait_tiles(ys_hbm, bufs[slot], 2 * tm, sem.at[slot])
        start_gather(nxt_ref, 1 - slot)
        wts = route_ref[...]
        h = (h_ref[...] + wts[:, 0:1] * _from_row_tiles(bufs[slot], 0, tm)
             + wts[:, 1:2] * _from_row_tiles(bufs[slot], tm, tm))
        gate = jax.nn.sigmoid(_dot(_rms(h, gple_ref[...]).astype(BF16), wpg_ref[...]))
        o_ref[...] = h + _dot(p_ref[...].astype(BF16), wple_ref[...]) * gate

    for slot in range(2):
        pl.when(lax.rem(i, 2) == slot)(functools.partial(step, slot))

    for slot in range(2):
        pl.when((i == n_steps - 1) & (lax.rem(n_steps, 2) == slot))(
            functools.partial(_wait_tiles, ys_hbm, bufs[slot], 2 * tm, sem.at[slot]))


def _ple_call(pos, ys, h, route, p2, wple, gple, wpg):
    t, d = h.shape
    tm = TOK_TM
    n_steps = t // tm
    n_rows = 2 * tm
    idx_spec = lambda f: pl.BlockSpec((ROUTE_FIELDS, tm), lambda i: (0, f(i)), memory_space=pltpu.SMEM)
    const = lambda *shape: pl.BlockSpec(shape, lambda i: (0,) * len(shape))
    tok = lambda w: pl.BlockSpec((tm, w), lambda i: (i, 0))
    return pl.pallas_call(
        _ple_kernel,
        grid=(n_steps,),
        in_specs=[idx_spec(lambda i: i), idx_spec(lambda i: lax.rem(i + 1, n_steps)),
                  pl.BlockSpec(memory_space=pl.ANY),
                  tok(d), tok(LANES), tok(D_PLE), const(D_PLE, d), const(1, d), const(d, d)],
        out_specs=tok(d),
        out_shape=jax.ShapeDtypeStruct((t, d), F32),
        scratch_shapes=[pltpu.VMEM((n_rows * ROW_SUB, LANES), F32), pltpu.VMEM((n_rows * ROW_SUB, LANES), F32),
                        pltpu.SemaphoreType.DMA((2,))],
        compiler_params=pltpu.CompilerParams(
            dimension_semantics=("arbitrary",), vmem_limit_bytes=VMEM_LIMIT),
        name="ple",
    )(pos, pos, ys, h, route, p2, wple, gple, wpg)


def _segment_tables(counts, t):
    tm = EXP_TM
    nt = 2 * t // tm + N_EXPERTS
    padded = ((counts + tm - 1) // tm) * tm
    seg_end = jnp.cumsum(padded)
    seg_start = seg_end - padded
    gap_len = jnp.concatenate([padded - counts, nt * tm - seg_end[-1:]])
    gap_start = jnp.concatenate([seg_start + counts, seg_end[-1:]])
    tile_start = jnp.arange(nt, dtype=jnp.int32) * tm
    tile_e = jnp.minimum(jnp.sum(tile_start[:, None] >= seg_end[None, :], axis=1), N_EXPERTS - 1)
    n_tiles = (seg_end[-1] // tm).reshape(1)
    i32 = lambda a: a.astype(jnp.int32)
    return i32(seg_start), i32(gap_start), i32(gap_len), i32(tile_e), i32(n_tiles), nt * tm


def _layer(h, p_i, g_mix, w_in, g_v_a, w_s, b_s, g_q, g_k, g_out_a, g_out_b, w_out, g_ffn, w_group,
           w_expert, w_gate_e, w_up_e, w_down_e, g_ple, w_ple, w_ple_gate):
    b, s, d = h.shape
    t = b * s
    row = lambda g: g.reshape(1, -1).astype(F32)
    bs = jnp.broadcast_to(b_s[:, :, None], (A_GROUPS, A_CHUNK, A_DH)).astype(F32)
    gk = row(jnp.tile(g_k, B_HEADS))
    gq = (jnp.tile(g_q, B_HEADS) * (1.0 / float(B_DH) ** 0.5)).reshape(D_B, 1).astype(F32)
    head = jnp.arange(D_B) // B_DH
    bd = jnp.where(head[:, None] == head[None, :], 1.0 / B_DH, 0.0).astype(BF16)
    w_rt = jnp.zeros((ROUTER_ROWS, d), F32)
    w_rt = w_rt.at[:N_GROUPS].set(w_group.T).at[N_GROUPS:N_GROUPS + N_EXPERTS].set(w_expert.T)
    wrt_hi, wrt_lo = _split_bf16(w_rt)

    ya, k, qt, vt = _proj_call(h, row(g_mix), w_in.astype(F32), row(g_v_a), w_s.astype(F32), bs,
                               row(g_out_a), gk, gq, bd)
    bound = (BOUND_MARGIN * LOG2E * float(B_DH) ** 0.5) * jnp.max(jnp.abs(g_q)) * jnp.max(jnp.abs(g_k))
    yb = _moba_call(bound.reshape(1, 1).astype(F32), qt, k, vt, row(g_out_b))
    h1, hn_tiles, route, route_i, counts = _outmix_call(
        h.reshape(t, d), ya.reshape(t, D_A), yb.reshape(t, D_B), w_out.astype(BF16), row(g_ffn), wrt_hi, wrt_lo)
    counts = counts[N_GROUPS:N_GROUPS + N_EXPERTS, 0].astype(jnp.int32)
    seg_start, gap_start, gap_len, tile_e, n_tiles, n_sorted_rows = _segment_tables(counts, t)
    pos = _slots_call(seg_start, route_i)
    xs = _dispatch_call(gap_start, gap_len, pos, hn_tiles, n_sorted_rows)
    ys = _experts_call(tile_e, n_tiles, xs, w_gate_e, w_up_e, w_down_e)
    out = _ple_call(pos, ys, h1, route, p_i.reshape(t, D_PLE), w_ple.astype(BF16), row(g_ple),
                    w_ple_gate.astype(BF16))
    return out.reshape(b, s, d)


def kernel(x, p, g_mix, w_in, g_v_a, w_s, b_s, g_q, g_k, g_out_a, g_out_b, w_out, g_ffn, w_group, w_expert,
           w_gate_e, w_up_e, w_down_e, g_ple, w_ple, w_ple_gate):
    params = (g_mix, w_in, g_v_a, w_s, b_s, g_q, g_k, g_out_a, g_out_b, w_out, g_ffn, w_group, w_expert,
              w_gate_e, w_up_e, w_down_e, g_ple, w_ple, w_ple_gate)
    h = x
    for i in range(p.shape[0]):
        h = _layer(h, p[i], *(w[i] for w in params))
    return h
```

```python
import functools

import jax
import jax.numpy as jnp
from jax import lax
from jax.experimental import pallas as pl
from jax.experimental.pallas import tpu as pltpu

F32 = jnp.float32
BF16 = jnp.bfloat16

D_MODEL = 1024
D_A = 512
A_GROUPS = 4
A_DH = D_A // A_GROUPS
A_CHUNK = 128
D_B = 512
B_HEADS = 8
B_DH = D_B // B_HEADS
B_BLOCK = 256
B_TOPK = 3
D_PLE = 256
N_GROUPS = 4
EXP_PER_GROUP = 8
N_EXPERTS = N_GROUPS * EXP_PER_GROUP
D_FF_EXP = 256
EPS = 1e-6
NEG = -1e30
LOG2E = 1.4426950408889634

LANES = 128
HEAD_PAIR = 2 * B_DH
N_PAIRS = B_HEADS // 2
ROUTER_ROWS = 48
ROUTE_FIELDS = 8
VMEM_LIMIT = 56 * 1024 * 1024

PROJ_TM = 1024
OUT_TM = 1024
EXP_TM = 512
MOBA_STEPS_PER_TRIP = 4
BOUND_MARGIN = 1.02
MIN_SHIFTED_SUM = 2.0 ** -64
TOK_TM = 1024
ROW_SUB = D_MODEL // LANES


def _rms(x, g):
    return x * lax.rsqrt(jnp.mean(x * x, axis=-1, keepdims=True) + EPS) * g


def _dot(a, b):
    return jnp.dot(a, b, preferred_element_type=F32)


def _dot_nt(a, b):
    return lax.dot_general(a, b, (((1,), (1,)), ((), ())), preferred_element_type=F32)


def _split_bf16(x):
    hi = x.astype(BF16)
    lo = (x - hi.astype(F32)).astype(BF16)
    return hi, lo


def _proj_kernel(x_ref, gmix_ref, win_ref, gva_ref, ws_ref, bs_ref, goa_ref, gk_ref, gq_ref, bd_ref,
                 ya_ref, k_ref, qt_ref, vt_ref, prod_ref, wuv_ref, wk_ref, wqt_ref, wvt_ref):
    tm = x_ref.shape[1]
    n_chunks = tm // A_CHUNK

    @pl.when((pl.program_id(0) == 0) & (pl.program_id(1) == 0))
    def _():
        q0, k0, v0 = 2 * D_A, 2 * D_A + D_B, 2 * D_A + 2 * D_B
        wuv_ref[...] = win_ref[:, :q0].astype(BF16)
        wk_ref[...] = win_ref[:, k0:v0].astype(BF16)
        wqt_ref[...] = win_ref[:, q0:k0].T.astype(BF16)
        wvt_ref[...] = win_ref[:, v0:].T.astype(BF16)

    xn = _rms(x_ref[0], gmix_ref[...]).astype(BF16)

    uv = _dot(xn, wuv_ref[...])
    u = jax.nn.gelu(uv[:, :D_A])
    v = jax.nn.gelu(uv[:, D_A:])
    row = lax.broadcasted_iota(jnp.int32, (A_CHUNK, A_CHUNK), 0)
    col = lax.broadcasted_iota(jnp.int32, (A_CHUNK, A_CHUNK), 1)
    causal = col <= row
    for g in range(A_GROUPS):
        gs = slice(g * A_DH, (g + 1) * A_DH)
        vn = _rms(v[:, gs], gva_ref[:, gs]).astype(BF16)
        rhs = jnp.concatenate([vn[c * A_CHUNK:(c + 1) * A_CHUNK] for c in range(n_chunks)], axis=1)
        ws = jnp.where(causal, ws_ref[g], 0.0).astype(BF16)
        mixed = _dot(ws, rhs)
        for c in range(n_chunks):
            cs = slice(c * A_CHUNK, (c + 1) * A_CHUNK)
            prod_ref[cs, gs] = u[cs, gs] * (mixed[:, cs] + bs_ref[g])
    ya_ref[0] = _rms(prod_ref[...], goa_ref[...]).astype(ya_ref.dtype)

    k = _dot(xn, wk_ref[...])
    k2_hi, k2_lo = _split_bf16(k * k)
    k_ms = _dot(k2_hi, bd_ref[...]) + _dot(k2_lo, bd_ref[...])
    k_ref[0] = (k * lax.rsqrt(k_ms + EPS) * gk_ref[...]).astype(k_ref.dtype)

    qt = _dot_nt(wqt_ref[...], xn)
    vt = _dot_nt(wvt_ref[...], xn).astype(vt_ref.dtype)
    qn = []
    for h in range(B_HEADS):
        qh = qt[h * B_DH:(h + 1) * B_DH]
        qn.append(qh * lax.rsqrt(jnp.mean(qh * qh, axis=0, keepdims=True) + EPS))
    qn = (jnp.concatenate(qn, axis=0) * gq_ref[...]).astype(qt_ref.dtype)
    for p in range(N_PAIRS):
        ps = slice(p * HEAD_PAIR, (p + 1) * HEAD_PAIR)
        for j in range(tm // B_BLOCK):
            js = slice(j * B_BLOCK, (j + 1) * B_BLOCK)
            qt_ref[0, p, j] = qn[ps, js]
            vt_ref[0, p, j] = vt[ps, js]


def _proj_call(x, gmix, w_in, gva, ws, bs, goa, gk, gq, bd):
    b, s, d = x.shape
    tm = PROJ_TM
    nb = s // B_BLOCK
    jb = tm // B_BLOCK
    const = lambda *shape: pl.BlockSpec(shape, lambda bi, ti: (0,) * len(shape))
    return pl.pallas_call(
        _proj_kernel,
        grid=(b, s // tm),
        in_specs=[
            pl.BlockSpec((1, tm, d), lambda bi, ti: (bi, ti, 0)),
            const(1, d), const(d, 2 * D_A + 3 * D_B),
            const(1, D_A), const(A_GROUPS, A_CHUNK, A_CHUNK), const(A_GROUPS, A_CHUNK, A_DH),
            const(1, D_A), const(1, D_B), const(D_B, 1), const(D_B, D_B),
        ],
        out_specs=[
            pl.BlockSpec((1, tm, D_A), lambda bi, ti: (bi, ti, 0)),
            pl.BlockSpec((1, tm, D_B), lambda bi, ti: (bi, ti, 0)),
            pl.BlockSpec((1, N_PAIRS, jb, HEAD_PAIR, B_BLOCK), lambda bi, ti: (bi, 0, ti, 0, 0)),
            pl.BlockSpec((1, N_PAIRS, jb, HEAD_PAIR, B_BLOCK), lambda bi, ti: (bi, 0, ti, 0, 0)),
        ],
        out_shape=[
            jax.ShapeDtypeStruct((b, s, D_A), BF16),
            jax.ShapeDtypeStruct((b, s, D_B), BF16),
            jax.ShapeDtypeStruct((b, N_PAIRS, nb, HEAD_PAIR, B_BLOCK), BF16),
            jax.ShapeDtypeStruct((b, N_PAIRS, nb, HEAD_PAIR, B_BLOCK), BF16),
        ],
        scratch_shapes=[pltpu.VMEM((tm, D_A), F32), pltpu.VMEM((d, 2 * D_A), BF16), pltpu.VMEM((d, D_B), BF16),
                        pltpu.VMEM((D_B, d), BF16), pltpu.VMEM((D_B, d), BF16)],
        compiler_params=pltpu.CompilerParams(
            dimension_semantics=("arbitrary", "arbitrary"), vmem_limit_bytes=VMEM_LIMIT),
        name="proj",
    )(x, gmix, w_in, gva, ws, bs, goa, gk, gq, bd)


def _moba_kernel(bound_ref, qt_ref, k_ref, vt_ref, gob_ref, y_ref, kmean_ref, qz_ref, sel_ref, m_ref, l_ref,
                 acc_ref, stage0_ref, stage1_ref):
    qb = pl.program_id(1)
    nb = k_ref.shape[1] // B_BLOCK
    nbp = sel_ref.shape[1]

    @pl.when(qb == 0)
    def _():
        rows = [jnp.mean(k_ref[0, j * B_BLOCK:(j + 1) * B_BLOCK, :].astype(F32), axis=0, keepdims=True)
                for j in range(nb)]
        rows += [jnp.zeros((1, D_B), F32)] * (nbp - nb)
        km = jnp.concatenate(rows, axis=0)
        hi = km.astype(BF16)
        rest = km - hi.astype(F32)
        mid = rest.astype(BF16)
        lo = (rest - mid.astype(F32)).astype(BF16)
        for p in range(N_PAIRS):
            ps = slice(p * HEAD_PAIR, (p + 1) * HEAD_PAIR)
            kmean_ref[p] = jnp.concatenate([hi[:, ps], mid[:, ps], lo[:, ps]], axis=0)

    head_row = lax.broadcasted_iota(jnp.int32, (HEAD_PAIR, B_BLOCK), 0) // B_DH
    blk = lax.broadcasted_iota(jnp.int32, (nbp, B_BLOCK), 0)
    past = blk < qb

    for h in range(B_HEADS):
        p, hh = divmod(h, 2)
        q_pair = qt_ref[0, p, 0]
        qz = jnp.where(head_row == hh, q_pair, jnp.zeros_like(q_pair))
        qz_ref[h] = qz
        g3 = _dot(kmean_ref[p], qz)
        gate = (g3[:nbp] + g3[nbp:2 * nbp] + g3[2 * nbp:]) * float(B_DH) ** 0.5
        gate = jnp.where(past, gate, NEG)
        left = gate
        taken = blk < 0
        for _ in range(min(B_TOPK, nb)):
            top = jnp.max(left, axis=0, keepdims=True)
            first = jnp.min(jnp.where(left == top, blk, nbp), axis=0, keepdims=True)
            taken = taken | (blk == first)
            left = jnp.where(blk == first, -jnp.inf, left)
        chosen = taken & (gate > NEG * 0.5)
        sel_ref[h] = jnp.where(chosen, -bound_ref[0, 0], NEG)

    def scores(h, j):
        p = h // 2
        kj = k_ref[0, pl.ds(pl.multiple_of(j * B_BLOCK, B_BLOCK), B_BLOCK), p * HEAD_PAIR:(p + 1) * HEAD_PAIR]
        return _dot(kj, qz_ref[h])

    def values(h, j):
        p, hh = divmod(h, 2)
        return vt_ref[0, p, j, hh * B_DH:(hh + 1) * B_DH, :]

    key_pos = lax.broadcasted_iota(jnp.int32, (B_BLOCK, B_BLOCK), 0)
    q_pos = lax.broadcasted_iota(jnp.int32, (B_BLOCK, B_BLOCK), 1)
    causal = key_pos <= q_pos

    bound = bound_ref[0, 0]
    stage = (stage0_ref, stage1_ref)
    own_shift = jnp.where(causal, -bound, NEG)

    def stage_scores(h, block, shift, parity):
        stage[parity][h] = scores(h, block) * LOG2E + shift

    def consume(h, s, parity):
        pr = jnp.exp2(stage[parity][h])
        l_ref[h] += jnp.sum(pr, axis=0, keepdims=True)
        acc_ref[h] += _dot(values(h, jnp.where(s == 0, qb, s - 1)), pr.astype(BF16))

    def full_step(s, parity):
        for h in range(B_HEADS):
            stage_scores(h, s - 1, sel_ref[h, pl.ds(s - 1, 1), :], parity)
            consume(h, s - 1, 1 - parity)

    for h in range(B_HEADS):
        stage_scores(h, qb, own_shift, 0)
        l_ref[h] = jnp.zeros((1, B_BLOCK), F32)
        acc_ref[h] = jnp.zeros((B_DH, B_BLOCK), F32)

    def trip(u, carry):
        for v in range(MOBA_STEPS_PER_TRIP):
            full_step(MOBA_STEPS_PER_TRIP * u + v + 1, (v + 1) % 2)
        return carry

    lax.fori_loop(0, qb // MOBA_STEPS_PER_TRIP, trip, 0)
    done = (qb // MOBA_STEPS_PER_TRIP) * MOBA_STEPS_PER_TRIP
    left_over = qb - done
    run = MOBA_STEPS_PER_TRIP // 2
    while run:
        first = done + (left_over // (2 * run)) * (2 * run)
        def steps(first=first, run=run):
            for v in range(run):
                full_step(first + v + 1, (v + 1) % 2)
        pl.when((left_over // run) % 2 == 1)(steps)
        run //= 2

    def finish():
        yt = jnp.concatenate([acc_ref[h] / l_ref[h] for h in range(B_HEADS)], axis=0)
        yt = yt * lax.rsqrt(jnp.mean(yt * yt, axis=0, keepdims=True) + EPS)
        y_ref[0] = (yt.T * gob_ref[...]).astype(y_ref.dtype)

    for parity in range(2):
        @pl.when(lax.rem(qb, 2) == parity)
        def _(parity=parity):
            for h in range(B_HEADS):
                consume(h, qb, parity)
            finish()

    smallest = jnp.min(jnp.concatenate([l_ref[h] for h in range(B_HEADS)], axis=0))

    @pl.when(jnp.logical_not(smallest > MIN_SHIFTED_SUM))
    def _():
        def online(h, j, own, first):
            visible = causal if own else sel_ref[h, pl.ds(j, 1), :] > NEG * 0.5
            sc = jnp.where(visible, scores(h, j) * LOG2E, NEG)
            m_tile = jnp.max(sc, axis=0, keepdims=True)
            m_new = m_tile if first else jnp.maximum(m_ref[h], m_tile)
            pr = jnp.exp2(sc - m_new)
            l_new, acc_new = jnp.sum(pr, axis=0, keepdims=True), _dot(values(h, j), pr.astype(BF16))
            if not first:
                alpha = jnp.exp2(m_ref[h] - m_new)
                l_new, acc_new = alpha * l_ref[h] + l_new, alpha * acc_ref[h] + acc_new
            m_ref[h], l_ref[h], acc_ref[h] = m_new, l_new, acc_new

        for h in range(B_HEADS):
            online(h, qb, True, True)

        def past(j, carry):
            for h in range(B_HEADS):
                online(h, j, False, False)
            return carry

        lax.fori_loop(0, qb, past, 0)
        finish()


def _moba_call(bound, qt, k, vt, gob):
    b, s, _ = k.shape
    nb = s // B_BLOCK
    nbp = -(-nb // 8) * 8
    return pl.pallas_call(
        _moba_kernel,
        grid=(b, nb),
        in_specs=[
            pl.BlockSpec(memory_space=pltpu.SMEM),
            pl.BlockSpec((1, N_PAIRS, 1, HEAD_PAIR, B_BLOCK), lambda bi, qi: (bi, 0, qi, 0, 0)),
            pl.BlockSpec((1, s, D_B), lambda bi, qi: (bi, 0, 0)),
            pl.BlockSpec((1, N_PAIRS, nb, HEAD_PAIR, B_BLOCK), lambda bi, qi: (bi, 0, 0, 0, 0)),
            pl.BlockSpec((1, D_B), lambda bi, qi: (0, 0)),
        ],
        out_specs=pl.BlockSpec((1, B_BLOCK, D_B), lambda bi, qi: (bi, qi, 0)),
        out_shape=jax.ShapeDtypeStruct((b, s, D_B), BF16),
        scratch_shapes=[
            pltpu.VMEM((N_PAIRS, 3 * nbp, HEAD_PAIR), BF16),
            pltpu.VMEM((B_HEADS, HEAD_PAIR, B_BLOCK), BF16),
            pltpu.VMEM((B_HEADS, nbp, B_BLOCK), F32),
            pltpu.VMEM((B_HEADS, 1, B_BLOCK), F32),
            pltpu.VMEM((B_HEADS, 1, B_BLOCK), F32),
            pltpu.VMEM((B_HEADS, B_DH, B_BLOCK), F32),
            pltpu.VMEM((B_HEADS, B_BLOCK, B_BLOCK), F32),
            pltpu.VMEM((B_HEADS, B_BLOCK, B_BLOCK), F32),
        ],
        compiler_params=pltpu.CompilerParams(
            dimension_semantics=("parallel", "arbitrary"), vmem_limit_bytes=VMEM_LIMIT),
        name="moba",
    )(bound, qt, k, vt, gob)


def _outmix_kernel(x_ref, ya_ref, yb_ref, wout_ref, gffn_ref, wrt_hi_ref, wrt_lo_ref,
                   h_ref, hn_ref, route_ref, route_i_ref, counts_ref, cnt_ref):
    h = x_ref[...] + _dot(ya_ref[...], wout_ref[:D_A]) + _dot(yb_ref[...], wout_ref[D_A:])
    h_ref[...] = h
    hn = _rms(h, gffn_ref[...])
    _to_row_tiles(hn_ref, hn)
    tm = hn.shape[0]

    hn_hi, hn_lo = _split_bf16(hn)
    lg = _dot_nt(wrt_hi_ref[...], hn_hi) + (_dot_nt(wrt_hi_ref[...], hn_lo) + _dot_nt(wrt_lo_ref[...], hn_hi))
    row = lax.broadcasted_iota(jnp.int32, lg.shape, 0)

    def first_row(mask):
        return jnp.min(jnp.where(mask, row, ROUTER_ROWS), axis=0, keepdims=True)

    is_g = row < N_GROUPS
    g_max = jnp.max(jnp.where(is_g, lg, -jnp.inf), axis=0, keepdims=True)
    p_g = 1.0 / jnp.sum(jnp.where(is_g, jnp.exp(lg - g_max), 0.0), axis=0, keepdims=True)
    g_sel = first_row(is_g & (lg == g_max))
    e_row = row - N_GROUPS
    in_grp = (e_row >= 0) & (e_row < N_EXPERTS) & (lax.shift_right_arithmetic(e_row, 3) == g_sel)
    l1 = jnp.max(jnp.where(in_grp, lg, -jnp.inf), axis=0, keepdims=True)
    i1 = first_row(in_grp & (lg == l1))
    rest = in_grp & (row != i1)
    l2 = jnp.max(jnp.where(rest, lg, -jnp.inf), axis=0, keepdims=True)
    i2 = first_row(rest & (lg == l2))
    e2 = jnp.exp(l2 - l1)
    w1 = 1.0 / (1.0 + e2)
    w2 = e2 / (1.0 + e2)

    @pl.when(pl.program_id(0) == 0)
    def _():
        cnt_ref[...] = jnp.zeros_like(cnt_ref)

    picked = jnp.where((row == i1) | (row == i2), 1.0, 0.0)
    tok_r = lax.broadcasted_iota(jnp.int32, (tm, tm), 0)
    tok_c = lax.broadcasted_iota(jnp.int32, (tm, tm), 1)
    earlier = jnp.where(tok_r < tok_c, 1.0, 0.0).astype(BF16)
    before = _dot(picked.astype(BF16), earlier) + cnt_ref[:, 0:1]
    r1 = jnp.sum(jnp.where(row == i1, before, 0.0), axis=0, keepdims=True)
    r2 = jnp.sum(jnp.where(row == i2, before, 0.0), axis=0, keepdims=True)
    cnt_ref[...] += jnp.sum(picked, axis=1, keepdims=True)
    counts_ref[...] = cnt_ref[...]

    e1, e2x = i1 - N_GROUPS, i2 - N_GROUPS
    route_i_ref[...] = jnp.concatenate(
        [e1, e2x, r1.astype(jnp.int32), r2.astype(jnp.int32), jnp.zeros((ROUTE_FIELDS - 4, tm), jnp.int32)], axis=0)
    wts = jnp.concatenate([p_g * w1, p_g * w2, jnp.zeros((LANES - 2, tm), F32)], axis=0)
    route_ref[...] = wts.T


def _outmix_call(x2, ya2, yb2, wout, gffn, wrt_hi, wrt_lo):
    t, d = x2.shape
    tm = OUT_TM
    const = lambda *shape: pl.BlockSpec(shape, lambda ti: (0,) * len(shape))
    tok = lambda w: pl.BlockSpec((tm, w), lambda ti: (ti, 0))
    return pl.pallas_call(
        _outmix_kernel,
        grid=(t // tm,),
        in_specs=[tok(d), tok(D_A), tok(D_B), const(d, d), const(1, d),
                  const(ROUTER_ROWS, d), const(ROUTER_ROWS, d)],
        out_specs=[tok(d), pl.BlockSpec((tm * ROW_SUB, LANES), lambda ti: (ti, 0)), tok(LANES),
                   pl.BlockSpec((ROUTE_FIELDS, tm), lambda ti: (0, ti)), const(ROUTER_ROWS, LANES)],
        out_shape=[jax.ShapeDtypeStruct((t, d), F32), jax.ShapeDtypeStruct((t * ROW_SUB, LANES), F32),
                   jax.ShapeDtypeStruct((t, LANES), F32),
                   jax.ShapeDtypeStruct((ROUTE_FIELDS, t), jnp.int32),
                   jax.ShapeDtypeStruct((ROUTER_ROWS, LANES), F32)],
        scratch_shapes=[pltpu.VMEM((ROUTER_ROWS, LANES), F32)],
        compiler_params=pltpu.CompilerParams(
            dimension_semantics=("arbitrary",), vmem_limit_bytes=VMEM_LIMIT),
        name="outmix",
    )(x2, ya2, yb2, wout, gffn, wrt_hi, wrt_lo)


def _to_row_tiles(ref, x):
    n = x.shape[0]
    for s in range(ROW_SUB):
        ref[pl.ds(s, n, stride=ROW_SUB), :] = x[:, s * LANES:(s + 1) * LANES]


def _from_row_tiles(ref, first, n):
    return jnp.concatenate(
        [ref[pl.ds(first * ROW_SUB + s, n, stride=ROW_SUB), :] for s in range(ROW_SUB)], axis=1)


def _tiles_at(ref, row, n=1):
    return ref.at[pl.ds(pl.multiple_of(row * ROW_SUB, ROW_SUB), n * ROW_SUB), :]


def _wait_tiles(hbm, vmem_ref, n_tiles, sem):
    cap = vmem_ref.shape[0] // ROW_SUB
    while n_tiles > 0:
        n = min(n_tiles, cap)
        pltpu.make_async_copy(hbm.at[pl.ds(0, n * ROW_SUB), :], vmem_ref.at[pl.ds(0, n * ROW_SUB), :], sem).wait()
        n_tiles -= n


def _slots_kernel(seg_start_ref, route_i_ref, pos_ref):
    experts = route_i_ref[0:2, :]
    start = jnp.zeros_like(experts)
    for e in range(N_EXPERTS):
        start = jnp.where(experts == e, seg_start_ref[e], start)
    pos_ref[...] = jnp.zeros_like(pos_ref)
    pos_ref[0:2, :] = start + route_i_ref[2:4, :]


def _slots_call(seg_start, route_i):
    return pl.pallas_call(
        _slots_kernel,
        grid_spec=pltpu.PrefetchScalarGridSpec(
            num_scalar_prefetch=1,
            grid=(1,),
            in_specs=[pl.BlockSpec(route_i.shape, lambda i, s: (0, 0))],
            out_specs=pl.BlockSpec(route_i.shape, lambda i, s: (0, 0)),
        ),
        out_shape=jax.ShapeDtypeStruct(route_i.shape, jnp.int32),
        name="slots",
    )(seg_start, route_i)


def _dispatch_kernel(gap_start_ref, gap_len_ref, pos_ref, hn_ref, xs_hbm, zero_ref, sem, fill_sem):
    tm = hn_ref.shape[0] // ROW_SUB

    @pl.when(pl.program_id(0) == 0)
    def _():
        zero_ref[...] = jnp.zeros_like(zero_ref)
        fills = []
        for g in range(N_EXPERTS):
            row, n = gap_start_ref[g], gap_len_ref[g]
            size = EXP_TM // 2
            while size >= 1:
                fills.append(((n & size) != 0, row, size))
                row = row + (n & size)
                size //= 2
        for tile in range(N_EXPERTS):
            fills.append((tile * EXP_TM < gap_len_ref[N_EXPERTS], gap_start_ref[N_EXPERTS] + tile * EXP_TM, EXP_TM))
        copies = [(cond, pltpu.make_async_copy(_tiles_at(zero_ref, 0, size), _tiles_at(xs_hbm, row, size), fill_sem))
                  for cond, row, size in fills]
        for cond, copy in copies:
            pl.when(cond)(copy.start)
        for cond, copy in copies:
            pl.when(cond)(copy.wait)

    for r in range(2 * tm):
        k, tok = divmod(r, tm)
        pltpu.make_async_copy(_tiles_at(hn_ref, tok), _tiles_at(xs_hbm, pos_ref[k, tok]), sem).start(priority=r % 2)
    _wait_tiles(xs_hbm, hn_ref, 2 * tm, sem)


def _dispatch_call(gap_start, gap_len, pos, hn_tiles, n_sorted_rows):
    tm = TOK_TM
    return pl.pallas_call(
        _dispatch_kernel,
        grid_spec=pltpu.PrefetchScalarGridSpec(
            num_scalar_prefetch=2,
            grid=(pos.shape[1] // tm,),
            in_specs=[pl.BlockSpec((ROUTE_FIELDS, tm), lambda i, *_: (0, i), memory_space=pltpu.SMEM),
                      pl.BlockSpec((tm * ROW_SUB, LANES), lambda i, *_: (i, 0))],
            out_specs=pl.BlockSpec(memory_space=pl.ANY),
            scratch_shapes=[pltpu.VMEM((EXP_TM * ROW_SUB, LANES), F32), pltpu.SemaphoreType.DMA(()),
                            pltpu.SemaphoreType.DMA(())],
        ),
        out_shape=jax.ShapeDtypeStruct((n_sorted_rows * ROW_SUB, LANES), F32),
        compiler_params=pltpu.CompilerParams(
            dimension_semantics=("arbitrary",), vmem_limit_bytes=VMEM_LIMIT),
        name="dispatch",
    )(gap_start, gap_len, pos, hn_tiles)


def _experts_kernel(tile_e_ref, n_tiles_ref, xs_ref, wg_ref, wu_ref, wd_ref, y_ref, wg16, wu16, wd16):
    i = pl.program_id(0)
    tm = xs_ref.shape[0] // ROW_SUB

    @pl.when((i == 0) | (tile_e_ref[i] != tile_e_ref[jnp.maximum(i - 1, 0)]))
    def _():
        wg16[...] = wg_ref[0].astype(BF16)
        wu16[...] = wu_ref[0].astype(BF16)
        wd16[...] = wd_ref[0].astype(BF16)

    @pl.when(i < n_tiles_ref[0])
    def _():
        x = _from_row_tiles(xs_ref, 0, tm).astype(BF16)
        a = jax.nn.silu(_dot(x, wg16[...])) * _dot(x, wu16[...])
        _to_row_tiles(y_ref, _dot(a.astype(BF16), wd16[...]))

    @pl.when(i >= n_tiles_ref[0])
    def _():
        y_ref[...] = jnp.zeros_like(y_ref)


def _experts_call(tile_e, n_tiles, xs, wg, wu, wd):
    d = wg.shape[1]
    tm = EXP_TM
    nt = xs.shape[0] // (tm * ROW_SUB)
    used = lambda i, n: jnp.minimum(i, n[0] - 1)
    w_spec = lambda *shape: pl.BlockSpec((1,) + shape, lambda i, te, n: (te[used(i, n)], 0, 0))
    return pl.pallas_call(
        _experts_kernel,
        grid_spec=pltpu.PrefetchScalarGridSpec(
            num_scalar_prefetch=2,
            grid=(nt,),
            in_specs=[pl.BlockSpec((tm * ROW_SUB, LANES), lambda i, te, n: (used(i, n), 0)),
                      w_spec(d, D_FF_EXP), w_spec(d, D_FF_EXP), w_spec(D_FF_EXP, d)],
            out_specs=pl.BlockSpec((tm * ROW_SUB, LANES), lambda i, te, n: (i, 0)),
            scratch_shapes=[pltpu.VMEM((d, D_FF_EXP), BF16), pltpu.VMEM((d, D_FF_EXP), BF16),
                            pltpu.VMEM((D_FF_EXP, d), BF16)],
        ),
        out_shape=jax.ShapeDtypeStruct(xs.shape, F32),
        compiler_params=pltpu.CompilerParams(
            dimension_semantics=("arbitrary",), vmem_limit_bytes=VMEM_LIMIT),
        name="experts",
    )(tile_e, n_tiles, xs, wg, wu, wd)


def _ple_kernel(cur_ref, nxt_ref, ys_hbm, h_ref, route_ref, p_ref, wple_ref, gple_ref, wpg_ref,
                o_ref, ybuf0, ybuf1, sem):
    tm = h_ref.shape[0]
    i = pl.program_id(0)
    n_steps = pl.num_programs(0)
    bufs = (ybuf0, ybuf1)

    def start_gather(pos_ref, slot):
        for r in range(2 * tm):
            pltpu.make_async_copy(_tiles_at(ys_hbm, pos_ref[r // tm, r % tm]), _tiles_at(bufs[slot], r),
                                  sem.at[slot]).start(priority=r % 2)

    @pl.when(i == 0)
    def _():
        start_gather(cur_ref, 0)

    def step(slot):
        _wait_tiles(ys_hbm, bufs[slot], 2 * tm, sem.at[slot])
        start_gather(nxt_ref, 1 - slot)
        wts = route_ref[...]
        h = (h_ref[...] + wts[:, 0:1] * _from_row_tiles(bufs[slot], 0, tm)
             + wts[:, 1:2] * _from_row_tiles(bufs[slot], tm, tm))
        gate = jax.nn.sigmoid(_dot(_rms(h, gple_ref[...]).astype(BF16), wpg_ref[...]))
        o_ref[...] = h + _dot(p_ref[...].astype(BF16), wple_ref[...]) * gate

    for slot in range(2):
        pl.when(lax.rem(i, 2) == slot)(functools.partial(step, slot))

    for slot in range(2):
        pl.when((i == n_steps - 1) & (lax.rem(n_steps, 2) == slot))(
            functools.partial(_wait_tiles, ys_hbm, bufs[slot], 2 * tm, sem.at[slot]))


def _ple_call(pos, ys, h, route, p2, wple, gple, wpg):
    t, d = h.shape
    tm = TOK_TM
    n_steps = t // tm
    n_rows = 2 * tm
    idx_spec = lambda f: pl.BlockSpec((ROUTE_FIELDS, tm), lambda i: (0, f(i)), memory_space=pltpu.SMEM)
    const = lambda *shape: pl.BlockSpec(shape, lambda i: (0,) * len(shape))
    tok = lambda w: pl.BlockSpec((tm, w), lambda i: (i, 0))
    return pl.pallas_call(
        _ple_kernel,
        grid=(n_steps,),
        in_specs=[idx_spec(lambda i: i), idx_spec(lambda i: lax.rem(i + 1, n_steps)),
                  pl.BlockSpec(memory_space=pl.ANY),
                  tok(d), tok(LANES), tok(D_PLE), const(D_PLE, d), const(1, d), const(d, d)],
        out_specs=tok(d),
        out_shape=jax.ShapeDtypeStruct((t, d), F32),
        scratch_shapes=[pltpu.VMEM((n_rows * ROW_SUB, LANES), F32), pltpu.VMEM((n_rows * ROW_SUB, LANES), F32),
                        pltpu.SemaphoreType.DMA((2,))],
        compiler_params=pltpu.CompilerParams(
            dimension_semantics=("arbitrary",), vmem_limit_bytes=VMEM_LIMIT),
        name="ple",
    )(pos, pos, ys, h, route, p2, wple, gple, wpg)


def _segment_tables(counts, t):
    tm = EXP_TM
    nt = 2 * t // tm + N_EXPERTS
    padded = ((counts + tm - 1) // tm) * tm
    seg_end = jnp.cumsum(padded)
    seg_start = seg_end - padded
    gap_len = jnp.concatenate([padded - counts, nt * tm - seg_end[-1:]])
    gap_start = jnp.concatenate([seg_start + counts, seg_end[-1:]])
    tile_start = jnp.arange(nt, dtype=jnp.int32) * tm
    tile_e = jnp.minimum(jnp.sum(tile_start[:, None] >= seg_end[None, :], axis=1), N_EXPERTS - 1)
    n_tiles = (seg_end[-1] // tm).reshape(1)
    i32 = lambda a: a.astype(jnp.int32)
    return i32(seg_start), i32(gap_start), i32(gap_len), i32(tile_e), i32(n_tiles), nt * tm


def _layer(h, p_i, g_mix, w_in, g_v_a, w_s, b_s, g_q, g_k, g_out_a, g_out_b, w_out, g_ffn, w_group,
           w_expert, w_gate_e, w_up_e, w_down_e, g_ple, w_ple, w_ple_gate):
    b, s, d = h.shape
    t = b * s
    row = lambda g: g.reshape(1, -1).astype(F32)
    bs = jnp.broadcast_to(b_s[:, :, None], (A_GROUPS, A_CHUNK, A_DH)).astype(F32)
    gk = row(jnp.tile(g_k, B_HEADS))
    gq = (jnp.tile(g_q, B_HEADS) * (1.0 / float(B_DH) ** 0.5)).reshape(D_B, 1).astype(F32)
    head = jnp.arange(D_B) // B_DH
    bd = jnp.where(head[:, None] == head[None, :], 1.0 / B_DH, 0.0).astype(BF16)
    w_rt = jnp.zeros((ROUTER_ROWS, d), F32)
    w_rt = w_rt.at[:N_GROUPS].set(w_group.T).at[N_GROUPS:N_GROUPS + N_EXPERTS].set(w_expert.T)
    wrt_hi, wrt_lo = _split_bf16(w_rt)

    ya, k, qt, vt = _proj_call(h, row(g_mix), w_in.astype(F32), row(g_v_a), w_s.astype(F32), bs,
                               row(g_out_a), gk, gq, bd)
    bound = (BOUND_MARGIN * LOG2E * float(B_DH) ** 0.5) * jnp.max(jnp.abs(g_q)) * jnp.max(jnp.abs(g_k))
    yb = _moba_call(bound.reshape(1, 1).astype(F32), qt, k, vt, row(g_out_b))
    h1, hn_tiles, route, route_i, counts = _outmix_call(
        h.reshape(t, d), ya.reshape(t, D_A), yb.reshape(t, D_B), w_out.astype(BF16), row(g_ffn), wrt_hi, wrt_lo)
    counts = counts[N_GROUPS:N_GROUPS + N_EXPERTS, 0].astype(jnp.int32)
    seg_start, gap_start, gap_len, tile_e, n_tiles, n_sorted_rows = _segment_tables(counts, t)
    pos = _slots_call(seg_start, route_i)
    xs = _dispatch_call(gap_start, gap_len, pos, hn_tiles, n_sorted_rows)
    ys = _experts_call(tile_e, n_tiles, xs, w_gate_e, w_up_e, w_down_e)
    out = _ple_call(pos, ys, h1, route, p_i.reshape(t, D_PLE), w_ple.astype(BF16), row(g_ple),
                    w_ple_gate.astype(BF16))
    return out.reshape(b, s, d)


def kernel(x, p, g_mix, w_in, g_v_a, w_s, b_s, g_q, g_k, g_out_a, g_out_b, w_out, g_ffn, w_group, w_expert,
           w_gate_e, w_up_e, w_down_e, g_ple, w_ple, w_ple_gate):
    params = (g_mix, w_in, g_v_a, w_s, b_s, g_q, g_k, g_out_a, g_out_b, w_out, g_ffn, w_group, w_expert,
              w_gate_e, w_up_e, w_down_e, g_ple, w_ple, w_ple_gate)
    h = x
    for i in range(p.shape[0]):
        h = _layer(h, p[i], *(w[i] for w in params))
    return h
```

```python
import functools

import jax
import jax.numpy as jnp
from jax import lax
from jax.experimental import pallas as pl
from jax.experimental.pallas import tpu as pltpu

F32 = jnp.float32
BF16 = jnp.bfloat16

D_MODEL = 1024
D_A = 512
A_GROUPS = 4
A_DH = D_A // A_GROUPS
A_CHUNK = 128
D_B = 512
B_HEADS = 8
B_DH = D_B // B_HEADS
B_BLOCK = 256
B_TOPK = 3
D_PLE = 256
N_GROUPS = 4
EXP_PER_GROUP = 8
N_EXPERTS = N_GROUPS * EXP_PER_GROUP
D_FF_EXP = 256
EPS = 1e-6
NEG = -1e30
LOG2E = 1.4426950408889634

LANES = 128
HEAD_PAIR = 2 * B_DH
N_PAIRS = B_HEADS // 2
ROUTER_ROWS = 48
ROUTE_FIELDS = 8
VMEM_LIMIT = 56 * 1024 * 1024

PROJ_TM = 1024
OUT_TM = 1024
EXP_TM = 512
MOBA_STEPS_PER_TRIP = 4
BOUND_MARGIN = 1.02
MIN_SHIFTED_SUM = 2.0 ** -64
TOK_TM = 1024
ROW_SUB = D_MODEL // LANES


def _rms(x, g):
    return x * lax.rsqrt(jnp.mean(x * x, axis=-1, keepdims=True) + EPS) * g


def _dot(a, b):
    return jnp.dot(a, b, preferred_element_type=F32)


def _dot_nt(a, b):
    return lax.dot_general(a, b, (((1,), (1,)), ((), ())), preferred_element_type=F32)


def _split_bf16(x):
    hi = x.astype(BF16)
    lo = (x - hi.astype(F32)).astype(BF16)
    return hi, lo


def _proj_kernel(x_ref, gmix_ref, win_ref, gva_ref, ws_ref, bs_ref, goa_ref, gk_ref, gq_ref, bd_ref,
                 ya_ref, k_ref, qt_ref, vt_ref, prod_ref, wuv_ref, wk_ref, wqt_ref, wvt_ref):
    tm = x_ref.shape[1]
    n_chunks = tm // A_CHUNK

    @pl.when((pl.program_id(0) == 0) & (pl.program_id(1) == 0))
    def _():
        q0, k0, v0 = 2 * D_A, 2 * D_A + D_B, 2 * D_A + 2 * D_B
        wuv_ref[...] = win_ref[:, :q0].astype(BF16)
        wk_ref[...] = win_ref[:, k0:v0].astype(BF16)
        wqt_ref[...] = win_ref[:, q0:k0].T.astype(BF16)
        wvt_ref[...] = win_ref[:, v0:].T.astype(BF16)

    xn = _rms(x_ref[0], gmix_ref[...]).astype(BF16)

    uv = _dot(xn, wuv_ref[...])
    u = jax.nn.gelu(uv[:, :D_A])
    v = jax.nn.gelu(uv[:, D_A:])
    row = lax.broadcasted_iota(jnp.int32, (A_CHUNK, A_CHUNK), 0)
    col = lax.broadcasted_iota(jnp.int32, (A_CHUNK, A_CHUNK), 1)
    causal = col <= row
    for g in range(A_GROUPS):
        gs = slice(g * A_DH, (g + 1) * A_DH)
        vn = _rms(v[:, gs], gva_ref[:, gs]).astype(BF16)
        rhs = jnp.concatenate([vn[c * A_CHUNK:(c + 1) * A_CHUNK] for c in range(n_chunks)], axis=1)
        ws = jnp.where(causal, ws_ref[g], 0.0).astype(BF16)
        mixed = _dot(ws, rhs)
        for c in range(n_chunks):
            cs = slice(c * A_CHUNK, (c + 1) * A_CHUNK)
            prod_ref[cs, gs] = u[cs, gs] * (mixed[:, cs] + bs_ref[g])
    ya_ref[0] = _rms(prod_ref[...], goa_ref[...]).astype(ya_ref.dtype)

    k = _dot(xn, wk_ref[...])
    k2_hi, k2_lo = _split_bf16(k * k)
    k_ms = _dot(k2_hi, bd_ref[...]) + _dot(k2_lo, bd_ref[...])
    k_ref[0] = (k * lax.rsqrt(k_ms + EPS) * gk_ref[...]).astype(k_ref.dtype)

    qt = _dot_nt(wqt_ref[...], xn)
    vt = _dot_nt(wvt_ref[...], xn).astype(vt_ref.dtype)
    qn = []
    for h in range(B_HEADS):
        qh = qt[h * B_DH:(h + 1) * B_DH]
        qn.append(qh * lax.rsqrt(jnp.mean(qh * qh, axis=0, keepdims=True) + EPS))
    qn = (jnp.concatenate(qn, axis=0) * gq_ref[...]).astype(qt_ref.dtype)
    for p in range(N_PAIRS):
        ps = slice(p * HEAD_PAIR, (p + 1) * HEAD_PAIR)
        for j in range(tm // B_BLOCK):
            js = slice(j * B_BLOCK, (j + 1) * B_BLOCK)
            qt_ref[0, p, j] = qn[ps, js]
            vt_ref[0, p, j] = vt[ps, js]


def _proj_call(x, gmix, w_in, gva, ws, bs, goa, gk, gq, bd):
    b, s, d = x.shape
    tm = PROJ_TM
    nb = s // B_BLOCK
    jb = tm // B_BLOCK
    const = lambda *shape: pl.BlockSpec(shape, lambda bi, ti: (0,) * len(shape))
    return pl.pallas_call(
        _proj_kernel,
        grid=(b, s // tm),
        in_specs=[
            pl.BlockSpec((1, tm, d), lambda bi, ti: (bi, ti, 0)),
            const(1, d), const(d, 2 * D_A + 3 * D_B),
            const(1, D_A), const(A_GROUPS, A_CHUNK, A_CHUNK), const(A_GROUPS, A_CHUNK, A_DH),
            const(1, D_A), const(1, D_B), const(D_B, 1), const(D_B, D_B),
        ],
        out_specs=[
            pl.BlockSpec((1, tm, D_A), lambda bi, ti: (bi, ti, 0)),
            pl.BlockSpec((1, tm, D_B), lambda bi, ti: (bi, ti, 0)),
            pl.BlockSpec((1, N_PAIRS, jb, HEAD_PAIR, B_BLOCK), lambda bi, ti: (bi, 0, ti, 0, 0)),
            pl.BlockSpec((1, N_PAIRS, jb, HEAD_PAIR, B_BLOCK), lambda bi, ti: (bi, 0, ti, 0, 0)),
        ],
        out_shape=[
            jax.ShapeDtypeStruct((b, s, D_A), BF16),
            jax.ShapeDtypeStruct((b, s, D_B), BF16),
            jax.ShapeDtypeStruct((b, N_PAIRS, nb, HEAD_PAIR, B_BLOCK), BF16),
            jax.ShapeDtypeStruct((b, N_PAIRS, nb, HEAD_PAIR, B_BLOCK), BF16),
        ],
        scratch_shapes=[pltpu.VMEM((tm, D_A), F32), pltpu.VMEM((d, 2 * D_A), BF16), pltpu.VMEM((d, D_B), BF16),
                        pltpu.VMEM((D_B, d), BF16), pltpu.VMEM((D_B, d), BF16)],
        compiler_params=pltpu.CompilerParams(
            dimension_semantics=("arbitrary", "arbitrary"), vmem_limit_bytes=VMEM_LIMIT),
        name="proj",
    )(x, gmix, w_in, gva, ws, bs, goa, gk, gq, bd)


def _moba_kernel(bound_ref, qt_ref, k_ref, vt_ref, gob_ref, y_ref, kmean_ref, qz_ref, sel_ref, m_ref, l_ref,
                 acc_ref, stage0_ref, stage1_ref):
    qb = pl.program_id(1)
    nb = k_ref.shape[1] // B_BLOCK
    nbp = sel_ref.shape[1]

    @pl.when(qb == 0)
    def _():
        rows = [jnp.mean(k_ref[0, j * B_BLOCK:(j + 1) * B_BLOCK, :].astype(F32), axis=0, keepdims=True)
                for j in range(nb)]
        rows += [jnp.zeros((1, D_B), F32)] * (nbp - nb)
        km = jnp.concatenate(rows, axis=0)
        hi = km.astype(BF16)
        rest = km - hi.astype(F32)
        mid = rest.astype(BF16)
        lo = (rest - mid.astype(F32)).astype(BF16)
        for p in range(N_PAIRS):
            ps = slice(p * HEAD_PAIR, (p + 1) * HEAD_PAIR)
            kmean_ref[p] = jnp.concatenate([hi[:, ps], mid[:, ps], lo[:, ps]], axis=0)

    head_row = lax.broadcasted_iota(jnp.int32, (HEAD_PAIR, B_BLOCK), 0) // B_DH
    blk = lax.broadcasted_iota(jnp.int32, (nbp, B_BLOCK), 0)
    past = blk < qb

    for h in range(B_HEADS):
        p, hh = divmod(h, 2)
        q_pair = qt_ref[0, p, 0]
        qz = jnp.where(head_row == hh, q_pair, jnp.zeros_like(q_pair))
        qz_ref[h] = qz
        g3 = _dot(kmean_ref[p], qz)
        gate = (g3[:nbp] + g3[nbp:2 * nbp] + g3[2 * nbp:]) * float(B_DH) ** 0.5
        gate = jnp.where(past, gate, NEG)
        left = gate
        taken = blk < 0
        for _ in range(min(B_TOPK, nb)):
            top = jnp.max(left, axis=0, keepdims=True)
            first = jnp.min(jnp.where(left == top, blk, nbp), axis=0, keepdims=True)
            taken = taken | (blk == first)
            left = jnp.where(blk == first, -jnp.inf, left)
        chosen = taken & (gate > NEG * 0.5)
        sel_ref[h] = jnp.where(chosen, -bound_ref[0, 0], NEG)

    def scores(h, j):
        p = h // 2
        kj = k_ref[0, pl.ds(pl.multiple_of(j * B_BLOCK, B_BLOCK), B_BLOCK), p * HEAD_PAIR:(p + 1) * HEAD_PAIR]
        return _dot(kj, qz_ref[h])

    def values(h, j):
        p, hh = divmod(h, 2)
        return vt_ref[0, p, j, hh * B_DH:(hh + 1) * B_DH, :]

    key_pos = lax.broadcasted_iota(jnp.int32, (B_BLOCK, B_BLOCK), 0)
    q_pos = lax.broadcasted_iota(jnp.int32, (B_BLOCK, B_BLOCK), 1)
    causal = key_pos <= q_pos

    bound = bound_ref[0, 0]
    stage = (stage0_ref, stage1_ref)
    own_shift = jnp.where(causal, -bound, NEG)

    def stage_scores(h, block, shift, parity):
        stage[parity][h] = scores(h, block) * LOG2E + shift

    def consume(h, s, parity):
        pr = jnp.exp2(stage[parity][h])
        l_ref[h] += jnp.sum(pr, axis=0, keepdims=True)
        acc_ref[h] += _dot(values(h, jnp.where(s == 0, qb, s - 1)), pr.astype(BF16))

    def full_step(s, parity):
        for h in range(B_HEADS):
            stage_scores(h, s - 1, sel_ref[h, pl.ds(s - 1, 1), :], parity)
            consume(h, s - 1, 1 - parity)

    for h in range(B_HEADS):
        stage_scores(h, qb, own_shift, 0)
        l_ref[h] = jnp.zeros((1, B_BLOCK), F32)
        acc_ref[h] = jnp.zeros((B_DH, B_BLOCK), F32)

    def trip(u, carry):
        for v in range(MOBA_STEPS_PER_TRIP):
            full_step(MOBA_STEPS_PER_TRIP * u + v + 1, (v + 1) % 2)
        return carry

    lax.fori_loop(0, qb // MOBA_STEPS_PER_TRIP, trip, 0)
    done = (qb // MOBA_STEPS_PER_TRIP) * MOBA_STEPS_PER_TRIP
    left_over = qb - done
    run = MOBA_STEPS_PER_TRIP // 2
    while run > 1:
        first = done + (left_over // (2 * run)) * (2 * run)
        def steps(first=first, run=run):
            for v in range(run):
                full_step(first + v + 1, (v + 1) % 2)
        pl.when((left_over // run) % 2 == 1)(steps)
        run //= 2

    def finish():
        yt = jnp.concatenate([acc_ref[h] / l_ref[h] for h in range(B_HEADS)], axis=0)
        yt = yt * lax.rsqrt(jnp.mean(yt * yt, axis=0, keepdims=True) + EPS)
        y_ref[0] = (yt.T * gob_ref[...]).astype(y_ref.dtype)

    for parity in range(2):
        @pl.when(lax.rem(qb, 2) == parity)
        def _(parity=parity):
            if parity == 1:
                full_step(qb, 1)
            for h in range(B_HEADS):
                consume(h, qb, parity)
            finish()

    smallest = jnp.min(jnp.concatenate([l_ref[h] for h in range(B_HEADS)], axis=0))

    @pl.when(jnp.logical_not(smallest > MIN_SHIFTED_SUM))
    def _():
        def online(h, j, own, first):
            visible = causal if own else sel_ref[h, pl.ds(j, 1), :] > NEG * 0.5
            sc = jnp.where(visible, scores(h, j) * LOG2E, NEG)
            m_tile = jnp.max(sc, axis=0, keepdims=True)
            m_new = m_tile if first else jnp.maximum(m_ref[h], m_tile)
            pr = jnp.exp2(sc - m_new)
            l_new, acc_new = jnp.sum(pr, axis=0, keepdims=True), _dot(values(h, j), pr.astype(BF16))
            if not first:
                alpha = jnp.exp2(m_ref[h] - m_new)
                l_new, acc_new = alpha * l_ref[h] + l_new, alpha * acc_ref[h] + acc_new
            m_ref[h], l_ref[h], acc_ref[h] = m_new, l_new, acc_new

        for h in range(B_HEADS):
            online(h, qb, True, True)

        def past(j, carry):
            for h in range(B_HEADS):
                online(h, j, False, False)
            return carry

        lax.fori_loop(0, qb, past, 0)
        finish()


def _moba_call(bound, qt, k, vt, gob):
    b, s, _ = k.shape
    nb = s // B_BLOCK
    nbp = -(-nb // 8) * 8
    return pl.pallas_call(
        _moba_kernel,
        grid=(b, nb),
        in_specs=[
            pl.BlockSpec(memory_space=pltpu.SMEM),
            pl.BlockSpec((1, N_PAIRS, 1, HEAD_PAIR, B_BLOCK), lambda bi, qi: (bi, 0, qi, 0, 0)),
            pl.BlockSpec((1, s, D_B), lambda bi, qi: (bi, 0, 0)),
            pl.BlockSpec((1, N_PAIRS, nb, HEAD_PAIR, B_BLOCK), lambda bi, qi: (bi, 0, 0, 0, 0)),
            pl.BlockSpec((1, D_B), lambda bi, qi: (0, 0)),
        ],
        out_specs=pl.BlockSpec((1, B_BLOCK, D_B), lambda bi, qi: (bi, qi, 0)),
        out_shape=jax.ShapeDtypeStruct((b, s, D_B), BF16),
        scratch_shapes=[
            pltpu.VMEM((N_PAIRS, 3 * nbp, HEAD_PAIR), BF16),
            pltpu.VMEM((B_HEADS, HEAD_PAIR, B_BLOCK), BF16),
            pltpu.VMEM((B_HEADS, nbp, B_BLOCK), F32),
            pltpu.VMEM((B_HEADS, 1, B_BLOCK), F32),
            pltpu.VMEM((B_HEADS, 1, B_BLOCK), F32),
            pltpu.VMEM((B_HEADS, B_DH, B_BLOCK), F32),
            pltpu.VMEM((B_HEADS, B_BLOCK, B_BLOCK), F32),
            pltpu.VMEM((B_HEADS, B_BLOCK, B_BLOCK), F32),
        ],
        compiler_params=pltpu.CompilerParams(
            dimension_semantics=("parallel", "arbitrary"), vmem_limit_bytes=VMEM_LIMIT),
        name="moba",
    )(bound, qt, k, vt, gob)


def _outmix_kernel(x_ref, ya_ref, yb_ref, wout_ref, gffn_ref, wrt_hi_ref, wrt_lo_ref,
                   h_ref, hn_ref, route_ref, route_i_ref, counts_ref, cnt_ref):
    h = x_ref[...] + _dot(ya_ref[...], wout_ref[:D_A]) + _dot(yb_ref[...], wout_ref[D_A:])
    h_ref[...] = h
    hn = _rms(h, gffn_ref[...])
    _to_row_tiles(hn_ref, hn)
    tm = hn.shape[0]

    hn_hi, hn_lo = _split_bf16(hn)
    lg = _dot_nt(wrt_hi_ref[...], hn_hi) + (_dot_nt(wrt_hi_ref[...], hn_lo) + _dot_nt(wrt_lo_ref[...], hn_hi))
    row = lax.broadcasted_iota(jnp.int32, lg.shape, 0)

    def first_row(mask):
        return jnp.min(jnp.where(mask, row, ROUTER_ROWS), axis=0, keepdims=True)

    is_g = row < N_GROUPS
    g_max = jnp.max(jnp.where(is_g, lg, -jnp.inf), axis=0, keepdims=True)
    p_g = 1.0 / jnp.sum(jnp.where(is_g, jnp.exp(lg - g_max), 0.0), axis=0, keepdims=True)
    g_sel = first_row(is_g & (lg == g_max))
    e_row = row - N_GROUPS
    in_grp = (e_row >= 0) & (e_row < N_EXPERTS) & (lax.shift_right_arithmetic(e_row, 3) == g_sel)
    l1 = jnp.max(jnp.where(in_grp, lg, -jnp.inf), axis=0, keepdims=True)
    i1 = first_row(in_grp & (lg == l1))
    rest = in_grp & (row != i1)
    l2 = jnp.max(jnp.where(rest, lg, -jnp.inf), axis=0, keepdims=True)
    i2 = first_row(rest & (lg == l2))
    e2 = jnp.exp(l2 - l1)
    w1 = 1.0 / (1.0 + e2)
    w2 = e2 / (1.0 + e2)

    @pl.when(pl.program_id(0) == 0)
    def _():
        cnt_ref[...] = jnp.zeros_like(cnt_ref)

    picked = jnp.where((row == i1) | (row == i2), 1.0, 0.0)
    tok_r = lax.broadcasted_iota(jnp.int32, (tm, tm), 0)
    tok_c = lax.broadcasted_iota(jnp.int32, (tm, tm), 1)
    earlier = jnp.where(tok_r < tok_c, 1.0, 0.0).astype(BF16)
    before = _dot(picked.astype(BF16), earlier) + cnt_ref[:, 0:1]
    r1 = jnp.sum(jnp.where(row == i1, before, 0.0), axis=0, keepdims=True)
    r2 = jnp.sum(jnp.where(row == i2, before, 0.0), axis=0, keepdims=True)
    cnt_ref[...] += jnp.sum(picked, axis=1, keepdims=True)
    counts_ref[...] = cnt_ref[...]

    e1, e2x = i1 - N_GROUPS, i2 - N_GROUPS
    route_i_ref[...] = jnp.concatenate(
        [e1, e2x, r1.astype(jnp.int32), r2.astype(jnp.int32), jnp.zeros((ROUTE_FIELDS - 4, tm), jnp.int32)], axis=0)
    wts = jnp.concatenate([p_g * w1, p_g * w2, jnp.zeros((LANES - 2, tm), F32)], axis=0)
    route_ref[...] = wts.T


def _outmix_call(x2, ya2, yb2, wout, gffn, wrt_hi, wrt_lo):
    t, d = x2.shape
    tm = OUT_TM
    const = lambda *shape: pl.BlockSpec(shape, lambda ti: (0,) * len(shape))
    tok = lambda w: pl.BlockSpec((tm, w), lambda ti: (ti, 0))
    return pl.pallas_call(
        _outmix_kernel,
        grid=(t // tm,),
        in_specs=[tok(d), tok(D_A), tok(D_B), const(d, d), const(1, d),
                  const(ROUTER_ROWS, d), const(ROUTER_ROWS, d)],
        out_specs=[tok(d), pl.BlockSpec((tm * ROW_SUB, LANES), lambda ti: (ti, 0)), tok(LANES),
                   pl.BlockSpec((ROUTE_FIELDS, tm), lambda ti: (0, ti)), const(ROUTER_ROWS, LANES)],
        out_shape=[jax.ShapeDtypeStruct((t, d), F32), jax.ShapeDtypeStruct((t * ROW_SUB, LANES), F32),
                   jax.ShapeDtypeStruct((t, LANES), F32),
                   jax.ShapeDtypeStruct((ROUTE_FIELDS, t), jnp.int32),
                   jax.ShapeDtypeStruct((ROUTER_ROWS, LANES), F32)],
        scratch_shapes=[pltpu.VMEM((ROUTER_ROWS, LANES), F32)],
        compiler_params=pltpu.CompilerParams(
            dimension_semantics=("arbitrary",), vmem_limit_bytes=VMEM_LIMIT),
        name="outmix",
    )(x2, ya2, yb2, wout, gffn, wrt_hi, wrt_lo)


def _to_row_tiles(ref, x):
    n = x.shape[0]
    for s in range(ROW_SUB):
        ref[pl.ds(s, n, stride=ROW_SUB), :] = x[:, s * LANES:(s + 1) * LANES]


def _from_row_tiles(ref, first, n):
    return jnp.concatenate(
        [ref[pl.ds(first * ROW_SUB + s, n, stride=ROW_SUB), :] for s in range(ROW_SUB)], axis=1)


def _tiles_at(ref, row, n=1):
    return ref.at[pl.ds(pl.multiple_of(row * ROW_SUB, ROW_SUB), n * ROW_SUB), :]


def _wait_tiles(hbm, vmem_ref, n_tiles, sem):
    cap = vmem_ref.shape[0] // ROW_SUB
    while n_tiles > 0:
        n = min(n_tiles, cap)
        pltpu.make_async_copy(hbm.at[pl.ds(0, n * ROW_SUB), :], vmem_ref.at[pl.ds(0, n * ROW_SUB), :], sem).wait()
        n_tiles -= n


def _slots_kernel(seg_start_ref, route_i_ref, pos_ref):
    experts = route_i_ref[0:2, :]
    start = jnp.zeros_like(experts)
    for e in range(N_EXPERTS):
        start = jnp.where(experts == e, seg_start_ref[e], start)
    pos_ref[...] = jnp.zeros_like(pos_ref)
    pos_ref[0:2, :] = start + route_i_ref[2:4, :]


def _slots_call(seg_start, route_i):
    return pl.pallas_call(
        _slots_kernel,
        grid_spec=pltpu.PrefetchScalarGridSpec(
            num_scalar_prefetch=1,
            grid=(1,),
            in_specs=[pl.BlockSpec(route_i.shape, lambda i, s: (0, 0))],
            out_specs=pl.BlockSpec(route_i.shape, lambda i, s: (0, 0)),
        ),
        out_shape=jax.ShapeDtypeStruct(route_i.shape, jnp.int32),
        name="slots",
    )(seg_start, route_i)


def _dispatch_kernel(gap_start_ref, gap_len_ref, pos_ref, hn_ref, xs_hbm, zero_ref, sem, fill_sem):
    tm = hn_ref.shape[0] // ROW_SUB

    @pl.when(pl.program_id(0) == 0)
    def _():
        zero_ref[...] = jnp.zeros_like(zero_ref)
        fills = []
        for g in range(N_EXPERTS):
            row, n = gap_start_ref[g], gap_len_ref[g]
            size = EXP_TM // 2
            while size >= 1:
                fills.append(((n & size) != 0, row, size))
                row = row + (n & size)
                size //= 2
        for tile in range(N_EXPERTS):
            fills.append((tile * EXP_TM < gap_len_ref[N_EXPERTS], gap_start_ref[N_EXPERTS] + tile * EXP_TM, EXP_TM))
        copies = [(cond, pltpu.make_async_copy(_tiles_at(zero_ref, 0, size), _tiles_at(xs_hbm, row, size), fill_sem))
                  for cond, row, size in fills]
        for cond, copy in copies:
            pl.when(cond)(copy.start)
        for cond, copy in copies:
            pl.when(cond)(copy.wait)

    for r in range(2 * tm):
        k, tok = divmod(r, tm)
        pltpu.make_async_copy(_tiles_at(hn_ref, tok), _tiles_at(xs_hbm, pos_ref[k, tok]), sem).start(priority=r % 2)
    _wait_tiles(xs_hbm, hn_ref, 2 * tm, sem)


def _dispatch_call(gap_start, gap_len, pos, hn_tiles, n_sorted_rows):
    tm = TOK_TM
    return pl.pallas_call(
        _dispatch_kernel,
        grid_spec=pltpu.PrefetchScalarGridSpec(
            num_scalar_prefetch=2,
            grid=(pos.shape[1] // tm,),
            in_specs=[pl.BlockSpec((ROUTE_FIELDS, tm), lambda i, *_: (0, i), memory_space=pltpu.SMEM),
                      pl.BlockSpec((tm * ROW_SUB, LANES), lambda i, *_: (i, 0))],
            out_specs=pl.BlockSpec(memory_space=pl.ANY),
            scratch_shapes=[pltpu.VMEM((EXP_TM * ROW_SUB, LANES), F32), pltpu.SemaphoreType.DMA(()),
                            pltpu.SemaphoreType.DMA(())],
        ),
        out_shape=jax.ShapeDtypeStruct((n_sorted_rows * ROW_SUB, LANES), F32),
        compiler_params=pltpu.CompilerParams(
            dimension_semantics=("arbitrary",), vmem_limit_bytes=VMEM_LIMIT),
        name="dispatch",
    )(gap_start, gap_len, pos, hn_tiles)


def _experts_kernel(tile_e_ref, n_tiles_ref, xs_ref, wg_ref, wu_ref, wd_ref, y_ref, wg16, wu16, wd16):
    i = pl.program_id(0)
    tm = xs_ref.shape[0] // ROW_SUB

    @pl.when((i == 0) | (tile_e_ref[i] != tile_e_ref[jnp.maximum(i - 1, 0)]))
    def _():
        wg16[...] = wg_ref[0].astype(BF16)
        wu16[...] = wu_ref[0].astype(BF16)
        wd16[...] = wd_ref[0].astype(BF16)

    @pl.when(i < n_tiles_ref[0])
    def _():
        x = _from_row_tiles(xs_ref, 0, tm).astype(BF16)
        a = jax.nn.silu(_dot(x, wg16[...])) * _dot(x, wu16[...])
        _to_row_tiles(y_ref, _dot(a.astype(BF16), wd16[...]))

    @pl.when(i >= n_tiles_ref[0])
    def _():
        y_ref[...] = jnp.zeros_like(y_ref)


def _experts_call(tile_e, n_tiles, xs, wg, wu, wd):
    d = wg.shape[1]
    tm = EXP_TM
    nt = xs.shape[0] // (tm * ROW_SUB)
    used = lambda i, n: jnp.minimum(i, n[0] - 1)
    w_spec = lambda *shape: pl.BlockSpec((1,) + shape, lambda i, te, n: (te[used(i, n)], 0, 0))
    return pl.pallas_call(
        _experts_kernel,
        grid_spec=pltpu.PrefetchScalarGridSpec(
            num_scalar_prefetch=2,
            grid=(nt,),
            in_specs=[pl.BlockSpec((tm * ROW_SUB, LANES), lambda i, te, n: (used(i, n), 0)),
                      w_spec(d, D_FF_EXP), w_spec(d, D_FF_EXP), w_spec(D_FF_EXP, d)],
            out_specs=pl.BlockSpec((tm * ROW_SUB, LANES), lambda i, te, n: (i, 0)),
            scratch_shapes=[pltpu.VMEM((d, D_FF_EXP), BF16), pltpu.VMEM((d, D_FF_EXP), BF16),
                            pltpu.VMEM((D_FF_EXP, d), BF16)],
        ),
        out_shape=jax.ShapeDtypeStruct(xs.shape, F32),
        compiler_params=pltpu.CompilerParams(
            dimension_semantics=("arbitrary",), vmem_limit_bytes=VMEM_LIMIT),
        name="experts",
    )(tile_e, n_tiles, xs, wg, wu, wd)


def _ple_kernel(cur_ref, nxt_ref, ys_hbm, h_ref, route_ref, p_ref, wple_ref, gple_ref, wpg_ref,
                o_ref, ybuf0, ybuf1, sem):
    tm = h_ref.shape[0]
    i = pl.program_id(0)
    n_steps = pl.num_programs(0)
    bufs = (ybuf0, ybuf1)

    def start_gather(pos_ref, slot):
        for r in range(2 * tm):
            pltpu.make_async_copy(_tiles_at(ys_hbm, pos_ref[r // tm, r % tm]), _tiles_at(bufs[slot], r),
                                  sem.at[slot]).start(priority=r % 2)

    @pl.when(i == 0)
    def _():
        start_gather(cur_ref, 0)

    def step(slot):
        _wait_tiles(ys_hbm, bufs[slot], 2 * tm, sem.at[slot])
        start_gather(nxt_ref, 1 - slot)
        wts = route_ref[...]
        h = (h_ref[...] + wts[:, 0:1] * _from_row_tiles(bufs[slot], 0, tm)
             + wts[:, 1:2] * _from_row_tiles(bufs[slot], tm, tm))
        gate = jax.nn.sigmoid(_dot(_rms(h, gple_ref[...]).astype(BF16), wpg_ref[...]))
        o_ref[...] = h + _dot(p_ref[...].astype(BF16), wple_ref[...]) * gate

    for slot in range(2):
        pl.when(lax.rem(i, 2) == slot)(functools.partial(step, slot))

    for slot in range(2):
        pl.when((i == n_steps - 1) & (lax.rem(n_steps, 2) == slot))(
            functools.partial(_wait_tiles, ys_hbm, bufs[slot], 2 * tm, sem.at[slot]))


def _ple_call(pos, ys, h, route, p2, wple, gple, wpg):
    t, d = h.shape
    tm = TOK_TM
    n_steps = t // tm
    n_rows = 2 * tm
    idx_spec = lambda f: pl.BlockSpec((ROUTE_FIELDS, tm), lambda i: (0, f(i)), memory_space=pltpu.SMEM)
    const = lambda *shape: pl.BlockSpec(shape, lambda i: (0,) * len(shape))
    tok = lambda w: pl.BlockSpec((tm, w), lambda i: (i, 0))
    return pl.pallas_call(
        _ple_kernel,
        grid=(n_steps,),
        in_specs=[idx_spec(lambda i: i), idx_spec(lambda i: lax.rem(i + 1, n_steps)),
                  pl.BlockSpec(memory_space=pl.ANY),
                  tok(d), tok(LANES), tok(D_PLE), const(D_PLE, d), const(1, d), const(d, d)],
        out_specs=tok(d),
        out_shape=jax.ShapeDtypeStruct((t, d), F32),
        scratch_shapes=[pltpu.VMEM((n_rows * ROW_SUB, LANES), F32), pltpu.VMEM((n_rows * ROW_SUB, LANES), F32),
                        pltpu.SemaphoreType.DMA((2,))],
        compiler_params=pltpu.CompilerParams(
            dimension_semantics=("arbitrary",), vmem_limit_bytes=VMEM_LIMIT),
        name="ple",
    )(pos, pos, ys, h, route, p2, wple, gple, wpg)


def _segment_tables(counts, t):
    tm = EXP_TM
    nt = 2 * t // tm + N_EXPERTS
    padded = ((counts + tm - 1) // tm) * tm
    seg_end = jnp.cumsum(padded)
    seg_start = seg_end - padded
    gap_len = jnp.concatenate([padded - counts, nt * tm - seg_end[-1:]])
    gap_start = jnp.concatenate([seg_start + counts, seg_end[-1:]])
    tile_start = jnp.arange(nt, dtype=jnp.int32) * tm
    tile_e = jnp.minimum(jnp.sum(tile_start[:, None] >= seg_end[None, :], axis=1), N_EXPERTS - 1)
    n_tiles = (seg_end[-1] // tm).reshape(1)
    i32 = lambda a: a.astype(jnp.int32)
    return i32(seg_start), i32(gap_start), i32(gap_len), i32(tile_e), i32(n_tiles), nt * tm


def _layer(h, p_i, g_mix, w_in, g_v_a, w_s, b_s, g_q, g_k, g_out_a, g_out_b, w_out, g_ffn, w_group,
           w_expert, w_gate_e, w_up_e, w_down_e, g_ple, w_ple, w_ple_gate):
    b, s, d = h.shape
    t = b * s
    row = lambda g: g.reshape(1, -1).astype(F32)
    bs = jnp.broadcast_to(b_s[:, :, None], (A_GROUPS, A_CHUNK, A_DH)).astype(F32)
    gk = row(jnp.tile(g_k, B_HEADS))
    gq = (jnp.tile(g_q, B_HEADS) * (1.0 / float(B_DH) ** 0.5)).reshape(D_B, 1).astype(F32)
    head = jnp.arange(D_B) // B_DH
    bd = jnp.where(head[:, None] == head[None, :], 1.0 / B_DH, 0.0).astype(BF16)
    w_rt = jnp.zeros((ROUTER_ROWS, d), F32)
    w_rt = w_rt.at[:N_GROUPS].set(w_group.T).at[N_GROUPS:N_GROUPS + N_EXPERTS].set(w_expert.T)
    wrt_hi, wrt_lo = _split_bf16(w_rt)

    ya, k, qt, vt = _proj_call(h, row(g_mix), w_in.astype(F32), row(g_v_a), w_s.astype(F32), bs,
                               row(g_out_a), gk, gq, bd)
    bound = (BOUND_MARGIN * LOG2E * float(B_DH) ** 0.5) * jnp.max(jnp.abs(g_q)) * jnp.max(jnp.abs(g_k))
    yb = _moba_call(bound.reshape(1, 1).astype(F32), qt, k, vt, row(g_out_b))
    h1, hn_tiles, route, route_i, counts = _outmix_call(
        h.reshape(t, d), ya.reshape(t, D_A), yb.reshape(t, D_B), w_out.astype(BF16), row(g_ffn), wrt_hi, wrt_lo)
    counts = counts[N_GROUPS:N_GROUPS + N_EXPERTS, 0].astype(jnp.int32)
    seg_start, gap_start, gap_len, tile_e, n_tiles, n_sorted_rows = _segment_tables(counts, t)
    pos = _slots_call(seg_start, route_i)
    xs = _dispatch_call(gap_start, gap_len, pos, hn_tiles, n_sorted_rows)
    ys = _experts_call(tile_e, n_tiles, xs, w_gate_e, w_up_e, w_down_e)
    out = _ple_call(pos, ys, h1, route, p_i.reshape(t, D_PLE), w_ple.astype(BF16), row(g_ple),
                    w_ple_gate.astype(BF16))
    return out.reshape(b, s, d)


def kernel(x, p, g_mix, w_in, g_v_a, w_s, b_s, g_q, g_k, g_out_a, g_out_b, w_out, g_ffn, w_group, w_expert,
           w_gate_e, w_up_e, w_down_e, g_ple, w_ple, w_ple_gate):
    params = (g_mix, w_in, g_v_a, w_s, b_s, g_q, g_k, g_out_a, g_out_b, w_out, g_ffn, w_group, w_expert,
              w_gate_e, w_up_e, w_down_e, g_ple, w_ple, w_ple_gate)
    h = x
    for i in range(p.shape[0]):
        h = _layer(h, p[i], *(w[i] for w in params))
    return h
```

```python
import functools

import jax
import jax.numpy as jnp
from jax import lax
from jax.experimental import pallas as pl
from jax.experimental.pallas import tpu as pltpu

F32 = jnp.float32
BF16 = jnp.bfloat16

D_MODEL = 1024
D_A = 512
A_GROUPS = 4
A_DH = D_A // A_GROUPS
A_CHUNK = 128
D_B = 512
B_HEADS = 8
B_DH = D_B // B_HEADS
B_BLOCK = 256
B_TOPK = 3
D_PLE = 256
N_GROUPS = 4
EXP_PER_GROUP = 8
N_EXPERTS = N_GROUPS * EXP_PER_GROUP
D_FF_EXP = 256
EPS = 1e-6
NEG = -1e30
LOG2E = 1.4426950408889634

LANES = 128
HEAD_PAIR = 2 * B_DH
N_PAIRS = B_HEADS // 2
ROUTER_ROWS = 48
ROUTE_FIELDS = 8
VMEM_LIMIT = 56 * 1024 * 1024

PROJ_TM = 1024
OUT_TM = 1024
EXP_TM = 512
EXP_ROW_BUFFERS = 3
MOBA_STEPS_PER_TRIP = 4
BOUND_MARGIN = 1.02
MIN_SHIFTED_SUM = 2.0 ** -64
TOK_TM = 1024
ROW_SUB = D_MODEL // LANES


def _rms(x, g):
    return x * lax.rsqrt(jnp.mean(x * x, axis=-1, keepdims=True) + EPS) * g


def _dot(a, b):
    return jnp.dot(a, b, preferred_element_type=F32)


def _dot_nt(a, b):
    return lax.dot_general(a, b, (((1,), (1,)), ((), ())), preferred_element_type=F32)


def _split_bf16(x):
    hi = x.astype(BF16)
    lo = (x - hi.astype(F32)).astype(BF16)
    return hi, lo


def _proj_kernel(x_ref, gmix_ref, win_ref, gva_ref, ws_ref, bs_ref, goa_ref, gk_ref, gq_ref, bd_ref,
                 ya_ref, k_ref, qt_ref, vt_ref, prod_ref, wuv_ref, wk_ref, wqt_ref, wvt_ref):
    tm = x_ref.shape[1]
    n_chunks = tm // A_CHUNK

    @pl.when((pl.program_id(0) == 0) & (pl.program_id(1) == 0))
    def _():
        q0, k0, v0 = 2 * D_A, 2 * D_A + D_B, 2 * D_A + 2 * D_B
        wuv_ref[...] = win_ref[:, :q0].astype(BF16)
        wk_ref[...] = win_ref[:, k0:v0].astype(BF16)
        wqt_ref[...] = win_ref[:, q0:k0].T.astype(BF16)
        wvt_ref[...] = win_ref[:, v0:].T.astype(BF16)

    xn = _rms(x_ref[0], gmix_ref[...]).astype(BF16)

    uv = _dot(xn, wuv_ref[...])
    u = jax.nn.gelu(uv[:, :D_A])
    v = jax.nn.gelu(uv[:, D_A:])
    row = lax.broadcasted_iota(jnp.int32, (A_CHUNK, A_CHUNK), 0)
    col = lax.broadcasted_iota(jnp.int32, (A_CHUNK, A_CHUNK), 1)
    causal = col <= row
    for g in range(A_GROUPS):
        gs = slice(g * A_DH, (g + 1) * A_DH)
        vn = _rms(v[:, gs], gva_ref[:, gs]).astype(BF16)
        rhs = jnp.concatenate([vn[c * A_CHUNK:(c + 1) * A_CHUNK] for c in range(n_chunks)], axis=1)
        ws = jnp.where(causal, ws_ref[g], 0.0).astype(BF16)
        mixed = _dot(ws, rhs)
        for c in range(n_chunks):
            cs = slice(c * A_CHUNK, (c + 1) * A_CHUNK)
            prod_ref[cs, gs] = u[cs, gs] * (mixed[:, cs] + bs_ref[g])
    ya_ref[0] = _rms(prod_ref[...], goa_ref[...]).astype(ya_ref.dtype)

    k = _dot(xn, wk_ref[...])
    k2_hi, k2_lo = _split_bf16(k * k)
    k_ms = _dot(k2_hi, bd_ref[...]) + _dot(k2_lo, bd_ref[...])
    k_ref[0] = (k * lax.rsqrt(k_ms + EPS) * gk_ref[...]).astype(k_ref.dtype)

    qt = _dot_nt(wqt_ref[...], xn)
    vt = _dot_nt(wvt_ref[...], xn).astype(vt_ref.dtype)
    qn = []
    for h in range(B_HEADS):
        qh = qt[h * B_DH:(h + 1) * B_DH]
        qn.append(qh * lax.rsqrt(jnp.mean(qh * qh, axis=0, keepdims=True) + EPS))
    qn = (jnp.concatenate(qn, axis=0) * gq_ref[...]).astype(qt_ref.dtype)
    for p in range(N_PAIRS):
        ps = slice(p * HEAD_PAIR, (p + 1) * HEAD_PAIR)
        for j in range(tm // B_BLOCK):
            js = slice(j * B_BLOCK, (j + 1) * B_BLOCK)
            qt_ref[0, p, j] = qn[ps, js]
            vt_ref[0, p, j] = vt[ps, js]


def _proj_call(x, gmix, w_in, gva, ws, bs, goa, gk, gq, bd):
    b, s, d = x.shape
    tm = PROJ_TM
    nb = s // B_BLOCK
    jb = tm // B_BLOCK
    const = lambda *shape: pl.BlockSpec(shape, lambda bi, ti: (0,) * len(shape))
    return pl.pallas_call(
        _proj_kernel,
        grid=(b, s // tm),
        in_specs=[
            pl.BlockSpec((1, tm, d), lambda bi, ti: (bi, ti, 0)),
            const(1, d), const(d, 2 * D_A + 3 * D_B),
            const(1, D_A), const(A_GROUPS, A_CHUNK, A_CHUNK), const(A_GROUPS, A_CHUNK, A_DH),
            const(1, D_A), const(1, D_B), const(D_B, 1), const(D_B, D_B),
        ],
        out_specs=[
            pl.BlockSpec((1, tm, D_A), lambda bi, ti: (bi, ti, 0)),
            pl.BlockSpec((1, tm, D_B), lambda bi, ti: (bi, ti, 0)),
            pl.BlockSpec((1, N_PAIRS, jb, HEAD_PAIR, B_BLOCK), lambda bi, ti: (bi, 0, ti, 0, 0)),
            pl.BlockSpec((1, N_PAIRS, jb, HEAD_PAIR, B_BLOCK), lambda bi, ti: (bi, 0, ti, 0, 0)),
        ],
        out_shape=[
            jax.ShapeDtypeStruct((b, s, D_A), BF16),
            jax.ShapeDtypeStruct((b, s, D_B), BF16),
            jax.ShapeDtypeStruct((b, N_PAIRS, nb, HEAD_PAIR, B_BLOCK), BF16),
            jax.ShapeDtypeStruct((b, N_PAIRS, nb, HEAD_PAIR, B_BLOCK), BF16),
        ],
        scratch_shapes=[pltpu.VMEM((tm, D_A), F32), pltpu.VMEM((d, 2 * D_A), BF16), pltpu.VMEM((d, D_B), BF16),
                        pltpu.VMEM((D_B, d), BF16), pltpu.VMEM((D_B, d), BF16)],
        compiler_params=pltpu.CompilerParams(
            dimension_semantics=("arbitrary", "arbitrary"), vmem_limit_bytes=VMEM_LIMIT),
        name="proj",
    )(x, gmix, w_in, gva, ws, bs, goa, gk, gq, bd)


def _moba_kernel(bound_ref, qt_ref, k_ref, vt_ref, gob_ref, y_ref, kmean_ref, qz_ref, sel_ref, m_ref, l_ref,
                 acc_ref, stage0_ref, stage1_ref):
    qb = pl.program_id(1)
    nb = k_ref.shape[1] // B_BLOCK
    nbp = sel_ref.shape[1]

    @pl.when(qb == 0)
    def _():
        rows = [jnp.mean(k_ref[0, j * B_BLOCK:(j + 1) * B_BLOCK, :].astype(F32), axis=0, keepdims=True)
                for j in range(nb)]
        rows += [jnp.zeros((1, D_B), F32)] * (nbp - nb)
        km = jnp.concatenate(rows, axis=0)
        hi = km.astype(BF16)
        rest = km - hi.astype(F32)
        mid = rest.astype(BF16)
        lo = (rest - mid.astype(F32)).astype(BF16)
        for p in range(N_PAIRS):
            ps = slice(p * HEAD_PAIR, (p + 1) * HEAD_PAIR)
            kmean_ref[p] = jnp.concatenate([hi[:, ps], mid[:, ps], lo[:, ps]], axis=0)

    head_row = lax.broadcasted_iota(jnp.int32, (HEAD_PAIR, B_BLOCK), 0) // B_DH
    blk = lax.broadcasted_iota(jnp.int32, (nbp, B_BLOCK), 0)
    past = blk < qb

    for h in range(B_HEADS):
        p, hh = divmod(h, 2)
        q_pair = qt_ref[0, p, 0]
        qz = jnp.where(head_row == hh, q_pair, jnp.zeros_like(q_pair))
        qz_ref[h] = qz
        g3 = _dot(kmean_ref[p], qz)
        gate = (g3[:nbp] + g3[nbp:2 * nbp] + g3[2 * nbp:]) * float(B_DH) ** 0.5
        gate = jnp.where(past, gate, NEG)
        left = gate
        taken = blk < 0
        for _ in range(min(B_TOPK, nb)):
            top = jnp.max(left, axis=0, keepdims=True)
            first = jnp.min(jnp.where(left == top, blk, nbp), axis=0, keepdims=True)
            taken = taken | (blk == first)
            left = jnp.where(blk == first, -jnp.inf, left)
        chosen = taken & (gate > NEG * 0.5)
        sel_ref[h] = jnp.where(chosen, -bound_ref[0, 0], NEG)

    def scores(h, j):
        p = h // 2
        kj = k_ref[0, pl.ds(pl.multiple_of(j * B_BLOCK, B_BLOCK), B_BLOCK), p * HEAD_PAIR:(p + 1) * HEAD_PAIR]
        return _dot(kj, qz_ref[h])

    def values(h, j):
        p, hh = divmod(h, 2)
        return vt_ref[0, p, j, hh * B_DH:(hh + 1) * B_DH, :]

    key_pos = lax.broadcasted_iota(jnp.int32, (B_BLOCK, B_BLOCK), 0)
    q_pos = lax.broadcasted_iota(jnp.int32, (B_BLOCK, B_BLOCK), 1)
    causal = key_pos <= q_pos

    bound = bound_ref[0, 0]
    stage = (stage0_ref, stage1_ref)
    own_shift = jnp.where(causal, -bound, NEG)

    def stage_scores(h, block, shift, parity):
        stage[parity][h] = scores(h, block) * LOG2E + shift

    def consume(h, s, parity):
        pr = jnp.exp2(stage[parity][h])
        l_ref[h] += jnp.sum(pr, axis=0, keepdims=True)
        acc_ref[h] += _dot(values(h, jnp.where(s == 0, qb, s - 1)), pr.astype(BF16))

    def full_step(s, parity):
        for h in range(B_HEADS):
            stage_scores(h, s - 1, sel_ref[h, pl.ds(s - 1, 1), :], parity)
            consume(h, s - 1, 1 - parity)

    for h in range(B_HEADS):
        stage_scores(h, qb, own_shift, 0)
        l_ref[h] = jnp.zeros((1, B_BLOCK), F32)
        acc_ref[h] = jnp.zeros((B_DH, B_BLOCK), F32)

    def trip(u, carry):
        for v in range(MOBA_STEPS_PER_TRIP):
            full_step(MOBA_STEPS_PER_TRIP * u + v + 1, (v + 1) % 2)
        return carry

    lax.fori_loop(0, qb // MOBA_STEPS_PER_TRIP, trip, 0)
    done = (qb // MOBA_STEPS_PER_TRIP) * MOBA_STEPS_PER_TRIP
    left_over = qb - done
    run = MOBA_STEPS_PER_TRIP // 2
    while run > 1:
        first = done + (left_over // (2 * run)) * (2 * run)
        def steps(first=first, run=run):
            for v in range(run):
                full_step(first + v + 1, (v + 1) % 2)
        pl.when((left_over // run) % 2 == 1)(steps)
        run //= 2

    def finish():
        yt = jnp.concatenate([acc_ref[h] / l_ref[h] for h in range(B_HEADS)], axis=0)
        yt = yt * lax.rsqrt(jnp.mean(yt * yt, axis=0, keepdims=True) + EPS)
        y_ref[0] = (yt.T * gob_ref[...]).astype(y_ref.dtype)

    for parity in range(2):
        @pl.when(lax.rem(qb, 2) == parity)
        def _(parity=parity):
            if parity == 1:
                full_step(qb, 1)
            for h in range(B_HEADS):
                consume(h, qb, parity)
            finish()

    smallest = jnp.min(jnp.concatenate([l_ref[h] for h in range(B_HEADS)], axis=0))

    @pl.when(jnp.logical_not(smallest > MIN_SHIFTED_SUM))
    def _():
        def online(h, j, own, first):
            visible = causal if own else sel_ref[h, pl.ds(j, 1), :] > NEG * 0.5
            sc = jnp.where(visible, scores(h, j) * LOG2E, NEG)
            m_tile = jnp.max(sc, axis=0, keepdims=True)
            m_new = m_tile if first else jnp.maximum(m_ref[h], m_tile)
            pr = jnp.exp2(sc - m_new)
            l_new, acc_new = jnp.sum(pr, axis=0, keepdims=True), _dot(values(h, j), pr.astype(BF16))
            if not first:
                alpha = jnp.exp2(m_ref[h] - m_new)
                l_new, acc_new = alpha * l_ref[h] + l_new, alpha * acc_ref[h] + acc_new
            m_ref[h], l_ref[h], acc_ref[h] = m_new, l_new, acc_new

        for h in range(B_HEADS):
            online(h, qb, True, True)

        def past(j, carry):
            for h in range(B_HEADS):
                online(h, j, False, False)
            return carry

        lax.fori_loop(0, qb, past, 0)
        finish()


def _moba_call(bound, qt, k, vt, gob):
    b, s, _ = k.shape
    nb = s // B_BLOCK
    nbp = -(-nb // 8) * 8
    return pl.pallas_call(
        _moba_kernel,
        grid=(b, nb),
        in_specs=[
            pl.BlockSpec(memory_space=pltpu.SMEM),
            pl.BlockSpec((1, N_PAIRS, 1, HEAD_PAIR, B_BLOCK), lambda bi, qi: (bi, 0, qi, 0, 0)),
            pl.BlockSpec((1, s, D_B), lambda bi, qi: (bi, 0, 0)),
            pl.BlockSpec((1, N_PAIRS, nb, HEAD_PAIR, B_BLOCK), lambda bi, qi: (bi, 0, 0, 0, 0)),
            pl.BlockSpec((1, D_B), lambda bi, qi: (0, 0)),
        ],
        out_specs=pl.BlockSpec((1, B_BLOCK, D_B), lambda bi, qi: (bi, qi, 0)),
        out_shape=jax.ShapeDtypeStruct((b, s, D_B), BF16),
        scratch_shapes=[
            pltpu.VMEM((N_PAIRS, 3 * nbp, HEAD_PAIR), BF16),
            pltpu.VMEM((B_HEADS, HEAD_PAIR, B_BLOCK), BF16),
            pltpu.VMEM((B_HEADS, nbp, B_BLOCK), F32),
            pltpu.VMEM((B_HEADS, 1, B_BLOCK), F32),
            pltpu.VMEM((B_HEADS, 1, B_BLOCK), F32),
            pltpu.VMEM((B_HEADS, B_DH, B_BLOCK), F32),
            pltpu.VMEM((B_HEADS, B_BLOCK, B_BLOCK), F32),
            pltpu.VMEM((B_HEADS, B_BLOCK, B_BLOCK), F32),
        ],
        compiler_params=pltpu.CompilerParams(
            dimension_semantics=("parallel", "arbitrary"), vmem_limit_bytes=VMEM_LIMIT),
        name="moba",
    )(bound, qt, k, vt, gob)


def _outmix_kernel(x_ref, ya_ref, yb_ref, wout_ref, gffn_ref, wrt_hi_ref, wrt_lo_ref,
                   h_ref, hn_ref, route_ref, route_i_ref, counts_ref, cnt_ref):
    h = x_ref[...] + _dot(ya_ref[...], wout_ref[:D_A]) + _dot(yb_ref[...], wout_ref[D_A:])
    h_ref[...] = h
    hn = _rms(h, gffn_ref[...])
    _to_row_tiles(hn_ref, hn)
    tm = hn.shape[0]

    hn_hi, hn_lo = _split_bf16(hn)
    lg = _dot_nt(wrt_hi_ref[...], hn_hi) + (_dot_nt(wrt_hi_ref[...], hn_lo) + _dot_nt(wrt_lo_ref[...], hn_hi))
    row = lax.broadcasted_iota(jnp.int32, lg.shape, 0)

    def first_row(mask):
        return jnp.min(jnp.where(mask, row, ROUTER_ROWS), axis=0, keepdims=True)

    is_g = row < N_GROUPS
    g_max = jnp.max(jnp.where(is_g, lg, -jnp.inf), axis=0, keepdims=True)
    p_g = 1.0 / jnp.sum(jnp.where(is_g, jnp.exp(lg - g_max), 0.0), axis=0, keepdims=True)
    g_sel = first_row(is_g & (lg == g_max))
    e_row = row - N_GROUPS
    in_grp = (e_row >= 0) & (e_row < N_EXPERTS) & (lax.shift_right_arithmetic(e_row, 3) == g_sel)
    l1 = jnp.max(jnp.where(in_grp, lg, -jnp.inf), axis=0, keepdims=True)
    i1 = first_row(in_grp & (lg == l1))
    rest = in_grp & (row != i1)
    l2 = jnp.max(jnp.where(rest, lg, -jnp.inf), axis=0, keepdims=True)
    i2 = first_row(rest & (lg == l2))
    e2 = jnp.exp(l2 - l1)
    w1 = 1.0 / (1.0 + e2)
    w2 = e2 / (1.0 + e2)

    @pl.when(pl.program_id(0) == 0)
    def _():
        cnt_ref[...] = jnp.zeros_like(cnt_ref)

    picked = jnp.where((row == i1) | (row == i2), 1.0, 0.0)
    tok_r = lax.broadcasted_iota(jnp.int32, (tm, tm), 0)
    tok_c = lax.broadcasted_iota(jnp.int32, (tm, tm), 1)
    earlier = jnp.where(tok_r < tok_c, 1.0, 0.0).astype(BF16)
    before = _dot(picked.astype(BF16), earlier) + cnt_ref[:, 0:1]
    r1 = jnp.sum(jnp.where(row == i1, before, 0.0), axis=0, keepdims=True)
    r2 = jnp.sum(jnp.where(row == i2, before, 0.0), axis=0, keepdims=True)
    cnt_ref[...] += jnp.sum(picked, axis=1, keepdims=True)
    counts_ref[...] = cnt_ref[...]

    e1, e2x = i1 - N_GROUPS, i2 - N_GROUPS
    route_i_ref[...] = jnp.concatenate(
        [e1, e2x, r1.astype(jnp.int32), r2.astype(jnp.int32), jnp.zeros((ROUTE_FIELDS - 4, tm), jnp.int32)], axis=0)
    wts = jnp.concatenate([p_g * w1, p_g * w2, jnp.zeros((LANES - 2, tm), F32)], axis=0)
    route_ref[...] = wts.T


def _outmix_call(x2, ya2, yb2, wout, gffn, wrt_hi, wrt_lo):
    t, d = x2.shape
    tm = OUT_TM
    const = lambda *shape: pl.BlockSpec(shape, lambda ti: (0,) * len(shape))
    tok = lambda w: pl.BlockSpec((tm, w), lambda ti: (ti, 0))
    return pl.pallas_call(
        _outmix_kernel,
        grid=(t // tm,),
        in_specs=[tok(d), tok(D_A), tok(D_B), const(d, d), const(1, d),
                  const(ROUTER_ROWS, d), const(ROUTER_ROWS, d)],
        out_specs=[tok(d), pl.BlockSpec((tm * ROW_SUB, LANES), lambda ti: (ti, 0)), tok(LANES),
                   pl.BlockSpec((ROUTE_FIELDS, tm), lambda ti: (0, ti)), const(ROUTER_ROWS, LANES)],
        out_shape=[jax.ShapeDtypeStruct((t, d), F32), jax.ShapeDtypeStruct((t * ROW_SUB, LANES), F32),
                   jax.ShapeDtypeStruct((t, LANES), F32),
                   jax.ShapeDtypeStruct((ROUTE_FIELDS, t), jnp.int32),
                   jax.ShapeDtypeStruct((ROUTER_ROWS, LANES), F32)],
        scratch_shapes=[pltpu.VMEM((ROUTER_ROWS, LANES), F32)],
        compiler_params=pltpu.CompilerParams(
            dimension_semantics=("arbitrary",), vmem_limit_bytes=VMEM_LIMIT),
        name="outmix",
    )(x2, ya2, yb2, wout, gffn, wrt_hi, wrt_lo)


def _to_row_tiles(ref, x):
    n = x.shape[0]
    for s in range(ROW_SUB):
        ref[pl.ds(s, n, stride=ROW_SUB), :] = x[:, s * LANES:(s + 1) * LANES]


def _from_row_tiles(ref, first, n):
    return jnp.concatenate(
        [ref[pl.ds(first * ROW_SUB + s, n, stride=ROW_SUB), :] for s in range(ROW_SUB)], axis=1)


def _tiles_at(ref, row, n=1):
    return ref.at[pl.ds(pl.multiple_of(row * ROW_SUB, ROW_SUB), n * ROW_SUB), :]


def _wait_tiles(hbm, vmem_ref, n_tiles, sem):
    cap = vmem_ref.shape[0] // ROW_SUB
    while n_tiles > 0:
        n = min(n_tiles, cap)
        pltpu.make_async_copy(hbm.at[pl.ds(0, n * ROW_SUB), :], vmem_ref.at[pl.ds(0, n * ROW_SUB), :], sem).wait()
        n_tiles -= n


def _slots_kernel(seg_start_ref, route_i_ref, pos_ref):
    experts = route_i_ref[0:2, :]
    start = jnp.zeros_like(experts)
    for e in range(N_EXPERTS):
        start = jnp.where(experts == e, seg_start_ref[e], start)
    pos_ref[...] = jnp.zeros_like(pos_ref)
    pos_ref[0:2, :] = start + route_i_ref[2:4, :]


def _slots_call(seg_start, route_i):
    return pl.pallas_call(
        _slots_kernel,
        grid_spec=pltpu.PrefetchScalarGridSpec(
            num_scalar_prefetch=1,
            grid=(1,),
            in_specs=[pl.BlockSpec(route_i.shape, lambda i, s: (0, 0))],
            out_specs=pl.BlockSpec(route_i.shape, lambda i, s: (0, 0)),
        ),
        out_shape=jax.ShapeDtypeStruct(route_i.shape, jnp.int32),
        name="slots",
    )(seg_start, route_i)


def _dispatch_kernel(gap_start_ref, gap_len_ref, pos_ref, hn_ref, xs_hbm, zero_ref, sem, fill_sem):
    tm = hn_ref.shape[0] // ROW_SUB

    @pl.when(pl.program_id(0) == 0)
    def _():
        zero_ref[...] = jnp.zeros_like(zero_ref)
        fills = []
        for g in range(N_EXPERTS):
            row, n = gap_start_ref[g], gap_len_ref[g]
            size = EXP_TM // 2
            while size >= 1:
                fills.append(((n & size) != 0, row, size))
                row = row + (n & size)
                size //= 2
        for tile in range(N_EXPERTS):
            fills.append((tile * EXP_TM < gap_len_ref[N_EXPERTS], gap_start_ref[N_EXPERTS] + tile * EXP_TM, EXP_TM))
        copies = [(cond, pltpu.make_async_copy(_tiles_at(zero_ref, 0, size), _tiles_at(xs_hbm, row, size), fill_sem))
                  for cond, row, size in fills]
        for cond, copy in copies:
            pl.when(cond)(copy.start)
        for cond, copy in copies:
            pl.when(cond)(copy.wait)

    for r in range(2 * tm):
        k, tok = divmod(r, tm)
        pltpu.make_async_copy(_tiles_at(hn_ref, tok), _tiles_at(xs_hbm, pos_ref[k, tok]), sem).start(priority=r % 2)
    _wait_tiles(xs_hbm, hn_ref, 2 * tm, sem)


def _dispatch_call(gap_start, gap_len, pos, hn_tiles, n_sorted_rows):
    tm = TOK_TM
    return pl.pallas_call(
        _dispatch_kernel,
        grid_spec=pltpu.PrefetchScalarGridSpec(
            num_scalar_prefetch=2,
            grid=(pos.shape[1] // tm,),
            in_specs=[pl.BlockSpec((ROUTE_FIELDS, tm), lambda i, *_: (0, i), memory_space=pltpu.SMEM),
                      pl.BlockSpec((tm * ROW_SUB, LANES), lambda i, *_: (i, 0))],
            out_specs=pl.BlockSpec(memory_space=pl.ANY),
            scratch_shapes=[pltpu.VMEM((EXP_TM * ROW_SUB, LANES), F32), pltpu.SemaphoreType.DMA(()),
                            pltpu.SemaphoreType.DMA(())],
        ),
        out_shape=jax.ShapeDtypeStruct((n_sorted_rows * ROW_SUB, LANES), F32),
        compiler_params=pltpu.CompilerParams(
            dimension_semantics=("arbitrary",), vmem_limit_bytes=VMEM_LIMIT),
        name="dispatch",
    )(gap_start, gap_len, pos, hn_tiles)


def _experts_kernel(tile_e_ref, n_tiles_ref, xs_hbm, wg_ref, wu_ref, wd_ref, y_ref, xbuf, sem, wg16, wu16, wd16):
    i = pl.program_id(0)
    tm = y_ref.shape[0] // ROW_SUB
    depth = sem.shape[0]

    def fetch(j):
        slot = j % depth
        return pltpu.make_async_copy(_tiles_at(xs_hbm, j * tm, tm), _tiles_at(xbuf, slot * tm, tm), sem.at[slot])

    @pl.when(i == 0)
    def _():
        for j in range(depth - 1):
            pl.when(j < n_tiles_ref[0])(lambda j=j: fetch(j).start())

    @pl.when(i + depth - 1 < n_tiles_ref[0])
    def _():
        fetch(i + depth - 1).start()

    @pl.when((i == 0) | (tile_e_ref[i] != tile_e_ref[jnp.maximum(i - 1, 0)]))
    def _():
        wg16[...] = wg_ref[0].astype(BF16)
        wu16[...] = wu_ref[0].astype(BF16)
        wd16[...] = wd_ref[0].astype(BF16)

    @pl.when(i < n_tiles_ref[0])
    def _():
        fetch(i).wait()
        x = _from_row_tiles(xbuf, (i % depth) * tm, tm).astype(BF16)
        a = jax.nn.silu(_dot(x, wg16[...])) * _dot(x, wu16[...])
        _to_row_tiles(y_ref, _dot(a.astype(BF16), wd16[...]))

    @pl.when(i >= n_tiles_ref[0])
    def _():
        y_ref[...] = jnp.zeros_like(y_ref)


def _experts_call(tile_e, n_tiles, xs, wg, wu, wd):
    d = wg.shape[1]
    tm = EXP_TM
    nt = xs.shape[0] // (tm * ROW_SUB)
    used = lambda i, n: jnp.minimum(i, n[0] - 1)
    w_spec = lambda *shape: pl.BlockSpec((1,) + shape, lambda i, te, n: (te[used(i, n)], 0, 0))
    return pl.pallas_call(
        _experts_kernel,
        grid_spec=pltpu.PrefetchScalarGridSpec(
            num_scalar_prefetch=2,
            grid=(nt,),
            in_specs=[pl.BlockSpec(memory_space=pl.ANY),
                      w_spec(d, D_FF_EXP), w_spec(d, D_FF_EXP), w_spec(D_FF_EXP, d)],
            out_specs=pl.BlockSpec((tm * ROW_SUB, LANES), lambda i, te, n: (i, 0)),
            scratch_shapes=[pltpu.VMEM((EXP_ROW_BUFFERS * tm * ROW_SUB, LANES), F32),
                            pltpu.SemaphoreType.DMA((EXP_ROW_BUFFERS,)),
                            pltpu.VMEM((d, D_FF_EXP), BF16), pltpu.VMEM((d, D_FF_EXP), BF16),
                            pltpu.VMEM((D_FF_EXP, d), BF16)],
        ),
        out_shape=jax.ShapeDtypeStruct(xs.shape, F32),
        compiler_params=pltpu.CompilerParams(
            dimension_semantics=("arbitrary",), vmem_limit_bytes=VMEM_LIMIT),
        name="experts",
    )(tile_e, n_tiles, xs, wg, wu, wd)


def _ple_kernel(cur_ref, nxt_ref, ys_hbm, h_ref, route_ref, p_ref, wple_ref, gple_ref, wpg_ref,
                o_ref, ybuf0, ybuf1, sem):
    tm = h_ref.shape[0]
    i = pl.program_id(0)
    n_steps = pl.num_programs(0)
    bufs = (ybuf0, ybuf1)

    def start_gather(pos_ref, slot):
        for r in range(2 * tm):
            pltpu.make_async_copy(_tiles_at(ys_hbm, pos_ref[r // tm, r % tm]), _tiles_at(bufs[slot], r),
                                  sem.at[slot]).start(priority=r % 2)

    @pl.when(i == 0)
    def _():
        start_gather(cur_ref, 0)

    def step(slot):
        _wait_tiles(ys_hbm, bufs[slot], 2 * tm, sem.at[slot])
        start_gather(nxt_ref, 1 - slot)
        wts = route_ref[...]
        h = (h_ref[...] + wts[:, 0:1] * _from_row_tiles(bufs[slot], 0, tm)
             + wts[:, 1:2] * _from_row_tiles(bufs[slot], tm, tm))
        gate = jax.nn.sigmoid(_dot(_rms(h, gple_ref[...]).astype(BF16), wpg_ref[...]))
        o_ref[...] = h + _dot(p_ref[...].astype(BF16), wple_ref[...]) * gate

    for slot in range(2):
        pl.when(lax.rem(i, 2) == slot)(functools.partial(step, slot))

    for slot in range(2):
        pl.when((i == n_steps - 1) & (lax.rem(n_steps, 2) == slot))(
            functools.partial(_wait_tiles, ys_hbm, bufs[slot], 2 * tm, sem.at[slot]))


def _ple_call(pos, ys, h, route, p2, wple, gple, wpg):
    t, d = h.shape
    tm = TOK_TM
    n_steps = t // tm
    n_rows = 2 * tm
    idx_spec = lambda f: pl.BlockSpec((ROUTE_FIELDS, tm), lambda i: (0, f(i)), memory_space=pltpu.SMEM)
    const = lambda *shape: pl.BlockSpec(shape, lambda i: (0,) * len(shape))
    tok = lambda w: pl.BlockSpec((tm, w), lambda i: (i, 0))
    return pl.pallas_call(
        _ple_kernel,
        grid=(n_steps,),
        in_specs=[idx_spec(lambda i: i), idx_spec(lambda i: lax.rem(i + 1, n_steps)),
                  pl.BlockSpec(memory_space=pl.ANY),
                  tok(d), tok(LANES), tok(D_PLE), const(D_PLE, d), const(1, d), const(d, d)],
        out_specs=tok(d),
        out_shape=jax.ShapeDtypeStruct((t, d), F32),
        scratch_shapes=[pltpu.VMEM((n_rows * ROW_SUB, LANES), F32), pltpu.VMEM((n_rows * ROW_SUB, LANES), F32),
                        pltpu.SemaphoreType.DMA((2,))],
        compiler_params=pltpu.CompilerParams(
            dimension_semantics=("arbitrary",), vmem_limit_bytes=VMEM_LIMIT),
        name="ple",
    )(pos, pos, ys, h, route, p2, wple, gple, wpg)


def _segment_tables(counts, t):
    tm = EXP_TM
    nt = 2 * t // tm + N_EXPERTS
    padded = ((counts + tm - 1) // tm) * tm
    seg_end = jnp.cumsum(padded)
    seg_start = seg_end - padded
    gap_len = jnp.concatenate([padded - counts, nt * tm - seg_end[-1:]])
    gap_start = jnp.concatenate([seg_start + counts, seg_end[-1:]])
    tile_start = jnp.arange(nt, dtype=jnp.int32) * tm
    tile_e = jnp.minimum(jnp.sum(tile_start[:, None] >= seg_end[None, :], axis=1), N_EXPERTS - 1)
    n_tiles = (seg_end[-1] // tm).reshape(1)
    i32 = lambda a: a.astype(jnp.int32)
    return i32(seg_start), i32(gap_start), i32(gap_len), i32(tile_e), i32(n_tiles), nt * tm


def _layer(h, p_i, g_mix, w_in, g_v_a, w_s, b_s, g_q, g_k, g_out_a, g_out_b, w_out, g_ffn, w_group,
           w_expert, w_gate_e, w_up_e, w_down_e, g_ple, w_ple, w_ple_gate):
    b, s, d = h.shape
    t = b * s
    row = lambda g: g.reshape(1, -1).astype(F32)
    bs = jnp.broadcast_to(b_s[:, :, None], (A_GROUPS, A_CHUNK, A_DH)).astype(F32)
    gk = row(jnp.tile(g_k, B_HEADS))
    gq = (jnp.tile(g_q, B_HEADS) * (1.0 / float(B_DH) ** 0.5)).reshape(D_B, 1).astype(F32)
    head = jnp.arange(D_B) // B_DH
    bd = jnp.where(head[:, None] == head[None, :], 1.0 / B_DH, 0.0).astype(BF16)
    w_rt = jnp.zeros((ROUTER_ROWS, d), F32)
    w_rt = w_rt.at[:N_GROUPS].set(w_group.T).at[N_GROUPS:N_GROUPS + N_EXPERTS].set(w_expert.T)
    wrt_hi, wrt_lo = _split_bf16(w_rt)

    ya, k, qt, vt = _proj_call(h, row(g_mix), w_in.astype(F32), row(g_v_a), w_s.astype(F32), bs,
                               row(g_out_a), gk, gq, bd)
    bound = (BOUND_MARGIN * LOG2E * float(B_DH) ** 0.5) * jnp.max(jnp.abs(g_q)) * jnp.max(jnp.abs(g_k))
    yb = _moba_call(bound.reshape(1, 1).astype(F32), qt, k, vt, row(g_out_b))
    h1, hn_tiles, route, route_i, counts = _outmix_call(
        h.reshape(t, d), ya.reshape(t, D_A), yb.reshape(t, D_B), w_out.astype(BF16), row(g_ffn), wrt_hi, wrt_lo)
    counts = counts[N_GROUPS:N_GROUPS + N_EXPERTS, 0].astype(jnp.int32)
    seg_start, gap_start, gap_len, tile_e, n_tiles, n_sorted_rows = _segment_tables(counts, t)
    pos = _slots_call(seg_start, route_i)
    xs = _dispatch_call(gap_start, gap_len, pos, hn_tiles, n_sorted_rows)
    ys = _experts_call(tile_e, n_tiles, xs, w_gate_e, w_up_e, w_down_e)
    out = _ple_call(pos, ys, h1, route, p_i.reshape(t, D_PLE), w_ple.astype(BF16), row(g_ple),
                    w_ple_gate.astype(BF16))
    return out.reshape(b, s, d)


def kernel(x, p, g_mix, w_in, g_v_a, w_s, b_s, g_q, g_k, g_out_a, g_out_b, w_out, g_ffn, w_group, w_expert,
           w_gate_e, w_up_e, w_down_e, g_ple, w_ple, w_ple_gate):
    params = (g_mix, w_in, g_v_a, w_s, b_s, g_q, g_k, g_out_a, g_out_b, w_out, g_ffn, w_group, w_expert,
              w_gate_e, w_up_e, w_down_e, g_ple, w_ple, w_ple_gate)
    h = x
    for i in range(p.shape[0]):
        h = _layer(h, p[i], *(w[i] for w in params))
    return h
```
